```python
import math
import jax, jax.numpy as jnp
from jax import lax
import numpy as np

D_MODEL = 1024
BATCH = 2
SEQ = 8192
DEPTH = 2
DEC_BATCH = 16
DEC_SEQ = 32
PAST_LEN = 1024

CHUNK = 64
EPS = 1e-6
NEG = -1e30
F_MIN = 1e-30
A_HEADS = 4
A_KDIM = 128
A_VDIM = 128
A_WIDTH = A_HEADS * A_VDIM
B_HEADS = 4
B_KDIM = 64
B_VDIM = 128
B_WIDTH = B_HEADS * B_VDIM
B_GATE_RANK = 16
B_TAU = 16.0
C_HEADS = 8
C_KV_HEADS = 2
C_HDIM = 64
C_WIDTH = C_HEADS * C_HDIM
IDX_HEADS = 8
IDX_DIM = 64
IDX_SCALE = (IDX_HEADS * IDX_DIM) ** -0.5
TOPK_MAX = 256
QBLK = 128
ROPE_THETA = 500000.0
ROPE_FRAC_DIV = 4
GLA_BLOCK = 16
D_FF = 2816

IN_WIDTHS = (
    A_HEADS * A_KDIM, A_HEADS * A_KDIM, A_WIDTH, A_WIDTH,
    B_HEADS * B_KDIM, B_HEADS * B_KDIM, B_WIDTH, B_GATE_RANK, B_WIDTH,
    C_WIDTH, C_KV_HEADS * C_HDIM, C_KV_HEADS * C_HDIM,
    IDX_HEADS * IDX_DIM, IDX_DIM, IDX_HEADS,
    D_MODEL, D_MODEL, D_MODEL,
)
D_IN = sum(IN_WIDTHS)

kernel_name = 'hybrid_hgrn2_gla_dsa_macaron_stream_step'


def split_cols(z):
    outs, o = [], 0
    for w in IN_WIDTHS:
        outs.append(z[..., o:o + w])
        o += w
    return outs


def rmsnorm(x, g):
    xf = x.astype(jnp.float32)
    y = xf * lax.rsqrt(jnp.mean(xf * xf, axis=-1, keepdims=True) + EPS)
    return (y * g.astype(jnp.float32)).astype(x.dtype)


def swiglu(x, w1, w3, w2):
    return (jax.nn.silu(x @ w1) * (x @ w3)) @ w2


def rope(x, pos):
    d = x.shape[-1]
    rd = d // ROPE_FRAC_DIV
    half = rd // 2
    inv = ROPE_THETA ** (-jnp.arange(half, dtype=jnp.float32) * (2.0 / rd))
    ang = pos.astype(jnp.float32)[:, None] * inv[None, :]
    cos = jnp.cos(ang)[:, None, :]
    sin = jnp.sin(ang)[:, None, :]
    xf = x.astype(jnp.float32)
    x1, x2, xp = xf[..., :half], xf[..., half:rd], xf[..., rd:]
    out = jnp.concatenate([x1 * cos - x2 * sin, x2 * cos + x1 * sin, xp], axis=-1)
    return out.astype(x.dtype)


def gated_linear_attention(q, k, v, logf, s0):
    Bn, T, H, dk = q.shape
    dv = v.shape[-1]
    blk = math.gcd(T, GLA_BLOCK)
    n = T // blk

    def blocks(a):
        return a.astype(jnp.float32).reshape(Bn, n, blk, H, a.shape[-1])

    q, k, v, g = blocks(q), blocks(k), blocks(v), blocks(logf)
    b = jnp.cumsum(g, axis=2)
    causal = jnp.tril(jnp.ones((blk, blk), dtype=bool))[None, None, :, :, None, None]
    diff = b[:, :, :, None] - b[:, :, None, :]
    decay = jnp.where(causal, jnp.exp(jnp.where(causal, diff, 0.0)), 0.0)
    attn = jnp.einsum('bnthd,bnshd,bntshd->bnhts', q, k, decay)
    o_intra = jnp.einsum('bnhts,bnshv->bnthv', attn, v)
    b_last = b[:, :, -1]
    q_dec = q * jnp.exp(b)
    k_dec = k * jnp.exp(b_last[:, :, None] - b)

    def step(S, xs):
        qd, kd, vv, bl = xs
        o = jnp.einsum('bthd,bhdv->bthv', qd, S)
        S = jnp.exp(bl)[..., None] * S + jnp.einsum('bshd,bshv->bhdv', kd, vv)
        return S, o

    xs = tuple(jnp.moveaxis(a, 1, 0) for a in (q_dec, k_dec, v, b_last))
    s_fin, o_inter = lax.scan(step, s0.astype(jnp.float32), xs)
    o = o_intra + jnp.moveaxis(o_inter, 0, 1)
    return o.reshape(Bn, T, H, dv), s_fin


def dsa_block(q, qi, wi, qpos, keys, vals, kidx, topk):
    Bn, Tq = q.shape[0], q.shape[1]
    L = keys.shape[1]
    kpos = jnp.arange(L, dtype=jnp.int32)
    qchunk = qpos // CHUNK
    adm = (kpos[None, :] // CHUNK) <= qchunk[:, None]
    dots = jnp.einsum('btjd,bsd->btjs', qi.astype(jnp.float32), kidx.astype(jnp.float32))
    score = jnp.einsum('btjs,btj->bts', jax.nn.relu(dots), wi.astype(jnp.float32)) * IDX_SCALE
    score = jnp.where(adm[None], score, NEG)
    _, sel = lax.top_k(score, topk)
    valid = (sel // CHUNK) <= qchunk[None, :, None]
    take = jax.vmap(lambda a, i: a[i])
    ks = take(keys, sel).astype(jnp.float32)
    vs = take(vals, sel).astype(jnp.float32)
    grp = C_HEADS // C_KV_HEADS
    qg = q.reshape(Bn, Tq, C_KV_HEADS, grp, C_HDIM).astype(jnp.float32)
    s = jnp.einsum('btkgd,btskd->btkgs', qg, ks) * (C_HDIM ** -0.5)
    s = jnp.where(valid[:, :, None, None, :], s, NEG)
    p = jax.nn.softmax(s, axis=-1)
    o = jnp.einsum('btkgs,btskd->btkgd', p, vs)
    return o.reshape(Bn, Tq, C_WIDTH).astype(q.dtype)


def trunk_layer(x, pos, s_a, s_b, past_k, past_v, past_kidx, lb, w_in, w_gla_up, b_gla,
                norm_hgrn, norm_gla, w_br_a, w_br_b, w_br_c, w_out, norm_ffn1, norm_mix, norm_ffn2,
                ffn1_w1, ffn1_w3, ffn1_w2, ffn2_w1, ffn2_w3, ffn2_w2):
    Bn, T, _ = x.shape
    dt = x.dtype
    f32 = jnp.float32
    h = x + 0.5 * swiglu(rmsnorm(x, norm_ffn1), ffn1_w1, ffn1_w3, ffn1_w2)
    u = rmsnorm(h, norm_mix)
    (a_q, a_f, a_i, a_g, b_q, b_k, b_v, b_r, b_g, c_q, c_k, c_v,
     i_q, i_k, i_w, gate_a, gate_b, gate_c) = split_cols(u @ w_in)

    def hd(z, n):
        return z.reshape(Bn, T, n, -1)

    zf = hd(a_f, A_HEADS).astype(f32)
    f_a = lb + (1.0 - lb) * jax.nn.sigmoid(zf)
    logf_a = jnp.log(jnp.maximum(f_a, F_MIN))
    k_a = (1.0 - lb) * jax.nn.sigmoid(-zf)
    q_a = jax.nn.silu(hd(a_q, A_HEADS).astype(f32)) * (A_KDIM ** -0.5)
    o_a, s_a_new = gated_linear_attention(q_a, k_a, hd(a_i, A_HEADS), logf_a, s_a)
    y_a = (rmsnorm(o_a, norm_hgrn) * jax.nn.silu(hd(a_g, A_HEADS).astype(f32)))
    y_a = y_a.reshape(Bn, T, A_WIDTH).astype(dt) @ w_br_a

    r = (b_r @ w_gla_up + b_gla).astype(f32)
    logf_b = hd(jax.nn.log_sigmoid(r) / B_TAU, B_HEADS)
    q_b = hd(b_q, B_HEADS).astype(f32) * (B_KDIM ** -0.5)
    o_b, s_b_new = gated_linear_attention(q_b, hd(b_k, B_HEADS), hd(b_v, B_HEADS), logf_b, s_b)
    y_b = (rmsnorm(o_b, norm_gla) * jax.nn.silu(hd(b_g, B_HEADS).astype(f32)))
    y_b = y_b.reshape(Bn, T, B_WIDTH).astype(dt) @ w_br_b

    q_c = rope(hd(c_q, C_HEADS), pos)
    k_c = rope(hd(c_k, C_KV_HEADS), pos)
    v_c = hd(c_v, C_KV_HEADS)
    q_i = rope(hd(i_q, IDX_HEADS), pos)
    k_i = rope(i_k[:, :, None, :], pos)[:, :, 0, :]
    keys = jnp.concatenate([past_k.astype(dt), k_c], axis=1)
    vals = jnp.concatenate([past_v.astype(dt), v_c], axis=1)
    kidx = jnp.concatenate([past_kidx.astype(dt), k_i], axis=1)
    L = keys.shape[1]
    topk = min(TOPK_MAX, L // 4)
    if T % QBLK == 0:
        nb = T // QBLK

        def blockify(a):
            return jnp.moveaxis(a.reshape(Bn, nb, QBLK, *a.shape[2:]), 1, 0)

        xs = (blockify(q_c), blockify(q_i), blockify(i_w), pos.reshape(nb, QBLK))
        o = lax.map(lambda z: dsa_block(z[0], z[1], z[2], z[3], keys, vals, kidx, topk), xs)
        o_c = jnp.moveaxis(o, 0, 1).reshape(Bn, T, C_WIDTH)
    else:
        o_c = dsa_block(q_c, q_i, i_w, pos, keys, vals, kidx, topk)
    y_c = o_c @ w_br_c

    merged = jax.nn.sigmoid(gate_a) * y_a + jax.nn.sigmoid(gate_b) * y_b + jax.nn.sigmoid(gate_c) * y_c
    h = h + merged @ w_out
    out = h + 0.5 * swiglu(rmsnorm(h, norm_ffn2), ffn2_w1, ffn2_w3, ffn2_w2)
    return out, s_a_new, s_b_new, k_c, v_c, k_i


def run_group(x, pos, s_a, s_b, past_k, past_v, past_kidx, lb_all, w_in, w_gla_up, b_gla,
              norm_hgrn, norm_gla, w_br_a, w_br_b, w_br_c, w_out, norm_ffn1, norm_mix, norm_ffn2,
              ffn1_w1, ffn1_w3, ffn1_w2, ffn2_w1, ffn2_w3, ffn2_w2, norm_final):
    new_a, new_b, new_k, new_v, new_ki = [], [], [], [], []
    h = x
    for l in range(DEPTH):
        h, sa, sb, kk, vv, ki = trunk_layer(
            h, pos, s_a[l], s_b[l], past_k[l], past_v[l], past_kidx[l],
            lb_all[l].reshape(A_HEADS, A_KDIM), w_in[l], w_gla_up[l], b_gla[l],
            norm_hgrn[l], norm_gla[l], w_br_a[l], w_br_b[l], w_br_c[l], w_out[l],
            norm_ffn1[l], norm_mix[l], norm_ffn2[l],
            ffn1_w1[l], ffn1_w3[l], ffn1_w2[l], ffn2_w1[l], ffn2_w3[l], ffn2_w2[l])
        new_a.append(sa)
        new_b.append(sb)
        new_k.append(kk)
        new_v.append(vv)
        new_ki.append(ki)
    y = rmsnorm(h, norm_final)
    return (y, jnp.stack(new_a), jnp.stack(new_b), jnp.stack(new_k), jnp.stack(new_v), jnp.stack(new_ki))


def setup_inputs(seed: int = 0) -> dict:
    key = jax.random.key(seed)
    ks = iter(jax.random.split(key, 40))

    def nrm(shape, scale):
        return jax.random.normal(next(ks), shape, jnp.float32) * scale

    D = D_MODEL
    return {
        'x_prompt': nrm((BATCH, SEQ, D), 1.0),
        'x_sample': nrm((DEC_BATCH, DEC_SEQ, D), 1.0),
        'state_hgrn': nrm((DEPTH, DEC_BATCH, A_HEADS, A_KDIM, A_VDIM), 0.5),
        'state_gla': nrm((DEPTH, DEC_BATCH, B_HEADS, B_KDIM, B_VDIM), 0.5),
        'cache_k': nrm((DEPTH, DEC_BATCH, PAST_LEN, C_KV_HEADS, C_HDIM), 1.0),
        'cache_v': nrm((DEPTH, DEC_BATCH, PAST_LEN, C_KV_HEADS, C_HDIM), 1.0),
        'cache_kidx': nrm((DEPTH, DEC_BATCH, PAST_LEN, IDX_DIM), 1.0),
        'hgrn_lb': nrm((DEPTH, A_HEADS * A_KDIM), 0.5),
        'w_in': nrm((DEPTH, D, D_IN), D ** -0.5),
        'w_gla_up': nrm((DEPTH, B_GATE_RANK, B_HEADS * B_KDIM), B_GATE_RANK ** -0.5),
        'b_gla': nrm((DEPTH, B_HEADS * B_KDIM), 0.1),
        'norm_hgrn': 1.0 + nrm((DEPTH, A_VDIM), 0.05),
        'norm_gla': 1.0 + nrm((DEPTH, B_VDIM), 0.05),
        'w_br_a': nrm((DEPTH, A_WIDTH, D), A_WIDTH ** -0.5),
        'w_br_b': nrm((DEPTH, B_WIDTH, D), B_WIDTH ** -0.5),
        'w_br_c': nrm((DEPTH, C_WIDTH, D), C_WIDTH ** -0.5),
        'w_out': nrm((DEPTH, D, D), D ** -0.5),
        'norm_ffn1': 1.0 + nrm((DEPTH, D), 0.05),
        'norm_mix': 1.0 + nrm((DEPTH, D), 0.05),
        'norm_ffn2': 1.0 + nrm((DEPTH, D), 0.05),
        'ffn1_w1': nrm((DEPTH, D, D_FF), D ** -0.5),
        'ffn1_w3': nrm((DEPTH, D, D_FF), D ** -0.5),
        'ffn1_w2': nrm((DEPTH, D_FF, D), D_FF ** -0.5),
        'ffn2_w1': nrm((DEPTH, D, D_FF), D ** -0.5),
        'ffn2_w3': nrm((DEPTH, D, D_FF), D ** -0.5),
        'ffn2_w2': nrm((DEPTH, D_FF, D), D_FF ** -0.5),
        'norm_final': 1.0 + nrm((D,), 0.05),
    }


def reference(x_prompt, x_sample, state_hgrn, state_gla, cache_k, cache_v, cache_kidx,
              hgrn_lb, w_in, w_gla_up, b_gla, norm_hgrn, norm_gla, w_br_a, w_br_b, w_br_c, w_out,
              norm_ffn1, norm_mix, norm_ffn2, ffn1_w1, ffn1_w3, ffn1_w2, ffn2_w1, ffn2_w3, ffn2_w2,
              norm_final):
    lb_sm = jax.nn.softmax(hgrn_lb.astype(jnp.float32), axis=0)
    lb_all = jnp.concatenate([jnp.zeros_like(lb_sm[:1]), jnp.cumsum(lb_sm[1:], axis=0)], axis=0)
    weights = (w_in, w_gla_up, b_gla, norm_hgrn, norm_gla, w_br_a, w_br_b, w_br_c, w_out,
               norm_ffn1, norm_mix, norm_ffn2, ffn1_w1, ffn1_w3, ffn1_w2, ffn2_w1, ffn2_w3, ffn2_w2,
               norm_final)

    bp, tp = x_prompt.shape[0], x_prompt.shape[1]
    pos_p = jnp.arange(tp, dtype=jnp.int32)
    zs_a = jnp.zeros((DEPTH, bp, A_HEADS, A_KDIM, A_VDIM), jnp.float32)
    zs_b = jnp.zeros((DEPTH, bp, B_HEADS, B_KDIM, B_VDIM), jnp.float32)
    zk = jnp.zeros((DEPTH, bp, 0, C_KV_HEADS, C_HDIM), x_prompt.dtype)
    zki = jnp.zeros((DEPTH, bp, 0, IDX_DIM), x_prompt.dtype)
    y_prompt, p_hgrn, p_gla, p_k, p_v, p_kidx = run_group(
        x_prompt, pos_p, zs_a, zs_b, zk, zk, zki, lb_all, *weights)

    past = cache_k.shape[2]
    pos_s = past + jnp.arange(x_sample.shape[1], dtype=jnp.int32)
    y_sample, s_hgrn, s_gla, s_k, s_v, s_kidx = run_group(
        x_sample, pos_s, state_hgrn, state_gla, cache_k, cache_v, cache_kidx, lb_all, *weights)

    return (y_prompt, y_sample, p_hgrn, p_gla, p_k, p_v, p_kidx, s_hgrn, s_gla, s_k, s_v, s_kidx)
```

```python
import functools

import jax
import jax.numpy as jnp
from jax import lax
from jax.experimental import pallas as pl
from jax.experimental.pallas import tpu as pltpu

F32 = jnp.float32
BF16 = jnp.bfloat16

D_MODEL = 1024
DEPTH = 2
CHUNK = 64
EPS = 1e-6
NEG = -1e30
F_MIN = 1e-30
A_HEADS = 4
A_KDIM = 128
A_VDIM = 128
B_HEADS = 4
B_KDIM = 64
B_VDIM = 128
B_GATE_RANK = 16
B_TAU = 16.0
C_HEADS = 8
C_KV_HEADS = 2
C_HDIM = 64
C_WIDTH = C_HEADS * C_HDIM
IDX_HEADS = 8
IDX_DIM = 64
IDX_SCALE = (IDX_HEADS * IDX_DIM) ** -0.5
TOPK_MAX = 256
ROPE_THETA = 500000.0
ROPE_DIMS = C_HDIM // 4
GLA_BLOCK = 16
D_FF = 2816

LANES = 128
HEAD_W = 128
MIX_W = 4 * HEAD_W
TOKEN_TILE = 512
FF_TILE = D_FF // 2
KEY_BLOCK = 128
INT_MIN = -(2 ** 31)

Z_GATES = 0
Z_A = 24
Z_B = 40
Z_CQ = 56
Z_IQ = 60
Z_CKVI = 64
Z_SMALL = 67
Z_WIDTH = 68 * LANES
SMALL_IW = B_GATE_RANK


def _params(sem, vmem_mb):
    return pltpu.CompilerParams(dimension_semantics=sem, vmem_limit_bytes=vmem_mb << 20)


def _dot(a, b):
    return jnp.dot(a, b, preferred_element_type=F32)


def _dot_nt(a, b):
    return lax.dot_general(a, b, (((1,), (1,)), ((), ())), preferred_element_type=F32)


def _dot_tn(a, b):
    return lax.dot_general(a, b, (((0,), (0,)), ((), ())), preferred_element_type=F32)


def _split3(x):
    h1 = x.astype(BF16)
    r1 = x - h1.astype(F32)
    h2 = r1.astype(BF16)
    h3 = (r1 - h2.astype(F32)).astype(BF16)
    return h1, h2, h3


def _dot_exact_lhs(m, x):
    h1, h2, h3 = _split3(x)
    return _dot(m, h1) + _dot(m, h2) + _dot(m, h3)


def _dot_hi(a, b):
    a1 = a.astype(BF16)
    a2 = (a - a1.astype(F32)).astype(BF16)
    b1 = b.astype(BF16)
    b2 = (b - b1.astype(F32)).astype(BF16)
    return _dot(a1, b1) + _dot(a1, b2) + _dot(a2, b1)


def _rmsnorm(x, g):
    return x * lax.rsqrt(jnp.mean(x * x, axis=-1, keepdims=True) + EPS) * g


def _ffn_body(x_ref, g_ref, w1_ref, w3_ref, w2_ref, gf_ref, o_ref, u_ref, acc_ref, *, final_norm):
    j = pl.program_id(1)

    @pl.when(j == 0)
    def _():
        u_ref[...] = _rmsnorm(x_ref[...], g_ref[...]).astype(BF16)
        acc_ref[...] = jnp.zeros_like(acc_ref)

    u = u_ref[...]
    a = _dot(u, w1_ref[...])
    b = _dot(u, w3_ref[...])
    hid = (a * jax.nn.sigmoid(a) * b).astype(BF16)
    acc_ref[...] += _dot(hid, w2_ref[...])

    @pl.when(j == pl.num_programs(1) - 1)
    def _():
        out = x_ref[...] + 0.5 * acc_ref[...]
        if final_norm:
            out = _rmsnorm(out, gf_ref[...])
        o_ref[...] = out


def _ffn(x, g, w1, w3, w2, gf, final_norm):
    n, d = x.shape
    dff = w1.shape[1]
    tm = min(TOKEN_TILE, n)
    tf = FF_TILE if dff == D_FF else dff
    return pl.pallas_call(
        functools.partial(_ffn_body, final_norm=final_norm),
        grid=(n // tm, dff // tf),
        in_specs=[
            pl.BlockSpec((tm, d), lambda i, j: (i, 0)),
            pl.BlockSpec((1, d), lambda i, j: (0, 0)),
            pl.BlockSpec((d, tf), lambda i, j: (0, j)),
            pl.BlockSpec((d, tf), lambda i, j: (0, j)),
            pl.BlockSpec((tf, d), lambda i, j: (j, 0)),
            pl.BlockSpec((1, d), lambda i, j: (0, 0)),
        ],
        out_specs=pl.BlockSpec((tm, d), lambda i, j: (i, 0)),
        out_shape=jax.ShapeDtypeStruct((n, d), F32),
        scratch_shapes=[pltpu.VMEM((tm, d), BF16), pltpu.VMEM((tm, d), F32)],
        compiler_params=_params(("parallel", "arbitrary"), 48),
        name="ffn_half_step",
    )(x, g, w1, w3, w2, gf)


def _inproj_body(x_ref, g_ref, w_ref, z_ref, u_ref):
    @pl.when(pl.program_id(1) == 0)
    def _():
        u_ref[...] = _rmsnorm(x_ref[...], g_ref[...]).astype(BF16)

    z_ref[...] = _dot(u_ref[...], w_ref[...])


def _inproj(h, g, w):
    n, d = h.shape
    zw = w.shape[1]
    tm = min(TOKEN_TILE, n)
    tn = 512
    return pl.pallas_call(
        _inproj_body,
        grid=(n // tm, zw // tn),
        in_specs=[
            pl.BlockSpec((tm, d), lambda i, j: (i, 0)),
            pl.BlockSpec((1, d), lambda i, j: (0, 0)),
            pl.BlockSpec((d, tn), lambda i, j: (0, j)),
        ],
        out_specs=pl.BlockSpec((tm, tn), lambda i, j: (i, j)),
        out_shape=jax.ShapeDtypeStruct((n, zw), F32),
        scratch_shapes=[pltpu.VMEM((tm, d), BF16)],
        compiler_params=_params(("parallel", "arbitrary"), 32),
        name="in_projection",
    )(h, g, w)


def _rope(x, c, s_lo, s_hi):
    w = x.shape[1]
    rep = w // LANES
    if rep > 1:
        c = jnp.concatenate([c] * rep, axis=1)
        s_lo = jnp.concatenate([s_lo] * rep, axis=1)
        s_hi = jnp.concatenate([s_hi] * rep, axis=1)
    half = ROPE_DIMS // 2
    return x * c + pltpu.roll(x, half, 1) * s_hi + pltpu.roll(x, w - half, 1) * s_lo


def _prep_body(cq_ref, iq_ref, kv_ref, c_ref, slo_ref, shi_ref,
               qc_ref, qi_ref, kc_ref, vc_ref, ki_ref, kcb_ref, vcb_ref, kib_ref):
    c, s_lo, s_hi = c_ref[...], slo_ref[...], shi_ref[...]
    qc_ref[...] = (_rope(cq_ref[...], c, s_lo, s_hi) * (C_HDIM ** -0.5)).astype(BF16)
    qi_ref[...] = _rope(iq_ref[...], c, s_lo, s_hi).astype(BF16)
    kc = _rope(kv_ref[:, 0:LANES], c, s_lo, s_hi)
    vc = kv_ref[:, LANES:2 * LANES]
    ki = _rope(kv_ref[:, 2 * LANES:3 * LANES], c, s_lo, s_hi)[:, :IDX_DIM]
    kc_ref[...] = kc
    vc_ref[...] = vc
    ki_ref[...] = ki
    kcb_ref[...] = kc.astype(BF16)
    vcb_ref[...] = vc.astype(BF16)
    kib_ref[...] = ki.astype(BF16)


def _prep(z, cos_t, sin_lo, sin_hi):
    n = z.shape[0]
    tm = min(TOKEN_TILE, n)
    row = lambda i: (i, 0)
    return pl.pallas_call(
        _prep_body,
        grid=(n // tm,),
        in_specs=[
            pl.BlockSpec((tm, MIX_W), lambda i: (i, Z_CQ // 4)),
            pl.BlockSpec((tm, MIX_W), lambda i: (i, Z_IQ // 4)),
            pl.BlockSpec((tm, MIX_W), lambda i: (i, Z_CKVI // 4)),
            pl.BlockSpec((tm, LANES), row),
            pl.BlockSpec((tm, LANES), row),
            pl.BlockSpec((tm, LANES), row),
        ],
        out_specs=[
            pl.BlockSpec((tm, C_WIDTH), row),
            pl.BlockSpec((tm, IDX_HEADS * IDX_DIM), row),
            pl.BlockSpec((tm, LANES), row),
            pl.BlockSpec((tm, LANES), row),
            pl.BlockSpec((tm, IDX_DIM), row),
            pl.BlockSpec((tm, LANES), row),
            pl.BlockSpec((tm, LANES), row),
            pl.BlockSpec((tm, IDX_DIM), row),
        ],
        out_shape=[
            jax.ShapeDtypeStruct((n, C_WIDTH), BF16),
            jax.ShapeDtypeStruct((n, IDX_HEADS * IDX_DIM), BF16),
            jax.ShapeDtypeStruct((n, LANES), F32),
            jax.ShapeDtypeStruct((n, LANES), F32),
            jax.ShapeDtypeStruct((n, IDX_DIM), F32),
            jax.ShapeDtypeStruct((n, LANES), BF16),
            jax.ShapeDtypeStruct((n, LANES), BF16),
            jax.ShapeDtypeStruct((n, IDX_DIM), BF16),
        ],
        compiler_params=_params(("parallel",), 32),
        name="rotary_kv_staging",
    )(z, z, z, cos_t, sin_lo, sin_hi)


def _gla_head(q, k, v, logf, st_ref, h, tc):
    row = lax.broadcasted_iota(jnp.int32, (tc, tc), 0)
    col = lax.broadcasted_iota(jnp.int32, (tc, tc), 1)
    tril = (col <= row).astype(BF16)
    cum = _dot_exact_lhs(tril, logf)

    nb = tc // GLA_BLOCK
    b3 = cum.reshape(nb, GLA_BLOCK, HEAD_W)
    q3 = q.reshape(nb, GLA_BLOCK, HEAD_W)
    k3 = k.reshape(nb, GLA_BLOCK, HEAD_W)
    v3 = v.reshape(nb, GLA_BLOCK, HEAD_W)
    tloc = lax.broadcasted_iota(jnp.int32, (nb, GLA_BLOCK, 1), 1)
    o3 = jnp.zeros((nb, GLA_BLOCK, HEAD_W), F32)
    for j in range(GLA_BLOCK):
        causal = tloc >= j
        decay = jnp.exp(jnp.where(causal, b3 - b3[:, j:j + 1, :], 0.0))
        w = jnp.sum(q3 * k3[:, j:j + 1, :] * decay, axis=-1, keepdims=True)
        o3 = o3 + jnp.where(causal, w, 0.0) * v3[:, j:j + 1, :]
    o = o3.reshape(tc, HEAD_W)

    vb = v.astype(BF16)
    attn = jnp.zeros((tc, tc), F32)
    half = tc // 2
    while half >= GLA_BLOCK:
        blk = 2 * half
        nblk = tc // blk
        bl = cum.reshape(nblk, blk, HEAD_W)
        x = bl - bl[:, half - 1:half, :]
        second = lax.broadcasted_iota(jnp.int32, (nblk, blk, 1), 1) >= half
        e = jnp.exp(jnp.where(second, x, -x))
        qt = jnp.where(second, q.reshape(nblk, blk, HEAD_W) * e, 0.0).reshape(tc, HEAD_W).astype(BF16)
        kt = jnp.where(second, 0.0, k.reshape(nblk, blk, HEAD_W) * e).reshape(tc, HEAD_W).astype(BF16)
        shift = blk.bit_length() - 1
        same = (row >> shift) == (col >> shift)
        attn = attn + jnp.where(same, _dot_nt(qt, kt), 0.0)
        half //= 2
    if tc > GLA_BLOCK:
        o = o + _dot(attn.astype(BF16), vb)

    st = st_ref[h]
    o = o + _dot_nt((q * jnp.exp(cum)).astype(BF16), st.astype(BF16))
    last = cum[tc - 1:tc, :]
    kd = (k * jnp.exp(last - cum)).astype(BF16)
    st_ref[h] = st * jnp.exp(last) + _dot_tn(vb, kd)
    return o


def _gla_finish(o, nw, gate):
    return _rmsnorm(o, nw) * (gate * jax.nn.sigmoid(gate))


def _gla_state_io(c, s0_ref, st_ref, heads):
    @pl.when(c == 0)
    def _():
        for h in range(heads):
            st_ref[h] = s0_ref[0, h].T


def _gla_state_out(c, sout_ref, st_ref, heads):
    @pl.when(c == pl.num_programs(1) - 1)
    def _():
        for h in range(heads):
            sout_ref[0, h] = st_ref[h].T


def _hgrn_body(q_ref, f_ref, v_ref, g_ref, lb_ref, nw_ref, s0_ref, y_ref, sout_ref, st_ref, *, tc):
    c = pl.program_id(1)
    _gla_state_io(c, s0_ref, st_ref, A_HEADS)
    for h in range(A_HEADS):
        hs = slice(h * HEAD_W, (h + 1) * HEAD_W)
        zf = f_ref[:, hs]
        lb = lb_ref[:, hs]
        f = lb + (1.0 - lb) * jax.nn.sigmoid(zf)
        logf = jnp.log(jnp.maximum(f, F_MIN))
        k = (1.0 - lb) * jax.nn.sigmoid(-zf)
        zq = q_ref[:, hs]
        q = zq * jax.nn.sigmoid(zq) * (A_KDIM ** -0.5)
        o = _gla_head(q, k, v_ref[:, hs], logf, st_ref, h, tc)
        y_ref[:, hs] = _gla_finish(o, nw_ref[...], g_ref[:, hs])
    _gla_state_out(c, sout_ref, st_ref, A_HEADS)


def _gla_body(q_ref, k_ref, v_ref, g_ref, r_ref, wup_ref, bup_ref, nw_ref, s0_ref,
              y_ref, sout_ref, st_ref, *, tc):
    c = pl.program_id(1)
    _gla_state_io(c, s0_ref, st_ref, B_HEADS)
    r = _dot_hi(r_ref[...], wup_ref[...]) + bup_ref[...]
    logf_all = (jnp.minimum(r, 0.0) - jnp.log1p(jnp.exp(-jnp.abs(r)))) / B_TAU
    for h in range(B_HEADS):
        hs = slice(h * HEAD_W, (h + 1) * HEAD_W)
        q = q_ref[:, hs] * (B_KDIM ** -0.5)
        o = _gla_head(q, k_ref[:, hs], v_ref[:, hs], logf_all[:, hs], st_ref, h, tc)
        y_ref[:, hs] = _gla_finish(o, nw_ref[...], g_ref[:, hs])
    _gla_state_out(c, sout_ref, st_ref, B_HEADS)


def _recurrent_mixer(mode, z, row0, nseq, t, s0, nw, extra):
    tc = min(128, t)
    nc = t // tc
    rb0 = row0 // tc
    zcol = (Z_A if mode == "hgrn" else Z_B) // 4

    def zspec(k):
        return pl.BlockSpec((tc, MIX_W), lambda b, c: (rb0 + b * nc + c, zcol + k))

    const = lambda b, c: (0, 0)
    state_spec = pl.BlockSpec((1, 4, HEAD_W, HEAD_W), lambda b, c: (b, 0, 0, 0))
    if mode == "hgrn":
        body = functools.partial(_hgrn_body, tc=tc)
        in_specs = [zspec(0), zspec(1), zspec(2), zspec(3),
                    pl.BlockSpec((1, MIX_W), const), pl.BlockSpec((1, HEAD_W), const), state_spec]
        args = (z, z, z, z, extra[0], nw, s0)
    else:
        body = functools.partial(_gla_body, tc=tc)
        in_specs = [zspec(0), zspec(1), zspec(2), zspec(3),
                    pl.BlockSpec((tc, LANES), lambda b, c: (rb0 + b * nc + c, Z_SMALL)),
                    pl.BlockSpec((LANES, MIX_W), const), pl.BlockSpec((1, MIX_W), const),
                    pl.BlockSpec((1, HEAD_W), const), state_spec]
        args = (z, z, z, z, z, extra[0], extra[1], nw, s0)
    return pl.pallas_call(
        body,
        grid=(nseq, nc),
        in_specs=in_specs,
        out_specs=[pl.BlockSpec((tc, MIX_W), lambda b, c: (b * nc + c, 0)), state_spec],
        out_shape=[jax.ShapeDtypeStruct((nseq * t, MIX_W), F32),
                   jax.ShapeDtypeStruct((nseq, 4, HEAD_W, HEAD_W), F32)],
        scratch_shapes=[pltpu.VMEM((4, HEAD_W, HEAD_W), F32)],
        compiler_params=_params(("parallel", "arbitrary"), 32),
        name=mode + "_mixer",
    )(*args)


def _dsa_body(qc_ref, qi_ref, sm_ref, kc_ref, vc_ref, ki_ref, o_ref,
              sk_ref, qis_ref, wb_ref, qs_ref, m_ref, l_ref, acc_ref, *, tq, past, nkeys, topk):
    i = pl.program_id(1)
    qpos0 = past + i * tq
    last_chunk = (qpos0 + tq - 1) // CHUNK
    n_adm = jnp.minimum((last_chunk + 1) * CHUNK, nkeys)
    nkb = (n_adm + KEY_BLOCK - 1) // KEY_BLOCK
    grp = C_HEADS // C_KV_HEADS

    for j in range(IDX_HEADS):
        qis_ref[j * tq:(j + 1) * tq, :] = qi_ref[:, j * IDX_DIM:(j + 1) * IDX_DIM]
        wb_ref[j] = jnp.broadcast_to(sm_ref[:, SMALL_IW + j:SMALL_IW + j + 1], (tq, KEY_BLOCK))
    for hq in range(C_HEADS):
        g, r = divmod(hq, grp)
        qs_ref[g, r * tq:(r + 1) * tq, :] = qc_ref[:, hq * C_HDIM:(hq + 1) * C_HDIM]

    qchunk = (qpos0 + lax.broadcasted_iota(jnp.int32, (tq, 1), 0)) >> 6
    lane = lax.broadcasted_iota(jnp.int32, (1, KEY_BLOCK), 1)

    def admissible(kb):
        kpos = kb * KEY_BLOCK + lane
        return ((kpos >> 6) <= qchunk) & (kpos < nkeys), kpos

    def score_step(kb, carry):
        kib = ki_ref[0, pl.ds(pl.multiple_of(kb * KEY_BLOCK, KEY_BLOCK), KEY_BLOCK), :]
        d = _dot_nt(qis_ref[...], kib)
        sc = jnp.maximum(d[0:tq], 0.0) * wb_ref[0]
        for j in range(1, IDX_HEADS):
            sc = sc + jnp.maximum(d[j * tq:(j + 1) * tq], 0.0) * wb_ref[j]
        adm, _ = admissible(kb)
        sc = jnp.where(adm, sc * IDX_SCALE, NEG) + 0.0
        bits = pltpu.bitcast(sc, jnp.int32)
        sk_ref[kb] = jnp.where(bits < 0, bits ^ 0x7FFFFFFF, bits)
        return carry

    lax.fori_loop(0, nkb, score_step, 0)

    def count(pred):
        def step(kb, acc):
            _, kpos = admissible(kb)
            return acc + jnp.where(pred(sk_ref[kb], kpos), 1.0, 0.0)
        acc = lax.fori_loop(0, nkb, step, jnp.zeros((tq, KEY_BLOCK), F32))
        return jnp.sum(acc, axis=1, keepdims=True)

    kf = float(topk)
    c0 = count(lambda s, p: s >= 0)
    thr = jnp.where(c0 >= kf, 0, INT_MIN).astype(jnp.int32)

    def bit_step(it, thr):
        cand = thr | jnp.left_shift(jnp.int32(1), 30 - it)
        c = count(lambda s, p: s >= cand)
        return jnp.where(c >= kf, cand, thr)

    thr = lax.fori_loop(0, 31, bit_step, thr)

    c_ge = count(lambda s, p: s >= thr)
    c_gt = count(lambda s, p: s > thr)
    need = kf - c_gt
    nbits = max(1, (nkeys - 1).bit_length())

    def tie_cut():
        def cut_step(it, cut):
            cand = cut | jnp.left_shift(jnp.int32(1), nbits - 1 - it)
            c = count(lambda s, p: (s == thr) & (p < cand))
            return jnp.where(c < need, cand, cut)
        return lax.fori_loop(0, nbits, cut_step, jnp.zeros((tq, 1), jnp.int32))

    cut = lax.cond(jnp.max(c_ge) > kf, tie_cut, lambda: jnp.full((tq, 1), 2 ** 30, jnp.int32))

    m_ref[...] = jnp.full(m_ref.shape, NEG, F32)
    l_ref[...] = jnp.zeros(l_ref.shape, F32)
    acc_ref[...] = jnp.zeros(acc_ref.shape, F32)

    def attn_step(kb, carry):
        adm, kpos = admissible(kb)
        s_key = sk_ref[kb]
        sel = ((s_key > thr) | ((s_key == thr) & (kpos <= cut))) & adm
        k0 = pl.multiple_of(kb * KEY_BLOCK, KEY_BLOCK)
        kblk = kc_ref[0, pl.ds(k0, KEY_BLOCK), :]
        vblk = vc_ref[0, pl.ds(k0, KEY_BLOCK), :]
        for g in range(C_KV_HEADS):
            gs = slice(g * C_HDIM, (g + 1) * C_HDIM)
            s = _dot_nt(qs_ref[g], kblk[:, gs]).reshape(grp, tq, KEY_BLOCK)
            s = jnp.where(sel[None], s, NEG)
            m_old = m_ref[g]
            m_new = jnp.maximum(m_old, jnp.max(s, axis=-1, keepdims=True))
            p = jnp.where(sel[None], jnp.exp(s - m_new), 0.0)
            alpha = jnp.exp(m_old - m_new)
            l_ref[g] = alpha * l_ref[g] + jnp.sum(p, axis=-1, keepdims=True)
            pv = _dot(p.reshape(grp * tq, KEY_BLOCK).astype(BF16), vblk[:, gs])
            acc_ref[g] = alpha * acc_ref[g] + pv.reshape(grp, tq, C_HDIM)
            m_ref[g] = m_new
        return carry

    lax.fori_loop(0, nkb, attn_step, 0)

    for hq in range(C_HEADS):
        g, r = divmod(hq, grp)
        o_ref[:, hq * C_HDIM:(hq + 1) * C_HDIM] = acc_ref[g, r] / l_ref[g, r]


def _dsa(qc, qi, z, row0, keys, vals, kidx, nseq, t, past, nkeys):
    tq = min(128, t)
    nq = t // tq
    rb0 = row0 // tq
    lp = keys.shape[1]
    topk = min(TOPK_MAX, nkeys // 4)
    grp = C_HEADS // C_KV_HEADS
    qrow = lambda b, i: (rb0 + b * nq + i, 0)
    seq = lambda b, i: (b, 0, 0)
    return pl.pallas_call(
        functools.partial(_dsa_body, tq=tq, past=past, nkeys=nkeys, topk=topk),
        grid=(nseq, nq),
        in_specs=[
            pl.BlockSpec((tq, C_WIDTH), qrow),
            pl.BlockSpec((tq, IDX_HEADS * IDX_DIM), qrow),
            pl.BlockSpec((tq, LANES), lambda b, i: (rb0 + b * nq + i, Z_SMALL)),
            pl.BlockSpec((1, lp, LANES), seq),
            pl.BlockSpec((1, lp, LANES), seq),
            pl.BlockSpec((1, lp, IDX_DIM), seq),
        ],
        out_specs=pl.BlockSpec((tq, C_WIDTH), lambda b, i: (b * nq + i, 0)),
        out_shape=jax.ShapeDtypeStruct((nseq * t, C_WIDTH), F32),
        scratch_shapes=[
            pltpu.VMEM((lp // KEY_BLOCK, tq, KEY_BLOCK), jnp.int32),
            pltpu.VMEM((IDX_HEADS * tq, IDX_DIM), BF16),
            pltpu.VMEM((IDX_HEADS, tq, KEY_BLOCK), F32),
            pltpu.VMEM((C_KV_HEADS, grp * tq, C_HDIM), BF16),
            pltpu.VMEM((C_KV_HEADS, grp, tq, 1), F32),
            pltpu.VMEM((C_KV_HEADS, grp, tq, 1), F32),
            pltpu.VMEM((C_KV_HEADS, grp, tq, C_HDIM), F32),
        ],
        compiler_params=_params(("parallel", "arbitrary"), 48),
        name="dsa_mixer",
    )(qc, qi, z, keys, vals, kidx)


def _merge_body(h_ref, ya_ref, yb_ref, oc_ref, ga_ref, gb_ref, gc_ref,
                wa_ref, wb_ref, wc_ref, wo_ref, o_ref):
    merged = (jax.nn.sigmoid(ga_ref[...]) * _dot(ya_ref[...].astype(BF16), wa_ref[...])
              + jax.nn.sigmoid(gb_ref[...]) * _dot(yb_ref[...].astype(BF16), wb_ref[...])
              + jax.nn.sigmoid(gc_ref[...]) * _dot(oc_ref[...].astype(BF16), wc_ref[...]))
    o_ref[...] = h_ref[...] + _dot(merged.astype(BF16), wo_ref[...])


def _merge(h, ya, yb, oc, z, wa, wb, wc, wo):
    n, d = h.shape
    tm = min(TOKEN_TILE, n)
    row = lambda i: (i, 0)
    const = lambda i: (0, 0)
    return pl.pallas_call(
        _merge_body,
        grid=(n // tm,),
        in_specs=[
            pl.BlockSpec((tm, d), row),
            pl.BlockSpec((tm, MIX_W), row),
            pl.BlockSpec((tm, MIX_W), row),
            pl.BlockSpec((tm, MIX_W), row),
            pl.BlockSpec((tm, d), lambda i: (i, 0)),
            pl.BlockSpec((tm, d), lambda i: (i, 1)),
            pl.BlockSpec((tm, d), lambda i: (i, 2)),
            pl.BlockSpec((MIX_W, d), const),
            pl.BlockSpec((MIX_W, d), const),
            pl.BlockSpec((MIX_W, d), const),
            pl.BlockSpec((d, d), const),
        ],
        out_specs=pl.BlockSpec((tm, d), row),
        out_shape=jax.ShapeDtypeStruct((n, d), F32),
        compiler_params=_params(("parallel",), 48),
        name="gated_merge",
    )(h, ya, yb, oc, z, z, z, wa, wb, wc, wo)


def _pad_heads(w, heads, dim):
    lead = w.shape[:-1]
    w = w.reshape(*lead, heads, dim)
    w = jnp.pad(w, [(0, 0)] * len(lead) + [(0, 0), (0, HEAD_W - dim)])
    return w.reshape(*lead, heads * HEAD_W)


def _layout_w_in(w):
    widths = (512, 512, 512, 512, 256, 256, 512, 16, 512, 512, 128, 128, 512, 64, 8, 1024, 1024, 1024)
    parts, o = [], 0
    for wd in widths:
        parts.append(w[:, o:o + wd])
        o += wd
    (a_q, a_f, a_i, a_g, b_q, b_k, b_v, b_r, b_g, c_q, c_k, c_v, i_q, i_k, i_w, g_a, g_b, g_c) = parts
    d = w.shape[0]
    small = jnp.concatenate([b_r, i_w, jnp.zeros((d, LANES - 24), w.dtype)], axis=1)
    i_k = jnp.pad(i_k, ((0, 0), (0, LANES - IDX_DIM)))
    cols = [g_a, g_b, g_c, a_q, a_f, a_i, a_g,
            _pad_heads(b_q, B_HEADS, B_KDIM), _pad_heads(b_k, B_HEADS, B_KDIM), b_v, b_g,
            c_q, i_q, c_k, c_v, i_k, small]
    out = jnp.concatenate(cols, axis=1)
    assert out.shape[1] == Z_WIDTH
    return out.astype(BF16)


def _rope_tables(pos):
    half = ROPE_DIMS // 2
    inv = ROPE_THETA ** (-jnp.arange(half, dtype=F32) * (2.0 / ROPE_DIMS))
    ang = pos.astype(F32)[:, None] * inv[None, :]
    cos, sin = jnp.cos(ang), jnp.sin(ang)
    n = pos.shape[0]
    ones = jnp.ones((n, C_HDIM - ROPE_DIMS), F32)
    zeros = jnp.zeros((n, C_HDIM - ROPE_DIMS), F32)
    zh = jnp.zeros((n, half), F32)
    c = jnp.concatenate([cos, cos, ones], axis=1)
    s_lo = jnp.concatenate([-sin, zh, zeros], axis=1)
    s_hi = jnp.concatenate([zh, sin, zeros], axis=1)
    two = lambda a: jnp.concatenate([a, a], axis=1)
    return two(c), two(s_lo), two(s_hi)


def kernel(x_prompt, x_sample, state_hgrn, state_gla, cache_k, cache_v, cache_kidx, hgrn_lb, w_in, w_gla_up, b_gla, norm_hgrn, norm_gla, w_br_a, w_br_b, w_br_c, w_out, norm_ffn1, norm_mix, norm_ffn2, ffn1_w1, ffn1_w3, ffn1_w2, ffn2_w1, ffn2_w3, ffn2_w2, norm_final):
    bp, tp, d = x_prompt.shape
    bs, ts, _ = x_sample.shape
    past = cache_k.shape[2]
    n_p, n_s = bp * tp, bs * ts

    lb_sm = jax.nn.softmax(hgrn_lb.astype(F32), axis=0)
    lb_all = jnp.concatenate([jnp.zeros_like(lb_sm[:1]), jnp.cumsum(lb_sm[1:], axis=0)], axis=0)

    pos = jnp.concatenate([jnp.tile(jnp.arange(tp, dtype=jnp.int32), bp),
                           jnp.tile(past + jnp.arange(ts, dtype=jnp.int32), bs)])
    cos_t, sin_lo, sin_hi = _rope_tables(pos)

    h = jnp.concatenate([x_prompt.reshape(n_p, d), x_sample.reshape(n_s, d)], axis=0)
    row2 = lambda a: a.reshape(1, -1)
    zero_state = jnp.zeros((bp, 4, HEAD_W, HEAD_W), F32)
    lp_s = -(-(past + ts) // KEY_BLOCK) * KEY_BLOCK
    key_pad = lp_s - past - ts

    outs = {k: [] for k in ("pa", "pb", "pk", "pv", "pki", "sa", "sb", "sk", "sv", "ski")}
    for l in range(DEPTH):
        bf = lambda a: a[l].astype(BF16)
        h = _ffn(h, row2(norm_ffn1[l]), bf(ffn1_w1), bf(ffn1_w3), bf(ffn1_w2), row2(norm_final), False)
        z = _inproj(h, row2(norm_mix[l]), _layout_w_in(w_in[l]))
        qc, qi, kc, vc, ki, kcb, vcb, kib = _prep(z, cos_t, sin_lo, sin_hi)

        lb = row2(lb_all[l])
        nwa, nwb = row2(norm_hgrn[l]), row2(norm_gla[l])
        wup = jnp.pad(_pad_heads(w_gla_up[l], B_HEADS, B_KDIM), ((0, LANES - B_GATE_RANK), (0, 0)))
        bup = row2(_pad_heads(b_gla[l], B_HEADS, B_KDIM))
        sb0 = jnp.pad(state_gla[l], ((0, 0), (0, 0), (0, HEAD_W - B_KDIM), (0, 0)))

        ya_p, sa_p = _recurrent_mixer("hgrn", z, 0, bp, tp, zero_state, nwa, (lb,))
        ya_s, sa_s = _recurrent_mixer("hgrn", z, n_p, bs, ts, state_hgrn[l], nwa, (lb,))
        yb_p, sb_p = _recurrent_mixer("gla", z, 0, bp, tp, zero_state, nwb, (wup, bup))
        yb_s, sb_s = _recurrent_mixer("gla", z, n_p, bs, ts, sb0, nwb, (wup, bup))

        seqs = lambda a, n, t: a.reshape(n, t, a.shape[-1])
        oc_p = _dsa(qc, qi, z, 0, seqs(kcb[:n_p], bp, tp), seqs(vcb[:n_p], bp, tp),
                    seqs(kib[:n_p], bp, tp), bp, tp, 0, tp)

        def with_cache(cache, new):
            full = jnp.concatenate([cache.reshape(bs, past, -1).astype(BF16), seqs(new[n_p:], bs, ts)], axis=1)
            return jnp.pad(full, ((0, 0), (0, key_pad), (0, 0)))

        oc_s = _dsa(qc, qi, z, n_p, with_cache(cache_k[l], kcb), with_cache(cache_v[l], vcb),
                    with_cache(cache_kidx[l], kib), bs, ts, past, past + ts)

        ya = jnp.concatenate([ya_p, ya_s], axis=0)
        yb = jnp.concatenate([yb_p, yb_s], axis=0)
        oc = jnp.concatenate([oc_p, oc_s], axis=0)
        h = _merge(h, ya, yb, oc, z, bf(w_br_a), bf(w_br_b), bf(w_br_c), bf(w_out))
        h = _ffn(h, row2(norm_ffn2[l]), bf(ffn2_w1), bf(ffn2_w3), bf(ffn2_w2), row2(norm_final),
                 l == DEPTH - 1)

        outs["pa"].append(sa_p)
        outs["sa"].append(sa_s)
        outs["pb"].append(sb_p[:, :, :B_KDIM, :])
        outs["sb"].append(sb_s[:, :, :B_KDIM, :])
        outs["pk"].append(kc[:n_p].reshape(bp, tp, C_KV_HEADS, C_HDIM))
        outs["pv"].append(vc[:n_p].reshape(bp, tp, C_KV_HEADS, C_HDIM))
        outs["pki"].append(ki[:n_p].reshape(bp, tp, IDX_DIM))
        outs["sk"].append(kc[n_p:].reshape(bs, ts, C_KV_HEADS, C_HDIM))
        outs["sv"].append(vc[n_p:].reshape(bs, ts, C_KV_HEADS, C_HDIM))
        outs["ski"].append(ki[n_p:].reshape(bs, ts, IDX_DIM))

    st = {k: jnp.stack(v) for k, v in outs.items()}
    return (h[:n_p].reshape(bp, tp, d), h[n_p:].reshape(bs, ts, d),
            st["pa"], st["pb"], st["pk"], st["pv"], st["pki"],
            st["sa"], st["sb"], st["sk"], st["sv"], st["ski"])
```

```python
import functools

import jax
import jax.numpy as jnp
from jax import lax
from jax.experimental import pallas as pl
from jax.experimental.pallas import tpu as pltpu

F32 = jnp.float32
BF16 = jnp.bfloat16

D_MODEL = 1024
DEPTH = 2
CHUNK = 64
EPS = 1e-6
NEG = -1e30
F_MIN = 1e-30
A_HEADS = 4
A_KDIM = 128
A_VDIM = 128
B_HEADS = 4
B_KDIM = 64
B_VDIM = 128
B_GATE_RANK = 16
B_TAU = 16.0
C_HEADS = 8
C_KV_HEADS = 2
C_HDIM = 64
C_WIDTH = C_HEADS * C_HDIM
IDX_HEADS = 8
IDX_DIM = 64
IDX_SCALE = (IDX_HEADS * IDX_DIM) ** -0.5
TOPK_MAX = 256
ROPE_THETA = 500000.0
ROPE_DIMS = C_HDIM // 4
GLA_BLOCK = 16
D_FF = 2816

LANES = 128
HEAD_W = 128
MIX_W = 4 * HEAD_W
TOKEN_TILE = 512
FF_TILE = D_FF // 2
KEY_BLOCK = 128
VT_ROWS = 80
INT_MIN = -(2 ** 31)

Z_GATES = 0
Z_A = 24
Z_B = 40
Z_CQ = 56
Z_IQ = 60
Z_CKVI = 64
Z_SMALL = 67
Z_WIDTH = 68 * LANES
SMALL_IW = B_GATE_RANK


def _params(sem, vmem_mb):
    return pltpu.CompilerParams(dimension_semantics=sem, vmem_limit_bytes=vmem_mb << 20)


def _dot(a, b):
    return jnp.dot(a, b, preferred_element_type=F32)


def _dot_nt(a, b):
    return lax.dot_general(a, b, (((1,), (1,)), ((), ())), preferred_element_type=F32)


def _dot_tn(a, b):
    return lax.dot_general(a, b, (((0,), (0,)), ((), ())), preferred_element_type=F32)


def _split3(x):
    h1 = x.astype(BF16)
    r1 = x - h1.astype(F32)
    h2 = r1.astype(BF16)
    h3 = (r1 - h2.astype(F32)).astype(BF16)
    return h1, h2, h3


def _dot_exact_lhs(m, x):
    h1, h2, h3 = _split3(x)
    return _dot(m, h1) + _dot(m, h2) + _dot(m, h3)


def _dot_hi(a, b):
    a1 = a.astype(BF16)
    a2 = (a - a1.astype(F32)).astype(BF16)
    b1 = b.astype(BF16)
    b2 = (b - b1.astype(F32)).astype(BF16)
    return _dot(a1, b1) + _dot(a1, b2) + _dot(a2, b1)


def _rmsnorm(x, g):
    return x * lax.rsqrt(jnp.mean(x * x, axis=-1, keepdims=True) + EPS) * g


def _ffn_body(x_ref, g_ref, w1_ref, w3_ref, w2_ref, gf_ref, o_ref, u_ref, acc_ref, *, final_norm):
    j = pl.program_id(1)

    @pl.when(j == 0)
    def _():
        u_ref[...] = _rmsnorm(x_ref[...], g_ref[...]).astype(BF16)
        acc_ref[...] = jnp.zeros_like(acc_ref)

    u = u_ref[...]
    a = _dot(u, w1_ref[...])
    b = _dot(u, w3_ref[...])
    hid = (a * jax.nn.sigmoid(a) * b).astype(BF16)
    acc_ref[...] += _dot(hid, w2_ref[...])

    @pl.when(j == pl.num_programs(1) - 1)
    def _():
        out = x_ref[...] + 0.5 * acc_ref[...]
        if final_norm:
            out = _rmsnorm(out, gf_ref[...])
        o_ref[...] = out


def _ffn(x, g, w1, w3, w2, gf, final_norm):
    n, d = x.shape
    dff = w1.shape[1]
    tm = min(TOKEN_TILE, n)
    tf = FF_TILE if dff == D_FF else dff
    return pl.pallas_call(
        functools.partial(_ffn_body, final_norm=final_norm),
        grid=(n // tm, dff // tf),
        in_specs=[
            pl.BlockSpec((tm, d), lambda i, j: (i, 0)),
            pl.BlockSpec((1, d), lambda i, j: (0, 0)),
            pl.BlockSpec((d, tf), lambda i, j: (0, j)),
            pl.BlockSpec((d, tf), lambda i, j: (0, j)),
            pl.BlockSpec((tf, d), lambda i, j: (j, 0)),
            pl.BlockSpec((1, d), lambda i, j: (0, 0)),
        ],
        out_specs=pl.BlockSpec((tm, d), lambda i, j: (i, 0)),
        out_shape=jax.ShapeDtypeStruct((n, d), F32),
        scratch_shapes=[pltpu.VMEM((tm, d), BF16), pltpu.VMEM((tm, d), F32)],
        compiler_params=_params(("parallel", "arbitrary"), 48),
        name="ffn_half_step",
    )(x, g, w1, w3, w2, gf)


def _inproj_body(x_ref, g_ref, w_ref, z_ref, u_ref):
    @pl.when(pl.program_id(1) == 0)
    def _():
        u_ref[...] = _rmsnorm(x_ref[...], g_ref[...]).astype(BF16)

    z_ref[...] = _dot(u_ref[...], w_ref[...])


def _inproj(h, g, w):
    n, d = h.shape
    zw = w.shape[1]
    tm = min(TOKEN_TILE, n)
    tn = 512
    return pl.pallas_call(
        _inproj_body,
        grid=(n // tm, zw // tn),
        in_specs=[
            pl.BlockSpec((tm, d), lambda i, j: (i, 0)),
            pl.BlockSpec((1, d), lambda i, j: (0, 0)),
            pl.BlockSpec((d, tn), lambda i, j: (0, j)),
        ],
        out_specs=pl.BlockSpec((tm, tn), lambda i, j: (i, j)),
        out_shape=jax.ShapeDtypeStruct((n, zw), F32),
        scratch_shapes=[pltpu.VMEM((tm, d), BF16)],
        compiler_params=_params(("parallel", "arbitrary"), 32),
        name="in_projection",
    )(h, g, w)


def _rope(x, c, s_lo, s_hi):
    w = x.shape[1]
    rep = w // LANES
    if rep > 1:
        c = jnp.concatenate([c] * rep, axis=1)
        s_lo = jnp.concatenate([s_lo] * rep, axis=1)
        s_hi = jnp.concatenate([s_hi] * rep, axis=1)
    half = ROPE_DIMS // 2
    return x * c + pltpu.roll(x, half, 1) * s_hi + pltpu.roll(x, w - half, 1) * s_lo


def _prep_body(cq_ref, iq_ref, kv_ref, c_ref, slo_ref, shi_ref,
               qc_ref, qi_ref, kc_ref, vc_ref, ki_ref, kcb_ref, vcb_ref, kib_ref, vt_ref):
    c, s_lo, s_hi = c_ref[...], slo_ref[...], shi_ref[...]
    qc_ref[...] = (_rope(cq_ref[...], c, s_lo, s_hi) * (C_HDIM ** -0.5)).astype(BF16)
    qi_ref[...] = _rope(iq_ref[...], c, s_lo, s_hi).astype(BF16)
    kc = _rope(kv_ref[:, 0:LANES], c, s_lo, s_hi)
    vc = kv_ref[:, LANES:2 * LANES]
    ki = _rope(kv_ref[:, 2 * LANES:3 * LANES], c, s_lo, s_hi)[:, :IDX_DIM]
    kc_ref[...] = kc
    vc_ref[...] = vc
    ki_ref[...] = ki
    kcb_ref[...] = kc.astype(BF16)
    vcb_ref[...] = vc.astype(BF16)
    kib_ref[...] = ki.astype(BF16)
    ones = jnp.ones((VT_ROWS - C_HDIM, KEY_BLOCK), BF16)
    for kk in range(vt_ref.shape[0]):
        vt = vc[kk * KEY_BLOCK:(kk + 1) * KEY_BLOCK, :].T.astype(BF16)
        vt_ref[kk] = jnp.concatenate([vt[:C_HDIM], ones, vt[C_HDIM:], ones], axis=0)


def _prep(z, cos_t, sin_lo, sin_hi):
    n = z.shape[0]
    tm = min(TOKEN_TILE, n)
    row = lambda i: (i, 0)
    return pl.pallas_call(
        _prep_body,
        grid=(n // tm,),
        in_specs=[
            pl.BlockSpec((tm, MIX_W), lambda i: (i, Z_CQ // 4)),
            pl.BlockSpec((tm, MIX_W), lambda i: (i, Z_IQ // 4)),
            pl.BlockSpec((tm, MIX_W), lambda i: (i, Z_CKVI // 4)),
            pl.BlockSpec((tm, LANES), row),
            pl.BlockSpec((tm, LANES), row),
            pl.BlockSpec((tm, LANES), row),
        ],
        out_specs=[
            pl.BlockSpec((tm, C_WIDTH), row),
            pl.BlockSpec((tm, IDX_HEADS * IDX_DIM), row),
            pl.BlockSpec((tm, LANES), row),
            pl.BlockSpec((tm, LANES), row),
            pl.BlockSpec((tm, IDX_DIM), row),
            pl.BlockSpec((tm, LANES), row),
            pl.BlockSpec((tm, LANES), row),
            pl.BlockSpec((tm, IDX_DIM), row),
            pl.BlockSpec((tm // KEY_BLOCK, C_KV_HEADS * VT_ROWS, KEY_BLOCK), lambda i: (i, 0, 0)),
        ],
        out_shape=[
            jax.ShapeDtypeStruct((n, C_WIDTH), BF16),
            jax.ShapeDtypeStruct((n, IDX_HEADS * IDX_DIM), BF16),
            jax.ShapeDtypeStruct((n, LANES), F32),
            jax.ShapeDtypeStruct((n, LANES), F32),
            jax.ShapeDtypeStruct((n, IDX_DIM), F32),
            jax.ShapeDtypeStruct((n, LANES), BF16),
            jax.ShapeDtypeStruct((n, LANES), BF16),
            jax.ShapeDtypeStruct((n, IDX_DIM), BF16),
            jax.ShapeDtypeStruct((n // KEY_BLOCK, C_KV_HEADS * VT_ROWS, KEY_BLOCK), BF16),
        ],
        compiler_params=_params(("parallel",), 32),
        name="rotary_kv_staging",
    )(z, z, z, cos_t, sin_lo, sin_hi)


def _gla_head(q, k, v, logf, st_ref, h, tc):
    row = lax.broadcasted_iota(jnp.int32, (tc, tc), 0)
    col = lax.broadcasted_iota(jnp.int32, (tc, tc), 1)
    tril = (col <= row).astype(BF16)
    cum = _dot_exact_lhs(tril, logf)

    nb = tc // GLA_BLOCK
    b3 = cum.reshape(nb, GLA_BLOCK, HEAD_W)
    q3 = q.reshape(nb, GLA_BLOCK, HEAD_W)
    k3 = k.reshape(nb, GLA_BLOCK, HEAD_W)
    v3 = v.reshape(nb, GLA_BLOCK, HEAD_W)
    tloc = lax.broadcasted_iota(jnp.int32, (nb, GLA_BLOCK, 1), 1)
    o3 = jnp.zeros((nb, GLA_BLOCK, HEAD_W), F32)
    for j in range(GLA_BLOCK):
        causal = tloc >= j
        decay = jnp.exp(jnp.where(causal, b3 - b3[:, j:j + 1, :], 0.0))
        w = jnp.sum(q3 * k3[:, j:j + 1, :] * decay, axis=-1, keepdims=True)
        o3 = o3 + jnp.where(causal, w, 0.0) * v3[:, j:j + 1, :]
    o = o3.reshape(tc, HEAD_W)

    vb = v.astype(BF16)
    attn = jnp.zeros((tc, tc), F32)
    half = tc // 2
    while half >= GLA_BLOCK:
        blk = 2 * half
        nblk = tc // blk
        bl = cum.reshape(nblk, blk, HEAD_W)
        x = bl - bl[:, half - 1:half, :]
        second = lax.broadcasted_iota(jnp.int32, (nblk, blk, 1), 1) >= half
        e = jnp.exp(jnp.where(second, x, -x))
        qt = jnp.where(second, q.reshape(nblk, blk, HEAD_W) * e, 0.0).reshape(tc, HEAD_W).astype(BF16)
        kt = jnp.where(second, 0.0, k.reshape(nblk, blk, HEAD_W) * e).reshape(tc, HEAD_W).astype(BF16)
        shift = blk.bit_length() - 1
        same = (row >> shift) == (col >> shift)
        attn = attn + jnp.where(same, _dot_nt(qt, kt), 0.0)
        half //= 2
    if tc > GLA_BLOCK:
        o = o + _dot(attn.astype(BF16), vb)

    st = st_ref[h]
    o = o + _dot_nt((q * jnp.exp(cum)).astype(BF16), st.astype(BF16))
    last = cum[tc - 1:tc, :]
    kd = (k * jnp.exp(last - cum)).astype(BF16)
    st_ref[h] = st * jnp.exp(last) + _dot_tn(vb, kd)
    return o


def _gla_finish(o, nw, gate):
    return _rmsnorm(o, nw) * (gate * jax.nn.sigmoid(gate))


def _gla_state_io(c, s0_ref, st_ref, heads):
    @pl.when(c == 0)
    def _():
        for h in range(heads):
            st_ref[h] = s0_ref[0, h].T


def _gla_state_out(c, sout_ref, st_ref, heads):
    @pl.when(c == pl.num_programs(1) - 1)
    def _():
        for h in range(heads):
            sout_ref[0, h] = st_ref[h].T


def _hgrn_body(q_ref, f_ref, v_ref, g_ref, lb_ref, nw_ref, s0_ref, y_ref, sout_ref, st_ref, *, tc):
    c = pl.program_id(1)
    _gla_state_io(c, s0_ref, st_ref, A_HEADS)
    for h in range(A_HEADS):
        hs = slice(h * HEAD_W, (h + 1) * HEAD_W)
        zf = f_ref[:, hs]
        lb = lb_ref[:, hs]
        f = lb + (1.0 - lb) * jax.nn.sigmoid(zf)
        logf = jnp.log(jnp.maximum(f, F_MIN))
        k = (1.0 - lb) * jax.nn.sigmoid(-zf)
        zq = q_ref[:, hs]
        q = zq * jax.nn.sigmoid(zq) * (A_KDIM ** -0.5)
        o = _gla_head(q, k, v_ref[:, hs], logf, st_ref, h, tc)
        y_ref[:, hs] = _gla_finish(o, nw_ref[...], g_ref[:, hs])
    _gla_state_out(c, sout_ref, st_ref, A_HEADS)


def _gla_body(q_ref, k_ref, v_ref, g_ref, r_ref, wup_ref, bup_ref, nw_ref, s0_ref,
              y_ref, sout_ref, st_ref, *, tc):
    c = pl.program_id(1)
    _gla_state_io(c, s0_ref, st_ref, B_HEADS)
    r = _dot_hi(r_ref[...], wup_ref[...]) + bup_ref[...]
    logf_all = (jnp.minimum(r, 0.0) - jnp.log1p(jnp.exp(-jnp.abs(r)))) / B_TAU
    for h in range(B_HEADS):
        hs = slice(h * HEAD_W, (h + 1) * HEAD_W)
        q = q_ref[:, hs] * (B_KDIM ** -0.5)
        o = _gla_head(q, k_ref[:, hs], v_ref[:, hs], logf_all[:, hs], st_ref, h, tc)
        y_ref[:, hs] = _gla_finish(o, nw_ref[...], g_ref[:, hs])
    _gla_state_out(c, sout_ref, st_ref, B_HEADS)


def _recurrent_mixer(mode, z, row0, nseq, t, s0, nw, extra):
    tc = min(128, t)
    nc = t // tc
    rb0 = row0 // tc
    zcol = (Z_A if mode == "hgrn" else Z_B) // 4

    def zspec(k):
        return pl.BlockSpec((tc, MIX_W), lambda b, c: (rb0 + b * nc + c, zcol + k))

    const = lambda b, c: (0, 0)
    state_spec = pl.BlockSpec((1, 4, HEAD_W, HEAD_W), lambda b, c: (b, 0, 0, 0))
    if mode == "hgrn":
        body = functools.partial(_hgrn_body, tc=tc)
        in_specs = [zspec(0), zspec(1), zspec(2), zspec(3),
                    pl.BlockSpec((1, MIX_W), const), pl.BlockSpec((1, HEAD_W), const), state_spec]
        args = (z, z, z, z, extra[0], nw, s0)
    else:
        body = functools.partial(_gla_body, tc=tc)
        in_specs = [zspec(0), zspec(1), zspec(2), zspec(3),
                    pl.BlockSpec((tc, LANES), lambda b, c: (rb0 + b * nc + c, Z_SMALL)),
                    pl.BlockSpec((LANES, MIX_W), const), pl.BlockSpec((1, MIX_W), const),
                    pl.BlockSpec((1, HEAD_W), const), state_spec]
        args = (z, z, z, z, z, extra[0], extra[1], nw, s0)
    return pl.pallas_call(
        body,
        grid=(nseq, nc),
        in_specs=in_specs,
        out_specs=[pl.BlockSpec((tc, MIX_W), lambda b, c: (b * nc + c, 0)), state_spec],
        out_shape=[jax.ShapeDtypeStruct((nseq * t, MIX_W), F32),
                   jax.ShapeDtypeStruct((nseq, 4, HEAD_W, HEAD_W), F32)],
        scratch_shapes=[pltpu.VMEM((4, HEAD_W, HEAD_W), F32)],
        compiler_params=_params(("parallel", "arbitrary"), 32),
        name=mode + "_mixer",
    )(*args)


def _dsa_body(qc_ref, qi_ref, sm_ref, kc_ref, vc_ref, ki_ref, o_ref,
              sk_ref, qis_ref, wb_ref, qs_ref, m_ref, l_ref, acc_ref, *, tq, past, nkeys, topk):
    i = pl.program_id(1)
    qpos0 = past + i * tq
    last_chunk = (qpos0 + tq - 1) // CHUNK
    n_adm = jnp.minimum((last_chunk + 1) * CHUNK, nkeys)
    nkb = (n_adm + KEY_BLOCK - 1) // KEY_BLOCK
    grp = C_HEADS // C_KV_HEADS

    for j in range(IDX_HEADS):
        qis_ref[j * tq:(j + 1) * tq, :] = qi_ref[:, j * IDX_DIM:(j + 1) * IDX_DIM]
        wb_ref[j] = jnp.broadcast_to(sm_ref[:, SMALL_IW + j:SMALL_IW + j + 1], (tq, KEY_BLOCK))
    for hq in range(C_HEADS):
        g, r = divmod(hq, grp)
        qs_ref[g, r * tq:(r + 1) * tq, :] = qc_ref[:, hq * C_HDIM:(hq + 1) * C_HDIM]

    qchunk = (qpos0 + lax.broadcasted_iota(jnp.int32, (tq, 1), 0)) >> 6
    lane = lax.broadcasted_iota(jnp.int32, (1, KEY_BLOCK), 1)

    def admissible(kb):
        kpos = kb * KEY_BLOCK + lane
        return ((kpos >> 6) <= qchunk) & (kpos < nkeys), kpos

    def score_step(kb, carry):
        kib = ki_ref[0, pl.ds(pl.multiple_of(kb * KEY_BLOCK, KEY_BLOCK), KEY_BLOCK), :]
        d = _dot_nt(qis_ref[...], kib)
        sc = jnp.maximum(d[0:tq], 0.0) * wb_ref[0]
        for j in range(1, IDX_HEADS):
            sc = sc + jnp.maximum(d[j * tq:(j + 1) * tq], 0.0) * wb_ref[j]
        adm, _ = admissible(kb)
        sc = jnp.where(adm, sc * IDX_SCALE, NEG) + 0.0
        bits = pltpu.bitcast(sc, jnp.int32)
        sk_ref[kb] = jnp.where(bits < 0, bits ^ 0x7FFFFFFF, bits)
        return carry

    lax.fori_loop(0, nkb, score_step, 0)

    def count(pred):
        def step(kb, acc):
            _, kpos = admissible(kb)
            return acc + jnp.where(pred(sk_ref[kb], kpos), 1.0, 0.0)
        acc = lax.fori_loop(0, nkb, step, jnp.zeros((tq, KEY_BLOCK), F32))
        return jnp.sum(acc, axis=1, keepdims=True)

    kf = float(topk)
    c0 = count(lambda s, p: s >= 0)
    thr = jnp.where(c0 >= kf, 0, INT_MIN).astype(jnp.int32)

    def bit_step(it, thr):
        cand = thr | jnp.left_shift(jnp.int32(1), 30 - it)
        c = count(lambda s, p: s >= cand)
        return jnp.where(c >= kf, cand, thr)

    thr = lax.fori_loop(0, 31, bit_step, thr)

    c_ge = count(lambda s, p: s >= thr)
    c_gt = count(lambda s, p: s > thr)
    need = kf - c_gt
    nbits = max(1, (nkeys - 1).bit_length())

    def tie_cut():
        def cut_step(it, cut):
            cand = cut | jnp.left_shift(jnp.int32(1), nbits - 1 - it)
            c = count(lambda s, p: (s == thr) & (p < cand))
            return jnp.where(c < need, cand, cut)
        return lax.fori_loop(0, nbits, cut_step, jnp.zeros((tq, 1), jnp.int32))

    cut = lax.cond(jnp.max(c_ge) > kf, tie_cut, lambda: jnp.full((tq, 1), 2 ** 30, jnp.int32))

    m_ref[...] = jnp.full(m_ref.shape, NEG, F32)
    l_ref[...] = jnp.zeros(l_ref.shape, F32)
    acc_ref[...] = jnp.zeros(acc_ref.shape, F32)

    def attn_step(kb, carry):
        adm, kpos = admissible(kb)
        s_key = sk_ref[kb]
        sel = ((s_key > thr) | ((s_key == thr) & (kpos <= cut))) & adm
        k0 = pl.multiple_of(kb * KEY_BLOCK, KEY_BLOCK)
        kblk = kc_ref[0, pl.ds(k0, KEY_BLOCK), :]
        vblk = vc_ref[0, pl.ds(k0, KEY_BLOCK), :]
        for g in range(C_KV_HEADS):
            gs = slice(g * C_HDIM, (g + 1) * C_HDIM)
            s = _dot_nt(qs_ref[g], kblk[:, gs]).reshape(grp, tq, KEY_BLOCK)
            s = jnp.where(sel[None], s, NEG)
            m_old = m_ref[g]
            m_new = jnp.maximum(m_old, jnp.max(s, axis=-1, keepdims=True))
            p = jnp.where(sel[None], jnp.exp(s - m_new), 0.0)
            alpha = jnp.exp(m_old - m_new)
            l_ref[g] = alpha * l_ref[g] + jnp.sum(p, axis=-1, keepdims=True)
            pv = _dot(p.reshape(grp * tq, KEY_BLOCK).astype(BF16), vblk[:, gs])
            acc_ref[g] = alpha * acc_ref[g] + pv.reshape(grp, tq, C_HDIM)
            m_ref[g] = m_new
        return carry

    lax.fori_loop(0, nkb, attn_step, 0)

    for hq in range(C_HEADS):
        g, r = divmod(hq, grp)
        o_ref[:, hq * C_HDIM:(hq + 1) * C_HDIM] = acc_ref[g, r] / l_ref[g, r]


SUPER = 4
ATTN_BLOCKS = 2


def _dsa_t_body(qc_ref, qi_ref, sm_ref, kc_ref, vt_ref, ki_ref, o_ref,
                sk_ref, qit_ref, qct_ref, wrow_ref, acc_ref, ot_ref, s0_ref, s1_ref, bias_ref, p_ref,
                *, tq, past, nkeys, topk):
    i = pl.program_id(1)
    qpos0 = past + i * tq
    last_chunk = (qpos0 + tq - 1) // CHUNK
    n_adm = jnp.minimum((last_chunk + 1) * CHUNK, nkeys)
    nkb = (n_adm + KEY_BLOCK - 1) // KEY_BLOCK
    nsb = (nkb + SUPER - 1) // SUPER
    grp = C_HEADS // C_KV_HEADS

    qit = qi_ref[...].astype(F32).T
    for j in range(IDX_HEADS):
        qit_ref[:, j * tq:(j + 1) * tq] = qit[j * IDX_DIM:(j + 1) * IDX_DIM, :].astype(BF16)
    qct = qc_ref[...].astype(F32).T
    zeros = jnp.zeros((C_HDIM, tq), BF16)
    for hq in range(C_HEADS):
        g, r = divmod(hq, grp)
        for gg in range(C_KV_HEADS):
            val = qct[hq * C_HDIM:(hq + 1) * C_HDIM, :].astype(BF16) if gg == g else zeros
            qct_ref[g, gg * C_HDIM:(gg + 1) * C_HDIM, r * tq:(r + 1) * tq] = val
    wrow_ref[...] = sm_ref[...].T

    qchunk = (qpos0 + lax.broadcasted_iota(jnp.int32, (1, tq), 1)) >> 6
    ksub = lax.broadcasted_iota(jnp.int32, (KEY_BLOCK, 1), 0)

    def admissible(kb):
        kpos = kb * KEY_BLOCK + ksub
        return ((kpos >> 6) <= qchunk) & (kpos < nkeys), kpos

    def score_step(kk, carry):
        for u in range(SUPER):
            kb = kk * SUPER + u
            kib = ki_ref[0, pl.ds(pl.multiple_of(kb * KEY_BLOCK, KEY_BLOCK), KEY_BLOCK), :]
            d = _dot(kib, qit_ref[...])
            sc = jnp.maximum(d[:, 0:tq], 0.0) * wrow_ref[SMALL_IW:SMALL_IW + 1, :]
            for j in range(1, IDX_HEADS):
                sc = sc + jnp.maximum(d[:, j * tq:(j + 1) * tq], 0.0) * wrow_ref[SMALL_IW + j:SMALL_IW + j + 1, :]
            adm, _ = admissible(kb)
            sc = jnp.where(adm, sc * IDX_SCALE, NEG) + 0.0
            bits = pltpu.bitcast(sc, jnp.int32)
            sk_ref[kb] = jnp.where(bits < 0, bits ^ 0x7FFFFFFF, bits)
        return carry

    lax.fori_loop(0, nsb, score_step, 0)

    def fold(hit):
        return jnp.sum(hit.reshape(KEY_BLOCK // 8, 8, tq), axis=0)

    def count(pred):
        def step(kk, acc):
            for u in range(SUPER):
                kb = kk * SUPER + u
                acc = acc + fold(jnp.where(pred(sk_ref[kb], kb), 1.0, 0.0))
            return acc
        acc = lax.fori_loop(0, nsb, step, jnp.zeros((8, tq), F32))
        return jnp.sum(acc, axis=0, keepdims=True)

    kf = float(topk)
    c0 = count(lambda s, kb: s >= 0)
    thr = jnp.where(c0 >= kf, 0, INT_MIN).astype(jnp.int32)

    def bit_step(it, thr):
        cand = thr | jnp.left_shift(jnp.int32(1), 30 - it)
        return jnp.where(count(lambda s, kb: s >= cand) >= kf, cand, thr)

    thr = lax.fori_loop(0, 31, bit_step, thr)

    c_ge = count(lambda s, kb: s >= thr)
    c_gt = count(lambda s, kb: s > thr)
    need = kf - c_gt
    nbits = max(1, (nkeys - 1).bit_length())

    def tie_cut():
        def cut_step(it, cut):
            cand = cut | jnp.left_shift(jnp.int32(1), nbits - 1 - it)
            c = count(lambda s, kb: (s == thr) & (kb * KEY_BLOCK + ksub < cand))
            return jnp.where(c < need, cand, cut)
        return lax.fori_loop(0, nbits, cut_step, jnp.zeros((1, tq), jnp.int32))

    cut = lax.cond(jnp.max(c_ge) > kf, tie_cut, lambda: jnp.full((1, tq), 2 ** 30, jnp.int32))

    acc_ref[...] = jnp.zeros(acc_ref.shape, F32)
    init = tuple(jnp.full((1, grp * tq), NEG, F32) for _ in range(C_KV_HEADS))

    span = ATTN_BLOCKS * KEY_BLOCK
    n_attn = nsb * (SUPER // ATTN_BLOCKS)

    s_bufs = (s0_ref, s1_ref)

    def qk_scores(kk, half):
        kblk = kc_ref[0, pl.ds(pl.multiple_of(kk * span, span), span), :]
        for g in range(C_KV_HEADS):
            s_bufs[half][g] = _dot(kblk, qct_ref[g])

    def softmax_pv(kk, half, m):
        m = list(m)
        for u in range(ATTN_BLOCKS):
            kb = kk * ATTN_BLOCKS + u
            adm, kpos = admissible(kb)
            s_key = sk_ref[kb]
            sel = ((s_key > thr) | ((s_key == thr) & (kpos <= cut))) & adm
            bias_ref[half, u * KEY_BLOCK:(u + 1) * KEY_BLOCK, :] = jnp.where(sel, 0.0, NEG)
        for g in range(C_KV_HEADS):
            m_cols, alpha_cols = [], []
            for r in range(grp):
                cols = slice(r * tq, (r + 1) * tq)
                sb = s_bufs[half][g, :, cols] + bias_ref[half]
                m_old = m[g][:, cols]
                m_new = jnp.maximum(m_old, jnp.max(sb, axis=0, keepdims=True))
                p_ref[half, g, :, cols] = jnp.exp(sb - m_new).astype(BF16)
                m_cols.append(m_new)
                alpha_cols.append(jnp.exp(m_old - m_new))
            alpha = jnp.concatenate(alpha_cols, axis=1)
            rows = slice(g * VT_ROWS, (g + 1) * VT_ROWS)
            pv = _dot(vt_ref[0, kk * ATTN_BLOCKS, rows, :], p_ref[half, g, 0:KEY_BLOCK, :])
            for u in range(1, ATTN_BLOCKS):
                pv = pv + _dot(vt_ref[0, kk * ATTN_BLOCKS + u, rows, :],
                               p_ref[half, g, u * KEY_BLOCK:(u + 1) * KEY_BLOCK, :])
            acc_ref[g] = alpha * acc_ref[g] + pv
            m[g] = jnp.concatenate(m_cols, axis=1)
        return tuple(m)

    qk_scores(0, 0)

    def pair_step(j, m):
        qk_scores(2 * j + 1, 1)
        m = softmax_pv(2 * j, 0, m)
        qk_scores(jnp.minimum(2 * j + 2, n_attn - 2), 0)
        return softmax_pv(2 * j + 1, 1, m)

    lax.fori_loop(0, n_attn // 2, pair_step, init)
    for g in range(C_KV_HEADS):
        og = acc_ref[g, 0:C_HDIM, :] / acc_ref[g, C_HDIM:C_HDIM + 1, :]
        for r in range(grp):
            hq = g * grp + r
            ot_ref[hq * C_HDIM:(hq + 1) * C_HDIM, :] = og[:, r * tq:(r + 1) * tq]
    o_ref[...] = ot_ref[...].T


def _dsa_t(qc, qi, z, row0, keys, vals_t, kidx, nseq, t, past, nkeys):
    tq = KEY_BLOCK
    nq = t // tq
    rb0 = row0 // tq
    lp = keys.shape[1]
    topk = min(TOPK_MAX, nkeys // 4)
    grp = C_HEADS // C_KV_HEADS
    qrow = lambda b, i: (rb0 + b * nq + i, 0)
    return pl.pallas_call(
        functools.partial(_dsa_t_body, tq=tq, past=past, nkeys=nkeys, topk=topk),
        grid=(nseq, nq),
        in_specs=[
            pl.BlockSpec((tq, C_WIDTH), qrow),
            pl.BlockSpec((tq, IDX_HEADS * IDX_DIM), qrow),
            pl.BlockSpec((tq, LANES), lambda b, i: (rb0 + b * nq + i, Z_SMALL)),
            pl.BlockSpec((1, lp, LANES), lambda b, i: (b, 0, 0)),
            pl.BlockSpec((1, lp // KEY_BLOCK, C_KV_HEADS * VT_ROWS, KEY_BLOCK), lambda b, i: (b, 0, 0, 0)),
            pl.BlockSpec((1, lp, IDX_DIM), lambda b, i: (b, 0, 0)),
        ],
        out_specs=pl.BlockSpec((tq, C_WIDTH), lambda b, i: (b * nq + i, 0)),
        out_shape=jax.ShapeDtypeStruct((nseq * t, C_WIDTH), F32),
        scratch_shapes=[
            pltpu.VMEM((lp // KEY_BLOCK, KEY_BLOCK, tq), jnp.int32),
            pltpu.VMEM((IDX_DIM, IDX_HEADS * tq), BF16),
            pltpu.VMEM((C_KV_HEADS, C_KV_HEADS * C_HDIM, grp * tq), BF16),
            pltpu.VMEM((LANES, tq), F32),
            pltpu.VMEM((C_KV_HEADS, VT_ROWS, grp * tq), F32),
            pltpu.VMEM((C_WIDTH, tq), F32),
            pltpu.VMEM((C_KV_HEADS, ATTN_BLOCKS * KEY_BLOCK, grp * tq), F32),
            pltpu.VMEM((C_KV_HEADS, ATTN_BLOCKS * KEY_BLOCK, grp * tq), F32),
            pltpu.VMEM((2, ATTN_BLOCKS * KEY_BLOCK, tq), F32),
            pltpu.VMEM((2, C_KV_HEADS, ATTN_BLOCKS * KEY_BLOCK, grp * tq), BF16),
        ],
        compiler_params=_params(("parallel", "arbitrary"), 48),
        name="dsa_mixer_t",
    )(qc, qi, z, keys, vals_t, kidx)


def _dsa(qc, qi, z, row0, keys, vals, kidx, nseq, t, past, nkeys):
    tq = min(128, t)
    nq = t // tq
    rb0 = row0 // tq
    lp = keys.shape[1]
    topk = min(TOPK_MAX, nkeys // 4)
    grp = C_HEADS // C_KV_HEADS
    qrow = lambda b, i: (rb0 + b * nq + i, 0)
    seq = lambda b, i: (b, 0, 0)
    return pl.pallas_call(
        functools.partial(_dsa_body, tq=tq, past=past, nkeys=nkeys, topk=topk),
        grid=(nseq, nq),
        in_specs=[
            pl.BlockSpec((tq, C_WIDTH), qrow),
            pl.BlockSpec((tq, IDX_HEADS * IDX_DIM), qrow),
            pl.BlockSpec((tq, LANES), lambda b, i: (rb0 + b * nq + i, Z_SMALL)),
            pl.BlockSpec((1, lp, LANES), seq),
            pl.BlockSpec((1, lp, LANES), seq),
            pl.BlockSpec((1, lp, IDX_DIM), seq),
        ],
        out_specs=pl.BlockSpec((tq, C_WIDTH), lambda b, i: (b * nq + i, 0)),
        out_shape=jax.ShapeDtypeStruct((nseq * t, C_WIDTH), F32),
        scratch_shapes=[
            pltpu.VMEM((lp // KEY_BLOCK, tq, KEY_BLOCK), jnp.int32),
            pltpu.VMEM((IDX_HEADS * tq, IDX_DIM), BF16),
            pltpu.VMEM((IDX_HEADS, tq, KEY_BLOCK), F32),
            pltpu.VMEM((C_KV_HEADS, grp * tq, C_HDIM), BF16),
            pltpu.VMEM((C_KV_HEADS, grp, tq, 1), F32),
            pltpu.VMEM((C_KV_HEADS, grp, tq, 1), F32),
            pltpu.VMEM((C_KV_HEADS, grp, tq, C_HDIM), F32),
        ],
        compiler_params=_params(("parallel", "arbitrary"), 48),
        name="dsa_mixer",
    )(qc, qi, z, keys, vals, kidx)


def _merge_body(h_ref, ya_ref, yb_ref, oc_ref, ga_ref, gb_ref, gc_ref,
                wa_ref, wb_ref, wc_ref, wo_ref, o_ref):
    merged = (jax.nn.sigmoid(ga_ref[...]) * _dot(ya_ref[...].astype(BF16), wa_ref[...])
              + jax.nn.sigmoid(gb_ref[...]) * _dot(yb_ref[...].astype(BF16), wb_ref[...])
              + jax.nn.sigmoid(gc_ref[...]) * _dot(oc_ref[...].astype(BF16), wc_ref[...]))
    o_ref[...] = h_ref[...] + _dot(merged.astype(BF16), wo_ref[...])


def _merge(h, ya, yb, oc, z, wa, wb, wc, wo):
    n, d = h.shape
    tm = min(TOKEN_TILE, n)
    row = lambda i: (i, 0)
    const = lambda i: (0, 0)
    return pl.pallas_call(
        _merge_body,
        grid=(n // tm,),
        in_specs=[
            pl.BlockSpec((tm, d), row),
            pl.BlockSpec((tm, MIX_W), row),
            pl.BlockSpec((tm, MIX_W), row),
            pl.BlockSpec((tm, MIX_W), row),
            pl.BlockSpec((tm, d), lambda i: (i, 0)),
            pl.BlockSpec((tm, d), lambda i: (i, 1)),
            pl.BlockSpec((tm, d), lambda i: (i, 2)),
            pl.BlockSpec((MIX_W, d), const),
            pl.BlockSpec((MIX_W, d), const),
            pl.BlockSpec((MIX_W, d), const),
            pl.BlockSpec((d, d), const),
        ],
        out_specs=pl.BlockSpec((tm, d), row),
        out_shape=jax.ShapeDtypeStruct((n, d), F32),
        compiler_params=_params(("parallel",), 48),
        name="gated_merge",
    )(h, ya, yb, oc, z, z, z, wa, wb, wc, wo)


def _pad_heads(w, heads, dim):
    lead = w.shape[:-1]
    w = w.reshape(*lead, heads, dim)
    w = jnp.pad(w, [(0, 0)] * len(lead) + [(0, 0), (0, HEAD_W - dim)])
    return w.reshape(*lead, heads * HEAD_W)


def _layout_w_in(w):
    widths = (512, 512, 512, 512, 256, 256, 512, 16, 512, 512, 128, 128, 512, 64, 8, 1024, 1024, 1024)
    parts, o = [], 0
    for wd in widths:
        parts.append(w[:, o:o + wd])
        o += wd
    (a_q, a_f, a_i, a_g, b_q, b_k, b_v, b_r, b_g, c_q, c_k, c_v, i_q, i_k, i_w, g_a, g_b, g_c) = parts
    d = w.shape[0]
    small = jnp.concatenate([b_r, i_w, jnp.zeros((d, LANES - 24), w.dtype)], axis=1)
    i_k = jnp.pad(i_k, ((0, 0), (0, LANES - IDX_DIM)))
    cols = [g_a, g_b, g_c, a_q, a_f, a_i, a_g,
            _pad_heads(b_q, B_HEADS, B_KDIM), _pad_heads(b_k, B_HEADS, B_KDIM), b_v, b_g,
            c_q, i_q, c_k, c_v, i_k, small]
    out = jnp.concatenate(cols, axis=1)
    assert out.shape[1] == Z_WIDTH
    return out.astype(BF16)


def _rope_tables(pos):
    half = ROPE_DIMS // 2
    inv = ROPE_THETA ** (-jnp.arange(half, dtype=F32) * (2.0 / ROPE_DIMS))
    ang = pos.astype(F32)[:, None] * inv[None, :]
    cos, sin = jnp.cos(ang), jnp.sin(ang)
    n = pos.shape[0]
    ones = jnp.ones((n, C_HDIM - ROPE_DIMS), F32)
    zeros = jnp.zeros((n, C_HDIM - ROPE_DIMS), F32)
    zh = jnp.zeros((n, half), F32)
    c = jnp.concatenate([cos, cos, ones], axis=1)
    s_lo = jnp.concatenate([-sin, zh, zeros], axis=1)
    s_hi = jnp.concatenate([zh, sin, zeros], axis=1)
    two = lambda a: jnp.concatenate([a, a], axis=1)
    return two(c), two(s_lo), two(s_hi)


def kernel(x_prompt, x_sample, state_hgrn, state_gla, cache_k, cache_v, cache_kidx, hgrn_lb, w_in, w_gla_up, b_gla, norm_hgrn, norm_gla, w_br_a, w_br_b, w_br_c, w_out, norm_ffn1, norm_mix, norm_ffn2, ffn1_w1, ffn1_w3, ffn1_w2, ffn2_w1, ffn2_w3, ffn2_w2, norm_final):
    bp, tp, d = x_prompt.shape
    bs, ts, _ = x_sample.shape
    past = cache_k.shape[2]
    n_p, n_s = bp * tp, bs * ts

    lb_sm = jax.nn.softmax(hgrn_lb.astype(F32), axis=0)
    lb_all = jnp.concatenate([jnp.zeros_like(lb_sm[:1]), jnp.cumsum(lb_sm[1:], axis=0)], axis=0)

    pos = jnp.concatenate([jnp.tile(jnp.arange(tp, dtype=jnp.int32), bp),
                           jnp.tile(past + jnp.arange(ts, dtype=jnp.int32), bs)])
    cos_t, sin_lo, sin_hi = _rope_tables(pos)

    h = jnp.concatenate([x_prompt.reshape(n_p, d), x_sample.reshape(n_s, d)], axis=0)
    row2 = lambda a: a.reshape(1, -1)
    zero_state = jnp.zeros((bp, 4, HEAD_W, HEAD_W), F32)
    lp_s = -(-(past + ts) // KEY_BLOCK) * KEY_BLOCK
    key_pad = lp_s - past - ts

    outs = {k: [] for k in ("pa", "pb", "pk", "pv", "pki", "sa", "sb", "sk", "sv", "ski")}
    for l in range(DEPTH):
        bf = lambda a: a[l].astype(BF16)
        h = _ffn(h, row2(norm_ffn1[l]), bf(ffn1_w1), bf(ffn1_w3), bf(ffn1_w2), row2(norm_final), False)
        z = _inproj(h, row2(norm_mix[l]), _layout_w_in(w_in[l]))
        qc, qi, kc, vc, ki, kcb, vcb, kib, vt = _prep(z, cos_t, sin_lo, sin_hi)

        lb = row2(lb_all[l])
        nwa, nwb = row2(norm_hgrn[l]), row2(norm_gla[l])
        wup = jnp.pad(_pad_heads(w_gla_up[l], B_HEADS, B_KDIM), ((0, LANES - B_GATE_RANK), (0, 0)))
        bup = row2(_pad_heads(b_gla[l], B_HEADS, B_KDIM))
        sb0 = jnp.pad(state_gla[l], ((0, 0), (0, 0), (0, HEAD_W - B_KDIM), (0, 0)))

        ya_p, sa_p = _recurrent_mixer("hgrn", z, 0, bp, tp, zero_state, nwa, (lb,))
        ya_s, sa_s = _recurrent_mixer("hgrn", z, n_p, bs, ts, state_hgrn[l], nwa, (lb,))
        yb_p, sb_p = _recurrent_mixer("gla", z, 0, bp, tp, zero_state, nwb, (wup, bup))
        yb_s, sb_s = _recurrent_mixer("gla", z, n_p, bs, ts, sb0, nwb, (wup, bup))

        seqs = lambda a, n, t: a.reshape(n, t, a.shape[-1])
        vt_p = vt[:n_p // KEY_BLOCK].reshape(bp, tp // KEY_BLOCK, C_KV_HEADS * VT_ROWS, KEY_BLOCK)
        oc_p = _dsa_t(qc, qi, z, 0, seqs(kcb[:n_p], bp, tp), vt_p, seqs(kib[:n_p], bp, tp), bp, tp, 0, tp)

        def with_cache(cache, new):
            full = jnp.concatenate([cache.reshape(bs, past, -1).astype(BF16), seqs(new[n_p:], bs, ts)], axis=1)
            return jnp.pad(full, ((0, 0), (0, key_pad), (0, 0)))

        oc_s = _dsa(qc, qi, z, n_p, with_cache(cache_k[l], kcb), with_cache(cache_v[l], vcb),
                    with_cache(cache_kidx[l], kib), bs, ts, past, past + ts)

        ya = jnp.concatenate([ya_p, ya_s], axis=0)
        yb = jnp.concatenate([yb_p, yb_s], axis=0)
        oc = jnp.concatenate([oc_p, oc_s], axis=0)
        h = _merge(h, ya, yb, oc, z, bf(w_br_a), bf(w_br_b), bf(w_br_c), bf(w_out))
        h = _ffn(h, row2(norm_ffn2[l]), bf(ffn2_w1), bf(ffn2_w3), bf(ffn2_w2), row2(norm_final),
                 l == DEPTH - 1)

        outs["pa"].append(sa_p)
        outs["sa"].append(sa_s)
        outs["pb"].append(sb_p[:, :, :B_KDIM, :])
        outs["sb"].append(sb_s[:, :, :B_KDIM, :])
        outs["pk"].append(kc[:n_p].reshape(bp, tp, C_KV_HEADS, C_HDIM))
        outs["pv"].append(vc[:n_p].reshape(bp, tp, C_KV_HEADS, C_HDIM))
        outs["pki"].append(ki[:n_p].reshape(bp, tp, IDX_DIM))
        outs["sk"].append(kc[n_p:].reshape(bs, ts, C_KV_HEADS, C_HDIM))
        outs["sv"].append(vc[n_p:].reshape(bs, ts, C_KV_HEADS, C_HDIM))
        outs["ski"].append(ki[n_p:].reshape(bs, ts, IDX_DIM))

    st = {k: jnp.stack(v) for k, v in outs.items()}
    return (h[:n_p].reshape(bp, tp, d), h[n_p:].reshape(bs, ts, d),
            st["pa"], st["pb"], st["pk"], st["pv"], st["pki"],
            st["sa"], st["sb"], st["sk"], st["sv"], st["ski"])
```

```python
import functools

import jax
import jax.numpy as jnp
from jax import lax
from jax.experimental import pallas as pl
from jax.experimental.pallas import tpu as pltpu

F32 = jnp.float32
BF16 = jnp.bfloat16

D_MODEL = 1024
DEPTH = 2
CHUNK = 64
EPS = 1e-6
NEG = -1e30
F_MIN = 1e-30
A_HEADS = 4
A_KDIM = 128
A_VDIM = 128
B_HEADS = 4
B_KDIM = 64
B_VDIM = 128
B_GATE_RANK = 16
B_TAU = 16.0
C_HEADS = 8
C_KV_HEADS = 2
C_HDIM = 64
C_WIDTH = C_HEADS * C_HDIM
IDX_HEADS = 8
IDX_DIM = 64
IDX_SCALE = (IDX_HEADS * IDX_DIM) ** -0.5
TOPK_MAX = 256
ROPE_THETA = 500000.0
ROPE_DIMS = C_HDIM // 4
GLA_BLOCK = 16
D_FF = 2816

LANES = 128
HEAD_W = 128
MIX_W = 4 * HEAD_W
TOKEN_TILE = 512
FF_TILE = D_FF // 2
KEY_BLOCK = 128
VT_ROWS = 80
INT_MIN = -(2 ** 31)

Z_GATES = 0
Z_A = 24
Z_B = 40
Z_CQ = 56
Z_IQ = 60
Z_CKVI = 64
Z_SMALL = 67
Z_WIDTH = 68 * LANES
SMALL_IW = B_GATE_RANK


def _params(sem, vmem_mb):
    return pltpu.CompilerParams(dimension_semantics=sem, vmem_limit_bytes=vmem_mb << 20)


def _dot(a, b):
    return jnp.dot(a, b, preferred_element_type=F32)


def _dot_nt(a, b):
    return lax.dot_general(a, b, (((1,), (1,)), ((), ())), preferred_element_type=F32)


def _dot_tn(a, b):
    return lax.dot_general(a, b, (((0,), (0,)), ((), ())), preferred_element_type=F32)


def _split3(x):
    h1 = x.astype(BF16)
    r1 = x - h1.astype(F32)
    h2 = r1.astype(BF16)
    h3 = (r1 - h2.astype(F32)).astype(BF16)
    return h1, h2, h3


def _dot_exact_lhs(m, x):
    h1, h2, h3 = _split3(x)
    return _dot(m, h1) + _dot(m, h2) + _dot(m, h3)


def _dot_hi(a, b):
    a1 = a.astype(BF16)
    a2 = (a - a1.astype(F32)).astype(BF16)
    b1 = b.astype(BF16)
    b2 = (b - b1.astype(F32)).astype(BF16)
    return _dot(a1, b1) + _dot(a1, b2) + _dot(a2, b1)


def _rmsnorm(x, g):
    return x * lax.rsqrt(jnp.mean(x * x, axis=-1, keepdims=True) + EPS) * g


def _ffn_body(x_ref, g_ref, w1_ref, w3_ref, w2_ref, gf_ref, o_ref, u_ref, acc_ref, *, final_norm):
    j = pl.program_id(1)

    @pl.when(j == 0)
    def _():
        u_ref[...] = _rmsnorm(x_ref[...], g_ref[...]).astype(BF16)
        acc_ref[...] = jnp.zeros_like(acc_ref)

    u = u_ref[...]
    a = _dot(u, w1_ref[...])
    b = _dot(u, w3_ref[...])
    hid = (a * jax.nn.sigmoid(a) * b).astype(BF16)
    acc_ref[...] += _dot(hid, w2_ref[...])

    @pl.when(j == pl.num_programs(1) - 1)
    def _():
        out = x_ref[...] + 0.5 * acc_ref[...]
        if final_norm:
            out = _rmsnorm(out, gf_ref[...])
        o_ref[...] = out


def _ffn(x, g, w1, w3, w2, gf, final_norm):
    n, d = x.shape
    dff = w1.shape[1]
    tm = min(TOKEN_TILE, n)
    tf = FF_TILE if dff == D_FF else dff
    return pl.pallas_call(
        functools.partial(_ffn_body, final_norm=final_norm),
        grid=(n // tm, dff // tf),
        in_specs=[
            pl.BlockSpec((tm, d), lambda i, j: (i, 0)),
            pl.BlockSpec((1, d), lambda i, j: (0, 0)),
            pl.BlockSpec((d, tf), lambda i, j: (0, j)),
            pl.BlockSpec((d, tf), lambda i, j: (0, j)),
            pl.BlockSpec((tf, d), lambda i, j: (j, 0)),
            pl.BlockSpec((1, d), lambda i, j: (0, 0)),
        ],
        out_specs=pl.BlockSpec((tm, d), lambda i, j: (i, 0)),
        out_shape=jax.ShapeDtypeStruct((n, d), F32),
        scratch_shapes=[pltpu.VMEM((tm, d), BF16), pltpu.VMEM((tm, d), F32)],
        compiler_params=_params(("parallel", "arbitrary"), 48),
        name="ffn_half_step",
    )(x, g, w1, w3, w2, gf)


def _inproj_body(x_ref, g_ref, w_ref, z_ref, u_ref):
    @pl.when(pl.program_id(1) == 0)
    def _():
        u_ref[...] = _rmsnorm(x_ref[...], g_ref[...]).astype(BF16)

    z_ref[...] = _dot(u_ref[...], w_ref[...])


def _inproj(h, g, w):
    n, d = h.shape
    zw = w.shape[1]
    tm = min(TOKEN_TILE, n)
    tn = 512
    return pl.pallas_call(
        _inproj_body,
        grid=(n // tm, zw // tn),
        in_specs=[
            pl.BlockSpec((tm, d), lambda i, j: (i, 0)),
            pl.BlockSpec((1, d), lambda i, j: (0, 0)),
            pl.BlockSpec((d, tn), lambda i, j: (0, j)),
        ],
        out_specs=pl.BlockSpec((tm, tn), lambda i, j: (i, j)),
        out_shape=jax.ShapeDtypeStruct((n, zw), F32),
        scratch_shapes=[pltpu.VMEM((tm, d), BF16)],
        compiler_params=_params(("parallel", "arbitrary"), 32),
        name="in_projection",
    )(h, g, w)


def _rope(x, c, s_lo, s_hi):
    w = x.shape[1]
    rep = w // LANES
    if rep > 1:
        c = jnp.concatenate([c] * rep, axis=1)
        s_lo = jnp.concatenate([s_lo] * rep, axis=1)
        s_hi = jnp.concatenate([s_hi] * rep, axis=1)
    half = ROPE_DIMS // 2
    return x * c + pltpu.roll(x, half, 1) * s_hi + pltpu.roll(x, w - half, 1) * s_lo


def _prep_body(cq_ref, iq_ref, kv_ref, c_ref, slo_ref, shi_ref,
               qc_ref, qi_ref, kc_ref, vc_ref, ki_ref, kcb_ref, vcb_ref, kib_ref, vt_ref):
    c, s_lo, s_hi = c_ref[...], slo_ref[...], shi_ref[...]
    qc_ref[...] = (_rope(cq_ref[...], c, s_lo, s_hi) * (C_HDIM ** -0.5)).astype(BF16)
    qi_ref[...] = _rope(iq_ref[...], c, s_lo, s_hi).astype(BF16)
    kc = _rope(kv_ref[:, 0:LANES], c, s_lo, s_hi)
    vc = kv_ref[:, LANES:2 * LANES]
    ki = _rope(kv_ref[:, 2 * LANES:3 * LANES], c, s_lo, s_hi)[:, :IDX_DIM]
    kc_ref[...] = kc
    vc_ref[...] = vc
    ki_ref[...] = ki
    kcb_ref[...] = kc.astype(BF16)
    vcb_ref[...] = vc.astype(BF16)
    kib_ref[...] = ki.astype(BF16)
    ones = jnp.ones((VT_ROWS - C_HDIM, KEY_BLOCK), BF16)
    for kk in range(vt_ref.shape[0]):
        vt = vc[kk * KEY_BLOCK:(kk + 1) * KEY_BLOCK, :].T.astype(BF16)
        vt_ref[kk] = jnp.concatenate([vt[:C_HDIM], ones, vt[C_HDIM:], ones], axis=0)


def _prep(z, cos_t, sin_lo, sin_hi):
    n = z.shape[0]
    tm = min(TOKEN_TILE, n)
    row = lambda i: (i, 0)
    return pl.pallas_call(
        _prep_body,
        grid=(n // tm,),
        in_specs=[
            pl.BlockSpec((tm, MIX_W), lambda i: (i, Z_CQ // 4)),
            pl.BlockSpec((tm, MIX_W), lambda i: (i, Z_IQ // 4)),
            pl.BlockSpec((tm, MIX_W), lambda i: (i, Z_CKVI // 4)),
            pl.BlockSpec((tm, LANES), row),
            pl.BlockSpec((tm, LANES), row),
            pl.BlockSpec((tm, LANES), row),
        ],
        out_specs=[
            pl.BlockSpec((tm, C_WIDTH), row),
            pl.BlockSpec((tm, IDX_HEADS * IDX_DIM), row),
            pl.BlockSpec((tm, LANES), row),
            pl.BlockSpec((tm, LANES), row),
            pl.BlockSpec((tm, IDX_DIM), row),
            pl.BlockSpec((tm, LANES), row),
            pl.BlockSpec((tm, LANES), row),
            pl.BlockSpec((tm, IDX_DIM), row),
            pl.BlockSpec((tm // KEY_BLOCK, C_KV_HEADS * VT_ROWS, KEY_BLOCK), lambda i: (i, 0, 0)),
        ],
        out_shape=[
            jax.ShapeDtypeStruct((n, C_WIDTH), BF16),
            jax.ShapeDtypeStruct((n, IDX_HEADS * IDX_DIM), BF16),
            jax.ShapeDtypeStruct((n, LANES), F32),
            jax.ShapeDtypeStruct((n, LANES), F32),
            jax.ShapeDtypeStruct((n, IDX_DIM), F32),
            jax.ShapeDtypeStruct((n, LANES), BF16),
            jax.ShapeDtypeStruct((n, LANES), BF16),
            jax.ShapeDtypeStruct((n, IDX_DIM), BF16),
            jax.ShapeDtypeStruct((n // KEY_BLOCK, C_KV_HEADS * VT_ROWS, KEY_BLOCK), BF16),
        ],
        compiler_params=_params(("parallel",), 32),
        name="rotary_kv_staging",
    )(z, z, z, cos_t, sin_lo, sin_hi)


def _gla_head(q, k, v, logf, st_ref, h, tc):
    row = lax.broadcasted_iota(jnp.int32, (tc, tc), 0)
    col = lax.broadcasted_iota(jnp.int32, (tc, tc), 1)
    tril = (col <= row).astype(BF16)
    cum = _dot_exact_lhs(tril, logf)

    nb = tc // GLA_BLOCK
    b3 = cum.reshape(nb, GLA_BLOCK, HEAD_W)
    q3 = q.reshape(nb, GLA_BLOCK, HEAD_W)
    k3 = k.reshape(nb, GLA_BLOCK, HEAD_W)
    v3 = v.reshape(nb, GLA_BLOCK, HEAD_W)
    tloc = lax.broadcasted_iota(jnp.int32, (nb, GLA_BLOCK, 1), 1)
    o3 = jnp.zeros((nb, GLA_BLOCK, HEAD_W), F32)
    for j in range(GLA_BLOCK):
        causal = tloc >= j
        decay = jnp.exp(jnp.where(causal, b3 - b3[:, j:j + 1, :], 0.0))
        w = jnp.sum(q3 * k3[:, j:j + 1, :] * decay, axis=-1, keepdims=True)
        o3 = o3 + jnp.where(causal, w, 0.0) * v3[:, j:j + 1, :]
    o = o3.reshape(tc, HEAD_W)

    vb = v.astype(BF16)
    attn = jnp.zeros((tc, tc), F32)
    half = tc // 2
    while half >= GLA_BLOCK:
        blk = 2 * half
        nblk = tc // blk
        bl = cum.reshape(nblk, blk, HEAD_W)
        x = bl - bl[:, half - 1:half, :]
        second = lax.broadcasted_iota(jnp.int32, (nblk, blk, 1), 1) >= half
        e = jnp.exp(jnp.where(second, x, -x))
        qt = jnp.where(second, q.reshape(nblk, blk, HEAD_W) * e, 0.0).reshape(tc, HEAD_W).astype(BF16)
        kt = jnp.where(second, 0.0, k.reshape(nblk, blk, HEAD_W) * e).reshape(tc, HEAD_W).astype(BF16)
        shift = blk.bit_length() - 1
        same = (row >> shift) == (col >> shift)
        attn = attn + jnp.where(same, _dot_nt(qt, kt), 0.0)
        half //= 2
    if tc > GLA_BLOCK:
        o = o + _dot(attn.astype(BF16), vb)

    st = st_ref[h]
    o = o + _dot_nt((q * jnp.exp(cum)).astype(BF16), st.astype(BF16))
    last = cum[tc - 1:tc, :]
    kd = (k * jnp.exp(last - cum)).astype(BF16)
    st_ref[h] = st * jnp.exp(last) + _dot_tn(vb, kd)
    return o


def _gla_finish(o, nw, gate):
    return _rmsnorm(o, nw) * (gate * jax.nn.sigmoid(gate))


def _gla_state_io(c, s0_ref, st_ref, heads):
    @pl.when(c == 0)
    def _():
        for h in range(heads):
            st_ref[h] = s0_ref[0, h].T


def _gla_state_out(c, sout_ref, st_ref, heads):
    @pl.when(c == pl.num_programs(1) - 1)
    def _():
        for h in range(heads):
            sout_ref[0, h] = st_ref[h].T


def _hgrn_body(q_ref, f_ref, v_ref, g_ref, lb_ref, nw_ref, s0_ref, y_ref, sout_ref, st_ref, *, tc):
    c = pl.program_id(1)
    _gla_state_io(c, s0_ref, st_ref, A_HEADS)
    for h in range(A_HEADS):
        hs = slice(h * HEAD_W, (h + 1) * HEAD_W)
        zf = f_ref[:, hs]
        lb = lb_ref[:, hs]
        f = lb + (1.0 - lb) * jax.nn.sigmoid(zf)
        logf = jnp.log(jnp.maximum(f, F_MIN))
        k = (1.0 - lb) * jax.nn.sigmoid(-zf)
        zq = q_ref[:, hs]
        q = zq * jax.nn.sigmoid(zq) * (A_KDIM ** -0.5)
        o = _gla_head(q, k, v_ref[:, hs], logf, st_ref, h, tc)
        y_ref[:, hs] = _gla_finish(o, nw_ref[...], g_ref[:, hs])
    _gla_state_out(c, sout_ref, st_ref, A_HEADS)


def _gla_body(q_ref, k_ref, v_ref, g_ref, r_ref, wup_ref, bup_ref, nw_ref, s0_ref,
              y_ref, sout_ref, st_ref, *, tc):
    c = pl.program_id(1)
    _gla_state_io(c, s0_ref, st_ref, B_HEADS)
    r = _dot_hi(r_ref[...], wup_ref[...]) + bup_ref[...]
    logf_all = (jnp.minimum(r, 0.0) - jnp.log1p(jnp.exp(-jnp.abs(r)))) / B_TAU
    for h in range(B_HEADS):
        hs = slice(h * HEAD_W, (h + 1) * HEAD_W)
        q = q_ref[:, hs] * (B_KDIM ** -0.5)
        o = _gla_head(q, k_ref[:, hs], v_ref[:, hs], logf_all[:, hs], st_ref, h, tc)
        y_ref[:, hs] = _gla_finish(o, nw_ref[...], g_ref[:, hs])
    _gla_state_out(c, sout_ref, st_ref, B_HEADS)


def _recurrent_mixer(mode, z, row0, nseq, t, s0, nw, extra):
    tc = min(128, t)
    nc = t // tc
    rb0 = row0 // tc
    zcol = (Z_A if mode == "hgrn" else Z_B) // 4

    def zspec(k):
        return pl.BlockSpec((tc, MIX_W), lambda b, c: (rb0 + b * nc + c, zcol + k))

    const = lambda b, c: (0, 0)
    state_spec = pl.BlockSpec((1, 4, HEAD_W, HEAD_W), lambda b, c: (b, 0, 0, 0))
    if mode == "hgrn":
        body = functools.partial(_hgrn_body, tc=tc)
        in_specs = [zspec(0), zspec(1), zspec(2), zspec(3),
                    pl.BlockSpec((1, MIX_W), const), pl.BlockSpec((1, HEAD_W), const), state_spec]
        args = (z, z, z, z, extra[0], nw, s0)
    else:
        body = functools.partial(_gla_body, tc=tc)
        in_specs = [zspec(0), zspec(1), zspec(2), zspec(3),
                    pl.BlockSpec((tc, LANES), lambda b, c: (rb0 + b * nc + c, Z_SMALL)),
                    pl.BlockSpec((LANES, MIX_W), const), pl.BlockSpec((1, MIX_W), const),
                    pl.BlockSpec((1, HEAD_W), const), state_spec]
        args = (z, z, z, z, z, extra[0], extra[1], nw, s0)
    return pl.pallas_call(
        body,
        grid=(nseq, nc),
        in_specs=in_specs,
        out_specs=[pl.BlockSpec((tc, MIX_W), lambda b, c: (b * nc + c, 0)), state_spec],
        out_shape=[jax.ShapeDtypeStruct((nseq * t, MIX_W), F32),
                   jax.ShapeDtypeStruct((nseq, 4, HEAD_W, HEAD_W), F32)],
        scratch_shapes=[pltpu.VMEM((4, HEAD_W, HEAD_W), F32)],
        compiler_params=_params(("parallel", "arbitrary"), 32),
        name=mode + "_mixer",
    )(*args)


def _dsa_body(qc_ref, qi_ref, sm_ref, kc_ref, vc_ref, ki_ref, o_ref,
              sk_ref, qis_ref, wb_ref, qs_ref, m_ref, l_ref, acc_ref, *, tq, past, nkeys, topk):
    i = pl.program_id(1)
    qpos0 = past + i * tq
    last_chunk = (qpos0 + tq - 1) // CHUNK
    n_adm = jnp.minimum((last_chunk + 1) * CHUNK, nkeys)
    nkb = (n_adm + KEY_BLOCK - 1) // KEY_BLOCK
    grp = C_HEADS // C_KV_HEADS

    for j in range(IDX_HEADS):
        qis_ref[j * tq:(j + 1) * tq, :] = qi_ref[:, j * IDX_DIM:(j + 1) * IDX_DIM]
        wb_ref[j] = jnp.broadcast_to(sm_ref[:, SMALL_IW + j:SMALL_IW + j + 1], (tq, KEY_BLOCK))
    for hq in range(C_HEADS):
        g, r = divmod(hq, grp)
        qs_ref[g, r * tq:(r + 1) * tq, :] = qc_ref[:, hq * C_HDIM:(hq + 1) * C_HDIM]

    qchunk = (qpos0 + lax.broadcasted_iota(jnp.int32, (tq, 1), 0)) >> 6
    lane = lax.broadcasted_iota(jnp.int32, (1, KEY_BLOCK), 1)

    def admissible(kb):
        kpos = kb * KEY_BLOCK + lane
        return ((kpos >> 6) <= qchunk) & (kpos < nkeys), kpos

    def score_step(kb, carry):
        kib = ki_ref[0, pl.ds(pl.multiple_of(kb * KEY_BLOCK, KEY_BLOCK), KEY_BLOCK), :]
        d = _dot_nt(qis_ref[...], kib)
        sc = jnp.maximum(d[0:tq], 0.0) * wb_ref[0]
        for j in range(1, IDX_HEADS):
            sc = sc + jnp.maximum(d[j * tq:(j + 1) * tq], 0.0) * wb_ref[j]
        adm, _ = admissible(kb)
        sc = jnp.where(adm, sc * IDX_SCALE, NEG) + 0.0
        bits = pltpu.bitcast(sc, jnp.int32)
        sk_ref[kb] = jnp.where(bits < 0, bits ^ 0x7FFFFFFF, bits)
        return carry

    lax.fori_loop(0, nkb, score_step, 0)

    def count(pred):
        def step(kb, acc):
            _, kpos = admissible(kb)
            return acc + jnp.where(pred(sk_ref[kb], kpos), 1.0, 0.0)
        acc = lax.fori_loop(0, nkb, step, jnp.zeros((tq, KEY_BLOCK), F32))
        return jnp.sum(acc, axis=1, keepdims=True)

    kf = float(topk)
    c0 = count(lambda s, p: s >= 0)
    thr = jnp.where(c0 >= kf, 0, INT_MIN).astype(jnp.int32)

    def bit_step(it, thr):
        cand = thr | jnp.left_shift(jnp.int32(1), 30 - it)
        c = count(lambda s, p: s >= cand)
        return jnp.where(c >= kf, cand, thr)

    thr = lax.fori_loop(0, 31, bit_step, thr)

    c_ge = count(lambda s, p: s >= thr)
    c_gt = count(lambda s, p: s > thr)
    need = kf - c_gt
    nbits = max(1, (nkeys - 1).bit_length())

    def tie_cut():
        def cut_step(it, cut):
            cand = cut | jnp.left_shift(jnp.int32(1), nbits - 1 - it)
            c = count(lambda s, p: (s == thr) & (p < cand))
            return jnp.where(c < need, cand, cut)
        return lax.fori_loop(0, nbits, cut_step, jnp.zeros((tq, 1), jnp.int32))

    cut = lax.cond(jnp.max(c_ge) > kf, tie_cut, lambda: jnp.full((tq, 1), 2 ** 30, jnp.int32))

    m_ref[...] = jnp.full(m_ref.shape, NEG, F32)
    l_ref[...] = jnp.zeros(l_ref.shape, F32)
    acc_ref[...] = jnp.zeros(acc_ref.shape, F32)

    def attn_step(kb, carry):
        adm, kpos = admissible(kb)
        s_key = sk_ref[kb]
        sel = ((s_key > thr) | ((s_key == thr) & (kpos <= cut))) & adm
        k0 = pl.multiple_of(kb * KEY_BLOCK, KEY_BLOCK)
        kblk = kc_ref[0, pl.ds(k0, KEY_BLOCK), :]
        vblk = vc_ref[0, pl.ds(k0, KEY_BLOCK), :]
        for g in range(C_KV_HEADS):
            gs = slice(g * C_HDIM, (g + 1) * C_HDIM)
            s = _dot_nt(qs_ref[g], kblk[:, gs]).reshape(grp, tq, KEY_BLOCK)
            s = jnp.where(sel[None], s, NEG)
            m_old = m_ref[g]
            m_new = jnp.maximum(m_old, jnp.max(s, axis=-1, keepdims=True))
            p = jnp.where(sel[None], jnp.exp(s - m_new), 0.0)
            alpha = jnp.exp(m_old - m_new)
            l_ref[g] = alpha * l_ref[g] + jnp.sum(p, axis=-1, keepdims=True)
            pv = _dot(p.reshape(grp * tq, KEY_BLOCK).astype(BF16), vblk[:, gs])
            acc_ref[g] = alpha * acc_ref[g] + pv.reshape(grp, tq, C_HDIM)
            m_ref[g] = m_new
        return carry

    lax.fori_loop(0, nkb, attn_step, 0)

    for hq in range(C_HEADS):
        g, r = divmod(hq, grp)
        o_ref[:, hq * C_HDIM:(hq + 1) * C_HDIM] = acc_ref[g, r] / l_ref[g, r]


SUPER = 4
ATTN_BLOCKS = 2


def _dsa_t_body(qc_ref, qi_ref, sm_ref, kc_ref, vt_ref, ki_ref, o_ref,
                sk_ref, hi_ref, lo_ref, qit_ref, qct_ref, wrow_ref, acc_ref, ot_ref, s0_ref, s1_ref,
                bias_ref, p_ref,
                *, tq, past, nkeys, topk):
    i = pl.program_id(1)
    qpos0 = past + i * tq
    last_chunk = (qpos0 + tq - 1) // CHUNK
    n_adm = jnp.minimum((last_chunk + 1) * CHUNK, nkeys)
    nkb = (n_adm + KEY_BLOCK - 1) // KEY_BLOCK
    nsb = (nkb + SUPER - 1) // SUPER
    grp = C_HEADS // C_KV_HEADS

    qit = qi_ref[...].astype(F32).T
    for j in range(IDX_HEADS):
        qit_ref[:, j * tq:(j + 1) * tq] = qit[j * IDX_DIM:(j + 1) * IDX_DIM, :].astype(BF16)
    qct = qc_ref[...].astype(F32).T
    zeros = jnp.zeros((C_HDIM, tq), BF16)
    for hq in range(C_HEADS):
        g, r = divmod(hq, grp)
        for gg in range(C_KV_HEADS):
            val = qct[hq * C_HDIM:(hq + 1) * C_HDIM, :].astype(BF16) if gg == g else zeros
            qct_ref[g, gg * C_HDIM:(gg + 1) * C_HDIM, r * tq:(r + 1) * tq] = val
    wrow_ref[...] = sm_ref[...].T

    qchunk = (qpos0 + lax.broadcasted_iota(jnp.int32, (1, tq), 1)) >> 6
    ksub = lax.broadcasted_iota(jnp.int32, (KEY_BLOCK, 1), 0)

    def admissible(kb):
        kpos = kb * KEY_BLOCK + ksub
        return ((kpos >> 6) <= qchunk) & (kpos < nkeys), kpos

    def score_step(kk, carry):
        for u in range(SUPER):
            kb = kk * SUPER + u
            kib = ki_ref[0, pl.ds(pl.multiple_of(kb * KEY_BLOCK, KEY_BLOCK), KEY_BLOCK), :]
            d = _dot(kib, qit_ref[...])
            sc = jnp.maximum(d[:, 0:tq], 0.0) * wrow_ref[SMALL_IW:SMALL_IW + 1, :]
            for j in range(1, IDX_HEADS):
                sc = sc + jnp.maximum(d[:, j * tq:(j + 1) * tq], 0.0) * wrow_ref[SMALL_IW + j:SMALL_IW + j + 1, :]
            adm, _ = admissible(kb)
            sc = jnp.where(adm, sc * IDX_SCALE, NEG) + 0.0
            bits = pltpu.bitcast(sc, jnp.int32)
            key = jnp.where(bits < 0, bits ^ 0x7FFFFFFF, bits)
            sk_ref[kb] = key
            hi_ref[kb] = (key >> 16).astype(jnp.int16)
            lo_ref[kb] = ((key & 0xFFFF) - 0x8000).astype(jnp.int16)
        return carry

    lax.fori_loop(0, nsb, score_step, 0)

    def fold(hit):
        return jnp.sum(hit.reshape(KEY_BLOCK // 8, 8, tq), axis=0)

    def count(pred):
        def step(kk, acc):
            for u in range(SUPER):
                kb = kk * SUPER + u
                acc = acc + fold(jnp.where(pred(sk_ref[kb], kb), 1.0, 0.0))
            return acc
        acc = lax.fori_loop(0, nsb, step, jnp.zeros((8, tq), F32))
        return jnp.sum(acc, axis=0, keepdims=True)

    def count16(ref, cand):
        c16 = cand.astype(jnp.int16)

        def tree(parts):
            while len(parts) > 1:
                parts = [a + b for a, b in zip(parts[0::2], parts[1::2])]
            return parts[0]

        def step(kk, accs):
            out = []
            for u in range(SUPER):
                hit = jnp.where(ref[kk * SUPER + u] >= c16, jnp.int16(1), jnp.int16(0))
                out.append(accs[u] + tree([hit[v * 16:(v + 1) * 16, :] for v in range(KEY_BLOCK // 16)]))
            return tuple(out)
        accs = lax.fori_loop(0, nsb, step, tuple(jnp.zeros((16, tq), jnp.int16) for _ in range(SUPER)))
        return jnp.sum(tree([a.astype(F32) for a in accs]), axis=0, keepdims=True)

    kf = float(topk)

    def select16(ref):
        c0 = count16(ref, jnp.zeros((1, tq), jnp.int32))
        t0 = jnp.where(c0 >= kf, 0, -0x8000).astype(jnp.int32)

        def bit_step(it, t):
            cand = t | jnp.left_shift(jnp.int32(1), 14 - it)
            return jnp.where(count16(ref, cand) >= kf, cand, t)
        return lax.fori_loop(0, 15, bit_step, t0)

    thr_hi = select16(hi_ref)
    thr_hi16 = thr_hi.astype(jnp.int16)

    def low_half_step(kk, carry):
        for u in range(SUPER):
            kb = kk * SUPER + u
            hi = hi_ref[kb]
            lo_ref[kb] = jnp.where(hi > thr_hi16, jnp.int16(0x7FFF),
                                   jnp.where(hi == thr_hi16, lo_ref[kb], jnp.int16(-0x8000)))
        return carry

    lax.fori_loop(0, nsb, low_half_step, 0)
    thr_lo = select16(lo_ref)
    thr = (thr_hi << 16) | (thr_lo + 0x8000)

    c_ge = count(lambda s, kb: s >= thr)
    c_gt = count(lambda s, kb: s > thr)
    need = kf - c_gt
    nbits = max(1, (nkeys - 1).bit_length())

    def tie_cut():
        def cut_step(it, cut):
            cand = cut | jnp.left_shift(jnp.int32(1), nbits - 1 - it)
            c = count(lambda s, kb: (s == thr) & (kb * KEY_BLOCK + ksub < cand))
            return jnp.where(c < need, cand, cut)
        return lax.fori_loop(0, nbits, cut_step, jnp.zeros((1, tq), jnp.int32))

    cut = lax.cond(jnp.max(c_ge) > kf, tie_cut, lambda: jnp.full((1, tq), 2 ** 30, jnp.int32))

    acc_ref[...] = jnp.zeros(acc_ref.shape, F32)
    init = tuple(jnp.full((1, grp * tq), NEG, F32) for _ in range(C_KV_HEADS))

    span = ATTN_BLOCKS * KEY_BLOCK
    n_attn = nsb * (SUPER // ATTN_BLOCKS)

    s_bufs = (s0_ref, s1_ref)

    def qk_scores(kk, half):
        kblk = kc_ref[0, pl.ds(pl.multiple_of(kk * span, span), span), :]
        for g in range(C_KV_HEADS):
            s_bufs[half][g] = _dot(kblk, qct_ref[g])

    def softmax_pv(kk, half, m):
        m = list(m)
        for u in range(ATTN_BLOCKS):
            kb = kk * ATTN_BLOCKS + u
            adm, kpos = admissible(kb)
            s_key = sk_ref[kb]
            sel = ((s_key > thr) | ((s_key == thr) & (kpos <= cut))) & adm
            bias_ref[half, u * KEY_BLOCK:(u + 1) * KEY_BLOCK, :] = jnp.where(sel, 0.0, NEG)
        for g in range(C_KV_HEADS):
            m_cols, alpha_cols = [], []
            for r in range(grp):
                cols = slice(r * tq, (r + 1) * tq)
                sb = s_bufs[half][g, :, cols] + bias_ref[half]
                m_old = m[g][:, cols]
                m_new = jnp.maximum(m_old, jnp.max(sb, axis=0, keepdims=True))
                p_ref[half, g, :, cols] = jnp.exp(sb - m_new).astype(BF16)
                m_cols.append(m_new)
                alpha_cols.append(jnp.exp(m_old - m_new))
            alpha = jnp.concatenate(alpha_cols, axis=1)
            rows = slice(g * VT_ROWS, (g + 1) * VT_ROWS)
            pv = _dot(vt_ref[0, kk * ATTN_BLOCKS, rows, :], p_ref[half, g, 0:KEY_BLOCK, :])
            for u in range(1, ATTN_BLOCKS):
                pv = pv + _dot(vt_ref[0, kk * ATTN_BLOCKS + u, rows, :],
                               p_ref[half, g, u * KEY_BLOCK:(u + 1) * KEY_BLOCK, :])
            acc_ref[g] = alpha * acc_ref[g] + pv
            m[g] = jnp.concatenate(m_cols, axis=1)
        return tuple(m)

    qk_scores(0, 0)

    def pair_step(j, m):
        qk_scores(2 * j + 1, 1)
        m = softmax_pv(2 * j, 0, m)
        qk_scores(jnp.minimum(2 * j + 2, n_attn - 2), 0)
        return softmax_pv(2 * j + 1, 1, m)

    lax.fori_loop(0, n_attn // 2, pair_step, init)
    for g in range(C_KV_HEADS):
        og = acc_ref[g, 0:C_HDIM, :] / acc_ref[g, C_HDIM:C_HDIM + 1, :]
        for r in range(grp):
            hq = g * grp + r
            ot_ref[hq * C_HDIM:(hq + 1) * C_HDIM, :] = og[:, r * tq:(r + 1) * tq]
    o_ref[...] = ot_ref[...].T


def _dsa_t(qc, qi, z, row0, keys, vals_t, kidx, nseq, t, past, nkeys):
    tq = KEY_BLOCK
    nq = t // tq
    rb0 = row0 // tq
    lp = keys.shape[1]
    topk = min(TOPK_MAX, nkeys // 4)
    grp = C_HEADS // C_KV_HEADS
    qrow = lambda b, i: (rb0 + b * nq + i, 0)
    return pl.pallas_call(
        functools.partial(_dsa_t_body, tq=tq, past=past, nkeys=nkeys, topk=topk),
        grid=(nseq, nq),
        in_specs=[
            pl.BlockSpec((tq, C_WIDTH), qrow),
            pl.BlockSpec((tq, IDX_HEADS * IDX_DIM), qrow),
            pl.BlockSpec((tq, LANES), lambda b, i: (rb0 + b * nq + i, Z_SMALL)),
            pl.BlockSpec((1, lp, LANES), lambda b, i: (b, 0, 0)),
            pl.BlockSpec((1, lp // KEY_BLOCK, C_KV_HEADS * VT_ROWS, KEY_BLOCK), lambda b, i: (b, 0, 0, 0)),
            pl.BlockSpec((1, lp, IDX_DIM), lambda b, i: (b, 0, 0)),
        ],
        out_specs=pl.BlockSpec((tq, C_WIDTH), lambda b, i: (b * nq + i, 0)),
        out_shape=jax.ShapeDtypeStruct((nseq * t, C_WIDTH), F32),
        scratch_shapes=[
            pltpu.VMEM((lp // KEY_BLOCK, KEY_BLOCK, tq), jnp.int32),
            pltpu.VMEM((lp // KEY_BLOCK, KEY_BLOCK, tq), jnp.int16),
            pltpu.VMEM((lp // KEY_BLOCK, KEY_BLOCK, tq), jnp.int16),
            pltpu.VMEM((IDX_DIM, IDX_HEADS * tq), BF16),
            pltpu.VMEM((C_KV_HEADS, C_KV_HEADS * C_HDIM, grp * tq), BF16),
            pltpu.VMEM((LANES, tq), F32),
            pltpu.VMEM((C_KV_HEADS, VT_ROWS, grp * tq), F32),
            pltpu.VMEM((C_WIDTH, tq), F32),
            pltpu.VMEM((C_KV_HEADS, ATTN_BLOCKS * KEY_BLOCK, grp * tq), F32),
            pltpu.VMEM((C_KV_HEADS, ATTN_BLOCKS * KEY_BLOCK, grp * tq), F32),
            pltpu.VMEM((2, ATTN_BLOCKS * KEY_BLOCK, tq), F32),
            pltpu.VMEM((2, C_KV_HEADS, ATTN_BLOCKS * KEY_BLOCK, grp * tq), BF16),
        ],
        compiler_params=_params(("parallel", "arbitrary"), 48),
        name="dsa_mixer_t",
    )(qc, qi, z, keys, vals_t, kidx)


def _dsa(qc, qi, z, row0, keys, vals, kidx, nseq, t, past, nkeys):
    tq = min(128, t)
    nq = t // tq
    rb0 = row0 // tq
    lp = keys.shape[1]
    topk = min(TOPK_MAX, nkeys // 4)
    grp = C_HEADS // C_KV_HEADS
    qrow = lambda b, i: (rb0 + b * nq + i, 0)
    seq = lambda b, i: (b, 0, 0)
    return pl.pallas_call(
        functools.partial(_dsa_body, tq=tq, past=past, nkeys=nkeys, topk=topk),
        grid=(nseq, nq),
        in_specs=[
            pl.BlockSpec((tq, C_WIDTH), qrow),
            pl.BlockSpec((tq, IDX_HEADS * IDX_DIM), qrow),
            pl.BlockSpec((tq, LANES), lambda b, i: (rb0 + b * nq + i, Z_SMALL)),
            pl.BlockSpec((1, lp, LANES), seq),
            pl.BlockSpec((1, lp, LANES), seq),
            pl.BlockSpec((1, lp, IDX_DIM), seq),
        ],
        out_specs=pl.BlockSpec((tq, C_WIDTH), lambda b, i: (b * nq + i, 0)),
        out_shape=jax.ShapeDtypeStruct((nseq * t, C_WIDTH), F32),
        scratch_shapes=[
            pltpu.VMEM((lp // KEY_BLOCK, tq, KEY_BLOCK), jnp.int32),
            pltpu.VMEM((IDX_HEADS * tq, IDX_DIM), BF16),
            pltpu.VMEM((IDX_HEADS, tq, KEY_BLOCK), F32),
            pltpu.VMEM((C_KV_HEADS, grp * tq, C_HDIM), BF16),
            pltpu.VMEM((C_KV_HEADS, grp, tq, 1), F32),
            pltpu.VMEM((C_KV_HEADS, grp, tq, 1), F32),
            pltpu.VMEM((C_KV_HEADS, grp, tq, C_HDIM), F32),
        ],
        compiler_params=_params(("parallel", "arbitrary"), 48),
        name="dsa_mixer",
    )(qc, qi, z, keys, vals, kidx)


def _merge_body(h_ref, ya_ref, yb_ref, oc_ref, ga_ref, gb_ref, gc_ref,
                wa_ref, wb_ref, wc_ref, wo_ref, o_ref):
    merged = (jax.nn.sigmoid(ga_ref[...]) * _dot(ya_ref[...].astype(BF16), wa_ref[...])
              + jax.nn.sigmoid(gb_ref[...]) * _dot(yb_ref[...].astype(BF16), wb_ref[...])
              + jax.nn.sigmoid(gc_ref[...]) * _dot(oc_ref[...].astype(BF16), wc_ref[...]))
    o_ref[...] = h_ref[...] + _dot(merged.astype(BF16), wo_ref[...])


def _merge(h, ya, yb, oc, z, wa, wb, wc, wo):
    n, d = h.shape
    tm = min(TOKEN_TILE, n)
    row = lambda i: (i, 0)
    const = lambda i: (0, 0)
    return pl.pallas_call(
        _merge_body,
        grid=(n // tm,),
        in_specs=[
            pl.BlockSpec((tm, d), row),
            pl.BlockSpec((tm, MIX_W), row),
            pl.BlockSpec((tm, MIX_W), row),
            pl.BlockSpec((tm, MIX_W), row),
            pl.BlockSpec((tm, d), lambda i: (i, 0)),
            pl.BlockSpec((tm, d), lambda i: (i, 1)),
            pl.BlockSpec((tm, d), lambda i: (i, 2)),
            pl.BlockSpec((MIX_W, d), const),
            pl.BlockSpec((MIX_W, d), const),
            pl.BlockSpec((MIX_W, d), const),
            pl.BlockSpec((d, d), const),
        ],
        out_specs=pl.BlockSpec((tm, d), row),
        out_shape=jax.ShapeDtypeStruct((n, d), F32),
        compiler_params=_params(("parallel",), 48),
        name="gated_merge",
    )(h, ya, yb, oc, z, z, z, wa, wb, wc, wo)


def _pad_heads(w, heads, dim):
    lead = w.shape[:-1]
    w = w.reshape(*lead, heads, dim)
    w = jnp.pad(w, [(0, 0)] * len(lead) + [(0, 0), (0, HEAD_W - dim)])
    return w.reshape(*lead, heads * HEAD_W)


def _layout_w_in(w):
    widths = (512, 512, 512, 512, 256, 256, 512, 16, 512, 512, 128, 128, 512, 64, 8, 1024, 1024, 1024)
    parts, o = [], 0
    for wd in widths:
        parts.append(w[:, o:o + wd])
        o += wd
    (a_q, a_f, a_i, a_g, b_q, b_k, b_v, b_r, b_g, c_q, c_k, c_v, i_q, i_k, i_w, g_a, g_b, g_c) = parts
    d = w.shape[0]
    small = jnp.concatenate([b_r, i_w, jnp.zeros((d, LANES - 24), w.dtype)], axis=1)
    i_k = jnp.pad(i_k, ((0, 0), (0, LANES - IDX_DIM)))
    cols = [g_a, g_b, g_c, a_q, a_f, a_i, a_g,
            _pad_heads(b_q, B_HEADS, B_KDIM), _pad_heads(b_k, B_HEADS, B_KDIM), b_v, b_g,
            c_q, i_q, c_k, c_v, i_k, small]
    out = jnp.concatenate(cols, axis=1)
    assert out.shape[1] == Z_WIDTH
    return out.astype(BF16)


def _rope_tables(pos):
    half = ROPE_DIMS // 2
    inv = ROPE_THETA ** (-jnp.arange(half, dtype=F32) * (2.0 / ROPE_DIMS))
    ang = pos.astype(F32)[:, None] * inv[None, :]
    cos, sin = jnp.cos(ang), jnp.sin(ang)
    n = pos.shape[0]
    ones = jnp.ones((n, C_HDIM - ROPE_DIMS), F32)
    zeros = jnp.zeros((n, C_HDIM - ROPE_DIMS), F32)
    zh = jnp.zeros((n, half), F32)
    c = jnp.concatenate([cos, cos, ones], axis=1)
    s_lo = jnp.concatenate([-sin, zh, zeros], axis=1)
    s_hi = jnp.concatenate([zh, sin, zeros], axis=1)
    two = lambda a: jnp.concatenate([a, a], axis=1)
    return two(c), two(s_lo), two(s_hi)


def kernel(x_prompt, x_sample, state_hgrn, state_gla, cache_k, cache_v, cache_kidx, hgrn_lb, w_in, w_gla_up, b_gla, norm_hgrn, norm_gla, w_br_a, w_br_b, w_br_c, w_out, norm_ffn1, norm_mix, norm_ffn2, ffn1_w1, ffn1_w3, ffn1_w2, ffn2_w1, ffn2_w3, ffn2_w2, norm_final):
    bp, tp, d = x_prompt.shape
    bs, ts, _ = x_sample.shape
    past = cache_k.shape[2]
    n_p, n_s = bp * tp, bs * ts

    lb_sm = jax.nn.softmax(hgrn_lb.astype(F32), axis=0)
    lb_all = jnp.concatenate([jnp.zeros_like(lb_sm[:1]), jnp.cumsum(lb_sm[1:], axis=0)], axis=0)

    pos = jnp.concatenate([jnp.tile(jnp.arange(tp, dtype=jnp.int32), bp),
                           jnp.tile(past + jnp.arange(ts, dtype=jnp.int32), bs)])
    cos_t, sin_lo, sin_hi = _rope_tables(pos)

    h = jnp.concatenate([x_prompt.reshape(n_p, d), x_sample.reshape(n_s, d)], axis=0)
    row2 = lambda a: a.reshape(1, -1)
    zero_state = jnp.zeros((bp, 4, HEAD_W, HEAD_W), F32)
    lp_s = -(-(past + ts) // KEY_BLOCK) * KEY_BLOCK
    key_pad = lp_s - past - ts

    outs = {k: [] for k in ("pa", "pb", "pk", "pv", "pki", "sa", "sb", "sk", "sv", "ski")}
    for l in range(DEPTH):
        bf = lambda a: a[l].astype(BF16)
        h = _ffn(h, row2(norm_ffn1[l]), bf(ffn1_w1), bf(ffn1_w3), bf(ffn1_w2), row2(norm_final), False)
        z = _inproj(h, row2(norm_mix[l]), _layout_w_in(w_in[l]))
        qc, qi, kc, vc, ki, kcb, vcb, kib, vt = _prep(z, cos_t, sin_lo, sin_hi)

        lb = row2(lb_all[l])
        nwa, nwb = row2(norm_hgrn[l]), row2(norm_gla[l])
        wup = jnp.pad(_pad_heads(w_gla_up[l], B_HEADS, B_KDIM), ((0, LANES - B_GATE_RANK), (0, 0)))
        bup = row2(_pad_heads(b_gla[l], B_HEADS, B_KDIM))
        sb0 = jnp.pad(state_gla[l], ((0, 0), (0, 0), (0, HEAD_W - B_KDIM), (0, 0)))

        ya_p, sa_p = _recurrent_mixer("hgrn", z, 0, bp, tp, zero_state, nwa, (lb,))
        ya_s, sa_s = _recurrent_mixer("hgrn", z, n_p, bs, ts, state_hgrn[l], nwa, (lb,))
        yb_p, sb_p = _recurrent_mixer("gla", z, 0, bp, tp, zero_state, nwb, (wup, bup))
        yb_s, sb_s = _recurrent_mixer("gla", z, n_p, bs, ts, sb0, nwb, (wup, bup))

        seqs = lambda a, n, t: a.reshape(n, t, a.shape[-1])
        vt_p = vt[:n_p // KEY_BLOCK].reshape(bp, tp // KEY_BLOCK, C_KV_HEADS * VT_ROWS, KEY_BLOCK)
        oc_p = _dsa_t(qc, qi, z, 0, seqs(kcb[:n_p], bp, tp), vt_p, seqs(kib[:n_p], bp, tp), bp, tp, 0, tp)

        def with_cache(cache, new):
            full = jnp.concatenate([cache.reshape(bs, past, -1).astype(BF16), seqs(new[n_p:], bs, ts)], axis=1)
            return jnp.pad(full, ((0, 0), (0, key_pad), (0, 0)))

        oc_s = _dsa(qc, qi, z, n_p, with_cache(cache_k[l], kcb), with_cache(cache_v[l], vcb),
                    with_cache(cache_kidx[l], kib), bs, ts, past, past + ts)

        ya = jnp.concatenate([ya_p, ya_s], axis=0)
        yb = jnp.concatenate([yb_p, yb_s], axis=0)
        oc = jnp.concatenate([oc_p, oc_s], axis=0)
        h = _merge(h, ya, yb, oc, z, bf(w_br_a), bf(w_br_b), bf(w_br_c), bf(w_out))
        h = _ffn(h, row2(norm_ffn2[l]), bf(ffn2_w1), bf(ffn2_w3), bf(ffn2_w2), row2(norm_final),
                 l == DEPTH - 1)

        outs["pa"].append(sa_p)
        outs["sa"].append(sa_s)
        outs["pb"].append(sb_p[:, :, :B_KDIM, :])
        outs["sb"].append(sb_s[:, :, :B_KDIM, :])
        outs["pk"].append(kc[:n_p].reshape(bp, tp, C_KV_HEADS, C_HDIM))
        outs["pv"].append(vc[:n_p].reshape(bp, tp, C_KV_HEADS, C_HDIM))
        outs["pki"].append(ki[:n_p].reshape(bp, tp, IDX_DIM))
        outs["sk"].append(kc[n_p:].reshape(bs, ts, C_KV_HEADS, C_HDIM))
        outs["sv"].append(vc[n_p:].reshape(bs, ts, C_KV_HEADS, C_HDIM))
        outs["ski"].append(ki[n_p:].reshape(bs, ts, IDX_DIM))

    st = {k: jnp.stack(v) for k, v in outs.items()}
    return (h[:n_p].reshape(bp, tp, d), h[n_p:].reshape(bs, ts, d),
            st["pa"], st["pb"], st["pk"], st["pv"], st["pki"],
            st["sa"], st["sb"], st["sk"], st["sv"], st["ski"])
```

```python
import functools

import jax
import jax.numpy as jnp
from jax import lax
from jax.experimental import pallas as pl
from jax.experimental.pallas import tpu as pltpu

F32 = jnp.float32
BF16 = jnp.bfloat16

D_MODEL = 1024
DEPTH = 2
CHUNK = 64
EPS = 1e-6
NEG = -1e30
F_MIN = 1e-30
A_HEADS = 4
A_KDIM = 128
A_VDIM = 128
B_HEADS = 4
B_KDIM = 64
B_VDIM = 128
B_GATE_RANK = 16
B_TAU = 16.0
C_HEADS = 8
C_KV_HEADS = 2
C_HDIM = 64
C_WIDTH = C_HEADS * C_HDIM
IDX_HEADS = 8
IDX_DIM = 64
IDX_SCALE = (IDX_HEADS * IDX_DIM) ** -0.5
TOPK_MAX = 256
ROPE_THETA = 500000.0
ROPE_DIMS = C_HDIM // 4
GLA_BLOCK = 16
D_FF = 2816

LANES = 128
HEAD_W = 128
MIX_W = 4 * HEAD_W
TOKEN_TILE = 512
FF_TILE = D_FF // 2
KEY_BLOCK = 128
VT_ROWS = 80
INT_MIN = -(2 ** 31)

Z_GATES = 0
Z_A = 24
Z_B = 40
Z_CQ = 56
Z_IQ = 60
Z_CKVI = 64
Z_SMALL = 67
Z_WIDTH = 68 * LANES
SMALL_IW = B_GATE_RANK


def _params(sem, vmem_mb):
    return pltpu.CompilerParams(dimension_semantics=sem, vmem_limit_bytes=vmem_mb << 20)


def _dot(a, b):
    return jnp.dot(a, b, preferred_element_type=F32)


def _dot_nt(a, b):
    return lax.dot_general(a, b, (((1,), (1,)), ((), ())), preferred_element_type=F32)


def _dot_tn(a, b):
    return lax.dot_general(a, b, (((0,), (0,)), ((), ())), preferred_element_type=F32)


def _split3(x):
    h1 = x.astype(BF16)
    r1 = x - h1.astype(F32)
    h2 = r1.astype(BF16)
    h3 = (r1 - h2.astype(F32)).astype(BF16)
    return h1, h2, h3


def _dot_exact_lhs(m, x):
    h1, h2, h3 = _split3(x)
    return _dot(m, h1) + _dot(m, h2) + _dot(m, h3)


def _dot_hi(a, b):
    a1 = a.astype(BF16)
    a2 = (a - a1.astype(F32)).astype(BF16)
    b1 = b.astype(BF16)
    b2 = (b - b1.astype(F32)).astype(BF16)
    return _dot(a1, b1) + _dot(a1, b2) + _dot(a2, b1)


def _rmsnorm(x, g):
    return x * lax.rsqrt(jnp.mean(x * x, axis=-1, keepdims=True) + EPS) * g


def _ffn_body(x_ref, g_ref, w1_ref, w3_ref, w2_ref, gf_ref, o_ref, u_ref, acc_ref, *, final_norm):
    j = pl.program_id(1)

    @pl.when(j == 0)
    def _():
        u_ref[...] = _rmsnorm(x_ref[...], g_ref[...]).astype(BF16)
        acc_ref[...] = jnp.zeros_like(acc_ref)

    u = u_ref[...]
    a = _dot(u, w1_ref[...])
    b = _dot(u, w3_ref[...])
    hid = (a * jax.nn.sigmoid(a) * b).astype(BF16)
    acc_ref[...] += _dot(hid, w2_ref[...])

    @pl.when(j == pl.num_programs(1) - 1)
    def _():
        out = x_ref[...] + 0.5 * acc_ref[...]
        if final_norm:
            out = _rmsnorm(out, gf_ref[...])
        o_ref[...] = out


def _ffn(x, g, w1, w3, w2, gf, final_norm):
    n, d = x.shape
    dff = w1.shape[1]
    tm = min(TOKEN_TILE, n)
    tf = FF_TILE if dff == D_FF else dff
    return pl.pallas_call(
        functools.partial(_ffn_body, final_norm=final_norm),
        grid=(n // tm, dff // tf),
        in_specs=[
            pl.BlockSpec((tm, d), lambda i, j: (i, 0)),
            pl.BlockSpec((1, d), lambda i, j: (0, 0)),
            pl.BlockSpec((d, tf), lambda i, j: (0, j)),
            pl.BlockSpec((d, tf), lambda i, j: (0, j)),
            pl.BlockSpec((tf, d), lambda i, j: (j, 0)),
            pl.BlockSpec((1, d), lambda i, j: (0, 0)),
        ],
        out_specs=pl.BlockSpec((tm, d), lambda i, j: (i, 0)),
        out_shape=jax.ShapeDtypeStruct((n, d), F32),
        scratch_shapes=[pltpu.VMEM((tm, d), BF16), pltpu.VMEM((tm, d), F32)],
        compiler_params=_params(("parallel", "arbitrary"), 48),
        name="ffn_half_step",
    )(x, g, w1, w3, w2, gf)


def _inproj_body(x_ref, g_ref, w_ref, z_ref, u_ref):
    @pl.when(pl.program_id(1) == 0)
    def _():
        u_ref[...] = _rmsnorm(x_ref[...], g_ref[...]).astype(BF16)

    z_ref[...] = _dot(u_ref[...], w_ref[...])


def _inproj(h, g, w):
    n, d = h.shape
    zw = w.shape[1]
    tm = min(TOKEN_TILE, n)
    tn = 512
    return pl.pallas_call(
        _inproj_body,
        grid=(n // tm, zw // tn),
        in_specs=[
            pl.BlockSpec((tm, d), lambda i, j: (i, 0)),
            pl.BlockSpec((1, d), lambda i, j: (0, 0)),
            pl.BlockSpec((d, tn), lambda i, j: (0, j)),
        ],
        out_specs=pl.BlockSpec((tm, tn), lambda i, j: (i, j)),
        out_shape=jax.ShapeDtypeStruct((n, zw), F32),
        scratch_shapes=[pltpu.VMEM((tm, d), BF16)],
        compiler_params=_params(("parallel", "arbitrary"), 32),
        name="in_projection",
    )(h, g, w)


def _rope(x, c, s_lo, s_hi):
    w = x.shape[1]
    rep = w // LANES
    if rep > 1:
        c = jnp.concatenate([c] * rep, axis=1)
        s_lo = jnp.concatenate([s_lo] * rep, axis=1)
        s_hi = jnp.concatenate([s_hi] * rep, axis=1)
    half = ROPE_DIMS // 2
    return x * c + pltpu.roll(x, half, 1) * s_hi + pltpu.roll(x, w - half, 1) * s_lo


def _prep_body(cq_ref, iq_ref, kv_ref, c_ref, slo_ref, shi_ref,
               qc_ref, qi_ref, kc_ref, vc_ref, ki_ref, kcb_ref, vcb_ref, kib_ref, vt_ref):
    c, s_lo, s_hi = c_ref[...], slo_ref[...], shi_ref[...]
    qc_ref[...] = (_rope(cq_ref[...], c, s_lo, s_hi) * (C_HDIM ** -0.5)).astype(BF16)
    qi_ref[...] = _rope(iq_ref[...], c, s_lo, s_hi).astype(BF16)
    kc = _rope(kv_ref[:, 0:LANES], c, s_lo, s_hi)
    vc = kv_ref[:, LANES:2 * LANES]
    ki = _rope(kv_ref[:, 2 * LANES:3 * LANES], c, s_lo, s_hi)[:, :IDX_DIM]
    kc_ref[...] = kc
    vc_ref[...] = vc
    ki_ref[...] = ki
    kcb_ref[...] = kc.astype(BF16)
    vcb_ref[...] = vc.astype(BF16)
    kib_ref[...] = ki.astype(BF16)
    ones = jnp.ones((VT_ROWS - C_HDIM, KEY_BLOCK), BF16)
    for kk in range(vt_ref.shape[0]):
        vt = vc[kk * KEY_BLOCK:(kk + 1) * KEY_BLOCK, :].T.astype(BF16)
        vt_ref[kk] = jnp.concatenate([vt[:C_HDIM], ones, vt[C_HDIM:], ones], axis=0)


def _prep(z, cos_t, sin_lo, sin_hi):
    n = z.shape[0]
    tm = min(TOKEN_TILE, n)
    row = lambda i: (i, 0)
    return pl.pallas_call(
        _prep_body,
        grid=(n // tm,),
        in_specs=[
            pl.BlockSpec((tm, MIX_W), lambda i: (i, Z_CQ // 4)),
            pl.BlockSpec((tm, MIX_W), lambda i: (i, Z_IQ // 4)),
            pl.BlockSpec((tm, MIX_W), lambda i: (i, Z_CKVI // 4)),
            pl.BlockSpec((tm, LANES), row),
            pl.BlockSpec((tm, LANES), row),
            pl.BlockSpec((tm, LANES), row),
        ],
        out_specs=[
            pl.BlockSpec((tm, C_WIDTH), row),
            pl.BlockSpec((tm, IDX_HEADS * IDX_DIM), row),
            pl.BlockSpec((tm, LANES), row),
            pl.BlockSpec((tm, LANES), row),
            pl.BlockSpec((tm, IDX_DIM), row),
            pl.BlockSpec((tm, LANES), row),
            pl.BlockSpec((tm, LANES), row),
            pl.BlockSpec((tm, IDX_DIM), row),
            pl.BlockSpec((tm // KEY_BLOCK, C_KV_HEADS * VT_ROWS, KEY_BLOCK), lambda i: (i, 0, 0)),
        ],
        out_shape=[
            jax.ShapeDtypeStruct((n, C_WIDTH), BF16),
            jax.ShapeDtypeStruct((n, IDX_HEADS * IDX_DIM), BF16),
            jax.ShapeDtypeStruct((n, LANES), F32),
            jax.ShapeDtypeStruct((n, LANES), F32),
            jax.ShapeDtypeStruct((n, IDX_DIM), F32),
            jax.ShapeDtypeStruct((n, LANES), BF16),
            jax.ShapeDtypeStruct((n, LANES), BF16),
            jax.ShapeDtypeStruct((n, IDX_DIM), BF16),
            jax.ShapeDtypeStruct((n // KEY_BLOCK, C_KV_HEADS * VT_ROWS, KEY_BLOCK), BF16),
        ],
        compiler_params=_params(("parallel",), 32),
        name="rotary_kv_staging",
    )(z, z, z, cos_t, sin_lo, sin_hi)


def _gla_head(q, k, v, logf, st_ref, h, tc):
    row = lax.broadcasted_iota(jnp.int32, (tc, tc), 0)
    col = lax.broadcasted_iota(jnp.int32, (tc, tc), 1)
    tril = (col <= row).astype(BF16)
    cum = _dot_exact_lhs(tril, logf)

    nb = tc // GLA_BLOCK
    b3 = cum.reshape(nb, GLA_BLOCK, HEAD_W)
    q3 = q.reshape(nb, GLA_BLOCK, HEAD_W)
    k3 = k.reshape(nb, GLA_BLOCK, HEAD_W)
    v3 = v.reshape(nb, GLA_BLOCK, HEAD_W)
    tloc = lax.broadcasted_iota(jnp.int32, (nb, GLA_BLOCK, 1), 1)
    o3 = jnp.zeros((nb, GLA_BLOCK, HEAD_W), F32)
    for j in range(GLA_BLOCK):
        causal = tloc >= j
        decay = jnp.exp(jnp.where(causal, b3 - b3[:, j:j + 1, :], 0.0))
        w = jnp.sum(q3 * k3[:, j:j + 1, :] * decay, axis=-1, keepdims=True)
        o3 = o3 + jnp.where(causal, w, 0.0) * v3[:, j:j + 1, :]
    o = o3.reshape(tc, HEAD_W)

    vb = v.astype(BF16)
    attn = jnp.zeros((tc, tc), F32)
    half = tc // 2
    while half >= GLA_BLOCK:
        blk = 2 * half
        nblk = tc // blk
        bl = cum.reshape(nblk, blk, HEAD_W)
        x = bl - bl[:, half - 1:half, :]
        second = lax.broadcasted_iota(jnp.int32, (nblk, blk, 1), 1) >= half
        e = jnp.exp(jnp.where(second, x, -x))
        qt = jnp.where(second, q.reshape(nblk, blk, HEAD_W) * e, 0.0).reshape(tc, HEAD_W).astype(BF16)
        kt = jnp.where(second, 0.0, k.reshape(nblk, blk, HEAD_W) * e).reshape(tc, HEAD_W).astype(BF16)
        shift = blk.bit_length() - 1
        same = (row >> shift) == (col >> shift)
        attn = attn + jnp.where(same, _dot_nt(qt, kt), 0.0)
        half //= 2
    if tc > GLA_BLOCK:
        o = o + _dot(attn.astype(BF16), vb)

    st = st_ref[h]
    o = o + _dot_nt((q * jnp.exp(cum)).astype(BF16), st.astype(BF16))
    last = cum[tc - 1:tc, :]
    kd = (k * jnp.exp(last - cum)).astype(BF16)
    st_ref[h] = st * jnp.exp(last) + _dot_tn(vb, kd)
    return o


def _gla_finish(o, nw, gate):
    return _rmsnorm(o, nw) * (gate * jax.nn.sigmoid(gate))


def _gla_state_io(c, s0_ref, st_ref, heads):
    @pl.when(c == 0)
    def _():
        for h in range(heads):
            st_ref[h] = s0_ref[0, h].T


def _gla_state_out(c, sout_ref, st_ref, heads):
    @pl.when(c == pl.num_programs(1) - 1)
    def _():
        for h in range(heads):
            sout_ref[0, h] = st_ref[h].T


def _hgrn_body(q_ref, f_ref, v_ref, g_ref, lb_ref, nw_ref, s0_ref, y_ref, sout_ref, st_ref, *, tc):
    c = pl.program_id(1)
    _gla_state_io(c, s0_ref, st_ref, A_HEADS)
    for h in range(A_HEADS):
        hs = slice(h * HEAD_W, (h + 1) * HEAD_W)
        zf = f_ref[:, hs]
        lb = lb_ref[:, hs]
        f = lb + (1.0 - lb) * jax.nn.sigmoid(zf)
        logf = jnp.log(jnp.maximum(f, F_MIN))
        k = (1.0 - lb) * jax.nn.sigmoid(-zf)
        zq = q_ref[:, hs]
        q = zq * jax.nn.sigmoid(zq) * (A_KDIM ** -0.5)
        o = _gla_head(q, k, v_ref[:, hs], logf, st_ref, h, tc)
        y_ref[:, hs] = _gla_finish(o, nw_ref[...], g_ref[:, hs])
    _gla_state_out(c, sout_ref, st_ref, A_HEADS)


def _gla_body(q_ref, k_ref, v_ref, g_ref, r_ref, wup_ref, bup_ref, nw_ref, s0_ref,
              y_ref, sout_ref, st_ref, *, tc):
    c = pl.program_id(1)
    _gla_state_io(c, s0_ref, st_ref, B_HEADS)
    r = _dot_hi(r_ref[...], wup_ref[...]) + bup_ref[...]
    logf_all = (jnp.minimum(r, 0.0) - jnp.log1p(jnp.exp(-jnp.abs(r)))) / B_TAU
    for h in range(B_HEADS):
        hs = slice(h * HEAD_W, (h + 1) * HEAD_W)
        q = q_ref[:, hs] * (B_KDIM ** -0.5)
        o = _gla_head(q, k_ref[:, hs], v_ref[:, hs], logf_all[:, hs], st_ref, h, tc)
        y_ref[:, hs] = _gla_finish(o, nw_ref[...], g_ref[:, hs])
    _gla_state_out(c, sout_ref, st_ref, B_HEADS)


def _recurrent_mixer(mode, z, row0, nseq, t, s0, nw, extra):
    tc = min(128, t)
    nc = t // tc
    rb0 = row0 // tc
    zcol = (Z_A if mode == "hgrn" else Z_B) // 4

    def zspec(k):
        return pl.BlockSpec((tc, MIX_W), lambda b, c: (rb0 + b * nc + c, zcol + k))

    const = lambda b, c: (0, 0)
    state_spec = pl.BlockSpec((1, 4, HEAD_W, HEAD_W), lambda b, c: (b, 0, 0, 0))
    if mode == "hgrn":
        body = functools.partial(_hgrn_body, tc=tc)
        in_specs = [zspec(0), zspec(1), zspec(2), zspec(3),
                    pl.BlockSpec((1, MIX_W), const), pl.BlockSpec((1, HEAD_W), const), state_spec]
        args = (z, z, z, z, extra[0], nw, s0)
    else:
        body = functools.partial(_gla_body, tc=tc)
        in_specs = [zspec(0), zspec(1), zspec(2), zspec(3),
                    pl.BlockSpec((tc, LANES), lambda b, c: (rb0 + b * nc + c, Z_SMALL)),
                    pl.BlockSpec((LANES, MIX_W), const), pl.BlockSpec((1, MIX_W), const),
                    pl.BlockSpec((1, HEAD_W), const), state_spec]
        args = (z, z, z, z, z, extra[0], extra[1], nw, s0)
    return pl.pallas_call(
        body,
        grid=(nseq, nc),
        in_specs=in_specs,
        out_specs=[pl.BlockSpec((tc, MIX_W), lambda b, c: (b * nc + c, 0)), state_spec],
        out_shape=[jax.ShapeDtypeStruct((nseq * t, MIX_W), F32),
                   jax.ShapeDtypeStruct((nseq, 4, HEAD_W, HEAD_W), F32)],
        scratch_shapes=[pltpu.VMEM((4, HEAD_W, HEAD_W), F32)],
        compiler_params=_params(("parallel", "arbitrary"), 32),
        name=mode + "_mixer",
    )(*args)


def _dsa_body(qc_ref, qi_ref, sm_ref, kc_ref, vc_ref, ki_ref, o_ref,
              sk_ref, qis_ref, wb_ref, qs_ref, m_ref, l_ref, acc_ref, *, tq, past, nkeys, topk):
    i = pl.program_id(1)
    qpos0 = past + i * tq
    last_chunk = (qpos0 + tq - 1) // CHUNK
    n_adm = jnp.minimum((last_chunk + 1) * CHUNK, nkeys)
    nkb = (n_adm + KEY_BLOCK - 1) // KEY_BLOCK
    grp = C_HEADS // C_KV_HEADS

    for j in range(IDX_HEADS):
        qis_ref[j * tq:(j + 1) * tq, :] = qi_ref[:, j * IDX_DIM:(j + 1) * IDX_DIM]
        wb_ref[j] = jnp.broadcast_to(sm_ref[:, SMALL_IW + j:SMALL_IW + j + 1], (tq, KEY_BLOCK))
    for hq in range(C_HEADS):
        g, r = divmod(hq, grp)
        qs_ref[g, r * tq:(r + 1) * tq, :] = qc_ref[:, hq * C_HDIM:(hq + 1) * C_HDIM]

    qchunk = (qpos0 + lax.broadcasted_iota(jnp.int32, (tq, 1), 0)) >> 6
    lane = lax.broadcasted_iota(jnp.int32, (1, KEY_BLOCK), 1)

    def admissible(kb):
        kpos = kb * KEY_BLOCK + lane
        return ((kpos >> 6) <= qchunk) & (kpos < nkeys), kpos

    def score_step(kb, carry):
        kib = ki_ref[0, pl.ds(pl.multiple_of(kb * KEY_BLOCK, KEY_BLOCK), KEY_BLOCK), :]
        d = _dot_nt(qis_ref[...], kib)
        sc = jnp.maximum(d[0:tq], 0.0) * wb_ref[0]
        for j in range(1, IDX_HEADS):
            sc = sc + jnp.maximum(d[j * tq:(j + 1) * tq], 0.0) * wb_ref[j]
        adm, _ = admissible(kb)
        sc = jnp.where(adm, sc * IDX_SCALE, NEG) + 0.0
        bits = pltpu.bitcast(sc, jnp.int32)
        sk_ref[kb] = jnp.where(bits < 0, bits ^ 0x7FFFFFFF, bits)
        return carry

    lax.fori_loop(0, nkb, score_step, 0)

    def count(pred):
        def step(kb, acc):
            _, kpos = admissible(kb)
            return acc + jnp.where(pred(sk_ref[kb], kpos), 1.0, 0.0)
        acc = lax.fori_loop(0, nkb, step, jnp.zeros((tq, KEY_BLOCK), F32))
        return jnp.sum(acc, axis=1, keepdims=True)

    kf = float(topk)
    c0 = count(lambda s, p: s >= 0)
    thr = jnp.where(c0 >= kf, 0, INT_MIN).astype(jnp.int32)

    def bit_step(it, thr):
        cand = thr | jnp.left_shift(jnp.int32(1), 30 - it)
        c = count(lambda s, p: s >= cand)
        return jnp.where(c >= kf, cand, thr)

    thr = lax.fori_loop(0, 31, bit_step, thr)

    c_ge = count(lambda s, p: s >= thr)
    c_gt = count(lambda s, p: s > thr)
    need = kf - c_gt
    nbits = max(1, (nkeys - 1).bit_length())

    def tie_cut():
        def cut_step(it, cut):
            cand = cut | jnp.left_shift(jnp.int32(1), nbits - 1 - it)
            c = count(lambda s, p: (s == thr) & (p < cand))
            return jnp.where(c < need, cand, cut)
        return lax.fori_loop(0, nbits, cut_step, jnp.zeros((tq, 1), jnp.int32))

    cut = lax.cond(jnp.max(c_ge) > kf, tie_cut, lambda: jnp.full((tq, 1), 2 ** 30, jnp.int32))

    m_ref[...] = jnp.full(m_ref.shape, NEG, F32)
    l_ref[...] = jnp.zeros(l_ref.shape, F32)
    acc_ref[...] = jnp.zeros(acc_ref.shape, F32)

    def attn_step(kb, carry):
        adm, kpos = admissible(kb)
        s_key = sk_ref[kb]
        sel = ((s_key > thr) | ((s_key == thr) & (kpos <= cut))) & adm
        k0 = pl.multiple_of(kb * KEY_BLOCK, KEY_BLOCK)
        kblk = kc_ref[0, pl.ds(k0, KEY_BLOCK), :]
        vblk = vc_ref[0, pl.ds(k0, KEY_BLOCK), :]
        for g in range(C_KV_HEADS):
            gs = slice(g * C_HDIM, (g + 1) * C_HDIM)
            s = _dot_nt(qs_ref[g], kblk[:, gs]).reshape(grp, tq, KEY_BLOCK)
            s = jnp.where(sel[None], s, NEG)
            m_old = m_ref[g]
            m_new = jnp.maximum(m_old, jnp.max(s, axis=-1, keepdims=True))
            p = jnp.where(sel[None], jnp.exp(s - m_new), 0.0)
            alpha = jnp.exp(m_old - m_new)
            l_ref[g] = alpha * l_ref[g] + jnp.sum(p, axis=-1, keepdims=True)
            pv = _dot(p.reshape(grp * tq, KEY_BLOCK).astype(BF16), vblk[:, gs])
            acc_ref[g] = alpha * acc_ref[g] + pv.reshape(grp, tq, C_HDIM)
            m_ref[g] = m_new
        return carry

    lax.fori_loop(0, nkb, attn_step, 0)

    for hq in range(C_HEADS):
        g, r = divmod(hq, grp)
        o_ref[:, hq * C_HDIM:(hq + 1) * C_HDIM] = acc_ref[g, r] / l_ref[g, r]


SUPER = 4
ATTN_BLOCKS = 2


def _dsa_t_body(qc_ref, qi_ref, sm_ref, kc_ref, vt_ref, ki_ref, o_ref,
                sk_ref, hi_ref, lo_ref, qit_ref, qct_ref, wrow_ref, acc_ref, ot_ref, s0_ref, s1_ref,
                bias_ref, p_ref,
                *, tq, past, nkeys, topk):
    i = pl.program_id(1)
    qpos0 = past + i * tq
    last_chunk = (qpos0 + tq - 1) // CHUNK
    n_adm = jnp.minimum((last_chunk + 1) * CHUNK, nkeys)
    nkb = (n_adm + KEY_BLOCK - 1) // KEY_BLOCK
    nsb = (nkb + SUPER - 1) // SUPER
    grp = C_HEADS // C_KV_HEADS

    qit = qi_ref[...].astype(F32).T
    for j in range(IDX_HEADS):
        qit_ref[:, j * tq:(j + 1) * tq] = qit[j * IDX_DIM:(j + 1) * IDX_DIM, :].astype(BF16)
    qct = qc_ref[...].astype(F32).T
    zeros = jnp.zeros((C_HDIM, tq), BF16)
    for hq in range(C_HEADS):
        g, r = divmod(hq, grp)
        for gg in range(C_KV_HEADS):
            val = qct[hq * C_HDIM:(hq + 1) * C_HDIM, :].astype(BF16) if gg == g else zeros
            qct_ref[g, gg * C_HDIM:(gg + 1) * C_HDIM, r * tq:(r + 1) * tq] = val
    wrow_ref[...] = sm_ref[...].T

    qchunk = (qpos0 + lax.broadcasted_iota(jnp.int32, (1, tq), 1)) >> 6
    ksub = lax.broadcasted_iota(jnp.int32, (KEY_BLOCK, 1), 0)

    def admissible(kb):
        kpos = kb * KEY_BLOCK + ksub
        return ((kpos >> 6) <= qchunk) & (kpos < nkeys)

    def score_step(kk, carry):
        for u in range(SUPER):
            kb = kk * SUPER + u
            kib = ki_ref[0, pl.ds(pl.multiple_of(kb * KEY_BLOCK, KEY_BLOCK), KEY_BLOCK), :]
            d = _dot(kib, qit_ref[...])
            sc = jnp.maximum(d[:, 0:tq], 0.0) * wrow_ref[SMALL_IW:SMALL_IW + 1, :]
            for j in range(1, IDX_HEADS):
                sc = sc + jnp.maximum(d[:, j * tq:(j + 1) * tq], 0.0) * wrow_ref[SMALL_IW + j:SMALL_IW + j + 1, :]
            bits = pltpu.bitcast(sc * IDX_SCALE + 0.0, jnp.int32)
            key = jnp.where(admissible(kb), jnp.where(bits < 0, bits ^ 0x7FFFFFFF, bits), INT_MIN)
            sk_ref[kb] = key
            hi_ref[kb] = (key >> 16).astype(jnp.int16)
            lo_ref[kb] = ((key & 0xFFFF) - 0x8000).astype(jnp.int16)
        return carry

    lax.fori_loop(0, nsb, score_step, 0)

    def fold(hit):
        return jnp.sum(hit.reshape(KEY_BLOCK // 8, 8, tq), axis=0)

    def count(pred):
        def step(kk, acc):
            for u in range(SUPER):
                kb = kk * SUPER + u
                acc = acc + fold(jnp.where(pred(sk_ref[kb], kb), 1.0, 0.0))
            return acc
        acc = lax.fori_loop(0, nsb, step, jnp.zeros((8, tq), F32))
        return jnp.sum(acc, axis=0, keepdims=True)

    def count16(ref, cand):
        c16 = cand.astype(jnp.int16)

        def tree(parts):
            while len(parts) > 1:
                parts = [a + b for a, b in zip(parts[0::2], parts[1::2])]
            return parts[0]

        def step(kk, accs):
            out = []
            for u in range(SUPER):
                hit = jnp.where(ref[kk * SUPER + u] >= c16, jnp.int16(1), jnp.int16(0))
                out.append(accs[u] + tree([hit[v * 16:(v + 1) * 16, :] for v in range(KEY_BLOCK // 16)]))
            return tuple(out)
        accs = lax.fori_loop(0, nsb, step, tuple(jnp.zeros((16, tq), jnp.int16) for _ in range(SUPER)))
        return jnp.sum(tree([a.astype(F32) for a in accs]), axis=0, keepdims=True)

    kf = float(topk)

    def select16(ref):
        c0 = count16(ref, jnp.zeros((1, tq), jnp.int32))
        t0 = jnp.where(c0 >= kf, 0, -0x8000).astype(jnp.int32)

        def bit_step(it, t):
            cand = t | jnp.left_shift(jnp.int32(1), 14 - it)
            return jnp.where(count16(ref, cand) >= kf, cand, t)
        return lax.fori_loop(0, 15, bit_step, t0)

    thr_hi = select16(hi_ref)
    thr_hi16 = thr_hi.astype(jnp.int16)

    def low_half_step(kk, carry):
        for u in range(SUPER):
            kb = kk * SUPER + u
            hi = hi_ref[kb]
            lo_ref[kb] = jnp.where(hi > thr_hi16, jnp.int16(0x7FFF),
                                   jnp.where(hi == thr_hi16, lo_ref[kb], jnp.int16(-0x8000)))
        return carry

    lax.fori_loop(0, nsb, low_half_step, 0)
    thr_lo = select16(lo_ref)
    thr = jnp.maximum((thr_hi << 16) | (thr_lo + 0x8000), INT_MIN + 1)

    c_ge = count(lambda s, kb: s >= thr)
    nbits = max(1, (nkeys - 1).bit_length())

    @pl.when(jnp.max(c_ge) > kf)
    def _():
        need = kf - count(lambda s, kb: s > thr)

        def cut_step(it, cut):
            cand = cut | jnp.left_shift(jnp.int32(1), nbits - 1 - it)
            c = count(lambda s, kb: (s == thr) & (kb * KEY_BLOCK + ksub < cand))
            return jnp.where(c < need, cand, cut)
        cut = lax.fori_loop(0, nbits, cut_step, jnp.zeros((1, tq), jnp.int32))

        def strike(kb, carry):
            key = sk_ref[kb]
            sk_ref[kb] = jnp.where((key == thr) & (kb * KEY_BLOCK + ksub > cut), INT_MIN, key)
            return carry
        lax.fori_loop(0, nsb * SUPER, strike, 0)

    acc_ref[...] = jnp.zeros(acc_ref.shape, F32)
    init = tuple(jnp.full((1, grp * tq), NEG, F32) for _ in range(C_KV_HEADS))

    span = ATTN_BLOCKS * KEY_BLOCK
    n_attn = nsb * (SUPER // ATTN_BLOCKS)

    s_bufs = (s0_ref, s1_ref)

    def qk_scores(kk, half):
        kblk = kc_ref[0, pl.ds(pl.multiple_of(kk * span, span), span), :]
        for g in range(C_KV_HEADS):
            s_bufs[half][g] = _dot(kblk, qct_ref[g])

    def softmax_pv(kk, half, m):
        m = list(m)
        for u in range(ATTN_BLOCKS):
            sel = sk_ref[kk * ATTN_BLOCKS + u] >= thr
            bias_ref[half, u * KEY_BLOCK:(u + 1) * KEY_BLOCK, :] = jnp.where(sel, 0.0, NEG)
        for g in range(C_KV_HEADS):
            m_cols, alpha_cols = [], []
            for r in range(grp):
                cols = slice(r * tq, (r + 1) * tq)
                sb = s_bufs[half][g, :, cols] + bias_ref[half]
                m_old = m[g][:, cols]
                m_new = jnp.maximum(m_old, jnp.max(sb, axis=0, keepdims=True))
                p_ref[half, g, :, cols] = jnp.exp(sb - m_new).astype(BF16)
                m_cols.append(m_new)
                alpha_cols.append(jnp.exp(m_old - m_new))
            alpha = jnp.concatenate(alpha_cols, axis=1)
            rows = slice(g * VT_ROWS, (g + 1) * VT_ROWS)
            pv = _dot(vt_ref[0, kk * ATTN_BLOCKS, rows, :], p_ref[half, g, 0:KEY_BLOCK, :])
            for u in range(1, ATTN_BLOCKS):
                pv = pv + _dot(vt_ref[0, kk * ATTN_BLOCKS + u, rows, :],
                               p_ref[half, g, u * KEY_BLOCK:(u + 1) * KEY_BLOCK, :])
            acc_ref[g] = alpha * acc_ref[g] + pv
            m[g] = jnp.concatenate(m_cols, axis=1)
        return tuple(m)

    qk_scores(0, 0)

    def pair_step(j, m):
        qk_scores(2 * j + 1, 1)
        m = softmax_pv(2 * j, 0, m)
        qk_scores(jnp.minimum(2 * j + 2, n_attn - 2), 0)
        return softmax_pv(2 * j + 1, 1, m)

    lax.fori_loop(0, n_attn // 2, pair_step, init)
    for g in range(C_KV_HEADS):
        og = acc_ref[g, 0:C_HDIM, :] / acc_ref[g, C_HDIM:C_HDIM + 1, :]
        for r in range(grp):
            hq = g * grp + r
            ot_ref[hq * C_HDIM:(hq + 1) * C_HDIM, :] = og[:, r * tq:(r + 1) * tq]
    o_ref[...] = ot_ref[...].T


def _dsa_t(qc, qi, z, row0, keys, vals_t, kidx, nseq, t, past, nkeys):
    tq = KEY_BLOCK
    nq = t // tq
    rb0 = row0 // tq
    lp = keys.shape[1]
    topk = min(TOPK_MAX, nkeys // 4)
    grp = C_HEADS // C_KV_HEADS
    qrow = lambda b, i: (rb0 + b * nq + i, 0)
    return pl.pallas_call(
        functools.partial(_dsa_t_body, tq=tq, past=past, nkeys=nkeys, topk=topk),
        grid=(nseq, nq),
        in_specs=[
            pl.BlockSpec((tq, C_WIDTH), qrow),
            pl.BlockSpec((tq, IDX_HEADS * IDX_DIM), qrow),
            pl.BlockSpec((tq, LANES), lambda b, i: (rb0 + b * nq + i, Z_SMALL)),
            pl.BlockSpec((1, lp, LANES), lambda b, i: (b, 0, 0)),
            pl.BlockSpec((1, lp // KEY_BLOCK, C_KV_HEADS * VT_ROWS, KEY_BLOCK), lambda b, i: (b, 0, 0, 0)),
            pl.BlockSpec((1, lp, IDX_DIM), lambda b, i: (b, 0, 0)),
        ],
        out_specs=pl.BlockSpec((tq, C_WIDTH), lambda b, i: (b * nq + i, 0)),
        out_shape=jax.ShapeDtypeStruct((nseq * t, C_WIDTH), F32),
        scratch_shapes=[
            pltpu.VMEM((lp // KEY_BLOCK, KEY_BLOCK, tq), jnp.int32),
            pltpu.VMEM((lp // KEY_BLOCK, KEY_BLOCK, tq), jnp.int16),
            pltpu.VMEM((lp // KEY_BLOCK, KEY_BLOCK, tq), jnp.int16),
            pltpu.VMEM((IDX_DIM, IDX_HEADS * tq), BF16),
            pltpu.VMEM((C_KV_HEADS, C_KV_HEADS * C_HDIM, grp * tq), BF16),
            pltpu.VMEM((LANES, tq), F32),
            pltpu.VMEM((C_KV_HEADS, VT_ROWS, grp * tq), F32),
            pltpu.VMEM((C_WIDTH, tq), F32),
            pltpu.VMEM((C_KV_HEADS, ATTN_BLOCKS * KEY_BLOCK, grp * tq), F32),
            pltpu.VMEM((C_KV_HEADS, ATTN_BLOCKS * KEY_BLOCK, grp * tq), F32),
            pltpu.VMEM((2, ATTN_BLOCKS * KEY_BLOCK, tq), F32),
            pltpu.VMEM((2, C_KV_HEADS, ATTN_BLOCKS * KEY_BLOCK, grp * tq), BF16),
        ],
        compiler_params=_params(("parallel", "arbitrary"), 48),
        name="dsa_mixer_t",
    )(qc, qi, z, keys, vals_t, kidx)


def _dsa(qc, qi, z, row0, keys, vals, kidx, nseq, t, past, nkeys):
    tq = min(128, t)
    nq = t // tq
    rb0 = row0 // tq
    lp = keys.shape[1]
    topk = min(TOPK_MAX, nkeys // 4)
    grp = C_HEADS // C_KV_HEADS
    qrow = lambda b, i: (rb0 + b * nq + i, 0)
    seq = lambda b, i: (b, 0, 0)
    return pl.pallas_call(
        functools.partial(_dsa_body, tq=tq, past=past, nkeys=nkeys, topk=topk),
        grid=(nseq, nq),
        in_specs=[
            pl.BlockSpec((tq, C_WIDTH), qrow),
            pl.BlockSpec((tq, IDX_HEADS * IDX_DIM), qrow),
            pl.BlockSpec((tq, LANES), lambda b, i: (rb0 + b * nq + i, Z_SMALL)),
            pl.BlockSpec((1, lp, LANES), seq),
            pl.BlockSpec((1, lp, LANES), seq),
            pl.BlockSpec((1, lp, IDX_DIM), seq),
        ],
        out_specs=pl.BlockSpec((tq, C_WIDTH), lambda b, i: (b * nq + i, 0)),
        out_shape=jax.ShapeDtypeStruct((nseq * t, C_WIDTH), F32),
        scratch_shapes=[
            pltpu.VMEM((lp // KEY_BLOCK, tq, KEY_BLOCK), jnp.int32),
            pltpu.VMEM((IDX_HEADS * tq, IDX_DIM), BF16),
            pltpu.VMEM((IDX_HEADS, tq, KEY_BLOCK), F32),
            pltpu.VMEM((C_KV_HEADS, grp * tq, C_HDIM), BF16),
            pltpu.VMEM((C_KV_HEADS, grp, tq, 1), F32),
            pltpu.VMEM((C_KV_HEADS, grp, tq, 1), F32),
            pltpu.VMEM((C_KV_HEADS, grp, tq, C_HDIM), F32),
        ],
        compiler_params=_params(("parallel", "arbitrary"), 48),
        name="dsa_mixer",
    )(qc, qi, z, keys, vals, kidx)


def _merge_body(h_ref, ya_ref, yb_ref, oc_ref, ga_ref, gb_ref, gc_ref,
                wa_ref, wb_ref, wc_ref, wo_ref, o_ref):
    merged = (jax.nn.sigmoid(ga_ref[...]) * _dot(ya_ref[...].astype(BF16), wa_ref[...])
              + jax.nn.sigmoid(gb_ref[...]) * _dot(yb_ref[...].astype(BF16), wb_ref[...])
              + jax.nn.sigmoid(gc_ref[...]) * _dot(oc_ref[...].astype(BF16), wc_ref[...]))
    o_ref[...] = h_ref[...] + _dot(merged.astype(BF16), wo_ref[...])


def _merge(h, ya, yb, oc, z, wa, wb, wc, wo):
    n, d = h.shape
    tm = min(TOKEN_TILE, n)
    row = lambda i: (i, 0)
    const = lambda i: (0, 0)
    return pl.pallas_call(
        _merge_body,
        grid=(n // tm,),
        in_specs=[
            pl.BlockSpec((tm, d), row),
            pl.BlockSpec((tm, MIX_W), row),
            pl.BlockSpec((tm, MIX_W), row),
            pl.BlockSpec((tm, MIX_W), row),
            pl.BlockSpec((tm, d), lambda i: (i, 0)),
            pl.BlockSpec((tm, d), lambda i: (i, 1)),
            pl.BlockSpec((tm, d), lambda i: (i, 2)),
            pl.BlockSpec((MIX_W, d), const),
            pl.BlockSpec((MIX_W, d), const),
            pl.BlockSpec((MIX_W, d), const),
            pl.BlockSpec((d, d), const),
        ],
        out_specs=pl.BlockSpec((tm, d), row),
        out_shape=jax.ShapeDtypeStruct((n, d), F32),
        compiler_params=_params(("parallel",), 48),
        name="gated_merge",
    )(h, ya, yb, oc, z, z, z, wa, wb, wc, wo)


def _pad_heads(w, heads, dim):
    lead = w.shape[:-1]
    w = w.reshape(*lead, heads, dim)
    w = jnp.pad(w, [(0, 0)] * len(lead) + [(0, 0), (0, HEAD_W - dim)])
    return w.reshape(*lead, heads * HEAD_W)


def _layout_w_in(w):
    widths = (512, 512, 512, 512, 256, 256, 512, 16, 512, 512, 128, 128, 512, 64, 8, 1024, 1024, 1024)
    parts, o = [], 0
    for wd in widths:
        parts.append(w[:, o:o + wd])
        o += wd
    (a_q, a_f, a_i, a_g, b_q, b_k, b_v, b_r, b_g, c_q, c_k, c_v, i_q, i_k, i_w, g_a, g_b, g_c) = parts
    d = w.shape[0]
    small = jnp.concatenate([b_r, i_w, jnp.zeros((d, LANES - 24), w.dtype)], axis=1)
    i_k = jnp.pad(i_k, ((0, 0), (0, LANES - IDX_DIM)))
    cols = [g_a, g_b, g_c, a_q, a_f, a_i, a_g,
            _pad_heads(b_q, B_HEADS, B_KDIM), _pad_heads(b_k, B_HEADS, B_KDIM), b_v, b_g,
            c_q, i_q, c_k, c_v, i_k, small]
    out = jnp.concatenate(cols, axis=1)
    assert out.shape[1] == Z_WIDTH
    return out.astype(BF16)


def _rope_tables(pos):
    half = ROPE_DIMS // 2
    inv = ROPE_THETA ** (-jnp.arange(half, dtype=F32) * (2.0 / ROPE_DIMS))
    ang = pos.astype(F32)[:, None] * inv[None, :]
    cos, sin = jnp.cos(ang), jnp.sin(ang)
    n = pos.shape[0]
    ones = jnp.ones((n, C_HDIM - ROPE_DIMS), F32)
    zeros = jnp.zeros((n, C_HDIM - ROPE_DIMS), F32)
    zh = jnp.zeros((n, half), F32)
    c = jnp.concatenate([cos, cos, ones], axis=1)
    s_lo = jnp.concatenate([-sin, zh, zeros], axis=1)
    s_hi = jnp.concatenate([zh, sin, zeros], axis=1)
    two = lambda a: jnp.concatenate([a, a], axis=1)
    return two(c), two(s_lo), two(s_hi)


def kernel(x_prompt, x_sample, state_hgrn, state_gla, cache_k, cache_v, cache_kidx, hgrn_lb, w_in, w_gla_up, b_gla, norm_hgrn, norm_gla, w_br_a, w_br_b, w_br_c, w_out, norm_ffn1, norm_mix, norm_ffn2, ffn1_w1, ffn1_w3, ffn1_w2, ffn2_w1, ffn2_w3, ffn2_w2, norm_final):
    bp, tp, d = x_prompt.shape
    bs, ts, _ = x_sample.shape
    past = cache_k.shape[2]
    n_p, n_s = bp * tp, bs * ts

    lb_sm = jax.nn.softmax(hgrn_lb.astype(F32), axis=0)
    lb_all = jnp.concatenate([jnp.zeros_like(lb_sm[:1]), jnp.cumsum(lb_sm[1:], axis=0)], axis=0)

    pos = jnp.concatenate([jnp.tile(jnp.arange(tp, dtype=jnp.int32), bp),
                           jnp.tile(past + jnp.arange(ts, dtype=jnp.int32), bs)])
    cos_t, sin_lo, sin_hi = _rope_tables(pos)

    h = jnp.concatenate([x_prompt.reshape(n_p, d), x_sample.reshape(n_s, d)], axis=0)
    row2 = lambda a: a.reshape(1, -1)
    zero_state = jnp.zeros((bp, 4, HEAD_W, HEAD_W), F32)
    lp_s = -(-(past + ts) // KEY_BLOCK) * KEY_BLOCK
    key_pad = lp_s - past - ts

    outs = {k: [] for k in ("pa", "pb", "pk", "pv", "pki", "sa", "sb", "sk", "sv", "ski")}
    for l in range(DEPTH):
        bf = lambda a: a[l].astype(BF16)
        h = _ffn(h, row2(norm_ffn1[l]), bf(ffn1_w1), bf(ffn1_w3), bf(ffn1_w2), row2(norm_final), False)
        z = _inproj(h, row2(norm_mix[l]), _layout_w_in(w_in[l]))
        qc, qi, kc, vc, ki, kcb, vcb, kib, vt = _prep(z, cos_t, sin_lo, sin_hi)

        lb = row2(lb_all[l])
        nwa, nwb = row2(norm_hgrn[l]), row2(norm_gla[l])
        wup = jnp.pad(_pad_heads(w_gla_up[l], B_HEADS, B_KDIM), ((0, LANES - B_GATE_RANK), (0, 0)))
        bup = row2(_pad_heads(b_gla[l], B_HEADS, B_KDIM))
        sb0 = jnp.pad(state_gla[l], ((0, 0), (0, 0), (0, HEAD_W - B_KDIM), (0, 0)))

        ya_p, sa_p = _recurrent_mixer("hgrn", z, 0, bp, tp, zero_state, nwa, (lb,))
        ya_s, sa_s = _recurrent_mixer("hgrn", z, n_p, bs, ts, state_hgrn[l], nwa, (lb,))
        yb_p, sb_p = _recurrent_mixer("gla", z, 0, bp, tp, zero_state, nwb, (wup, bup))
        yb_s, sb_s = _recurrent_mixer("gla", z, n_p, bs, ts, sb0, nwb, (wup, bup))

        seqs = lambda a, n, t: a.reshape(n, t, a.shape[-1])
        vt_p = vt[:n_p // KEY_BLOCK].reshape(bp, tp // KEY_BLOCK, C_KV_HEADS * VT_ROWS, KEY_BLOCK)
        oc_p = _dsa_t(qc, qi, z, 0, seqs(kcb[:n_p], bp, tp), vt_p, seqs(kib[:n_p], bp, tp), bp, tp, 0, tp)

        def with_cache(cache, new):
            full = jnp.concatenate([cache.reshape(bs, past, -1).astype(BF16), seqs(new[n_p:], bs, ts)], axis=1)
            return jnp.pad(full, ((0, 0), (0, key_pad), (0, 0)))

        oc_s = _dsa(qc, qi, z, n_p, with_cache(cache_k[l], kcb), with_cache(cache_v[l], vcb),
                    with_cache(cache_kidx[l], kib), bs, ts, past, past + ts)

        ya = jnp.concatenate([ya_p, ya_s], axis=0)
        yb = jnp.concatenate([yb_p, yb_s], axis=0)
        oc = jnp.concatenate([oc_p, oc_s], axis=0)
        h = _merge(h, ya, yb, oc, z, bf(w_br_a), bf(w_br_b), bf(w_br_c), bf(w_out))
        h = _ffn(h, row2(norm_ffn2[l]), bf(ffn2_w1), bf(ffn2_w3), bf(ffn2_w2), row2(norm_final),
                 l == DEPTH - 1)

        outs["pa"].append(sa_p)
        outs["sa"].append(sa_s)
        outs["pb"].append(sb_p[:, :, :B_KDIM, :])
        outs["sb"].append(sb_s[:, :, :B_KDIM, :])
        outs["pk"].append(kc[:n_p].reshape(bp, tp, C_KV_HEADS, C_HDIM))
        outs["pv"].append(vc[:n_p].reshape(bp, tp, C_KV_HEADS, C_HDIM))
        outs["pki"].append(ki[:n_p].reshape(bp, tp, IDX_DIM))
        outs["sk"].append(kc[n_p:].reshape(bs, ts, C_KV_HEADS, C_HDIM))
        outs["sv"].append(vc[n_p:].reshape(bs, ts, C_KV_HEADS, C_HDIM))
        outs["ski"].append(ki[n_p:].reshape(bs, ts, IDX_DIM))

    st = {k: jnp.stack(v) for k, v in outs.items()}
    return (h[:n_p].reshape(bp, tp, d), h[n_p:].reshape(bs, ts, d),
            st["pa"], st["pb"], st["pk"], st["pv"], st["pki"],
            st["sa"], st["sb"], st["sk"], st["sv"], st["ski"])
```

```python
import functools

import jax
import jax.numpy as jnp
from jax import lax
from jax.experimental import pallas as pl
from jax.experimental.pallas import tpu as pltpu

F32 = jnp.float32
BF16 = jnp.bfloat16

D_MODEL = 1024
DEPTH = 2
CHUNK = 64
EPS = 1e-6
NEG = -1e30
F_MIN = 1e-30
A_HEADS = 4
A_KDIM = 128
A_VDIM = 128
B_HEADS = 4
B_KDIM = 64
B_VDIM = 128
B_GATE_RANK = 16
B_TAU = 16.0
C_HEADS = 8
C_KV_HEADS = 2
C_HDIM = 64
C_WIDTH = C_HEADS * C_HDIM
IDX_HEADS = 8
IDX_DIM = 64
IDX_SCALE = (IDX_HEADS * IDX_DIM) ** -0.5
TOPK_MAX = 256
ROPE_THETA = 500000.0
ROPE_DIMS = C_HDIM // 4
GLA_BLOCK = 16
D_FF = 2816

LANES = 128
HEAD_W = 128
MIX_W = 4 * HEAD_W
TOKEN_TILE = 512
FF_TILE = D_FF // 2
KEY_BLOCK = 128
VT_ROWS = 80
INT_MIN = -(2 ** 31)

Z_GATES = 0
Z_A = 24
Z_B = 40
Z_CQ = 56
Z_IQ = 60
Z_CKVI = 64
Z_SMALL = 67
Z_WIDTH = 68 * LANES
SMALL_IW = B_GATE_RANK


def _params(sem, vmem_mb):
    return pltpu.CompilerParams(dimension_semantics=sem, vmem_limit_bytes=vmem_mb << 20)


def _dot(a, b):
    return jnp.dot(a, b, preferred_element_type=F32)


def _dot_nt(a, b):
    return lax.dot_general(a, b, (((1,), (1,)), ((), ())), preferred_element_type=F32)


def _dot_tn(a, b):
    return lax.dot_general(a, b, (((0,), (0,)), ((), ())), preferred_element_type=F32)


def _split3(x):
    h1 = x.astype(BF16)
    r1 = x - h1.astype(F32)
    h2 = r1.astype(BF16)
    h3 = (r1 - h2.astype(F32)).astype(BF16)
    return h1, h2, h3


def _dot_exact_lhs(m, x):
    h1, h2, h3 = _split3(x)
    return _dot(m, h1) + _dot(m, h2) + _dot(m, h3)


def _dot_hi(a, b):
    a1 = a.astype(BF16)
    a2 = (a - a1.astype(F32)).astype(BF16)
    b1 = b.astype(BF16)
    b2 = (b - b1.astype(F32)).astype(BF16)
    return _dot(a1, b1) + _dot(a1, b2) + _dot(a2, b1)


def _rmsnorm(x, g):
    return x * lax.rsqrt(jnp.mean(x * x, axis=-1, keepdims=True) + EPS) * g


def _ffn_body(x_ref, g_ref, w1_ref, w3_ref, w2_ref, gf_ref, o_ref, u_ref, acc_ref, *, final_norm):
    j = pl.program_id(1)

    @pl.when(j == 0)
    def _():
        u_ref[...] = _rmsnorm(x_ref[...], g_ref[...]).astype(BF16)
        acc_ref[...] = jnp.zeros_like(acc_ref)

    u = u_ref[...]
    a = _dot(u, w1_ref[...])
    b = _dot(u, w3_ref[...])
    hid = (a * jax.nn.sigmoid(a) * b).astype(BF16)
    acc_ref[...] += _dot(hid, w2_ref[...])

    @pl.when(j == pl.num_programs(1) - 1)
    def _():
        out = x_ref[...] + 0.5 * acc_ref[...]
        if final_norm:
            out = _rmsnorm(out, gf_ref[...])
        o_ref[...] = out


def _ffn(x, g, w1, w3, w2, gf, final_norm):
    n, d = x.shape
    dff = w1.shape[1]
    tm = min(TOKEN_TILE, n)
    tf = FF_TILE if dff == D_FF else dff
    return pl.pallas_call(
        functools.partial(_ffn_body, final_norm=final_norm),
        grid=(n // tm, dff // tf),
        in_specs=[
            pl.BlockSpec((tm, d), lambda i, j: (i, 0)),
            pl.BlockSpec((1, d), lambda i, j: (0, 0)),
            pl.BlockSpec((d, tf), lambda i, j: (0, j)),
            pl.BlockSpec((d, tf), lambda i, j: (0, j)),
            pl.BlockSpec((tf, d), lambda i, j: (j, 0)),
            pl.BlockSpec((1, d), lambda i, j: (0, 0)),
        ],
        out_specs=pl.BlockSpec((tm, d), lambda i, j: (i, 0)),
        out_shape=jax.ShapeDtypeStruct((n, d), F32),
        scratch_shapes=[pltpu.VMEM((tm, d), BF16), pltpu.VMEM((tm, d), F32)],
        compiler_params=_params(("parallel", "arbitrary"), 48),
        name="ffn_half_step",
    )(x, g, w1, w3, w2, gf)


def _inproj_body(x_ref, g_ref, w_ref, z_ref, u_ref):
    @pl.when(pl.program_id(1) == 0)
    def _():
        u_ref[...] = _rmsnorm(x_ref[...], g_ref[...]).astype(BF16)

    z_ref[...] = _dot(u_ref[...], w_ref[...])


def _inproj(h, g, w):
    n, d = h.shape
    zw = w.shape[1]
    tm = min(TOKEN_TILE, n)
    tn = zw // 4 if zw % (4 * LANES) == 0 else zw
    return pl.pallas_call(
        _inproj_body,
        grid=(n // tm, zw // tn),
        in_specs=[
            pl.BlockSpec((tm, d), lambda i, j: (i, 0)),
            pl.BlockSpec((1, d), lambda i, j: (0, 0)),
            pl.BlockSpec((d, tn), lambda i, j: (0, j)),
        ],
        out_specs=pl.BlockSpec((tm, tn), lambda i, j: (i, j)),
        out_shape=jax.ShapeDtypeStruct((n, zw), F32),
        scratch_shapes=[pltpu.VMEM((tm, d), BF16)],
        compiler_params=_params(("parallel", "arbitrary"), 32),
        name="in_projection",
    )(h, g, w)


def _rope(x, c, s_lo, s_hi):
    w = x.shape[1]
    rep = w // LANES
    if rep > 1:
        c = jnp.concatenate([c] * rep, axis=1)
        s_lo = jnp.concatenate([s_lo] * rep, axis=1)
        s_hi = jnp.concatenate([s_hi] * rep, axis=1)
    half = ROPE_DIMS // 2
    return x * c + pltpu.roll(x, half, 1) * s_hi + pltpu.roll(x, w - half, 1) * s_lo


def _prep_body(cq_ref, iq_ref, kv_ref, c_ref, slo_ref, shi_ref,
               qc_ref, qi_ref, kc_ref, vc_ref, ki_ref, kcb_ref, vcb_ref, kib_ref, vt_ref):
    c, s_lo, s_hi = c_ref[...], slo_ref[...], shi_ref[...]
    qc_ref[...] = (_rope(cq_ref[...], c, s_lo, s_hi) * (C_HDIM ** -0.5)).astype(BF16)
    qi_ref[...] = _rope(iq_ref[...], c, s_lo, s_hi).astype(BF16)
    kc = _rope(kv_ref[:, 0:LANES], c, s_lo, s_hi)
    vc = kv_ref[:, LANES:2 * LANES]
    ki = _rope(kv_ref[:, 2 * LANES:3 * LANES], c, s_lo, s_hi)[:, :IDX_DIM]
    kc_ref[...] = kc
    vc_ref[...] = vc
    ki_ref[...] = ki
    kcb_ref[...] = kc.astype(BF16)
    vcb_ref[...] = vc.astype(BF16)
    kib_ref[...] = ki.astype(BF16)
    ones = jnp.ones((VT_ROWS - C_HDIM, KEY_BLOCK), BF16)
    for kk in range(vt_ref.shape[0]):
        vt = vc[kk * KEY_BLOCK:(kk + 1) * KEY_BLOCK, :].T.astype(BF16)
        vt_ref[kk] = jnp.concatenate([vt[:C_HDIM], ones, vt[C_HDIM:], ones], axis=0)


def _prep(z, cos_t, sin_lo, sin_hi):
    n = z.shape[0]
    tm = min(TOKEN_TILE, n)
    row = lambda i: (i, 0)
    return pl.pallas_call(
        _prep_body,
        grid=(n // tm,),
        in_specs=[
            pl.BlockSpec((tm, MIX_W), lambda i: (i, Z_CQ // 4)),
            pl.BlockSpec((tm, MIX_W), lambda i: (i, Z_IQ // 4)),
            pl.BlockSpec((tm, MIX_W), lambda i: (i, Z_CKVI // 4)),
            pl.BlockSpec((tm, LANES), row),
            pl.BlockSpec((tm, LANES), row),
            pl.BlockSpec((tm, LANES), row),
        ],
        out_specs=[
            pl.BlockSpec((tm, C_WIDTH), row),
            pl.BlockSpec((tm, IDX_HEADS * IDX_DIM), row),
            pl.BlockSpec((tm, LANES), row),
            pl.BlockSpec((tm, LANES), row),
            pl.BlockSpec((tm, IDX_DIM), row),
            pl.BlockSpec((tm, LANES), row),
            pl.BlockSpec((tm, LANES), row),
            pl.BlockSpec((tm, IDX_DIM), row),
            pl.BlockSpec((tm // KEY_BLOCK, C_KV_HEADS * VT_ROWS, KEY_BLOCK), lambda i: (i, 0, 0)),
        ],
        out_shape=[
            jax.ShapeDtypeStruct((n, C_WIDTH), BF16),
            jax.ShapeDtypeStruct((n, IDX_HEADS * IDX_DIM), BF16),
            jax.ShapeDtypeStruct((n, LANES), F32),
            jax.ShapeDtypeStruct((n, LANES), F32),
            jax.ShapeDtypeStruct((n, IDX_DIM), F32),
            jax.ShapeDtypeStruct((n, LANES), BF16),
            jax.ShapeDtypeStruct((n, LANES), BF16),
            jax.ShapeDtypeStruct((n, IDX_DIM), BF16),
            jax.ShapeDtypeStruct((n // KEY_BLOCK, C_KV_HEADS * VT_ROWS, KEY_BLOCK), BF16),
        ],
        compiler_params=_params(("parallel",), 32),
        name="rotary_kv_staging",
    )(z, z, z, cos_t, sin_lo, sin_hi)


def _gla_head(q, k, v, logf, st_ref, h, tc):
    row = lax.broadcasted_iota(jnp.int32, (tc, tc), 0)
    col = lax.broadcasted_iota(jnp.int32, (tc, tc), 1)
    tril = (col <= row).astype(BF16)
    cum = _dot_exact_lhs(tril, logf)

    nb = tc // GLA_BLOCK
    b3 = cum.reshape(nb, GLA_BLOCK, HEAD_W)
    q3 = q.reshape(nb, GLA_BLOCK, HEAD_W)
    k3 = k.reshape(nb, GLA_BLOCK, HEAD_W)
    v3 = v.reshape(nb, GLA_BLOCK, HEAD_W)
    tloc = lax.broadcasted_iota(jnp.int32, (nb, GLA_BLOCK, 1), 1)
    o3 = jnp.zeros((nb, GLA_BLOCK, HEAD_W), F32)
    for j in range(GLA_BLOCK):
        causal = tloc >= j
        decay = jnp.exp(jnp.where(causal, b3 - b3[:, j:j + 1, :], 0.0))
        w = jnp.sum(q3 * k3[:, j:j + 1, :] * decay, axis=-1, keepdims=True)
        o3 = o3 + jnp.where(causal, w, 0.0) * v3[:, j:j + 1, :]
    o = o3.reshape(tc, HEAD_W)

    vb = v.astype(BF16)
    attn = jnp.zeros((tc, tc), F32)
    half = tc // 2
    while half >= GLA_BLOCK:
        blk = 2 * half
        nblk = tc // blk
        bl = cum.reshape(nblk, blk, HEAD_W)
        x = bl - bl[:, half - 1:half, :]
        second = lax.broadcasted_iota(jnp.int32, (nblk, blk, 1), 1) >= half
        e = jnp.exp(jnp.where(second, x, -x))
        qt = jnp.where(second, q.reshape(nblk, blk, HEAD_W) * e, 0.0).reshape(tc, HEAD_W).astype(BF16)
        kt = jnp.where(second, 0.0, k.reshape(nblk, blk, HEAD_W) * e).reshape(tc, HEAD_W).astype(BF16)
        shift = blk.bit_length() - 1
        same = (row >> shift) == (col >> shift)
        attn = attn + jnp.where(same, _dot_nt(qt, kt), 0.0)
        half //= 2
    if tc > GLA_BLOCK:
        o = o + _dot(attn.astype(BF16), vb)

    st = st_ref[h]
    o = o + _dot_nt((q * jnp.exp(cum)).astype(BF16), st.astype(BF16))
    last = cum[tc - 1:tc, :]
    kd = (k * jnp.exp(last - cum)).astype(BF16)
    st_ref[h] = st * jnp.exp(last) + _dot_tn(vb, kd)
    return o


def _gla_finish(o, nw, gate):
    return _rmsnorm(o, nw) * (gate * jax.nn.sigmoid(gate))


def _gla_state_io(c, s0_ref, st_ref, heads):
    @pl.when(c == 0)
    def _():
        for h in range(heads):
            st_ref[h] = s0_ref[0, h].T


def _gla_state_out(c, sout_ref, st_ref, heads):
    @pl.when(c == pl.num_programs(1) - 1)
    def _():
        for h in range(heads):
            sout_ref[0, h] = st_ref[h].T


def _hgrn_body(q_ref, f_ref, v_ref, g_ref, lb_ref, nw_ref, s0_ref, y_ref, sout_ref, st_ref, *, tc):
    c = pl.program_id(1)
    _gla_state_io(c, s0_ref, st_ref, A_HEADS)
    for h in range(A_HEADS):
        hs = slice(h * HEAD_W, (h + 1) * HEAD_W)
        zf = f_ref[:, hs]
        lb = lb_ref[:, hs]
        f = lb + (1.0 - lb) * jax.nn.sigmoid(zf)
        logf = jnp.log(jnp.maximum(f, F_MIN))
        k = (1.0 - lb) * jax.nn.sigmoid(-zf)
        zq = q_ref[:, hs]
        q = zq * jax.nn.sigmoid(zq) * (A_KDIM ** -0.5)
        o = _gla_head(q, k, v_ref[:, hs], logf, st_ref, h, tc)
        y_ref[:, hs] = _gla_finish(o, nw_ref[...], g_ref[:, hs])
    _gla_state_out(c, sout_ref, st_ref, A_HEADS)


def _gla_body(q_ref, k_ref, v_ref, g_ref, r_ref, wup_ref, bup_ref, nw_ref, s0_ref,
              y_ref, sout_ref, st_ref, *, tc):
    c = pl.program_id(1)
    _gla_state_io(c, s0_ref, st_ref, B_HEADS)
    r = _dot_hi(r_ref[...], wup_ref[...]) + bup_ref[...]
    logf_all = (jnp.minimum(r, 0.0) - jnp.log1p(jnp.exp(-jnp.abs(r)))) / B_TAU
    for h in range(B_HEADS):
        hs = slice(h * HEAD_W, (h + 1) * HEAD_W)
        q = q_ref[:, hs] * (B_KDIM ** -0.5)
        o = _gla_head(q, k_ref[:, hs], v_ref[:, hs], logf_all[:, hs], st_ref, h, tc)
        y_ref[:, hs] = _gla_finish(o, nw_ref[...], g_ref[:, hs])
    _gla_state_out(c, sout_ref, st_ref, B_HEADS)


def _recurrent_mixer(mode, z, row0, nseq, t, s0, nw, extra):
    tc = min(128, t)
    nc = t // tc
    rb0 = row0 // tc
    zcol = (Z_A if mode == "hgrn" else Z_B) // 4

    def zspec(k):
        return pl.BlockSpec((tc, MIX_W), lambda b, c: (rb0 + b * nc + c, zcol + k))

    const = lambda b, c: (0, 0)
    state_spec = pl.BlockSpec((1, 4, HEAD_W, HEAD_W), lambda b, c: (b, 0, 0, 0))
    if mode == "hgrn":
        body = functools.partial(_hgrn_body, tc=tc)
        in_specs = [zspec(0), zspec(1), zspec(2), zspec(3),
                    pl.BlockSpec((1, MIX_W), const), pl.BlockSpec((1, HEAD_W), const), state_spec]
        args = (z, z, z, z, extra[0], nw, s0)
    else:
        body = functools.partial(_gla_body, tc=tc)
        in_specs = [zspec(0), zspec(1), zspec(2), zspec(3),
                    pl.BlockSpec((tc, LANES), lambda b, c: (rb0 + b * nc + c, Z_SMALL)),
                    pl.BlockSpec((LANES, MIX_W), const), pl.BlockSpec((1, MIX_W), const),
                    pl.BlockSpec((1, HEAD_W), const), state_spec]
        args = (z, z, z, z, z, extra[0], extra[1], nw, s0)
    return pl.pallas_call(
        body,
        grid=(nseq, nc),
        in_specs=in_specs,
        out_specs=[pl.BlockSpec((tc, MIX_W), lambda b, c: (b * nc + c, 0)), state_spec],
        out_shape=[jax.ShapeDtypeStruct((nseq * t, MIX_W), F32),
                   jax.ShapeDtypeStruct((nseq, 4, HEAD_W, HEAD_W), F32)],
        scratch_shapes=[pltpu.VMEM((4, HEAD_W, HEAD_W), F32)],
        compiler_params=_params(("parallel", "arbitrary"), 32),
        name=mode + "_mixer",
    )(*args)


def _dsa_body(qc_ref, qi_ref, sm_ref, kc_ref, vc_ref, ki_ref, o_ref,
              sk_ref, qis_ref, wb_ref, qs_ref, m_ref, l_ref, acc_ref, *, tq, past, nkeys, topk):
    i = pl.program_id(1)
    qpos0 = past + i * tq
    last_chunk = (qpos0 + tq - 1) // CHUNK
    n_adm = jnp.minimum((last_chunk + 1) * CHUNK, nkeys)
    nkb = (n_adm + KEY_BLOCK - 1) // KEY_BLOCK
    grp = C_HEADS // C_KV_HEADS

    for j in range(IDX_HEADS):
        qis_ref[j * tq:(j + 1) * tq, :] = qi_ref[:, j * IDX_DIM:(j + 1) * IDX_DIM]
        wb_ref[j] = jnp.broadcast_to(sm_ref[:, SMALL_IW + j:SMALL_IW + j + 1], (tq, KEY_BLOCK))
    for hq in range(C_HEADS):
        g, r = divmod(hq, grp)
        qs_ref[g, r * tq:(r + 1) * tq, :] = qc_ref[:, hq * C_HDIM:(hq + 1) * C_HDIM]

    qchunk = (qpos0 + lax.broadcasted_iota(jnp.int32, (tq, 1), 0)) >> 6
    lane = lax.broadcasted_iota(jnp.int32, (1, KEY_BLOCK), 1)

    def admissible(kb):
        kpos = kb * KEY_BLOCK + lane
        return ((kpos >> 6) <= qchunk) & (kpos < nkeys), kpos

    def score_step(kb, carry):
        kib = ki_ref[0, pl.ds(pl.multiple_of(kb * KEY_BLOCK, KEY_BLOCK), KEY_BLOCK), :]
        d = _dot_nt(qis_ref[...], kib)
        sc = jnp.maximum(d[0:tq], 0.0) * wb_ref[0]
        for j in range(1, IDX_HEADS):
            sc = sc + jnp.maximum(d[j * tq:(j + 1) * tq], 0.0) * wb_ref[j]
        adm, _ = admissible(kb)
        sc = jnp.where(adm, sc * IDX_SCALE, NEG) + 0.0
        bits = pltpu.bitcast(sc, jnp.int32)
        sk_ref[kb] = jnp.where(bits < 0, bits ^ 0x7FFFFFFF, bits)
        return carry

    lax.fori_loop(0, nkb, score_step, 0)

    def count(pred):
        def step(kb, acc):
            _, kpos = admissible(kb)
            return acc + jnp.where(pred(sk_ref[kb], kpos), 1.0, 0.0)
        acc = lax.fori_loop(0, nkb, step, jnp.zeros((tq, KEY_BLOCK), F32))
        return jnp.sum(acc, axis=1, keepdims=True)

    kf = float(topk)
    c0 = count(lambda s, p: s >= 0)
    thr = jnp.where(c0 >= kf, 0, INT_MIN).astype(jnp.int32)

    def bit_step(it, thr):
        cand = thr | jnp.left_shift(jnp.int32(1), 30 - it)
        c = count(lambda s, p: s >= cand)
        return jnp.where(c >= kf, cand, thr)

    thr = lax.fori_loop(0, 31, bit_step, thr)

    c_ge = count(lambda s, p: s >= thr)
    c_gt = count(lambda s, p: s > thr)
    need = kf - c_gt
    nbits = max(1, (nkeys - 1).bit_length())

    def tie_cut():
        def cut_step(it, cut):
            cand = cut | jnp.left_shift(jnp.int32(1), nbits - 1 - it)
            c = count(lambda s, p: (s == thr) & (p < cand))
            return jnp.where(c < need, cand, cut)
        return lax.fori_loop(0, nbits, cut_step, jnp.zeros((tq, 1), jnp.int32))

    cut = lax.cond(jnp.max(c_ge) > kf, tie_cut, lambda: jnp.full((tq, 1), 2 ** 30, jnp.int32))

    m_ref[...] = jnp.full(m_ref.shape, NEG, F32)
    l_ref[...] = jnp.zeros(l_ref.shape, F32)
    acc_ref[...] = jnp.zeros(acc_ref.shape, F32)

    def attn_step(kb, carry):
        adm, kpos = admissible(kb)
        s_key = sk_ref[kb]
        sel = ((s_key > thr) | ((s_key == thr) & (kpos <= cut))) & adm
        k0 = pl.multiple_of(kb * KEY_BLOCK, KEY_BLOCK)
        kblk = kc_ref[0, pl.ds(k0, KEY_BLOCK), :]
        vblk = vc_ref[0, pl.ds(k0, KEY_BLOCK), :]
        for g in range(C_KV_HEADS):
            gs = slice(g * C_HDIM, (g + 1) * C_HDIM)
            s = _dot_nt(qs_ref[g], kblk[:, gs]).reshape(grp, tq, KEY_BLOCK)
            s = jnp.where(sel[None], s, NEG)
            m_old = m_ref[g]
            m_new = jnp.maximum(m_old, jnp.max(s, axis=-1, keepdims=True))
            p = jnp.where(sel[None], jnp.exp(s - m_new), 0.0)
            alpha = jnp.exp(m_old - m_new)
            l_ref[g] = alpha * l_ref[g] + jnp.sum(p, axis=-1, keepdims=True)
            pv = _dot(p.reshape(grp * tq, KEY_BLOCK).astype(BF16), vblk[:, gs])
            acc_ref[g] = alpha * acc_ref[g] + pv.reshape(grp, tq, C_HDIM)
            m_ref[g] = m_new
        return carry

    lax.fori_loop(0, nkb, attn_step, 0)

    for hq in range(C_HEADS):
        g, r = divmod(hq, grp)
        o_ref[:, hq * C_HDIM:(hq + 1) * C_HDIM] = acc_ref[g, r] / l_ref[g, r]


SUPER = 4
ATTN_BLOCKS = 2


def _dsa_t_body(qc_ref, qi_ref, sm_ref, kc_ref, vt_ref, ki_ref, o_ref,
                sk_ref, hi_ref, lo_ref, qit_ref, qct_ref, wrow_ref, acc_ref, ot_ref, s0_ref, s1_ref,
                bias_ref, p_ref,
                *, tq, past, nkeys, topk):
    i = pl.program_id(1)
    qpos0 = past + i * tq
    last_chunk = (qpos0 + tq - 1) // CHUNK
    n_adm = jnp.minimum((last_chunk + 1) * CHUNK, nkeys)
    nkb = (n_adm + KEY_BLOCK - 1) // KEY_BLOCK
    nsb = (nkb + SUPER - 1) // SUPER
    grp = C_HEADS // C_KV_HEADS

    qit = qi_ref[...].astype(F32).T
    for j in range(IDX_HEADS):
        qit_ref[:, j * tq:(j + 1) * tq] = qit[j * IDX_DIM:(j + 1) * IDX_DIM, :].astype(BF16)
    qct = qc_ref[...].astype(F32).T
    zeros = jnp.zeros((C_HDIM, tq), BF16)
    for hq in range(C_HEADS):
        g, r = divmod(hq, grp)
        for gg in range(C_KV_HEADS):
            val = qct[hq * C_HDIM:(hq + 1) * C_HDIM, :].astype(BF16) if gg == g else zeros
            qct_ref[g, gg * C_HDIM:(gg + 1) * C_HDIM, r * tq:(r + 1) * tq] = val
    wrow_ref[...] = sm_ref[...].T

    qchunk = (qpos0 + lax.broadcasted_iota(jnp.int32, (1, tq), 1)) >> 6
    ksub = lax.broadcasted_iota(jnp.int32, (KEY_BLOCK, 1), 0)

    def admissible(kb):
        kpos = kb * KEY_BLOCK + ksub
        return ((kpos >> 6) <= qchunk) & (kpos < nkeys)

    def score_step(kk, carry):
        for u in range(SUPER):
            kb = kk * SUPER + u
            kib = ki_ref[0, pl.ds(pl.multiple_of(kb * KEY_BLOCK, KEY_BLOCK), KEY_BLOCK), :]
            d = _dot(kib, qit_ref[...])
            sc = jnp.maximum(d[:, 0:tq], 0.0) * wrow_ref[SMALL_IW:SMALL_IW + 1, :]
            for j in range(1, IDX_HEADS):
                sc = sc + jnp.maximum(d[:, j * tq:(j + 1) * tq], 0.0) * wrow_ref[SMALL_IW + j:SMALL_IW + j + 1, :]
            bits = pltpu.bitcast(sc * IDX_SCALE + 0.0, jnp.int32)
            key = jnp.where(admissible(kb), jnp.where(bits < 0, bits ^ 0x7FFFFFFF, bits), INT_MIN)
            sk_ref[kb] = key
            hi_ref[kb] = (key >> 16).astype(jnp.int16)
            lo_ref[kb] = ((key & 0xFFFF) - 0x8000).astype(jnp.int16)
        return carry

    lax.fori_loop(0, nsb, score_step, 0)

    def fold(hit):
        return jnp.sum(hit.reshape(KEY_BLOCK // 8, 8, tq), axis=0)

    def count(pred):
        def step(kk, acc):
            for u in range(SUPER):
                kb = kk * SUPER + u
                acc = acc + fold(jnp.where(pred(sk_ref[kb], kb), 1.0, 0.0))
            return acc
        acc = lax.fori_loop(0, nsb, step, jnp.zeros((8, tq), F32))
        return jnp.sum(acc, axis=0, keepdims=True)

    def count16(ref, cand):
        c16 = cand.astype(jnp.int16)

        def tree(parts):
            while len(parts) > 1:
                parts = [a + b for a, b in zip(parts[0::2], parts[1::2])]
            return parts[0]

        def step(kk, accs):
            out = []
            for u in range(SUPER):
                hit = jnp.where(ref[kk * SUPER + u] >= c16, jnp.int16(1), jnp.int16(0))
                out.append(accs[u] + tree([hit[v * 16:(v + 1) * 16, :] for v in range(KEY_BLOCK // 16)]))
            return tuple(out)
        accs = lax.fori_loop(0, nsb, step, tuple(jnp.zeros((16, tq), jnp.int16) for _ in range(SUPER)))
        return jnp.sum(tree([a.astype(F32) for a in accs]), axis=0, keepdims=True)

    kf = float(topk)

    def select16(ref):
        c0 = count16(ref, jnp.zeros((1, tq), jnp.int32))
        t0 = jnp.where(c0 >= kf, 0, -0x8000).astype(jnp.int32)

        def bit_step(it, t):
            cand = t | jnp.left_shift(jnp.int32(1), 14 - it)
            return jnp.where(count16(ref, cand) >= kf, cand, t)
        return lax.fori_loop(0, 15, bit_step, t0)

    thr_hi = select16(hi_ref)
    thr_hi16 = thr_hi.astype(jnp.int16)

    def low_half_step(kk, carry):
        for u in range(SUPER):
            kb = kk * SUPER + u
            hi = hi_ref[kb]
            lo_ref[kb] = jnp.where(hi > thr_hi16, jnp.int16(0x7FFF),
                                   jnp.where(hi == thr_hi16, lo_ref[kb], jnp.int16(-0x8000)))
        return carry

    lax.fori_loop(0, nsb, low_half_step, 0)
    thr_lo = select16(lo_ref)
    thr = jnp.maximum((thr_hi << 16) | (thr_lo + 0x8000), INT_MIN + 1)

    c_ge = count(lambda s, kb: s >= thr)
    nbits = max(1, (nkeys - 1).bit_length())

    @pl.when(jnp.max(c_ge) > kf)
    def _():
        need = kf - count(lambda s, kb: s > thr)

        def cut_step(it, cut):
            cand = cut | jnp.left_shift(jnp.int32(1), nbits - 1 - it)
            c = count(lambda s, kb: (s == thr) & (kb * KEY_BLOCK + ksub < cand))
            return jnp.where(c < need, cand, cut)
        cut = lax.fori_loop(0, nbits, cut_step, jnp.zeros((1, tq), jnp.int32))

        def strike(kb, carry):
            key = sk_ref[kb]
            sk_ref[kb] = jnp.where((key == thr) & (kb * KEY_BLOCK + ksub > cut), INT_MIN, key)
            return carry
        lax.fori_loop(0, nsb * SUPER, strike, 0)

    acc_ref[...] = jnp.zeros(acc_ref.shape, F32)
    init = tuple(jnp.full((1, grp * tq), NEG, F32) for _ in range(C_KV_HEADS))

    span = ATTN_BLOCKS * KEY_BLOCK
    n_attn = nsb * (SUPER // ATTN_BLOCKS)

    s_bufs = (s0_ref, s1_ref)

    def qk_scores(kk, half):
        kblk = kc_ref[0, pl.ds(pl.multiple_of(kk * span, span), span), :]
        for g in range(C_KV_HEADS):
            s_bufs[half][g] = _dot(kblk, qct_ref[g])
        for u in range(ATTN_BLOCKS):
            sel = sk_ref[kk * ATTN_BLOCKS + u] >= thr
            bias_ref[half, u * KEY_BLOCK:(u + 1) * KEY_BLOCK, :] = jnp.where(sel, 0.0, NEG)

    def softmax_pv(kk, half, m):
        m = list(m)
        for g in range(C_KV_HEADS):
            m_cols, alpha_cols = [], []
            for r in range(grp):
                cols = slice(r * tq, (r + 1) * tq)
                sb = s_bufs[half][g, :, cols] + bias_ref[half]
                m_old = m[g][:, cols]
                m_new = jnp.maximum(m_old, jnp.max(sb, axis=0, keepdims=True))
                p_ref[half, g, :, cols] = jnp.exp(sb - m_new).astype(BF16)
                m_cols.append(m_new)
                alpha_cols.append(jnp.exp(m_old - m_new))
            alpha = jnp.concatenate(alpha_cols, axis=1)
            rows = slice(g * VT_ROWS, (g + 1) * VT_ROWS)
            pv = _dot(vt_ref[0, kk * ATTN_BLOCKS, rows, :], p_ref[half, g, 0:KEY_BLOCK, :])
            for u in range(1, ATTN_BLOCKS):
                pv = pv + _dot(vt_ref[0, kk * ATTN_BLOCKS + u, rows, :],
                               p_ref[half, g, u * KEY_BLOCK:(u + 1) * KEY_BLOCK, :])
            acc_ref[g] = alpha * acc_ref[g] + pv
            m[g] = jnp.concatenate(m_cols, axis=1)
        return tuple(m)

    qk_scores(0, 0)

    def pair_step(j, m):
        qk_scores(2 * j + 1, 1)
        m = softmax_pv(2 * j, 0, m)
        qk_scores(jnp.minimum(2 * j + 2, n_attn - 2), 0)
        return softmax_pv(2 * j + 1, 1, m)

    lax.fori_loop(0, n_attn // 2, pair_step, init)
    for g in range(C_KV_HEADS):
        og = acc_ref[g, 0:C_HDIM, :] / acc_ref[g, C_HDIM:C_HDIM + 1, :]
        for r in range(grp):
            hq = g * grp + r
            ot_ref[hq * C_HDIM:(hq + 1) * C_HDIM, :] = og[:, r * tq:(r + 1) * tq]
    o_ref[...] = ot_ref[...].T


def _dsa_t(qc, qi, z, row0, keys, vals_t, kidx, nseq, t, past, nkeys):
    tq = KEY_BLOCK
    nq = t // tq
    rb0 = row0 // tq
    lp = keys.shape[1]
    topk = min(TOPK_MAX, nkeys // 4)
    grp = C_HEADS // C_KV_HEADS
    qrow = lambda b, i: (rb0 + b * nq + i, 0)
    return pl.pallas_call(
        functools.partial(_dsa_t_body, tq=tq, past=past, nkeys=nkeys, topk=topk),
        grid=(nseq, nq),
        in_specs=[
            pl.BlockSpec((tq, C_WIDTH), qrow),
            pl.BlockSpec((tq, IDX_HEADS * IDX_DIM), qrow),
            pl.BlockSpec((tq, LANES), lambda b, i: (rb0 + b * nq + i, Z_SMALL)),
            pl.BlockSpec((1, lp, LANES), lambda b, i: (b, 0, 0)),
            pl.BlockSpec((1, lp // KEY_BLOCK, C_KV_HEADS * VT_ROWS, KEY_BLOCK), lambda b, i: (b, 0, 0, 0)),
            pl.BlockSpec((1, lp, IDX_DIM), lambda b, i: (b, 0, 0)),
        ],
        out_specs=pl.BlockSpec((tq, C_WIDTH), lambda b, i: (b * nq + i, 0)),
        out_shape=jax.ShapeDtypeStruct((nseq * t, C_WIDTH), F32),
        scratch_shapes=[
            pltpu.VMEM((lp // KEY_BLOCK, KEY_BLOCK, tq), jnp.int32),
            pltpu.VMEM((lp // KEY_BLOCK, KEY_BLOCK, tq), jnp.int16),
            pltpu.VMEM((lp // KEY_BLOCK, KEY_BLOCK, tq), jnp.int16),
            pltpu.VMEM((IDX_DIM, IDX_HEADS * tq), BF16),
            pltpu.VMEM((C_KV_HEADS, C_KV_HEADS * C_HDIM, grp * tq), BF16),
            pltpu.VMEM((LANES, tq), F32),
            pltpu.VMEM((C_KV_HEADS, VT_ROWS, grp * tq), F32),
            pltpu.VMEM((C_WIDTH, tq), F32),
            pltpu.VMEM((C_KV_HEADS, ATTN_BLOCKS * KEY_BLOCK, grp * tq), F32),
            pltpu.VMEM((C_KV_HEADS, ATTN_BLOCKS * KEY_BLOCK, grp * tq), F32),
            pltpu.VMEM((2, ATTN_BLOCKS * KEY_BLOCK, tq), F32),
            pltpu.VMEM((2, C_KV_HEADS, ATTN_BLOCKS * KEY_BLOCK, grp * tq), BF16),
        ],
        compiler_params=_params(("parallel", "arbitrary"), 48),
        name="dsa_mixer_t",
    )(qc, qi, z, keys, vals_t, kidx)


def _dsa(qc, qi, z, row0, keys, vals, kidx, nseq, t, past, nkeys):
    tq = min(128, t)
    nq = t // tq
    rb0 = row0 // tq
    lp = keys.shape[1]
    topk = min(TOPK_MAX, nkeys // 4)
    grp = C_HEADS // C_KV_HEADS
    qrow = lambda b, i: (rb0 + b * nq + i, 0)
    seq = lambda b, i: (b, 0, 0)
    return pl.pallas_call(
        functools.partial(_dsa_body, tq=tq, past=past, nkeys=nkeys, topk=topk),
        grid=(nseq, nq),
        in_specs=[
            pl.BlockSpec((tq, C_WIDTH), qrow),
            pl.BlockSpec((tq, IDX_HEADS * IDX_DIM), qrow),
            pl.BlockSpec((tq, LANES), lambda b, i: (rb0 + b * nq + i, Z_SMALL)),
            pl.BlockSpec((1, lp, LANES), seq),
            pl.BlockSpec((1, lp, LANES), seq),
            pl.BlockSpec((1, lp, IDX_DIM), seq),
        ],
        out_specs=pl.BlockSpec((tq, C_WIDTH), lambda b, i: (b * nq + i, 0)),
        out_shape=jax.ShapeDtypeStruct((nseq * t, C_WIDTH), F32),
        scratch_shapes=[
            pltpu.VMEM((lp // KEY_BLOCK, tq, KEY_BLOCK), jnp.int32),
            pltpu.VMEM((IDX_HEADS * tq, IDX_DIM), BF16),
            pltpu.VMEM((IDX_HEADS, tq, KEY_BLOCK), F32),
            pltpu.VMEM((C_KV_HEADS, grp * tq, C_HDIM), BF16),
            pltpu.VMEM((C_KV_HEADS, grp, tq, 1), F32),
            pltpu.VMEM((C_KV_HEADS, grp, tq, 1), F32),
            pltpu.VMEM((C_KV_HEADS, grp, tq, C_HDIM), F32),
        ],
        compiler_params=_params(("parallel", "arbitrary"), 48),
        name="dsa_mixer",
    )(qc, qi, z, keys, vals, kidx)


def _merge_body(h_ref, yap_ref, yas_ref, ybp_ref, ybs_ref, ocp_ref, ocs_ref, ga_ref, gb_ref, gc_ref,
                wa_ref, wb_ref, wc_ref, wo_ref, o_ref, *, prompt_tiles):
    is_prompt = pl.program_id(0) < prompt_tiles
    pick = lambda p_ref, s_ref: jnp.where(is_prompt, p_ref[...], s_ref[...]).astype(BF16)
    merged = (jax.nn.sigmoid(ga_ref[...]) * _dot(pick(yap_ref, yas_ref), wa_ref[...])
              + jax.nn.sigmoid(gb_ref[...]) * _dot(pick(ybp_ref, ybs_ref), wb_ref[...])
              + jax.nn.sigmoid(gc_ref[...]) * _dot(pick(ocp_ref, ocs_ref), wc_ref[...]))
    o_ref[...] = h_ref[...] + _dot(merged.astype(BF16), wo_ref[...])


def _merge(h, ya, yb, oc, z, wa, wb, wc, wo):
    n, d = h.shape
    tm = TOKEN_TILE
    pt = ya[0].shape[0] // tm
    assert ya[0].shape[0] % tm == 0 and ya[1].shape[0] % tm == 0 and n == ya[0].shape[0] + ya[1].shape[0]
    row = lambda i: (i, 0)
    const = lambda i: (0, 0)
    prow = lambda i: (jnp.minimum(i, pt - 1), 0)
    srow = lambda i: (jnp.maximum(i - pt, 0), 0)
    pair = [pl.BlockSpec((tm, MIX_W), prow), pl.BlockSpec((tm, MIX_W), srow)]
    return pl.pallas_call(
        functools.partial(_merge_body, prompt_tiles=pt),
        grid=(n // tm,),
        in_specs=[
            pl.BlockSpec((tm, d), row),
            *pair, *pair, *pair,
            pl.BlockSpec((tm, d), lambda i: (i, 0)),
            pl.BlockSpec((tm, d), lambda i: (i, 1)),
            pl.BlockSpec((tm, d), lambda i: (i, 2)),
            pl.BlockSpec((MIX_W, d), const),
            pl.BlockSpec((MIX_W, d), const),
            pl.BlockSpec((MIX_W, d), const),
            pl.BlockSpec((d, d), const),
        ],
        out_specs=pl.BlockSpec((tm, d), row),
        out_shape=jax.ShapeDtypeStruct((n, d), F32),
        compiler_params=_params(("parallel",), 48),
        name="gated_merge",
    )(h, *ya, *yb, *oc, z, z, z, wa, wb, wc, wo)


def _pad_heads(w, heads, dim):
    lead = w.shape[:-1]
    w = w.reshape(*lead, heads, dim)
    w = jnp.pad(w, [(0, 0)] * len(lead) + [(0, 0), (0, HEAD_W - dim)])
    return w.reshape(*lead, heads * HEAD_W)


def _layout_w_in(w):
    widths = (512, 512, 512, 512, 256, 256, 512, 16, 512, 512, 128, 128, 512, 64, 8, 1024, 1024, 1024)
    parts, o = [], 0
    for wd in widths:
        parts.append(w[:, o:o + wd])
        o += wd
    (a_q, a_f, a_i, a_g, b_q, b_k, b_v, b_r, b_g, c_q, c_k, c_v, i_q, i_k, i_w, g_a, g_b, g_c) = parts
    d = w.shape[0]
    small = jnp.concatenate([b_r, i_w, jnp.zeros((d, LANES - 24), w.dtype)], axis=1)
    i_k = jnp.pad(i_k, ((0, 0), (0, LANES - IDX_DIM)))
    cols = [g_a, g_b, g_c, a_q, a_f, a_i, a_g,
            _pad_heads(b_q, B_HEADS, B_KDIM), _pad_heads(b_k, B_HEADS, B_KDIM), b_v, b_g,
            c_q, i_q, c_k, c_v, i_k, small]
    out = jnp.concatenate(cols, axis=1)
    assert out.shape[1] == Z_WIDTH
    return out.astype(BF16)


def _rope_tables(pos):
    half = ROPE_DIMS // 2
    inv = ROPE_THETA ** (-jnp.arange(half, dtype=F32) * (2.0 / ROPE_DIMS))
    ang = pos.astype(F32)[:, None] * inv[None, :]
    cos, sin = jnp.cos(ang), jnp.sin(ang)
    n = pos.shape[0]
    ones = jnp.ones((n, C_HDIM - ROPE_DIMS), F32)
    zeros = jnp.zeros((n, C_HDIM - ROPE_DIMS), F32)
    zh = jnp.zeros((n, half), F32)
    c = jnp.concatenate([cos, cos, ones], axis=1)
    s_lo = jnp.concatenate([-sin, zh, zeros], axis=1)
    s_hi = jnp.concatenate([zh, sin, zeros], axis=1)
    two = lambda a: jnp.concatenate([a, a], axis=1)
    return two(c), two(s_lo), two(s_hi)


def kernel(x_prompt, x_sample, state_hgrn, state_gla, cache_k, cache_v, cache_kidx, hgrn_lb, w_in, w_gla_up, b_gla, norm_hgrn, norm_gla, w_br_a, w_br_b, w_br_c, w_out, norm_ffn1, norm_mix, norm_ffn2, ffn1_w1, ffn1_w3, ffn1_w2, ffn2_w1, ffn2_w3, ffn2_w2, norm_final):
    bp, tp, d = x_prompt.shape
    bs, ts, _ = x_sample.shape
    past = cache_k.shape[2]
    n_p, n_s = bp * tp, bs * ts

    lb_sm = jax.nn.softmax(hgrn_lb.astype(F32), axis=0)
    lb_all = jnp.concatenate([jnp.zeros_like(lb_sm[:1]), jnp.cumsum(lb_sm[1:], axis=0)], axis=0)

    pos = jnp.concatenate([jnp.tile(jnp.arange(tp, dtype=jnp.int32), bp),
                           jnp.tile(past + jnp.arange(ts, dtype=jnp.int32), bs)])
    cos_t, sin_lo, sin_hi = _rope_tables(pos)

    h = jnp.concatenate([x_prompt.reshape(n_p, d), x_sample.reshape(n_s, d)], axis=0)
    row2 = lambda a: a.reshape(1, -1)
    zero_state = jnp.zeros((bp, 4, HEAD_W, HEAD_W), F32)
    lp_s = -(-(past + ts) // KEY_BLOCK) * KEY_BLOCK
    key_pad = lp_s - past - ts

    outs = {k: [] for k in ("pa", "pb", "pk", "pv", "pki", "sa", "sb", "sk", "sv", "ski")}
    for l in range(DEPTH):
        bf = lambda a: a[l].astype(BF16)
        h = _ffn(h, row2(norm_ffn1[l]), bf(ffn1_w1), bf(ffn1_w3), bf(ffn1_w2), row2(norm_final), False)
        z = _inproj(h, row2(norm_mix[l]), _layout_w_in(w_in[l]))
        qc, qi, kc, vc, ki, kcb, vcb, kib, vt = _prep(z, cos_t, sin_lo, sin_hi)

        lb = row2(lb_all[l])
        nwa, nwb = row2(norm_hgrn[l]), row2(norm_gla[l])
        wup = jnp.pad(_pad_heads(w_gla_up[l], B_HEADS, B_KDIM), ((0, LANES - B_GATE_RANK), (0, 0)))
        bup = row2(_pad_heads(b_gla[l], B_HEADS, B_KDIM))
        sb0 = jnp.pad(state_gla[l], ((0, 0), (0, 0), (0, HEAD_W - B_KDIM), (0, 0)))

        ya_p, sa_p = _recurrent_mixer("hgrn", z, 0, bp, tp, zero_state, nwa, (lb,))
        ya_s, sa_s = _recurrent_mixer("hgrn", z, n_p, bs, ts, state_hgrn[l], nwa, (lb,))
        yb_p, sb_p = _recurrent_mixer("gla", z, 0, bp, tp, zero_state, nwb, (wup, bup))
        yb_s, sb_s = _recurrent_mixer("gla", z, n_p, bs, ts, sb0, nwb, (wup, bup))

        seqs = lambda a, n, t: a.reshape(n, t, a.shape[-1])
        vt_p = vt[:n_p // KEY_BLOCK].reshape(bp, tp // KEY_BLOCK, C_KV_HEADS * VT_ROWS, KEY_BLOCK)
        oc_p = _dsa_t(qc, qi, z, 0, seqs(kcb[:n_p], bp, tp), vt_p, seqs(kib[:n_p], bp, tp), bp, tp, 0, tp)

        def with_cache(cache, new):
            full = jnp.concatenate([cache.reshape(bs, past, -1).astype(BF16), seqs(new[n_p:], bs, ts)], axis=1)
            return jnp.pad(full, ((0, 0), (0, key_pad), (0, 0)))

        oc_s = _dsa(qc, qi, z, n_p, with_cache(cache_k[l], kcb), with_cache(cache_v[l], vcb),
                    with_cache(cache_kidx[l], kib), bs, ts, past, past + ts)

        h = _merge(h, (ya_p, ya_s), (yb_p, yb_s), (oc_p, oc_s), z,
                   bf(w_br_a), bf(w_br_b), bf(w_br_c), bf(w_out))
        h = _ffn(h, row2(norm_ffn2[l]), bf(ffn2_w1), bf(ffn2_w3), bf(ffn2_w2), row2(norm_final),
                 l == DEPTH - 1)

        outs["pa"].append(sa_p)
        outs["sa"].append(sa_s)
        outs["pb"].append(sb_p[:, :, :B_KDIM, :])
        outs["sb"].append(sb_s[:, :, :B_KDIM, :])
        outs["pk"].append(kc[:n_p].reshape(bp, tp, C_KV_HEADS, C_HDIM))
        outs["pv"].append(vc[:n_p].reshape(bp, tp, C_KV_HEADS, C_HDIM))
        outs["pki"].append(ki[:n_p].reshape(bp, tp, IDX_DIM))
        outs["sk"].append(kc[n_p:].reshape(bs, ts, C_KV_HEADS, C_HDIM))
        outs["sv"].append(vc[n_p:].reshape(bs, ts, C_KV_HEADS, C_HDIM))
        outs["ski"].append(ki[n_p:].reshape(bs, ts, IDX_DIM))

    st = {k: jnp.stack(v) for k, v in outs.items()}
    return (h[:n_p].reshape(bp, tp, d), h[n_p:].reshape(bs, ts, d),
            st["pa"], st["pb"], st["pk"], st["pv"], st["pki"],
            st["sa"], st["sb"], st["sk"], st["sv"], st["ski"])
```

```python
import functools

import jax
import jax.numpy as jnp
from jax import lax
from jax.experimental import pallas as pl
from jax.experimental.pallas import tpu as pltpu

F32 = jnp.float32
BF16 = jnp.bfloat16

D_MODEL = 1024
DEPTH = 2
CHUNK = 64
EPS = 1e-6
NEG = -1e30
F_MIN = 1e-30
A_HEADS = 4
A_KDIM = 128
A_VDIM = 128
B_HEADS = 4
B_KDIM = 64
B_VDIM = 128
B_GATE_RANK = 16
B_TAU = 16.0
C_HEADS = 8
C_KV_HEADS = 2
C_HDIM = 64
C_WIDTH = C_HEADS * C_HDIM
IDX_HEADS = 8
IDX_DIM = 64
IDX_SCALE = (IDX_HEADS * IDX_DIM) ** -0.5
TOPK_MAX = 256
ROPE_THETA = 500000.0
ROPE_DIMS = C_HDIM // 4
GLA_BLOCK = 16
D_FF = 2816

LANES = 128
HEAD_W = 128
MIX_W = 4 * HEAD_W
TOKEN_TILE = 512
FF_TILE = D_FF // 2
KEY_BLOCK = 128
VT_ROWS = 80
INT_MIN = -(2 ** 31)

Z_GATES = 0
Z_A = 24
Z_B = 40
Z_CQ = 56
Z_IQ = 60
Z_CKVI = 64
Z_SMALL = 67
Z_WIDTH = 68 * LANES
SMALL_IW = B_GATE_RANK


def _params(sem, vmem_mb):
    return pltpu.CompilerParams(dimension_semantics=sem, vmem_limit_bytes=vmem_mb << 20)


def _dot(a, b):
    return jnp.dot(a, b, preferred_element_type=F32)


def _dot_nt(a, b):
    return lax.dot_general(a, b, (((1,), (1,)), ((), ())), preferred_element_type=F32)


def _dot_tn(a, b):
    return lax.dot_general(a, b, (((0,), (0,)), ((), ())), preferred_element_type=F32)


def _split3(x):
    h1 = x.astype(BF16)
    r1 = x - h1.astype(F32)
    h2 = r1.astype(BF16)
    h3 = (r1 - h2.astype(F32)).astype(BF16)
    return h1, h2, h3


def _dot_exact_lhs(m, x):
    h1, h2, h3 = _split3(x)
    return _dot(m, h1) + _dot(m, h2) + _dot(m, h3)


def _dot_hi(a, b):
    a1 = a.astype(BF16)
    a2 = (a - a1.astype(F32)).astype(BF16)
    b1 = b.astype(BF16)
    b2 = (b - b1.astype(F32)).astype(BF16)
    return _dot(a1, b1) + _dot(a1, b2) + _dot(a2, b1)


def _rmsnorm(x, g):
    return x * lax.rsqrt(jnp.mean(x * x, axis=-1, keepdims=True) + EPS) * g


def _ffn_body(x_ref, g_ref, w1_ref, w3_ref, w2_ref, gf_ref, o_ref, u_ref, acc_ref, *, final_norm):
    j = pl.program_id(1)

    @pl.when(j == 0)
    def _():
        u_ref[...] = _rmsnorm(x_ref[...], g_ref[...]).astype(BF16)
        acc_ref[...] = jnp.zeros_like(acc_ref)

    u = u_ref[...]
    a = _dot(u, w1_ref[...])
    b = _dot(u, w3_ref[...])
    hid = (a * jax.nn.sigmoid(a) * b).astype(BF16)
    acc_ref[...] += _dot(hid, w2_ref[...])

    @pl.when(j == pl.num_programs(1) - 1)
    def _():
        out = x_ref[...] + 0.5 * acc_ref[...]
        if final_norm:
            out = _rmsnorm(out, gf_ref[...])
        o_ref[...] = out


def _ffn(x, g, w1, w3, w2, gf, final_norm):
    n, d = x.shape
    dff = w1.shape[1]
    tm = min(TOKEN_TILE, n)
    tf = FF_TILE if dff == D_FF else dff
    return pl.pallas_call(
        functools.partial(_ffn_body, final_norm=final_norm),
        grid=(n // tm, dff // tf),
        in_specs=[
            pl.BlockSpec((tm, d), lambda i, j: (i, 0)),
            pl.BlockSpec((1, d), lambda i, j: (0, 0)),
            pl.BlockSpec((d, tf), lambda i, j: (0, j)),
            pl.BlockSpec((d, tf), lambda i, j: (0, j)),
            pl.BlockSpec((tf, d), lambda i, j: (j, 0)),
            pl.BlockSpec((1, d), lambda i, j: (0, 0)),
        ],
        out_specs=pl.BlockSpec((tm, d), lambda i, j: (i, 0)),
        out_shape=jax.ShapeDtypeStruct((n, d), F32),
        scratch_shapes=[pltpu.VMEM((tm, d), BF16), pltpu.VMEM((tm, d), F32)],
        compiler_params=_params(("parallel", "arbitrary"), 48),
        name="ffn_half_step",
    )(x, g, w1, w3, w2, gf)


def _inproj_body(x_ref, g_ref, w_ref, z_ref, u_ref):
    @pl.when(pl.program_id(1) == 0)
    def _():
        u_ref[...] = _rmsnorm(x_ref[...], g_ref[...]).astype(BF16)

    z_ref[...] = _dot(u_ref[...], w_ref[...])


def _inproj(h, g, w):
    n, d = h.shape
    zw = w.shape[1]
    tm = min(TOKEN_TILE, n)
    tn = zw // 4 if zw % (4 * LANES) == 0 else zw
    return pl.pallas_call(
        _inproj_body,
        grid=(n // tm, zw // tn),
        in_specs=[
            pl.BlockSpec((tm, d), lambda i, j: (i, 0)),
            pl.BlockSpec((1, d), lambda i, j: (0, 0)),
            pl.BlockSpec((d, tn), lambda i, j: (0, j)),
        ],
        out_specs=pl.BlockSpec((tm, tn), lambda i, j: (i, j)),
        out_shape=jax.ShapeDtypeStruct((n, zw), F32),
        scratch_shapes=[pltpu.VMEM((tm, d), BF16)],
        compiler_params=_params(("parallel", "arbitrary"), 32),
        name="in_projection",
    )(h, g, w)


def _rope(x, c, s_lo, s_hi):
    w = x.shape[1]
    rep = w // LANES
    if rep > 1:
        c = jnp.concatenate([c] * rep, axis=1)
        s_lo = jnp.concatenate([s_lo] * rep, axis=1)
        s_hi = jnp.concatenate([s_hi] * rep, axis=1)
    half = ROPE_DIMS // 2
    return x * c + pltpu.roll(x, half, 1) * s_hi + pltpu.roll(x, w - half, 1) * s_lo


def _prep_body(cq_ref, iq_ref, kv_ref, c_ref, slo_ref, shi_ref,
               qc_ref, qi_ref, kc_ref, vc_ref, ki_ref, kcb_ref, vcb_ref, kib_ref, vt_ref):
    c, s_lo, s_hi = c_ref[...], slo_ref[...], shi_ref[...]
    qc_ref[...] = (_rope(cq_ref[...], c, s_lo, s_hi) * (C_HDIM ** -0.5)).astype(BF16)
    qi_ref[...] = _rope(iq_ref[...], c, s_lo, s_hi).astype(BF16)
    kc = _rope(kv_ref[:, 0:LANES], c, s_lo, s_hi)
    vc = kv_ref[:, LANES:2 * LANES]
    ki = _rope(kv_ref[:, 2 * LANES:3 * LANES], c, s_lo, s_hi)[:, :IDX_DIM]
    kc_ref[...] = kc
    vc_ref[...] = vc
    ki_ref[...] = ki
    kcb_ref[...] = kc.astype(BF16)
    vcb_ref[...] = vc.astype(BF16)
    kib_ref[...] = ki.astype(BF16)
    ones = jnp.ones((VT_ROWS - C_HDIM, KEY_BLOCK), BF16)
    for kk in range(vt_ref.shape[0]):
        vt = vc[kk * KEY_BLOCK:(kk + 1) * KEY_BLOCK, :].T.astype(BF16)
        vt_ref[kk] = jnp.concatenate([vt[:C_HDIM], ones, vt[C_HDIM:], ones], axis=0)


def _prep(z, cos_t, sin_lo, sin_hi):
    n = z.shape[0]
    tm = min(TOKEN_TILE, n)
    row = lambda i: (i, 0)
    return pl.pallas_call(
        _prep_body,
        grid=(n // tm,),
        in_specs=[
            pl.BlockSpec((tm, MIX_W), lambda i: (i, Z_CQ // 4)),
            pl.BlockSpec((tm, MIX_W), lambda i: (i, Z_IQ // 4)),
            pl.BlockSpec((tm, MIX_W), lambda i: (i, Z_CKVI // 4)),
            pl.BlockSpec((tm, LANES), row),
            pl.BlockSpec((tm, LANES), row),
            pl.BlockSpec((tm, LANES), row),
        ],
        out_specs=[
            pl.BlockSpec((tm, C_WIDTH), row),
            pl.BlockSpec((tm, IDX_HEADS * IDX_DIM), row),
            pl.BlockSpec((tm, LANES), row),
            pl.BlockSpec((tm, LANES), row),
            pl.BlockSpec((tm, IDX_DIM), row),
            pl.BlockSpec((tm, LANES), row),
            pl.BlockSpec((tm, LANES), row),
            pl.BlockSpec((tm, IDX_DIM), row),
            pl.BlockSpec((tm // KEY_BLOCK, C_KV_HEADS * VT_ROWS, KEY_BLOCK), lambda i: (i, 0, 0)),
        ],
        out_shape=[
            jax.ShapeDtypeStruct((n, C_WIDTH), BF16),
            jax.ShapeDtypeStruct((n, IDX_HEADS * IDX_DIM), BF16),
            jax.ShapeDtypeStruct((n, LANES), F32),
            jax.ShapeDtypeStruct((n, LANES), F32),
            jax.ShapeDtypeStruct((n, IDX_DIM), F32),
            jax.ShapeDtypeStruct((n, LANES), BF16),
            jax.ShapeDtypeStruct((n, LANES), BF16),
            jax.ShapeDtypeStruct((n, IDX_DIM), BF16),
            jax.ShapeDtypeStruct((n // KEY_BLOCK, C_KV_HEADS * VT_ROWS, KEY_BLOCK), BF16),
        ],
        compiler_params=_params(("parallel",), 32),
        name="rotary_kv_staging",
    )(z, z, z, cos_t, sin_lo, sin_hi)


def _gla_head(q, k, v, logf, st_ref, h, tc):
    row = lax.broadcasted_iota(jnp.int32, (tc, tc), 0)
    col = lax.broadcasted_iota(jnp.int32, (tc, tc), 1)
    tril = (col <= row).astype(BF16)
    cum = _dot_exact_lhs(tril, logf)

    nb = tc // GLA_BLOCK
    b3 = cum.reshape(nb, GLA_BLOCK, HEAD_W)
    q3 = q.reshape(nb, GLA_BLOCK, HEAD_W)
    k3 = k.reshape(nb, GLA_BLOCK, HEAD_W)
    v3 = v.reshape(nb, GLA_BLOCK, HEAD_W)
    tloc = lax.broadcasted_iota(jnp.int32, (nb, GLA_BLOCK, 1), 1)
    o3 = jnp.zeros((nb, GLA_BLOCK, HEAD_W), F32)
    for j in range(GLA_BLOCK):
        causal = tloc >= j
        decay = jnp.exp(jnp.where(causal, b3 - b3[:, j:j + 1, :], 0.0))
        w = jnp.sum(q3 * k3[:, j:j + 1, :] * decay, axis=-1, keepdims=True)
        o3 = o3 + jnp.where(causal, w, 0.0) * v3[:, j:j + 1, :]
    o = o3.reshape(tc, HEAD_W)

    vb = v.astype(BF16)
    attn = jnp.zeros((tc, tc), F32)
    half = tc // 2
    while half >= GLA_BLOCK:
        blk = 2 * half
        nblk = tc // blk
        bl = cum.reshape(nblk, blk, HEAD_W)
        x = bl - bl[:, half - 1:half, :]
        second = lax.broadcasted_iota(jnp.int32, (nblk, blk, 1), 1) >= half
        e = jnp.exp(jnp.where(second, x, -x))
        qt = jnp.where(second, q.reshape(nblk, blk, HEAD_W) * e, 0.0).reshape(tc, HEAD_W).astype(BF16)
        kt = jnp.where(second, 0.0, k.reshape(nblk, blk, HEAD_W) * e).reshape(tc, HEAD_W).astype(BF16)
        shift = blk.bit_length() - 1
        same = (row >> shift) == (col >> shift)
        attn = attn + jnp.where(same, _dot_nt(qt, kt), 0.0)
        half //= 2
    if tc > GLA_BLOCK:
        o = o + _dot(attn.astype(BF16), vb)

    st = st_ref[h]
    o = o + _dot_nt((q * jnp.exp(cum)).astype(BF16), st.astype(BF16))
    last = cum[tc - 1:tc, :]
    kd = (k * jnp.exp(last - cum)).astype(BF16)
    st_ref[h] = st * jnp.exp(last) + _dot_tn(vb, kd)
    return o


def _gla_finish(o, nw, gate):
    return _rmsnorm(o, nw) * (gate * jax.nn.sigmoid(gate))


def _gla_state_io(c, s0_ref, st_ref, heads):
    @pl.when(c == 0)
    def _():
        for h in range(heads):
            st_ref[h] = s0_ref[0, h].T


def _gla_state_out(c, sout_ref, st_ref, heads):
    @pl.when(c == pl.num_programs(1) - 1)
    def _():
        for h in range(heads):
            sout_ref[0, h] = st_ref[h].T


def _hgrn_body(q_ref, f_ref, v_ref, g_ref, lb_ref, nw_ref, s0_ref, y_ref, sout_ref, st_ref, *, tc):
    c = pl.program_id(1)
    _gla_state_io(c, s0_ref, st_ref, A_HEADS)
    for h in range(A_HEADS):
        hs = slice(h * HEAD_W, (h + 1) * HEAD_W)
        zf = f_ref[:, hs]
        lb = lb_ref[:, hs]
        f = lb + (1.0 - lb) * jax.nn.sigmoid(zf)
        logf = jnp.log(jnp.maximum(f, F_MIN))
        k = (1.0 - lb) * jax.nn.sigmoid(-zf)
        zq = q_ref[:, hs]
        q = zq * jax.nn.sigmoid(zq) * (A_KDIM ** -0.5)
        o = _gla_head(q, k, v_ref[:, hs], logf, st_ref, h, tc)
        y_ref[:, hs] = _gla_finish(o, nw_ref[...], g_ref[:, hs])
    _gla_state_out(c, sout_ref, st_ref, A_HEADS)


def _gla_body(q_ref, k_ref, v_ref, g_ref, r_ref, wup_ref, bup_ref, nw_ref, s0_ref,
              y_ref, sout_ref, st_ref, *, tc):
    c = pl.program_id(1)
    _gla_state_io(c, s0_ref, st_ref, B_HEADS)
    r = _dot_hi(r_ref[...], wup_ref[...]) + bup_ref[...]
    logf_all = (jnp.minimum(r, 0.0) - jnp.log1p(jnp.exp(-jnp.abs(r)))) / B_TAU
    for h in range(B_HEADS):
        hs = slice(h * HEAD_W, (h + 1) * HEAD_W)
        q = q_ref[:, hs] * (B_KDIM ** -0.5)
        o = _gla_head(q, k_ref[:, hs], v_ref[:, hs], logf_all[:, hs], st_ref, h, tc)
        y_ref[:, hs] = _gla_finish(o, nw_ref[...], g_ref[:, hs])
    _gla_state_out(c, sout_ref, st_ref, B_HEADS)


def _recurrent_mixer(mode, z, row0, nseq, t, s0, nw, extra):
    tc = min(128, t)
    nc = t // tc
    rb0 = row0 // tc
    zcol = (Z_A if mode == "hgrn" else Z_B) // 4

    def zspec(k):
        return pl.BlockSpec((tc, MIX_W), lambda b, c: (rb0 + b * nc + c, zcol + k))

    const = lambda b, c: (0, 0)
    state_spec = pl.BlockSpec((1, 4, HEAD_W, HEAD_W), lambda b, c: (b, 0, 0, 0))
    if mode == "hgrn":
        body = functools.partial(_hgrn_body, tc=tc)
        in_specs = [zspec(0), zspec(1), zspec(2), zspec(3),
                    pl.BlockSpec((1, MIX_W), const), pl.BlockSpec((1, HEAD_W), const), state_spec]
        args = (z, z, z, z, extra[0], nw, s0)
    else:
        body = functools.partial(_gla_body, tc=tc)
        in_specs = [zspec(0), zspec(1), zspec(2), zspec(3),
                    pl.BlockSpec((tc, LANES), lambda b, c: (rb0 + b * nc + c, Z_SMALL)),
                    pl.BlockSpec((LANES, MIX_W), const), pl.BlockSpec((1, MIX_W), const),
                    pl.BlockSpec((1, HEAD_W), const), state_spec]
        args = (z, z, z, z, z, extra[0], extra[1], nw, s0)
    return pl.pallas_call(
        body,
        grid=(nseq, nc),
        in_specs=in_specs,
        out_specs=[pl.BlockSpec((tc, MIX_W), lambda b, c: (b * nc + c, 0)), state_spec],
        out_shape=[jax.ShapeDtypeStruct((nseq * t, MIX_W), F32),
                   jax.ShapeDtypeStruct((nseq, 4, HEAD_W, HEAD_W), F32)],
        scratch_shapes=[pltpu.VMEM((4, HEAD_W, HEAD_W), F32)],
        compiler_params=_params(("parallel", "arbitrary"), 32),
        name=mode + "_mixer",
    )(*args)


def _dsa_body(qc_ref, qi_ref, sm_ref, kc_ref, vc_ref, ki_ref, o_ref,
              sk_ref, qis_ref, wb_ref, qs_ref, m_ref, l_ref, acc_ref, *, tq, past, nkeys, topk):
    i = pl.program_id(1)
    qpos0 = past + i * tq
    last_chunk = (qpos0 + tq - 1) // CHUNK
    n_adm = jnp.minimum((last_chunk + 1) * CHUNK, nkeys)
    nkb = (n_adm + KEY_BLOCK - 1) // KEY_BLOCK
    grp = C_HEADS // C_KV_HEADS

    for j in range(IDX_HEADS):
        qis_ref[j * tq:(j + 1) * tq, :] = qi_ref[:, j * IDX_DIM:(j + 1) * IDX_DIM]
        wb_ref[j] = jnp.broadcast_to(sm_ref[:, SMALL_IW + j:SMALL_IW + j + 1], (tq, KEY_BLOCK))
    for hq in range(C_HEADS):
        g, r = divmod(hq, grp)
        qs_ref[g, r * tq:(r + 1) * tq, :] = qc_ref[:, hq * C_HDIM:(hq + 1) * C_HDIM]

    qchunk = (qpos0 + lax.broadcasted_iota(jnp.int32, (tq, 1), 0)) >> 6
    lane = lax.broadcasted_iota(jnp.int32, (1, KEY_BLOCK), 1)

    def admissible(kb):
        kpos = kb * KEY_BLOCK + lane
        return ((kpos >> 6) <= qchunk) & (kpos < nkeys), kpos

    def score_step(kb, carry):
        kib = ki_ref[0, pl.ds(pl.multiple_of(kb * KEY_BLOCK, KEY_BLOCK), KEY_BLOCK), :]
        d = _dot_nt(qis_ref[...], kib)
        sc = jnp.maximum(d[0:tq], 0.0) * wb_ref[0]
        for j in range(1, IDX_HEADS):
            sc = sc + jnp.maximum(d[j * tq:(j + 1) * tq], 0.0) * wb_ref[j]
        adm, _ = admissible(kb)
        sc = jnp.where(adm, sc * IDX_SCALE, NEG) + 0.0
        bits = pltpu.bitcast(sc, jnp.int32)
        sk_ref[kb] = jnp.where(bits < 0, bits ^ 0x7FFFFFFF, bits)
        return carry

    lax.fori_loop(0, nkb, score_step, 0)

    def count(pred):
        def step(kb, acc):
            _, kpos = admissible(kb)
            return acc + jnp.where(pred(sk_ref[kb], kpos), 1.0, 0.0)
        acc = lax.fori_loop(0, nkb, step, jnp.zeros((tq, KEY_BLOCK), F32))
        return jnp.sum(acc, axis=1, keepdims=True)

    kf = float(topk)
    c0 = count(lambda s, p: s >= 0)
    thr = jnp.where(c0 >= kf, 0, INT_MIN).astype(jnp.int32)

    def bit_step(it, thr):
        cand = thr | jnp.left_shift(jnp.int32(1), 30 - it)
        c = count(lambda s, p: s >= cand)
        return jnp.where(c >= kf, cand, thr)

    thr = lax.fori_loop(0, 31, bit_step, thr)

    c_ge = count(lambda s, p: s >= thr)
    c_gt = count(lambda s, p: s > thr)
    need = kf - c_gt
    nbits = max(1, (nkeys - 1).bit_length())

    def tie_cut():
        def cut_step(it, cut):
            cand = cut | jnp.left_shift(jnp.int32(1), nbits - 1 - it)
            c = count(lambda s, p: (s == thr) & (p < cand))
            return jnp.where(c < need, cand, cut)
        return lax.fori_loop(0, nbits, cut_step, jnp.zeros((tq, 1), jnp.int32))

    cut = lax.cond(jnp.max(c_ge) > kf, tie_cut, lambda: jnp.full((tq, 1), 2 ** 30, jnp.int32))

    m_ref[...] = jnp.full(m_ref.shape, NEG, F32)
    l_ref[...] = jnp.zeros(l_ref.shape, F32)
    acc_ref[...] = jnp.zeros(acc_ref.shape, F32)

    def attn_step(kb, carry):
        adm, kpos = admissible(kb)
        s_key = sk_ref[kb]
        sel = ((s_key > thr) | ((s_key == thr) & (kpos <= cut))) & adm
        k0 = pl.multiple_of(kb * KEY_BLOCK, KEY_BLOCK)
        kblk = kc_ref[0, pl.ds(k0, KEY_BLOCK), :]
        vblk = vc_ref[0, pl.ds(k0, KEY_BLOCK), :]
        for g in range(C_KV_HEADS):
            gs = slice(g * C_HDIM, (g + 1) * C_HDIM)
            s = _dot_nt(qs_ref[g], kblk[:, gs]).reshape(grp, tq, KEY_BLOCK)
            s = jnp.where(sel[None], s, NEG)
            m_old = m_ref[g]
            m_new = jnp.maximum(m_old, jnp.max(s, axis=-1, keepdims=True))
            p = jnp.where(sel[None], jnp.exp(s - m_new), 0.0)
            alpha = jnp.exp(m_old - m_new)
            l_ref[g] = alpha * l_ref[g] + jnp.sum(p, axis=-1, keepdims=True)
            pv = _dot(p.reshape(grp * tq, KEY_BLOCK).astype(BF16), vblk[:, gs])
            acc_ref[g] = alpha * acc_ref[g] + pv.reshape(grp, tq, C_HDIM)
            m_ref[g] = m_new
        return carry

    lax.fori_loop(0, nkb, attn_step, 0)

    for hq in range(C_HEADS):
        g, r = divmod(hq, grp)
        o_ref[:, hq * C_HDIM:(hq + 1) * C_HDIM] = acc_ref[g, r] / l_ref[g, r]


SUPER = 4
ATTN_BLOCKS = 2


def _dsa_t_body(qc_ref, qi_ref, sm_ref, kc_ref, vt_ref, ki_ref, o_ref,
                sk_ref, qit_ref, qct_ref, wrow_ref, acc_ref, ot_ref, s0_ref, s1_ref, p_ref,
                *, tq, past, nkeys, topk):
    i = pl.program_id(1)
    qpos0 = past + i * tq
    last_chunk = (qpos0 + tq - 1) // CHUNK
    n_adm = jnp.minimum((last_chunk + 1) * CHUNK, nkeys)
    nkb = (n_adm + KEY_BLOCK - 1) // KEY_BLOCK
    nsb = (nkb + SUPER - 1) // SUPER
    grp = C_HEADS // C_KV_HEADS

    qit = qi_ref[...].astype(F32).T
    for j in range(IDX_HEADS):
        qit_ref[:, j * tq:(j + 1) * tq] = qit[j * IDX_DIM:(j + 1) * IDX_DIM, :].astype(BF16)
    qct = qc_ref[...].astype(F32).T
    zeros = jnp.zeros((C_HDIM, tq), BF16)
    for hq in range(C_HEADS):
        g, r = divmod(hq, grp)
        for gg in range(C_KV_HEADS):
            val = qct[hq * C_HDIM:(hq + 1) * C_HDIM, :].astype(BF16) if gg == g else zeros
            qct_ref[g, gg * C_HDIM:(gg + 1) * C_HDIM, r * tq:(r + 1) * tq] = val
    wrow_ref[...] = sm_ref[...].T

    qchunk = (qpos0 + lax.broadcasted_iota(jnp.int32, (1, tq), 1)) >> 6
    ksub = lax.broadcasted_iota(jnp.int32, (KEY_BLOCK, 1), 0)

    def admissible(kb):
        kpos = kb * KEY_BLOCK + ksub
        return ((kpos >> 6) <= qchunk) & (kpos < nkeys)

    def score_step(kk, carry):
        for u in range(SUPER):
            kb = kk * SUPER + u
            kib = ki_ref[0, pl.ds(pl.multiple_of(kb * KEY_BLOCK, KEY_BLOCK), KEY_BLOCK), :]
            d = _dot(kib, qit_ref[...])
            sc = jnp.maximum(d[:, 0:tq], 0.0) * wrow_ref[SMALL_IW:SMALL_IW + 1, :]
            for j in range(1, IDX_HEADS):
                sc = sc + jnp.maximum(d[:, j * tq:(j + 1) * tq], 0.0) * wrow_ref[SMALL_IW + j:SMALL_IW + j + 1, :]
            bits = pltpu.bitcast(sc * IDX_SCALE + 0.0, jnp.int32)
            key = jnp.where(admissible(kb), jnp.where(bits < 0, bits ^ 0x7FFFFFFF, bits), INT_MIN)
            sk_ref[kb] = key
        return carry

    lax.fori_loop(0, nsb, score_step, 0)

    def fold(hit):
        return jnp.sum(hit.reshape(KEY_BLOCK // 8, 8, tq), axis=0)

    def count(pred):
        def step(kk, acc):
            for u in range(SUPER):
                kb = kk * SUPER + u
                acc = acc + fold(jnp.where(pred(sk_ref[kb], kb), 1.0, 0.0))
            return acc
        acc = lax.fori_loop(0, nsb, step, jnp.zeros((8, tq), F32))
        return jnp.sum(acc, axis=0, keepdims=True)

    kf = float(topk)
    c0 = count(lambda s, kb: s >= 0)
    t0 = jnp.where(c0 >= kf, 0, INT_MIN).astype(jnp.int32)

    def bit_step(it, t):
        cand = t | jnp.left_shift(jnp.int32(1), 30 - it)
        return jnp.where(count(lambda s, kb: s >= cand) >= kf, cand, t)
    thr = lax.fori_loop(0, 31, bit_step, t0)
    thr = jnp.maximum(thr, INT_MIN + 1)

    c_ge = count(lambda s, kb: s >= thr)
    nbits = max(1, (nkeys - 1).bit_length())

    @pl.when(jnp.max(c_ge) > kf)
    def _():
        need = kf - count(lambda s, kb: s > thr)

        def cut_step(it, cut):
            cand = cut | jnp.left_shift(jnp.int32(1), nbits - 1 - it)
            c = count(lambda s, kb: (s == thr) & (kb * KEY_BLOCK + ksub < cand))
            return jnp.where(c < need, cand, cut)
        cut = lax.fori_loop(0, nbits, cut_step, jnp.zeros((1, tq), jnp.int32))

        def strike(kb, carry):
            key = sk_ref[kb]
            sk_ref[kb] = jnp.where((key == thr) & (kb * KEY_BLOCK + ksub > cut), INT_MIN, key)
            return carry
        lax.fori_loop(0, nsb * SUPER, strike, 0)

    acc_ref[...] = jnp.zeros(acc_ref.shape, F32)
    init = tuple(jnp.full((1, grp * tq), NEG, F32) for _ in range(C_KV_HEADS))

    span = ATTN_BLOCKS * KEY_BLOCK
    n_attn = nsb * (SUPER // ATTN_BLOCKS)

    s_bufs = (s0_ref, s1_ref)

    def qk_scores(kk, half):
        kblk = kc_ref[0, pl.ds(pl.multiple_of(kk * span, span), span), :]
        bias = jnp.concatenate(
            [jnp.where(sk_ref[kk * ATTN_BLOCKS + u] >= thr, 0.0, NEG) for u in range(ATTN_BLOCKS)], axis=0)
        bias = jnp.concatenate([bias] * grp, axis=1)
        for g in range(C_KV_HEADS):
            s = _dot(kblk, qct_ref[g]) + bias
            for r in range(grp):
                s_bufs[half][g, r] = s[:, r * tq:(r + 1) * tq]

    def softmax_pv(kk, half, m):
        m = list(m)
        for g in range(C_KV_HEADS):
            m_cols, alpha_cols = [], []
            for r in range(grp):
                cols = slice(r * tq, (r + 1) * tq)
                sb = s_bufs[half][g, r]
                m_old = m[g][:, cols]
                m_new = jnp.maximum(m_old, jnp.max(sb, axis=0, keepdims=True))
                p_ref[half, g, r] = jnp.exp(sb - m_new).astype(BF16)
                m_cols.append(m_new)
                alpha_cols.append(jnp.exp(m_old - m_new))
            rows = slice(g * VT_ROWS, (g + 1) * VT_ROWS)
            pv = None
            for u in range(ATTN_BLOCKS):
                keys_u = slice(u * KEY_BLOCK, (u + 1) * KEY_BLOCK)
                p_u = jnp.concatenate([p_ref[half, g, r, keys_u, :] for r in range(grp)], axis=1)
                d = _dot(vt_ref[0, kk * ATTN_BLOCKS + u, rows, :], p_u)
                pv = d if pv is None else pv + d
            for r in range(grp):
                acc_ref[g, r] = alpha_cols[r] * acc_ref[g, r] + pv[:, r * tq:(r + 1) * tq]
            m[g] = jnp.concatenate(m_cols, axis=1)
        return tuple(m)

    qk_scores(0, 0)

    def pair_step(j, m):
        qk_scores(2 * j + 1, 1)
        m = softmax_pv(2 * j, 0, m)
        qk_scores(jnp.minimum(2 * j + 2, n_attn - 2), 0)
        return softmax_pv(2 * j + 1, 1, m)

    lax.fori_loop(0, n_attn // 2, pair_step, init)
    for hq in range(C_HEADS):
        g, r = divmod(hq, grp)
        ot_ref[hq * C_HDIM:(hq + 1) * C_HDIM, :] = acc_ref[g, r, 0:C_HDIM, :] / acc_ref[g, r, C_HDIM:C_HDIM + 1, :]
    o_ref[...] = ot_ref[...].T


def _dsa_t(qc, qi, z, row0, keys, vals_t, kidx, nseq, t, past, nkeys):
    tq = KEY_BLOCK
    nq = t // tq
    rb0 = row0 // tq
    lp = keys.shape[1]
    topk = min(TOPK_MAX, nkeys // 4)
    grp = C_HEADS // C_KV_HEADS
    qrow = lambda b, i: (rb0 + b * nq + i, 0)
    return pl.pallas_call(
        functools.partial(_dsa_t_body, tq=tq, past=past, nkeys=nkeys, topk=topk),
        grid=(nseq, nq),
        in_specs=[
            pl.BlockSpec((tq, C_WIDTH), qrow),
            pl.BlockSpec((tq, IDX_HEADS * IDX_DIM), qrow),
            pl.BlockSpec((tq, LANES), lambda b, i: (rb0 + b * nq + i, Z_SMALL)),
            pl.BlockSpec((1, lp, LANES), lambda b, i: (b, 0, 0)),
            pl.BlockSpec((1, lp // KEY_BLOCK, C_KV_HEADS * VT_ROWS, KEY_BLOCK), lambda b, i: (b, 0, 0, 0)),
            pl.BlockSpec((1, lp, IDX_DIM), lambda b, i: (b, 0, 0)),
        ],
        out_specs=pl.BlockSpec((tq, C_WIDTH), lambda b, i: (b * nq + i, 0)),
        out_shape=jax.ShapeDtypeStruct((nseq * t, C_WIDTH), F32),
        scratch_shapes=[
            pltpu.VMEM((lp // KEY_BLOCK, KEY_BLOCK, tq), jnp.int32),
            pltpu.VMEM((IDX_DIM, IDX_HEADS * tq), BF16),
            pltpu.VMEM((C_KV_HEADS, C_KV_HEADS * C_HDIM, grp * tq), BF16),
            pltpu.VMEM((LANES, tq), F32),
            pltpu.VMEM((C_KV_HEADS, grp, VT_ROWS, tq), F32),
            pltpu.VMEM((C_WIDTH, tq), F32),
            pltpu.VMEM((C_KV_HEADS, grp, ATTN_BLOCKS * KEY_BLOCK, tq), F32),
            pltpu.VMEM((C_KV_HEADS, grp, ATTN_BLOCKS * KEY_BLOCK, tq), F32),
            pltpu.VMEM((2, C_KV_HEADS, grp, ATTN_BLOCKS * KEY_BLOCK, tq), BF16),
        ],
        compiler_params=_params(("parallel", "arbitrary"), 48),
        name="dsa_mixer_t",
    )(qc, qi, z, keys, vals_t, kidx)


def _dsa(qc, qi, z, row0, keys, vals, kidx, nseq, t, past, nkeys):
    tq = min(128, t)
    nq = t // tq
    rb0 = row0 // tq
    lp = keys.shape[1]
    topk = min(TOPK_MAX, nkeys // 4)
    grp = C_HEADS // C_KV_HEADS
    qrow = lambda b, i: (rb0 + b * nq + i, 0)
    seq = lambda b, i: (b, 0, 0)
    return pl.pallas_call(
        functools.partial(_dsa_body, tq=tq, past=past, nkeys=nkeys, topk=topk),
        grid=(nseq, nq),
        in_specs=[
            pl.BlockSpec((tq, C_WIDTH), qrow),
            pl.BlockSpec((tq, IDX_HEADS * IDX_DIM), qrow),
            pl.BlockSpec((tq, LANES), lambda b, i: (rb0 + b * nq + i, Z_SMALL)),
            pl.BlockSpec((1, lp, LANES), seq),
            pl.BlockSpec((1, lp, LANES), seq),
            pl.BlockSpec((1, lp, IDX_DIM), seq),
        ],
        out_specs=pl.BlockSpec((tq, C_WIDTH), lambda b, i: (b * nq + i, 0)),
        out_shape=jax.ShapeDtypeStruct((nseq * t, C_WIDTH), F32),
        scratch_shapes=[
            pltpu.VMEM((lp // KEY_BLOCK, tq, KEY_BLOCK), jnp.int32),
            pltpu.VMEM((IDX_HEADS * tq, IDX_DIM), BF16),
            pltpu.VMEM((IDX_HEADS, tq, KEY_BLOCK), F32),
            pltpu.VMEM((C_KV_HEADS, grp * tq, C_HDIM), BF16),
            pltpu.VMEM((C_KV_HEADS, grp, tq, 1), F32),
            pltpu.VMEM((C_KV_HEADS, grp, tq, 1), F32),
            pltpu.VMEM((C_KV_HEADS, grp, tq, C_HDIM), F32),
        ],
        compiler_params=_params(("parallel", "arbitrary"), 48),
        name="dsa_mixer",
    )(qc, qi, z, keys, vals, kidx)


def _merge_body(h_ref, yap_ref, yas_ref, ybp_ref, ybs_ref, ocp_ref, ocs_ref, ga_ref, gb_ref, gc_ref,
                wa_ref, wb_ref, wc_ref, wo_ref, o_ref, *, prompt_tiles):
    is_prompt = pl.program_id(0) < prompt_tiles
    pick = lambda p_ref, s_ref: jnp.where(is_prompt, p_ref[...], s_ref[...]).astype(BF16)
    merged = (jax.nn.sigmoid(ga_ref[...]) * _dot(pick(yap_ref, yas_ref), wa_ref[...])
              + jax.nn.sigmoid(gb_ref[...]) * _dot(pick(ybp_ref, ybs_ref), wb_ref[...])
              + jax.nn.sigmoid(gc_ref[...]) * _dot(pick(ocp_ref, ocs_ref), wc_ref[...]))
    o_ref[...] = h_ref[...] + _dot(merged.astype(BF16), wo_ref[...])


def _merge(h, ya, yb, oc, z, wa, wb, wc, wo):
    n, d = h.shape
    tm = TOKEN_TILE
    pt = ya[0].shape[0] // tm
    assert ya[0].shape[0] % tm == 0 and ya[1].shape[0] % tm == 0 and n == ya[0].shape[0] + ya[1].shape[0]
    row = lambda i: (i, 0)
    const = lambda i: (0, 0)
    prow = lambda i: (jnp.minimum(i, pt - 1), 0)
    srow = lambda i: (jnp.maximum(i - pt, 0), 0)
    pair = [pl.BlockSpec((tm, MIX_W), prow), pl.BlockSpec((tm, MIX_W), srow)]
    return pl.pallas_call(
        functools.partial(_merge_body, prompt_tiles=pt),
        grid=(n // tm,),
        in_specs=[
            pl.BlockSpec((tm, d), row),
            *pair, *pair, *pair,
            pl.BlockSpec((tm, d), lambda i: (i, 0)),
            pl.BlockSpec((tm, d), lambda i: (i, 1)),
            pl.BlockSpec((tm, d), lambda i: (i, 2)),
            pl.BlockSpec((MIX_W, d), const),
            pl.BlockSpec((MIX_W, d), const),
            pl.BlockSpec((MIX_W, d), const),
            pl.BlockSpec((d, d), const),
        ],
        out_specs=pl.BlockSpec((tm, d), row),
        out_shape=jax.ShapeDtypeStruct((n, d), F32),
        compiler_params=_params(("parallel",), 48),
        name="gated_merge",
    )(h, *ya, *yb, *oc, z, z, z, wa, wb, wc, wo)


def _pad_heads(w, heads, dim):
    lead = w.shape[:-1]
    w = w.reshape(*lead, heads, dim)
    w = jnp.pad(w, [(0, 0)] * len(lead) + [(0, 0), (0, HEAD_W - dim)])
    return w.reshape(*lead, heads * HEAD_W)


def _layout_w_in(w):
    widths = (512, 512, 512, 512, 256, 256, 512, 16, 512, 512, 128, 128, 512, 64, 8, 1024, 1024, 1024)
    parts, o = [], 0
    for wd in widths:
        parts.append(w[:, o:o + wd])
        o += wd
    (a_q, a_f, a_i, a_g, b_q, b_k, b_v, b_r, b_g, c_q, c_k, c_v, i_q, i_k, i_w, g_a, g_b, g_c) = parts
    d = w.shape[0]
    small = jnp.concatenate([b_r, i_w, jnp.zeros((d, LANES - 24), w.dtype)], axis=1)
    i_k = jnp.pad(i_k, ((0, 0), (0, LANES - IDX_DIM)))
    cols = [g_a, g_b, g_c, a_q, a_f, a_i, a_g,
            _pad_heads(b_q, B_HEADS, B_KDIM), _pad_heads(b_k, B_HEADS, B_KDIM), b_v, b_g,
            c_q, i_q, c_k, c_v, i_k, small]
    out = jnp.concatenate(cols, axis=1)
    assert out.shape[1] == Z_WIDTH
    return out.astype(BF16)


def _rope_tables(pos):
    half = ROPE_DIMS // 2
    inv = ROPE_THETA ** (-jnp.arange(half, dtype=F32) * (2.0 / ROPE_DIMS))
    ang = pos.astype(F32)[:, None] * inv[None, :]
    cos, sin = jnp.cos(ang), jnp.sin(ang)
    n = pos.shape[0]
    ones = jnp.ones((n, C_HDIM - ROPE_DIMS), F32)
    zeros = jnp.zeros((n, C_HDIM - ROPE_DIMS), F32)
    zh = jnp.zeros((n, half), F32)
    c = jnp.concatenate([cos, cos, ones], axis=1)
    s_lo = jnp.concatenate([-sin, zh, zeros], axis=1)
    s_hi = jnp.concatenate([zh, sin, zeros], axis=1)
    two = lambda a: jnp.concatenate([a, a], axis=1)
    return two(c), two(s_lo), two(s_hi)


def kernel(x_prompt, x_sample, state_hgrn, state_gla, cache_k, cache_v, cache_kidx, hgrn_lb, w_in, w_gla_up, b_gla, norm_hgrn, norm_gla, w_br_a, w_br_b, w_br_c, w_out, norm_ffn1, norm_mix, norm_ffn2, ffn1_w1, ffn1_w3, ffn1_w2, ffn2_w1, ffn2_w3, ffn2_w2, norm_final):
    bp, tp, d = x_prompt.shape
    bs, ts, _ = x_sample.shape
    past = cache_k.shape[2]
    n_p, n_s = bp * tp, bs * ts

    lb_sm = jax.nn.softmax(hgrn_lb.astype(F32), axis=0)
    lb_all = jnp.concatenate([jnp.zeros_like(lb_sm[:1]), jnp.cumsum(lb_sm[1:], axis=0)], axis=0)

    pos = jnp.concatenate([jnp.tile(jnp.arange(tp, dtype=jnp.int32), bp),
                           jnp.tile(past + jnp.arange(ts, dtype=jnp.int32), bs)])
    cos_t, sin_lo, sin_hi = _rope_tables(pos)

    h = jnp.concatenate([x_prompt.reshape(n_p, d), x_sample.reshape(n_s, d)], axis=0)
    row2 = lambda a: a.reshape(1, -1)
    zero_state = jnp.zeros((bp, 4, HEAD_W, HEAD_W), F32)
    lp_s = -(-(past + ts) // KEY_BLOCK) * KEY_BLOCK
    key_pad = lp_s - past - ts

    outs = {k: [] for k in ("pa", "pb", "pk", "pv", "pki", "sa", "sb", "sk", "sv", "ski")}
    for l in range(DEPTH):
        bf = lambda a: a[l].astype(BF16)
        h = _ffn(h, row2(norm_ffn1[l]), bf(ffn1_w1), bf(ffn1_w3), bf(ffn1_w2), row2(norm_final), False)
        z = _inproj(h, row2(norm_mix[l]), _layout_w_in(w_in[l]))
        qc, qi, kc, vc, ki, kcb, vcb, kib, vt = _prep(z, cos_t, sin_lo, sin_hi)

        lb = row2(lb_all[l])
        nwa, nwb = row2(norm_hgrn[l]), row2(norm_gla[l])
        wup = jnp.pad(_pad_heads(w_gla_up[l], B_HEADS, B_KDIM), ((0, LANES - B_GATE_RANK), (0, 0)))
        bup = row2(_pad_heads(b_gla[l], B_HEADS, B_KDIM))
        sb0 = jnp.pad(state_gla[l], ((0, 0), (0, 0), (0, HEAD_W - B_KDIM), (0, 0)))

        ya_p, sa_p = _recurrent_mixer("hgrn", z, 0, bp, tp, zero_state, nwa, (lb,))
        ya_s, sa_s = _recurrent_mixer("hgrn", z, n_p, bs, ts, state_hgrn[l], nwa, (lb,))
        yb_p, sb_p = _recurrent_mixer("gla", z, 0, bp, tp, zero_state, nwb, (wup, bup))
        yb_s, sb_s = _recurrent_mixer("gla", z, n_p, bs, ts, sb0, nwb, (wup, bup))

        seqs = lambda a, n, t: a.reshape(n, t, a.shape[-1])
        vt_p = vt[:n_p // KEY_BLOCK].reshape(bp, tp // KEY_BLOCK, C_KV_HEADS * VT_ROWS, KEY_BLOCK)
        oc_p = _dsa_t(qc, qi, z, 0, seqs(kcb[:n_p], bp, tp), vt_p, seqs(kib[:n_p], bp, tp), bp, tp, 0, tp)

        def with_cache(cache, new):
            full = jnp.concatenate([cache.reshape(bs, past, -1).astype(BF16), seqs(new[n_p:], bs, ts)], axis=1)
            return jnp.pad(full, ((0, 0), (0, key_pad), (0, 0)))

        oc_s = _dsa(qc, qi, z, n_p, with_cache(cache_k[l], kcb), with_cache(cache_v[l], vcb),
                    with_cache(cache_kidx[l], kib), bs, ts, past, past + ts)

        h = _merge(h, (ya_p, ya_s), (yb_p, yb_s), (oc_p, oc_s), z,
                   bf(w_br_a), bf(w_br_b), bf(w_br_c), bf(w_out))
        h = _ffn(h, row2(norm_ffn2[l]), bf(ffn2_w1), bf(ffn2_w3), bf(ffn2_w2), row2(norm_final),
                 l == DEPTH - 1)

        outs["pa"].append(sa_p)
        outs["sa"].append(sa_s)
        outs["pb"].append(sb_p[:, :, :B_KDIM, :])
        outs["sb"].append(sb_s[:, :, :B_KDIM, :])
        outs["pk"].append(kc[:n_p].reshape(bp, tp, C_KV_HEADS, C_HDIM))
        outs["pv"].append(vc[:n_p].reshape(bp, tp, C_KV_HEADS, C_HDIM))
        outs["pki"].append(ki[:n_p].reshape(bp, tp, IDX_DIM))
        outs["sk"].append(kc[n_p:].reshape(bs, ts, C_KV_HEADS, C_HDIM))
        outs["sv"].append(vc[n_p:].reshape(bs, ts, C_KV_HEADS, C_HDIM))
        outs["ski"].append(ki[n_p:].reshape(bs, ts, IDX_DIM))

    st = {k: jnp.stack(v) for k, v in outs.items()}
    return (h[:n_p].reshape(bp, tp, d), h[n_p:].reshape(bs, ts, d),
            st["pa"], st["pb"], st["pk"], st["pv"], st["pki"],
            st["sa"], st["sb"], st["sk"], st["sv"], st["ski"])
```

```python
import functools

import jax
import jax.numpy as jnp
from jax import lax
from jax.experimental import pallas as pl
from jax.experimental.pallas import tpu as pltpu

F32 = jnp.float32
BF16 = jnp.bfloat16

D_MODEL = 1024
DEPTH = 2
CHUNK = 64
EPS = 1e-6
NEG = -1e30
F_MIN = 1e-30
A_HEADS = 4
A_KDIM = 128
A_VDIM = 128
B_HEADS = 4
B_KDIM = 64
B_VDIM = 128
B_GATE_RANK = 16
B_TAU = 16.0
C_HEADS = 8
C_KV_HEADS = 2
C_HDIM = 64
C_WIDTH = C_HEADS * C_HDIM
IDX_HEADS = 8
IDX_DIM = 64
IDX_SCALE = (IDX_HEADS * IDX_DIM) ** -0.5
TOPK_MAX = 256
ROPE_THETA = 500000.0
ROPE_DIMS = C_HDIM // 4
PAIR_BLOCK = 4
D_FF = 2816

LANES = 128
HEAD_W = 128
MIX_W = 4 * HEAD_W
TOKEN_TILE = 512
FF_TILE = D_FF // 2
KEY_BLOCK = 128
VT_ROWS = 80
INT_MIN = -(2 ** 31)

Z_GATES = 0
Z_A = 24
Z_B = 40
Z_CQ = 56
Z_IQ = 60
Z_CKVI = 64
Z_SMALL = 67
Z_WIDTH = 68 * LANES
SMALL_IW = B_GATE_RANK


def _params(sem, vmem_mb):
    return pltpu.CompilerParams(dimension_semantics=sem, vmem_limit_bytes=vmem_mb << 20)


def _dot(a, b):
    return jnp.dot(a, b, preferred_element_type=F32)


def _dot_nt(a, b):
    return lax.dot_general(a, b, (((1,), (1,)), ((), ())), preferred_element_type=F32)


def _dot_tn(a, b):
    return lax.dot_general(a, b, (((0,), (0,)), ((), ())), preferred_element_type=F32)


def _split3(x):
    h1 = x.astype(BF16)
    r1 = x - h1.astype(F32)
    h2 = r1.astype(BF16)
    h3 = (r1 - h2.astype(F32)).astype(BF16)
    return h1, h2, h3


def _dot_exact_lhs(m, x):
    h1, h2, h3 = _split3(x)
    return _dot(m, h1) + _dot(m, h2) + _dot(m, h3)


def _dot_hi(a, b):
    a1 = a.astype(BF16)
    a2 = (a - a1.astype(F32)).astype(BF16)
    b1 = b.astype(BF16)
    b2 = (b - b1.astype(F32)).astype(BF16)
    return _dot(a1, b1) + _dot(a1, b2) + _dot(a2, b1)


def _rmsnorm(x, g):
    return x * lax.rsqrt(jnp.mean(x * x, axis=-1, keepdims=True) + EPS) * g


def _ffn_body(x_ref, g_ref, w1_ref, w3_ref, w2_ref, gf_ref, o_ref, u_ref, acc_ref, *, final_norm):
    j = pl.program_id(1)

    @pl.when(j == 0)
    def _():
        u_ref[...] = _rmsnorm(x_ref[...], g_ref[...]).astype(BF16)
        acc_ref[...] = jnp.zeros_like(acc_ref)

    u = u_ref[...]
    a = _dot(u, w1_ref[...])
    b = _dot(u, w3_ref[...])
    hid = (a * jax.nn.sigmoid(a) * b).astype(BF16)
    acc_ref[...] += _dot(hid, w2_ref[...])

    @pl.when(j == pl.num_programs(1) - 1)
    def _():
        out = x_ref[...] + 0.5 * acc_ref[...]
        if final_norm:
            out = _rmsnorm(out, gf_ref[...])
        o_ref[...] = out


def _ffn(x, g, w1, w3, w2, gf, final_norm):
    n, d = x.shape
    dff = w1.shape[1]
    tm = min(TOKEN_TILE, n)
    tf = FF_TILE if dff == D_FF else dff
    return pl.pallas_call(
        functools.partial(_ffn_body, final_norm=final_norm),
        grid=(n // tm, dff // tf),
        in_specs=[
            pl.BlockSpec((tm, d), lambda i, j: (i, 0)),
            pl.BlockSpec((1, d), lambda i, j: (0, 0)),
            pl.BlockSpec((d, tf), lambda i, j: (0, j)),
            pl.BlockSpec((d, tf), lambda i, j: (0, j)),
            pl.BlockSpec((tf, d), lambda i, j: (j, 0)),
            pl.BlockSpec((1, d), lambda i, j: (0, 0)),
        ],
        out_specs=pl.BlockSpec((tm, d), lambda i, j: (i, 0)),
        out_shape=jax.ShapeDtypeStruct((n, d), F32),
        scratch_shapes=[pltpu.VMEM((tm, d), BF16), pltpu.VMEM((tm, d), F32)],
        compiler_params=_params(("parallel", "arbitrary"), 48),
        name="ffn_half_step",
    )(x, g, w1, w3, w2, gf)


def _inproj_body(x_ref, g_ref, w_ref, z_ref, u_ref):
    @pl.when(pl.program_id(1) == 0)
    def _():
        u_ref[...] = _rmsnorm(x_ref[...], g_ref[...]).astype(BF16)

    z_ref[...] = _dot(u_ref[...], w_ref[...])


def _inproj(h, g, w):
    n, d = h.shape
    zw = w.shape[1]
    tm = min(TOKEN_TILE, n)
    tn = zw // 4 if zw % (4 * LANES) == 0 else zw
    return pl.pallas_call(
        _inproj_body,
        grid=(n // tm, zw // tn),
        in_specs=[
            pl.BlockSpec((tm, d), lambda i, j: (i, 0)),
            pl.BlockSpec((1, d), lambda i, j: (0, 0)),
            pl.BlockSpec((d, tn), lambda i, j: (0, j)),
        ],
        out_specs=pl.BlockSpec((tm, tn), lambda i, j: (i, j)),
        out_shape=jax.ShapeDtypeStruct((n, zw), F32),
        scratch_shapes=[pltpu.VMEM((tm, d), BF16)],
        compiler_params=_params(("parallel", "arbitrary"), 32),
        name="in_projection",
    )(h, g, w)


def _rope(x, c, s_lo, s_hi):
    w = x.shape[1]
    rep = w // LANES
    if rep > 1:
        c = jnp.concatenate([c] * rep, axis=1)
        s_lo = jnp.concatenate([s_lo] * rep, axis=1)
        s_hi = jnp.concatenate([s_hi] * rep, axis=1)
    half = ROPE_DIMS // 2
    return x * c + pltpu.roll(x, half, 1) * s_hi + pltpu.roll(x, w - half, 1) * s_lo


def _prep_body(cq_ref, iq_ref, kv_ref, c_ref, slo_ref, shi_ref,
               qc_ref, qi_ref, kc_ref, vc_ref, ki_ref, kcb_ref, vcb_ref, kib_ref, vt_ref):
    c, s_lo, s_hi = c_ref[...], slo_ref[...], shi_ref[...]
    qc_ref[...] = (_rope(cq_ref[...], c, s_lo, s_hi) * (C_HDIM ** -0.5)).astype(BF16)
    qi_ref[...] = _rope(iq_ref[...], c, s_lo, s_hi).astype(BF16)
    kc = _rope(kv_ref[:, 0:LANES], c, s_lo, s_hi)
    vc = kv_ref[:, LANES:2 * LANES]
    ki = _rope(kv_ref[:, 2 * LANES:3 * LANES], c, s_lo, s_hi)[:, :IDX_DIM]
    kc_ref[...] = kc
    vc_ref[...] = vc
    ki_ref[...] = ki
    kcb_ref[...] = kc.astype(BF16)
    vcb_ref[...] = vc.astype(BF16)
    kib_ref[...] = ki.astype(BF16)
    ones = jnp.ones((VT_ROWS - C_HDIM, KEY_BLOCK), BF16)
    for kk in range(vt_ref.shape[0]):
        vt = vc[kk * KEY_BLOCK:(kk + 1) * KEY_BLOCK, :].T.astype(BF16)
        vt_ref[kk] = jnp.concatenate([vt[:C_HDIM], ones, vt[C_HDIM:], ones], axis=0)


def _prep(z, cos_t, sin_lo, sin_hi):
    n = z.shape[0]
    tm = min(TOKEN_TILE, n)
    row = lambda i: (i, 0)
    return pl.pallas_call(
        _prep_body,
        grid=(n // tm,),
        in_specs=[
            pl.BlockSpec((tm, MIX_W), lambda i: (i, Z_CQ // 4)),
            pl.BlockSpec((tm, MIX_W), lambda i: (i, Z_IQ // 4)),
            pl.BlockSpec((tm, MIX_W), lambda i: (i, Z_CKVI // 4)),
            pl.BlockSpec((tm, LANES), row),
            pl.BlockSpec((tm, LANES), row),
            pl.BlockSpec((tm, LANES), row),
        ],
        out_specs=[
            pl.BlockSpec((tm, C_WIDTH), row),
            pl.BlockSpec((tm, IDX_HEADS * IDX_DIM), row),
            pl.BlockSpec((tm, LANES), row),
            pl.BlockSpec((tm, LANES), row),
            pl.BlockSpec((tm, IDX_DIM), row),
            pl.BlockSpec((tm, LANES), row),
            pl.BlockSpec((tm, LANES), row),
            pl.BlockSpec((tm, IDX_DIM), row),
            pl.BlockSpec((tm // KEY_BLOCK, C_KV_HEADS * VT_ROWS, KEY_BLOCK), lambda i: (i, 0, 0)),
        ],
        out_shape=[
            jax.ShapeDtypeStruct((n, C_WIDTH), BF16),
            jax.ShapeDtypeStruct((n, IDX_HEADS * IDX_DIM), BF16),
            jax.ShapeDtypeStruct((n, LANES), F32),
            jax.ShapeDtypeStruct((n, LANES), F32),
            jax.ShapeDtypeStruct((n, IDX_DIM), F32),
            jax.ShapeDtypeStruct((n, LANES), BF16),
            jax.ShapeDtypeStruct((n, LANES), BF16),
            jax.ShapeDtypeStruct((n, IDX_DIM), BF16),
            jax.ShapeDtypeStruct((n // KEY_BLOCK, C_KV_HEADS * VT_ROWS, KEY_BLOCK), BF16),
        ],
        compiler_params=_params(("parallel",), 32),
        name="rotary_kv_staging",
    )(z, z, z, cos_t, sin_lo, sin_hi)


def _gla_head(q, k, v, logf, st_ref, h, tc):
    row = lax.broadcasted_iota(jnp.int32, (tc, tc), 0)
    col = lax.broadcasted_iota(jnp.int32, (tc, tc), 1)
    tril = (col <= row).astype(BF16)
    cum = _dot_exact_lhs(tril, logf)

    tile = 2 * PAIR_BLOCK
    nt = tc // tile
    b3 = cum.reshape(nt, tile, HEAD_W)
    q3 = q.reshape(nt, tile, HEAD_W)
    k3 = k.reshape(nt, tile, HEAD_W)
    v3 = v.reshape(nt, tile, HEAD_W)
    srow = lax.broadcasted_iota(jnp.int32, (nt, tile, 1), 1)
    low = srow < PAIR_BLOCK
    tloc = srow & (PAIR_BLOCK - 1)

    def block_row(x3, j):
        return jnp.where(low, x3[:, j:j + 1, :], x3[:, PAIR_BLOCK + j:PAIR_BLOCK + j + 1, :])

    o3 = jnp.zeros((nt, tile, HEAD_W), F32)
    for j in range(PAIR_BLOCK):
        causal = tloc >= j
        decay = jnp.exp(jnp.where(causal, b3 - block_row(b3, j), 0.0))
        w = jnp.sum(q3 * block_row(k3, j) * decay, axis=-1, keepdims=True)
        o3 = o3 + jnp.where(causal, w, 0.0) * block_row(v3, j)
    o = o3.reshape(tc, HEAD_W)

    vb = v.astype(BF16)
    attn = jnp.zeros((tc, tc), F32)
    half = tc // 2
    while half >= PAIR_BLOCK:
        blk = 2 * half
        nblk = tc // blk
        bl = cum.reshape(nblk, blk, HEAD_W)
        x = bl - bl[:, half - 1:half, :]
        second = lax.broadcasted_iota(jnp.int32, (nblk, blk, 1), 1) >= half
        e = jnp.exp(jnp.where(second, x, -x))
        qt = jnp.where(second, q.reshape(nblk, blk, HEAD_W) * e, 0.0).reshape(tc, HEAD_W).astype(BF16)
        kt = jnp.where(second, 0.0, k.reshape(nblk, blk, HEAD_W) * e).reshape(tc, HEAD_W).astype(BF16)
        shift = blk.bit_length() - 1
        same = (row >> shift) == (col >> shift)
        attn = attn + jnp.where(same, _dot_nt(qt, kt), 0.0)
        half //= 2
    o = o + _dot(attn.astype(BF16), vb)

    st = st_ref[h]
    o = o + _dot_nt((q * jnp.exp(cum)).astype(BF16), st.astype(BF16))
    last = cum[tc - 1:tc, :]
    kd = (k * jnp.exp(last - cum)).astype(BF16)
    st_ref[h] = st * jnp.exp(last) + _dot_tn(vb, kd)
    return o


def _gla_finish(o, nw, gate):
    return _rmsnorm(o, nw) * (gate * jax.nn.sigmoid(gate))


def _gla_state_io(c, s0_ref, st_ref, heads):
    @pl.when(c == 0)
    def _():
        for h in range(heads):
            st_ref[h] = s0_ref[0, h].T


def _gla_state_out(c, sout_ref, st_ref, heads):
    @pl.when(c == pl.num_programs(1) - 1)
    def _():
        for h in range(heads):
            sout_ref[0, h] = st_ref[h].T


def _hgrn_body(q_ref, f_ref, v_ref, g_ref, lb_ref, nw_ref, s0_ref, y_ref, sout_ref, st_ref, *, tc):
    c = pl.program_id(1)
    _gla_state_io(c, s0_ref, st_ref, A_HEADS)
    for h in range(A_HEADS):
        hs = slice(h * HEAD_W, (h + 1) * HEAD_W)
        zf = f_ref[:, hs]
        lb = lb_ref[:, hs]
        f = lb + (1.0 - lb) * jax.nn.sigmoid(zf)
        logf = jnp.log(jnp.maximum(f, F_MIN))
        k = (1.0 - lb) * jax.nn.sigmoid(-zf)
        zq = q_ref[:, hs]
        q = zq * jax.nn.sigmoid(zq) * (A_KDIM ** -0.5)
        o = _gla_head(q, k, v_ref[:, hs], logf, st_ref, h, tc)
        y_ref[:, hs] = _gla_finish(o, nw_ref[...], g_ref[:, hs])
    _gla_state_out(c, sout_ref, st_ref, A_HEADS)


def _gla_body(q_ref, k_ref, v_ref, g_ref, r_ref, wup_ref, bup_ref, nw_ref, s0_ref,
              y_ref, sout_ref, st_ref, *, tc):
    c = pl.program_id(1)
    _gla_state_io(c, s0_ref, st_ref, B_HEADS)
    r = _dot_hi(r_ref[...], wup_ref[...]) + bup_ref[...]
    logf_all = (jnp.minimum(r, 0.0) - jnp.log1p(jnp.exp(-jnp.abs(r)))) / B_TAU
    for h in range(B_HEADS):
        hs = slice(h * HEAD_W, (h + 1) * HEAD_W)
        q = q_ref[:, hs] * (B_KDIM ** -0.5)
        o = _gla_head(q, k_ref[:, hs], v_ref[:, hs], logf_all[:, hs], st_ref, h, tc)
        y_ref[:, hs] = _gla_finish(o, nw_ref[...], g_ref[:, hs])
    _gla_state_out(c, sout_ref, st_ref, B_HEADS)


def _recurrent_mixer(mode, z, row0, nseq, t, s0, nw, extra):
    tc = min(128, t)
    nc = t // tc
    rb0 = row0 // tc
    zcol = (Z_A if mode == "hgrn" else Z_B) // 4

    def zspec(k):
        return pl.BlockSpec((tc, MIX_W), lambda b, c: (rb0 + b * nc + c, zcol + k))

    const = lambda b, c: (0, 0)
    state_spec = pl.BlockSpec((1, 4, HEAD_W, HEAD_W), lambda b, c: (b, 0, 0, 0))
    if mode == "hgrn":
        body = functools.partial(_hgrn_body, tc=tc)
        in_specs = [zspec(0), zspec(1), zspec(2), zspec(3),
                    pl.BlockSpec((1, MIX_W), const), pl.BlockSpec((1, HEAD_W), const), state_spec]
        args = (z, z, z, z, extra[0], nw, s0)
    else:
        body = functools.partial(_gla_body, tc=tc)
        in_specs = [zspec(0), zspec(1), zspec(2), zspec(3),
                    pl.BlockSpec((tc, LANES), lambda b, c: (rb0 + b * nc + c, Z_SMALL)),
                    pl.BlockSpec((LANES, MIX_W), const), pl.BlockSpec((1, MIX_W), const),
                    pl.BlockSpec((1, HEAD_W), const), state_spec]
        args = (z, z, z, z, z, extra[0], extra[1], nw, s0)
    return pl.pallas_call(
        body,
        grid=(nseq, nc),
        in_specs=in_specs,
        out_specs=[pl.BlockSpec((tc, MIX_W), lambda b, c: (b * nc + c, 0)), state_spec],
        out_shape=[jax.ShapeDtypeStruct((nseq * t, MIX_W), F32),
                   jax.ShapeDtypeStruct((nseq, 4, HEAD_W, HEAD_W), F32)],
        scratch_shapes=[pltpu.VMEM((4, HEAD_W, HEAD_W), F32)],
        compiler_params=_params(("parallel", "arbitrary"), 32),
        name=mode + "_mixer",
    )(*args)


def _dsa_body(qc_ref, qi_ref, sm_ref, kc_ref, vc_ref, ki_ref, o_ref,
              sk_ref, qis_ref, wb_ref, qs_ref, m_ref, l_ref, acc_ref, *, tq, past, nkeys, topk):
    i = pl.program_id(1)
    qpos0 = past + i * tq
    last_chunk = (qpos0 + tq - 1) // CHUNK
    n_adm = jnp.minimum((last_chunk + 1) * CHUNK, nkeys)
    nkb = (n_adm + KEY_BLOCK - 1) // KEY_BLOCK
    grp = C_HEADS // C_KV_HEADS

    for j in range(IDX_HEADS):
        qis_ref[j * tq:(j + 1) * tq, :] = qi_ref[:, j * IDX_DIM:(j + 1) * IDX_DIM]
        wb_ref[j] = jnp.broadcast_to(sm_ref[:, SMALL_IW + j:SMALL_IW + j + 1], (tq, KEY_BLOCK))
    for hq in range(C_HEADS):
        g, r = divmod(hq, grp)
        qs_ref[g, r * tq:(r + 1) * tq, :] = qc_ref[:, hq * C_HDIM:(hq + 1) * C_HDIM]

    qchunk = (qpos0 + lax.broadcasted_iota(jnp.int32, (tq, 1), 0)) >> 6
    lane = lax.broadcasted_iota(jnp.int32, (1, KEY_BLOCK), 1)

    def admissible(kb):
        kpos = kb * KEY_BLOCK + lane
        return ((kpos >> 6) <= qchunk) & (kpos < nkeys), kpos

    def score_step(kb, carry):
        kib = ki_ref[0, pl.ds(pl.multiple_of(kb * KEY_BLOCK, KEY_BLOCK), KEY_BLOCK), :]
        d = _dot_nt(qis_ref[...], kib)
        sc = jnp.maximum(d[0:tq], 0.0) * wb_ref[0]
        for j in range(1, IDX_HEADS):
            sc = sc + jnp.maximum(d[j * tq:(j + 1) * tq], 0.0) * wb_ref[j]
        adm, _ = admissible(kb)
        sc = jnp.where(adm, sc * IDX_SCALE, NEG) + 0.0
        bits = pltpu.bitcast(sc, jnp.int32)
        sk_ref[kb] = jnp.where(bits < 0, bits ^ 0x7FFFFFFF, bits)
        return carry

    lax.fori_loop(0, nkb, score_step, 0)

    def count(pred):
        def step(kb, acc):
            _, kpos = admissible(kb)
            return acc + jnp.where(pred(sk_ref[kb], kpos), 1.0, 0.0)
        acc = lax.fori_loop(0, nkb, step, jnp.zeros((tq, KEY_BLOCK), F32))
        return jnp.sum(acc, axis=1, keepdims=True)

    kf = float(topk)
    c0 = count(lambda s, p: s >= 0)
    thr = jnp.where(c0 >= kf, 0, INT_MIN).astype(jnp.int32)

    def bit_step(it, thr):
        cand = thr | jnp.left_shift(jnp.int32(1), 30 - it)
        c = count(lambda s, p: s >= cand)
        return jnp.where(c >= kf, cand, thr)

    thr = lax.fori_loop(0, 31, bit_step, thr)

    c_ge = count(lambda s, p: s >= thr)
    c_gt = count(lambda s, p: s > thr)
    need = kf - c_gt
    nbits = max(1, (nkeys - 1).bit_length())

    def tie_cut():
        def cut_step(it, cut):
            cand = cut | jnp.left_shift(jnp.int32(1), nbits - 1 - it)
            c = count(lambda s, p: (s == thr) & (p < cand))
            return jnp.where(c < need, cand, cut)
        return lax.fori_loop(0, nbits, cut_step, jnp.zeros((tq, 1), jnp.int32))

    cut = lax.cond(jnp.max(c_ge) > kf, tie_cut, lambda: jnp.full((tq, 1), 2 ** 30, jnp.int32))

    m_ref[...] = jnp.full(m_ref.shape, NEG, F32)
    l_ref[...] = jnp.zeros(l_ref.shape, F32)
    acc_ref[...] = jnp.zeros(acc_ref.shape, F32)

    def attn_step(kb, carry):
        adm, kpos = admissible(kb)
        s_key = sk_ref[kb]
        sel = ((s_key > thr) | ((s_key == thr) & (kpos <= cut))) & adm
        k0 = pl.multiple_of(kb * KEY_BLOCK, KEY_BLOCK)
        kblk = kc_ref[0, pl.ds(k0, KEY_BLOCK), :]
        vblk = vc_ref[0, pl.ds(k0, KEY_BLOCK), :]
        for g in range(C_KV_HEADS):
            gs = slice(g * C_HDIM, (g + 1) * C_HDIM)
            s = _dot_nt(qs_ref[g], kblk[:, gs]).reshape(grp, tq, KEY_BLOCK)
            s = jnp.where(sel[None], s, NEG)
            m_old = m_ref[g]
            m_new = jnp.maximum(m_old, jnp.max(s, axis=-1, keepdims=True))
            p = jnp.where(sel[None], jnp.exp(s - m_new), 0.0)
            alpha = jnp.exp(m_old - m_new)
            l_ref[g] = alpha * l_ref[g] + jnp.sum(p, axis=-1, keepdims=True)
            pv = _dot(p.reshape(grp * tq, KEY_BLOCK).astype(BF16), vblk[:, gs])
            acc_ref[g] = alpha * acc_ref[g] + pv.reshape(grp, tq, C_HDIM)
            m_ref[g] = m_new
        return carry

    lax.fori_loop(0, nkb, attn_step, 0)

    for hq in range(C_HEADS):
        g, r = divmod(hq, grp)
        o_ref[:, hq * C_HDIM:(hq + 1) * C_HDIM] = acc_ref[g, r] / l_ref[g, r]


SUPER = 4
ATTN_BLOCKS = 2


def _dsa_t_body(qc_ref, qi_ref, sm_ref, kc_ref, vt_ref, ki_ref, o_ref,
                sk_ref, qit_ref, qct_ref, wrow_ref, acc_ref, ot_ref, s0_ref, s1_ref, p_ref,
                *, tq, past, nkeys, topk):
    i = pl.program_id(1)
    qpos0 = past + i * tq
    last_chunk = (qpos0 + tq - 1) // CHUNK
    n_adm = jnp.minimum((last_chunk + 1) * CHUNK, nkeys)
    nkb = (n_adm + KEY_BLOCK - 1) // KEY_BLOCK
    nsb = (nkb + SUPER - 1) // SUPER
    grp = C_HEADS // C_KV_HEADS

    qit = qi_ref[...].astype(F32).T
    for j in range(IDX_HEADS):
        qit_ref[:, j * tq:(j + 1) * tq] = qit[j * IDX_DIM:(j + 1) * IDX_DIM, :].astype(BF16)
    qct = qc_ref[...].astype(F32).T
    zeros = jnp.zeros((C_HDIM, tq), BF16)
    for hq in range(C_HEADS):
        g, r = divmod(hq, grp)
        for gg in range(C_KV_HEADS):
            val = qct[hq * C_HDIM:(hq + 1) * C_HDIM, :].astype(BF16) if gg == g else zeros
            qct_ref[g, gg * C_HDIM:(gg + 1) * C_HDIM, r * tq:(r + 1) * tq] = val
    wrow_ref[...] = sm_ref[...].T

    qchunk = (qpos0 + lax.broadcasted_iota(jnp.int32, (1, tq), 1)) >> 6
    ksub = lax.broadcasted_iota(jnp.int32, (KEY_BLOCK, 1), 0)

    def admissible(kb):
        kpos = kb * KEY_BLOCK + ksub
        return ((kpos >> 6) <= qchunk) & (kpos < nkeys)

    def score_step(kk, carry):
        for u in range(SUPER):
            kb = kk * SUPER + u
            kib = ki_ref[0, pl.ds(pl.multiple_of(kb * KEY_BLOCK, KEY_BLOCK), KEY_BLOCK), :]
            d = _dot(kib, qit_ref[...])
            sc = jnp.maximum(d[:, 0:tq], 0.0) * wrow_ref[SMALL_IW:SMALL_IW + 1, :]
            for j in range(1, IDX_HEADS):
                sc = sc + jnp.maximum(d[:, j * tq:(j + 1) * tq], 0.0) * wrow_ref[SMALL_IW + j:SMALL_IW + j + 1, :]
            bits = pltpu.bitcast(sc * IDX_SCALE + 0.0, jnp.int32)
            key = jnp.where(admissible(kb), jnp.where(bits < 0, bits ^ 0x7FFFFFFF, bits), INT_MIN)
            sk_ref[kb] = key
        return carry

    lax.fori_loop(0, nsb, score_step, 0)

    def fold(hit):
        return jnp.sum(hit.reshape(KEY_BLOCK // 8, 8, tq), axis=0)

    def count(pred):
        def step(kk, acc):
            for u in range(SUPER):
                kb = kk * SUPER + u
                acc = acc + fold(jnp.where(pred(sk_ref[kb], kb), 1.0, 0.0))
            return acc
        acc = lax.fori_loop(0, nsb, step, jnp.zeros((8, tq), F32))
        return jnp.sum(acc, axis=0, keepdims=True)

    kf = float(topk)
    c0 = count(lambda s, kb: s >= 0)
    t0 = jnp.where(c0 >= kf, 0, INT_MIN).astype(jnp.int32)

    def bit_step(it, t):
        cand = t | jnp.left_shift(jnp.int32(1), 30 - it)
        return jnp.where(count(lambda s, kb: s >= cand) >= kf, cand, t)
    thr = lax.fori_loop(0, 31, bit_step, t0)
    thr = jnp.maximum(thr, INT_MIN + 1)

    c_ge = count(lambda s, kb: s >= thr)
    nbits = max(1, (nkeys - 1).bit_length())

    @pl.when(jnp.max(c_ge) > kf)
    def _():
        need = kf - count(lambda s, kb: s > thr)

        def cut_step(it, cut):
            cand = cut | jnp.left_shift(jnp.int32(1), nbits - 1 - it)
            c = count(lambda s, kb: (s == thr) & (kb * KEY_BLOCK + ksub < cand))
            return jnp.where(c < need, cand, cut)
        cut = lax.fori_loop(0, nbits, cut_step, jnp.zeros((1, tq), jnp.int32))

        def strike(kb, carry):
            key = sk_ref[kb]
            sk_ref[kb] = jnp.where((key == thr) & (kb * KEY_BLOCK + ksub > cut), INT_MIN, key)
            return carry
        lax.fori_loop(0, nsb * SUPER, strike, 0)

    acc_ref[...] = jnp.zeros(acc_ref.shape, F32)
    init = tuple(jnp.full((1, grp * tq), NEG, F32) for _ in range(C_KV_HEADS))

    span = ATTN_BLOCKS * KEY_BLOCK
    n_attn = nsb * (SUPER // ATTN_BLOCKS)

    s_bufs = (s0_ref, s1_ref)

    def qk_scores(kk, half):
        kblk = kc_ref[0, pl.ds(pl.multiple_of(kk * span, span), span), :]
        bias = jnp.concatenate(
            [jnp.where(sk_ref[kk * ATTN_BLOCKS + u] >= thr, 0.0, NEG) for u in range(ATTN_BLOCKS)], axis=0)
        bias = jnp.concatenate([bias] * grp, axis=1)
        for g in range(C_KV_HEADS):
            s = _dot(kblk, qct_ref[g]) + bias
            for r in range(grp):
                s_bufs[half][g, r] = s[:, r * tq:(r + 1) * tq]

    def softmax_pv(kk, half, m):
        m = list(m)
        for g in range(C_KV_HEADS):
            m_cols, alpha_cols = [], []
            for r in range(grp):
                cols = slice(r * tq, (r + 1) * tq)
                sb = s_bufs[half][g, r]
                m_old = m[g][:, cols]
                m_new = jnp.maximum(m_old, jnp.max(sb, axis=0, keepdims=True))
                p_ref[half, g, r] = jnp.exp(sb - m_new).astype(BF16)
                m_cols.append(m_new)
                alpha_cols.append(jnp.exp(m_old - m_new))
            rows = slice(g * VT_ROWS, (g + 1) * VT_ROWS)
            pv = None
            for u in range(ATTN_BLOCKS):
                keys_u = slice(u * KEY_BLOCK, (u + 1) * KEY_BLOCK)
                p_u = jnp.concatenate([p_ref[half, g, r, keys_u, :] for r in range(grp)], axis=1)
                d = _dot(vt_ref[0, kk * ATTN_BLOCKS + u, rows, :], p_u)
                pv = d if pv is None else pv + d
            for r in range(grp):
                acc_ref[g, r] = alpha_cols[r] * acc_ref[g, r] + pv[:, r * tq:(r + 1) * tq]
            m[g] = jnp.concatenate(m_cols, axis=1)
        return tuple(m)

    qk_scores(0, 0)

    def pair_step(j, m):
        qk_scores(2 * j + 1, 1)
        m = softmax_pv(2 * j, 0, m)
        qk_scores(jnp.minimum(2 * j + 2, n_attn - 2), 0)
        return softmax_pv(2 * j + 1, 1, m)

    lax.fori_loop(0, n_attn // 2, pair_step, init)
    for hq in range(C_HEADS):
        g, r = divmod(hq, grp)
        ot_ref[hq * C_HDIM:(hq + 1) * C_HDIM, :] = acc_ref[g, r, 0:C_HDIM, :] / acc_ref[g, r, C_HDIM:C_HDIM + 1, :]
    o_ref[...] = ot_ref[...].T


def _dsa_t(qc, qi, z, row0, keys, vals_t, kidx, nseq, t, past, nkeys):
    tq = KEY_BLOCK
    nq = t // tq
    rb0 = row0 // tq
    lp = keys.shape[1]
    topk = min(TOPK_MAX, nkeys // 4)
    grp = C_HEADS // C_KV_HEADS
    qrow = lambda b, i: (rb0 + b * nq + i, 0)
    return pl.pallas_call(
        functools.partial(_dsa_t_body, tq=tq, past=past, nkeys=nkeys, topk=topk),
        grid=(nseq, nq),
        in_specs=[
            pl.BlockSpec((tq, C_WIDTH), qrow),
            pl.BlockSpec((tq, IDX_HEADS * IDX_DIM), qrow),
            pl.BlockSpec((tq, LANES), lambda b, i: (rb0 + b * nq + i, Z_SMALL)),
            pl.BlockSpec((1, lp, LANES), lambda b, i: (b, 0, 0)),
            pl.BlockSpec((1, lp // KEY_BLOCK, C_KV_HEADS * VT_ROWS, KEY_BLOCK), lambda b, i: (b, 0, 0, 0)),
            pl.BlockSpec((1, lp, IDX_DIM), lambda b, i: (b, 0, 0)),
        ],
        out_specs=pl.BlockSpec((tq, C_WIDTH), lambda b, i: (b * nq + i, 0)),
        out_shape=jax.ShapeDtypeStruct((nseq * t, C_WIDTH), F32),
        scratch_shapes=[
            pltpu.VMEM((lp // KEY_BLOCK, KEY_BLOCK, tq), jnp.int32),
            pltpu.VMEM((IDX_DIM, IDX_HEADS * tq), BF16),
            pltpu.VMEM((C_KV_HEADS, C_KV_HEADS * C_HDIM, grp * tq), BF16),
            pltpu.VMEM((LANES, tq), F32),
            pltpu.VMEM((C_KV_HEADS, grp, VT_ROWS, tq), F32),
            pltpu.VMEM((C_WIDTH, tq), F32),
            pltpu.VMEM((C_KV_HEADS, grp, ATTN_BLOCKS * KEY_BLOCK, tq), F32),
            pltpu.VMEM((C_KV_HEADS, grp, ATTN_BLOCKS * KEY_BLOCK, tq), F32),
            pltpu.VMEM((2, C_KV_HEADS, grp, ATTN_BLOCKS * KEY_BLOCK, tq), BF16),
        ],
        compiler_params=_params(("parallel", "arbitrary"), 48),
        name="dsa_mixer_t",
    )(qc, qi, z, keys, vals_t, kidx)


def _dsa(qc, qi, z, row0, keys, vals, kidx, nseq, t, past, nkeys):
    tq = min(128, t)
    nq = t // tq
    rb0 = row0 // tq
    lp = keys.shape[1]
    topk = min(TOPK_MAX, nkeys // 4)
    grp = C_HEADS // C_KV_HEADS
    qrow = lambda b, i: (rb0 + b * nq + i, 0)
    seq = lambda b, i: (b, 0, 0)
    return pl.pallas_call(
        functools.partial(_dsa_body, tq=tq, past=past, nkeys=nkeys, topk=topk),
        grid=(nseq, nq),
        in_specs=[
            pl.BlockSpec((tq, C_WIDTH), qrow),
            pl.BlockSpec((tq, IDX_HEADS * IDX_DIM), qrow),
            pl.BlockSpec((tq, LANES), lambda b, i: (rb0 + b * nq + i, Z_SMALL)),
            pl.BlockSpec((1, lp, LANES), seq),
            pl.BlockSpec((1, lp, LANES), seq),
            pl.BlockSpec((1, lp, IDX_DIM), seq),
        ],
        out_specs=pl.BlockSpec((tq, C_WIDTH), lambda b, i: (b * nq + i, 0)),
        out_shape=jax.ShapeDtypeStruct((nseq * t, C_WIDTH), F32),
        scratch_shapes=[
            pltpu.VMEM((lp // KEY_BLOCK, tq, KEY_BLOCK), jnp.int32),
            pltpu.VMEM((IDX_HEADS * tq, IDX_DIM), BF16),
            pltpu.VMEM((IDX_HEADS, tq, KEY_BLOCK), F32),
            pltpu.VMEM((C_KV_HEADS, grp * tq, C_HDIM), BF16),
            pltpu.VMEM((C_KV_HEADS, grp, tq, 1), F32),
            pltpu.VMEM((C_KV_HEADS, grp, tq, 1), F32),
            pltpu.VMEM((C_KV_HEADS, grp, tq, C_HDIM), F32),
        ],
        compiler_params=_params(("parallel", "arbitrary"), 48),
        name="dsa_mixer",
    )(qc, qi, z, keys, vals, kidx)


def _merge_body(h_ref, yap_ref, yas_ref, ybp_ref, ybs_ref, ocp_ref, ocs_ref, ga_ref, gb_ref, gc_ref,
                wa_ref, wb_ref, wc_ref, wo_ref, o_ref, *, prompt_tiles):
    is_prompt = pl.program_id(0) < prompt_tiles
    pick = lambda p_ref, s_ref: jnp.where(is_prompt, p_ref[...], s_ref[...]).astype(BF16)
    merged = (jax.nn.sigmoid(ga_ref[...]) * _dot(pick(yap_ref, yas_ref), wa_ref[...])
              + jax.nn.sigmoid(gb_ref[...]) * _dot(pick(ybp_ref, ybs_ref), wb_ref[...])
              + jax.nn.sigmoid(gc_ref[...]) * _dot(pick(ocp_ref, ocs_ref), wc_ref[...]))
    o_ref[...] = h_ref[...] + _dot(merged.astype(BF16), wo_ref[...])


def _merge(h, ya, yb, oc, z, wa, wb, wc, wo):
    n, d = h.shape
    tm = TOKEN_TILE
    pt = ya[0].shape[0] // tm
    assert ya[0].shape[0] % tm == 0 and ya[1].shape[0] % tm == 0 and n == ya[0].shape[0] + ya[1].shape[0]
    row = lambda i: (i, 0)
    const = lambda i: (0, 0)
    prow = lambda i: (jnp.minimum(i, pt - 1), 0)
    srow = lambda i: (jnp.maximum(i - pt, 0), 0)
    pair = [pl.BlockSpec((tm, MIX_W), prow), pl.BlockSpec((tm, MIX_W), srow)]
    return pl.pallas_call(
        functools.partial(_merge_body, prompt_tiles=pt),
        grid=(n // tm,),
        in_specs=[
            pl.BlockSpec((tm, d), row),
            *pair, *pair, *pair,
            pl.BlockSpec((tm, d), lambda i: (i, 0)),
            pl.BlockSpec((tm, d), lambda i: (i, 1)),
            pl.BlockSpec((tm, d), lambda i: (i, 2)),
            pl.BlockSpec((MIX_W, d), const),
            pl.BlockSpec((MIX_W, d), const),
            pl.BlockSpec((MIX_W, d), const),
            pl.BlockSpec((d, d), const),
        ],
        out_specs=pl.BlockSpec((tm, d), row),
        out_shape=jax.ShapeDtypeStruct((n, d), F32),
        compiler_params=_params(("parallel",), 48),
        name="gated_merge",
    )(h, *ya, *yb, *oc, z, z, z, wa, wb, wc, wo)


def _pad_heads(w, heads, dim):
    lead = w.shape[:-1]
    w = w.reshape(*lead, heads, dim)
    w = jnp.pad(w, [(0, 0)] * len(lead) + [(0, 0), (0, HEAD_W - dim)])
    return w.reshape(*lead, heads * HEAD_W)


def _layout_w_in(w):
    widths = (512, 512, 512, 512, 256, 256, 512, 16, 512, 512, 128, 128, 512, 64, 8, 1024, 1024, 1024)
    parts, o = [], 0
    for wd in widths:
        parts.append(w[:, o:o + wd])
        o += wd
    (a_q, a_f, a_i, a_g, b_q, b_k, b_v, b_r, b_g, c_q, c_k, c_v, i_q, i_k, i_w, g_a, g_b, g_c) = parts
    d = w.shape[0]
    small = jnp.concatenate([b_r, i_w, jnp.zeros((d, LANES - 24), w.dtype)], axis=1)
    i_k = jnp.pad(i_k, ((0, 0), (0, LANES - IDX_DIM)))
    cols = [g_a, g_b, g_c, a_q, a_f, a_i, a_g,
            _pad_heads(b_q, B_HEADS, B_KDIM), _pad_heads(b_k, B_HEADS, B_KDIM), b_v, b_g,
            c_q, i_q, c_k, c_v, i_k, small]
    out = jnp.concatenate(cols, axis=1)
    assert out.shape[1] == Z_WIDTH
    return out.astype(BF16)


def _rope_tables(pos):
    half = ROPE_DIMS // 2
    inv = ROPE_THETA ** (-jnp.arange(half, dtype=F32) * (2.0 / ROPE_DIMS))
    ang = pos.astype(F32)[:, None] * inv[None, :]
    cos, sin = jnp.cos(ang), jnp.sin(ang)
    n = pos.shape[0]
    ones = jnp.ones((n, C_HDIM - ROPE_DIMS), F32)
    zeros = jnp.zeros((n, C_HDIM - ROPE_DIMS), F32)
    zh = jnp.zeros((n, half), F32)
    c = jnp.concatenate([cos, cos, ones], axis=1)
    s_lo = jnp.concatenate([-sin, zh, zeros], axis=1)
    s_hi = jnp.concatenate([zh, sin, zeros], axis=1)
    two = lambda a: jnp.concatenate([a, a], axis=1)
    return two(c), two(s_lo), two(s_hi)


def kernel(x_prompt, x_sample, state_hgrn, state_gla, cache_k, cache_v, cache_kidx, hgrn_lb, w_in, w_gla_up, b_gla, norm_hgrn, norm_gla, w_br_a, w_br_b, w_br_c, w_out, norm_ffn1, norm_mix, norm_ffn2, ffn1_w1, ffn1_w3, ffn1_w2, ffn2_w1, ffn2_w3, ffn2_w2, norm_final):
    bp, tp, d = x_prompt.shape
    bs, ts, _ = x_sample.shape
    past = cache_k.shape[2]
    n_p, n_s = bp * tp, bs * ts

    lb_sm = jax.nn.softmax(hgrn_lb.astype(F32), axis=0)
    lb_all = jnp.concatenate([jnp.zeros_like(lb_sm[:1]), jnp.cumsum(lb_sm[1:], axis=0)], axis=0)

    pos = jnp.concatenate([jnp.tile(jnp.arange(tp, dtype=jnp.int32), bp),
                           jnp.tile(past + jnp.arange(ts, dtype=jnp.int32), bs)])
    cos_t, sin_lo, sin_hi = _rope_tables(pos)

    h = jnp.concatenate([x_prompt.reshape(n_p, d), x_sample.reshape(n_s, d)], axis=0)
    row2 = lambda a: a.reshape(1, -1)
    zero_state = jnp.zeros((bp, 4, HEAD_W, HEAD_W), F32)
    lp_s = -(-(past + ts) // KEY_BLOCK) * KEY_BLOCK
    key_pad = lp_s - past - ts

    outs = {k: [] for k in ("pa", "pb", "pk", "pv", "pki", "sa", "sb", "sk", "sv", "ski")}
    for l in range(DEPTH):
        bf = lambda a: a[l].astype(BF16)
        h = _ffn(h, row2(norm_ffn1[l]), bf(ffn1_w1), bf(ffn1_w3), bf(ffn1_w2), row2(norm_final), False)
        z = _inproj(h, row2(norm_mix[l]), _layout_w_in(w_in[l]))
        qc, qi, kc, vc, ki, kcb, vcb, kib, vt = _prep(z, cos_t, sin_lo, sin_hi)

        lb = row2(lb_all[l])
        nwa, nwb = row2(norm_hgrn[l]), row2(norm_gla[l])
        wup = jnp.pad(_pad_heads(w_gla_up[l], B_HEADS, B_KDIM), ((0, LANES - B_GATE_RANK), (0, 0)))
        bup = row2(_pad_heads(b_gla[l], B_HEADS, B_KDIM))
        sb0 = jnp.pad(state_gla[l], ((0, 0), (0, 0), (0, HEAD_W - B_KDIM), (0, 0)))

        ya_p, sa_p = _recurrent_mixer("hgrn", z, 0, bp, tp, zero_state, nwa, (lb,))
        ya_s, sa_s = _recurrent_mixer("hgrn", z, n_p, bs, ts, state_hgrn[l], nwa, (lb,))
        yb_p, sb_p = _recurrent_mixer("gla", z, 0, bp, tp, zero_state, nwb, (wup, bup))
        yb_s, sb_s = _recurrent_mixer("gla", z, n_p, bs, ts, sb0, nwb, (wup, bup))

        seqs = lambda a, n, t: a.reshape(n, t, a.shape[-1])
        vt_p = vt[:n_p // KEY_BLOCK].reshape(bp, tp // KEY_BLOCK, C_KV_HEADS * VT_ROWS, KEY_BLOCK)
        oc_p = _dsa_t(qc, qi, z, 0, seqs(kcb[:n_p], bp, tp), vt_p, seqs(kib[:n_p], bp, tp), bp, tp, 0, tp)

        def with_cache(cache, new):
            full = jnp.concatenate([cache.reshape(bs, past, -1).astype(BF16), seqs(new[n_p:], bs, ts)], axis=1)
            return jnp.pad(full, ((0, 0), (0, key_pad), (0, 0)))

        oc_s = _dsa(qc, qi, z, n_p, with_cache(cache_k[l], kcb), with_cache(cache_v[l], vcb),
                    with_cache(cache_kidx[l], kib), bs, ts, past, past + ts)

        h = _merge(h, (ya_p, ya_s), (yb_p, yb_s), (oc_p, oc_s), z,
                   bf(w_br_a), bf(w_br_b), bf(w_br_c), bf(w_out))
        h = _ffn(h, row2(norm_ffn2[l]), bf(ffn2_w1), bf(ffn2_w3), bf(ffn2_w2), row2(norm_final),
                 l == DEPTH - 1)

        outs["pa"].append(sa_p)
        outs["sa"].append(sa_s)
        outs["pb"].append(sb_p[:, :, :B_KDIM, :])
        outs["sb"].append(sb_s[:, :, :B_KDIM, :])
        outs["pk"].append(kc[:n_p].reshape(bp, tp, C_KV_HEADS, C_HDIM))
        outs["pv"].append(vc[:n_p].reshape(bp, tp, C_KV_HEADS, C_HDIM))
        outs["pki"].append(ki[:n_p].reshape(bp, tp, IDX_DIM))
        outs["sk"].append(kc[n_p:].reshape(bs, ts, C_KV_HEADS, C_HDIM))
        outs["sv"].append(vc[n_p:].reshape(bs, ts, C_KV_HEADS, C_HDIM))
        outs["ski"].append(ki[n_p:].reshape(bs, ts, IDX_DIM))

    st = {k: jnp.stack(v) for k, v in outs.items()}
    return (h[:n_p].reshape(bp, tp, d), h[n_p:].reshape(bs, ts, d),
            st["pa"], st["pb"], st["pk"], st["pv"], st["pki"],
            st["sa"], st["sb"], st["sk"], st["sv"], st["ski"])
```

```python
import functools

import jax
import jax.numpy as jnp
from jax import lax
from jax.experimental import pallas as pl
from jax.experimental.pallas import tpu as pltpu

F32 = jnp.float32
BF16 = jnp.bfloat16

D_MODEL = 1024
DEPTH = 2
CHUNK = 64
EPS = 1e-6
NEG = -1e30
F_MIN = 1e-30
A_HEADS = 4
A_KDIM = 128
A_VDIM = 128
B_HEADS = 4
B_KDIM = 64
B_VDIM = 128
B_GATE_RANK = 16
B_TAU = 16.0
C_HEADS = 8
C_KV_HEADS = 2
C_HDIM = 64
C_WIDTH = C_HEADS * C_HDIM
IDX_HEADS = 8
IDX_DIM = 64
IDX_SCALE = (IDX_HEADS * IDX_DIM) ** -0.5
TOPK_MAX = 256
ROPE_THETA = 500000.0
ROPE_DIMS = C_HDIM // 4
PAIR_BLOCK = 4
D_FF = 2816

LANES = 128
HEAD_W = 128
MIX_W = 4 * HEAD_W
TOKEN_TILE = 512
FF_TILE = D_FF // 2
KEY_BLOCK = 128
VT_ROWS = 80
INT_MIN = -(2 ** 31)

Z_GATES = 0
Z_A = 24
Z_B = 40
Z_CQ = 56
Z_IQ = 60
Z_CKVI = 64
Z_SMALL = 67
Z_WIDTH = 68 * LANES
SMALL_IW = B_GATE_RANK


def _params(sem, vmem_mb):
    return pltpu.CompilerParams(dimension_semantics=sem, vmem_limit_bytes=vmem_mb << 20)


def _dot(a, b):
    return jnp.dot(a, b, preferred_element_type=F32)


def _dot_nt(a, b):
    return lax.dot_general(a, b, (((1,), (1,)), ((), ())), preferred_element_type=F32)


def _dot_tn(a, b):
    return lax.dot_general(a, b, (((0,), (0,)), ((), ())), preferred_element_type=F32)


def _split3(x):
    h1 = x.astype(BF16)
    r1 = x - h1.astype(F32)
    h2 = r1.astype(BF16)
    h3 = (r1 - h2.astype(F32)).astype(BF16)
    return h1, h2, h3


def _dot_exact_lhs(m, x):
    h1, h2, h3 = _split3(x)
    return _dot(m, h1) + _dot(m, h2) + _dot(m, h3)


def _dot_hi(a, b):
    a1 = a.astype(BF16)
    a2 = (a - a1.astype(F32)).astype(BF16)
    b1 = b.astype(BF16)
    b2 = (b - b1.astype(F32)).astype(BF16)
    return _dot(a1, b1) + _dot(a1, b2) + _dot(a2, b1)


def _rmsnorm(x, g):
    return x * lax.rsqrt(jnp.mean(x * x, axis=-1, keepdims=True) + EPS) * g


def _ffn_body(x_ref, g_ref, w1_ref, w3_ref, w2_ref, gf_ref, o_ref, u_ref, acc_ref, *, final_norm):
    j = pl.program_id(1)

    @pl.when(j == 0)
    def _():
        u_ref[...] = _rmsnorm(x_ref[...], g_ref[...]).astype(BF16)
        acc_ref[...] = jnp.zeros_like(acc_ref)

    u = u_ref[...]
    a = _dot(u, w1_ref[...])
    b = _dot(u, w3_ref[...])
    hid = (a * jax.nn.sigmoid(a) * b).astype(BF16)
    acc_ref[...] += _dot(hid, w2_ref[...])

    @pl.when(j == pl.num_programs(1) - 1)
    def _():
        out = x_ref[...] + 0.5 * acc_ref[...]
        if final_norm:
            out = _rmsnorm(out, gf_ref[...])
        o_ref[...] = out


def _ffn(x, g, w1, w3, w2, gf, final_norm):
    n, d = x.shape
    dff = w1.shape[1]
    tm = min(TOKEN_TILE, n)
    tf = FF_TILE if dff == D_FF else dff
    return pl.pallas_call(
        functools.partial(_ffn_body, final_norm=final_norm),
        grid=(n // tm, dff // tf),
        in_specs=[
            pl.BlockSpec((tm, d), lambda i, j: (i, 0)),
            pl.BlockSpec((1, d), lambda i, j: (0, 0)),
            pl.BlockSpec((d, tf), lambda i, j: (0, j)),
            pl.BlockSpec((d, tf), lambda i, j: (0, j)),
            pl.BlockSpec((tf, d), lambda i, j: (j, 0)),
            pl.BlockSpec((1, d), lambda i, j: (0, 0)),
        ],
        out_specs=pl.BlockSpec((tm, d), lambda i, j: (i, 0)),
        out_shape=jax.ShapeDtypeStruct((n, d), F32),
        scratch_shapes=[pltpu.VMEM((tm, d), BF16), pltpu.VMEM((tm, d), F32)],
        compiler_params=_params(("parallel", "arbitrary"), 48),
        name="ffn_half_step",
    )(x, g, w1, w3, w2, gf)


def _inproj_body(x_ref, g_ref, w_ref, z_ref, u_ref):
    @pl.when(pl.program_id(1) == 0)
    def _():
        u_ref[...] = _rmsnorm(x_ref[...], g_ref[...]).astype(BF16)

    z_ref[...] = _dot(u_ref[...], w_ref[...])


def _inproj(h, g, w):
    n, d = h.shape
    zw = w.shape[1]
    tm = min(TOKEN_TILE, n)
    tn = zw // 4 if zw % (4 * LANES) == 0 else zw
    return pl.pallas_call(
        _inproj_body,
        grid=(n // tm, zw // tn),
        in_specs=[
            pl.BlockSpec((tm, d), lambda i, j: (i, 0)),
            pl.BlockSpec((1, d), lambda i, j: (0, 0)),
            pl.BlockSpec((d, tn), lambda i, j: (0, j)),
        ],
        out_specs=pl.BlockSpec((tm, tn), lambda i, j: (i, j)),
        out_shape=jax.ShapeDtypeStruct((n, zw), F32),
        scratch_shapes=[pltpu.VMEM((tm, d), BF16)],
        compiler_params=_params(("parallel", "arbitrary"), 32),
        name="in_projection",
    )(h, g, w)


def _rope(x, c, s_lo, s_hi):
    w = x.shape[1]
    rep = w // LANES
    if rep > 1:
        c = jnp.concatenate([c] * rep, axis=1)
        s_lo = jnp.concatenate([s_lo] * rep, axis=1)
        s_hi = jnp.concatenate([s_hi] * rep, axis=1)
    half = ROPE_DIMS // 2
    return x * c + pltpu.roll(x, half, 1) * s_hi + pltpu.roll(x, w - half, 1) * s_lo


def _prep_body(cq_ref, iq_ref, kv_ref, c_ref, slo_ref, shi_ref,
               qc_ref, qi_ref, kc_ref, vc_ref, ki_ref, kcb_ref, vcb_ref, kib_ref, vt_ref):
    c, s_lo, s_hi = c_ref[...], slo_ref[...], shi_ref[...]
    qc_ref[...] = (_rope(cq_ref[...], c, s_lo, s_hi) * (C_HDIM ** -0.5)).astype(BF16)
    qi_ref[...] = _rope(iq_ref[...], c, s_lo, s_hi).astype(BF16)
    kc = _rope(kv_ref[:, 0:LANES], c, s_lo, s_hi)
    vc = kv_ref[:, LANES:2 * LANES]
    ki = _rope(kv_ref[:, 2 * LANES:3 * LANES], c, s_lo, s_hi)[:, :IDX_DIM]
    kc_ref[...] = kc
    vc_ref[...] = vc
    ki_ref[...] = ki
    kcb_ref[...] = kc.astype(BF16)
    vcb_ref[...] = vc.astype(BF16)
    kib_ref[...] = ki.astype(BF16)
    ones = jnp.ones((VT_ROWS - C_HDIM, KEY_BLOCK), BF16)
    for kk in range(vt_ref.shape[0]):
        vt = vc[kk * KEY_BLOCK:(kk + 1) * KEY_BLOCK, :].T.astype(BF16)
        vt_ref[kk] = jnp.concatenate([vt[:C_HDIM], ones, vt[C_HDIM:], ones], axis=0)


def _prep(z, cos_t, sin_lo, sin_hi):
    n = z.shape[0]
    tm = min(TOKEN_TILE, n)
    row = lambda i: (i, 0)
    return pl.pallas_call(
        _prep_body,
        grid=(n // tm,),
        in_specs=[
            pl.BlockSpec((tm, MIX_W), lambda i: (i, Z_CQ // 4)),
            pl.BlockSpec((tm, MIX_W), lambda i: (i, Z_IQ // 4)),
            pl.BlockSpec((tm, MIX_W), lambda i: (i, Z_CKVI // 4)),
            pl.BlockSpec((tm, LANES), row),
            pl.BlockSpec((tm, LANES), row),
            pl.BlockSpec((tm, LANES), row),
        ],
        out_specs=[
            pl.BlockSpec((tm, C_WIDTH), row),
            pl.BlockSpec((tm, IDX_HEADS * IDX_DIM), row),
            pl.BlockSpec((tm, LANES), row),
            pl.BlockSpec((tm, LANES), row),
            pl.BlockSpec((tm, IDX_DIM), row),
            pl.BlockSpec((tm, LANES), row),
            pl.BlockSpec((tm, LANES), row),
            pl.BlockSpec((tm, IDX_DIM), row),
            pl.BlockSpec((tm // KEY_BLOCK, C_KV_HEADS * VT_ROWS, KEY_BLOCK), lambda i: (i, 0, 0)),
        ],
        out_shape=[
            jax.ShapeDtypeStruct((n, C_WIDTH), BF16),
            jax.ShapeDtypeStruct((n, IDX_HEADS * IDX_DIM), BF16),
            jax.ShapeDtypeStruct((n, LANES), F32),
            jax.ShapeDtypeStruct((n, LANES), F32),
            jax.ShapeDtypeStruct((n, IDX_DIM), F32),
            jax.ShapeDtypeStruct((n, LANES), BF16),
            jax.ShapeDtypeStruct((n, LANES), BF16),
            jax.ShapeDtypeStruct((n, IDX_DIM), BF16),
            jax.ShapeDtypeStruct((n // KEY_BLOCK, C_KV_HEADS * VT_ROWS, KEY_BLOCK), BF16),
        ],
        compiler_params=_params(("parallel",), 32),
        name="rotary_kv_staging",
    )(z, z, z, cos_t, sin_lo, sin_hi)


def _gla_head(q, k, v, logf, st_ref, h, tc):
    row = lax.broadcasted_iota(jnp.int32, (tc, tc), 0)
    col = lax.broadcasted_iota(jnp.int32, (tc, tc), 1)
    tril = (col <= row).astype(BF16)
    cum = _dot_exact_lhs(tril, logf)

    tile = 2 * PAIR_BLOCK
    nt = tc // tile
    b3 = cum.reshape(nt, tile, HEAD_W)
    q3 = q.reshape(nt, tile, HEAD_W)
    k3 = k.reshape(nt, tile, HEAD_W)
    v3 = v.reshape(nt, tile, HEAD_W)
    srow = lax.broadcasted_iota(jnp.int32, (nt, tile, 1), 1)
    low = srow < PAIR_BLOCK
    tloc = srow & (PAIR_BLOCK - 1)

    def block_row(x3, j):
        return jnp.where(low, x3[:, j:j + 1, :], x3[:, PAIR_BLOCK + j:PAIR_BLOCK + j + 1, :])

    o3 = jnp.zeros((nt, tile, HEAD_W), F32)
    for j in range(PAIR_BLOCK):
        causal = tloc >= j
        decay = jnp.exp(jnp.where(causal, b3 - block_row(b3, j), 0.0))
        w = jnp.sum(q3 * block_row(k3, j) * decay, axis=-1, keepdims=True)
        o3 = o3 + jnp.where(causal, w, 0.0) * block_row(v3, j)
    o = o3.reshape(tc, HEAD_W)

    vb = v.astype(BF16)
    attn = jnp.zeros((tc, tc), F32)
    half = tc // 2
    while half >= PAIR_BLOCK:
        blk = 2 * half
        nblk = tc // blk
        bl = cum.reshape(nblk, blk, HEAD_W)
        x = bl - bl[:, half - 1:half, :]
        second = lax.broadcasted_iota(jnp.int32, (nblk, blk, 1), 1) >= half
        e = jnp.exp(jnp.where(second, x, -x))
        qt = jnp.where(second, q.reshape(nblk, blk, HEAD_W) * e, 0.0).reshape(tc, HEAD_W).astype(BF16)
        kt = jnp.where(second, 0.0, k.reshape(nblk, blk, HEAD_W) * e).reshape(tc, HEAD_W).astype(BF16)
        shift = blk.bit_length() - 1
        same = (row >> shift) == (col >> shift)
        attn = attn + jnp.where(same, _dot_nt(qt, kt), 0.0)
        half //= 2
    o = o + _dot(attn.astype(BF16), vb)

    st = st_ref[h]
    o = o + _dot_nt((q * jnp.exp(cum)).astype(BF16), st.astype(BF16))
    last = cum[tc - 1:tc, :]
    kd = (k * jnp.exp(last - cum)).astype(BF16)
    st_ref[h] = st * jnp.exp(last) + _dot_tn(vb, kd)
    return o


def _gla_finish(o, nw, gate):
    return _rmsnorm(o, nw) * (gate * jax.nn.sigmoid(gate))


def _gla_state_io(c, s0_ref, st_ref, heads):
    @pl.when(c == 0)
    def _():
        for h in range(heads):
            st_ref[h] = s0_ref[0, h].T


def _gla_state_out(c, sout_ref, st_ref, heads):
    @pl.when(c == pl.num_programs(1) - 1)
    def _():
        for h in range(heads):
            sout_ref[0, h] = st_ref[h].T


def _hgrn_body(q_ref, f_ref, v_ref, g_ref, lb_ref, nw_ref, s0_ref, y_ref, sout_ref, st_ref, *, tc):
    c = pl.program_id(1)
    _gla_state_io(c, s0_ref, st_ref, A_HEADS)
    for h in range(A_HEADS):
        hs = slice(h * HEAD_W, (h + 1) * HEAD_W)
        zf = f_ref[:, hs]
        lb = lb_ref[:, hs]
        f = lb + (1.0 - lb) * jax.nn.sigmoid(zf)
        logf = jnp.log(jnp.maximum(f, F_MIN))
        k = (1.0 - lb) * jax.nn.sigmoid(-zf)
        zq = q_ref[:, hs]
        q = zq * jax.nn.sigmoid(zq) * (A_KDIM ** -0.5)
        o = _gla_head(q, k, v_ref[:, hs], logf, st_ref, h, tc)
        y_ref[:, hs] = _gla_finish(o, nw_ref[...], g_ref[:, hs])
    _gla_state_out(c, sout_ref, st_ref, A_HEADS)


def _gla_body(q_ref, k_ref, v_ref, g_ref, r_ref, wup_ref, bup_ref, nw_ref, s0_ref,
              y_ref, sout_ref, st_ref, *, tc):
    c = pl.program_id(1)
    _gla_state_io(c, s0_ref, st_ref, B_HEADS)
    r = _dot_hi(r_ref[...], wup_ref[...]) + bup_ref[...]
    logf_all = (jnp.minimum(r, 0.0) - jnp.log1p(jnp.exp(-jnp.abs(r)))) / B_TAU
    for h in range(B_HEADS):
        hs = slice(h * HEAD_W, (h + 1) * HEAD_W)
        q = q_ref[:, hs] * (B_KDIM ** -0.5)
        o = _gla_head(q, k_ref[:, hs], v_ref[:, hs], logf_all[:, hs], st_ref, h, tc)
        y_ref[:, hs] = _gla_finish(o, nw_ref[...], g_ref[:, hs])
    _gla_state_out(c, sout_ref, st_ref, B_HEADS)


def _recurrent_mixer(mode, z, row0, nseq, t, s0, nw, extra):
    tc = min(128, t)
    nc = t // tc
    rb0 = row0 // tc
    zcol = (Z_A if mode == "hgrn" else Z_B) // 4

    def zspec(k):
        return pl.BlockSpec((tc, MIX_W), lambda b, c: (rb0 + b * nc + c, zcol + k))

    const = lambda b, c: (0, 0)
    state_spec = pl.BlockSpec((1, 4, HEAD_W, HEAD_W), lambda b, c: (b, 0, 0, 0))
    if mode == "hgrn":
        body = functools.partial(_hgrn_body, tc=tc)
        in_specs = [zspec(0), zspec(1), zspec(2), zspec(3),
                    pl.BlockSpec((1, MIX_W), const), pl.BlockSpec((1, HEAD_W), const), state_spec]
        args = (z, z, z, z, extra[0], nw, s0)
    else:
        body = functools.partial(_gla_body, tc=tc)
        in_specs = [zspec(0), zspec(1), zspec(2), zspec(3),
                    pl.BlockSpec((tc, LANES), lambda b, c: (rb0 + b * nc + c, Z_SMALL)),
                    pl.BlockSpec((LANES, MIX_W), const), pl.BlockSpec((1, MIX_W), const),
                    pl.BlockSpec((1, HEAD_W), const), state_spec]
        args = (z, z, z, z, z, extra[0], extra[1], nw, s0)
    return pl.pallas_call(
        body,
        grid=(nseq, nc),
        in_specs=in_specs,
        out_specs=[pl.BlockSpec((tc, MIX_W), lambda b, c: (b * nc + c, 0)), state_spec],
        out_shape=[jax.ShapeDtypeStruct((nseq * t, MIX_W), F32),
                   jax.ShapeDtypeStruct((nseq, 4, HEAD_W, HEAD_W), F32)],
        scratch_shapes=[pltpu.VMEM((4, HEAD_W, HEAD_W), F32)],
        compiler_params=_params(("parallel", "arbitrary"), 32),
        name=mode + "_mixer",
    )(*args)


SUPER = 4
ATTN_BLOCKS = 2


def _dsa_t_body(qc_ref, qi_ref, sm_ref, kc_ref, vt_ref, ki_ref, o_ref,
                sk_ref, qit_ref, qct_ref, wrow_ref, acc_ref, ot_ref, s0_ref, s1_ref, p_ref,
                *, tq, past, nkeys, topk):
    i = pl.program_id(1)
    qpos0 = past + i * tq
    last_chunk = (qpos0 + tq - 1) // CHUNK
    n_adm = jnp.minimum((last_chunk + 1) * CHUNK, nkeys)
    nkb = (n_adm + KEY_BLOCK - 1) // KEY_BLOCK
    nsb = (nkb + SUPER - 1) // SUPER
    grp = C_HEADS // C_KV_HEADS

    qit = qi_ref[...].astype(F32).T
    for j in range(IDX_HEADS):
        qit_ref[:, j * tq:(j + 1) * tq] = qit[j * IDX_DIM:(j + 1) * IDX_DIM, :].astype(BF16)
    qct = qc_ref[...].astype(F32).T
    zeros = jnp.zeros((C_HDIM, tq), BF16)
    for hq in range(C_HEADS):
        g, r = divmod(hq, grp)
        for gg in range(C_KV_HEADS):
            val = qct[hq * C_HDIM:(hq + 1) * C_HDIM, :].astype(BF16) if gg == g else zeros
            qct_ref[g, gg * C_HDIM:(gg + 1) * C_HDIM, r * tq:(r + 1) * tq] = val
    wrow_ref[...] = sm_ref[...].T

    qchunk = (qpos0 + lax.broadcasted_iota(jnp.int32, (1, tq), 1)) >> 6
    ksub = lax.broadcasted_iota(jnp.int32, (KEY_BLOCK, 1), 0)

    def admissible(kb):
        kpos = kb * KEY_BLOCK + ksub
        return ((kpos >> 6) <= qchunk) & (kpos < nkeys)

    def score_step(kk, carry):
        for u in range(SUPER):
            kb = kk * SUPER + u
            kib = ki_ref[0, pl.ds(pl.multiple_of(kb * KEY_BLOCK, KEY_BLOCK), KEY_BLOCK), :]
            d = _dot(kib, qit_ref[...])
            sc = jnp.maximum(d[:, 0:tq], 0.0) * wrow_ref[SMALL_IW:SMALL_IW + 1, :]
            for j in range(1, IDX_HEADS):
                sc = sc + jnp.maximum(d[:, j * tq:(j + 1) * tq], 0.0) * wrow_ref[SMALL_IW + j:SMALL_IW + j + 1, :]
            bits = pltpu.bitcast(sc * IDX_SCALE + 0.0, jnp.int32)
            key = jnp.where(admissible(kb), jnp.where(bits < 0, bits ^ 0x7FFFFFFF, bits), INT_MIN)
            sk_ref[kb] = key
        return carry

    lax.fori_loop(0, nsb, score_step, 0)

    def fold(hit):
        return jnp.sum(hit.reshape(KEY_BLOCK // 8, 8, tq), axis=0)

    def count(pred):
        def step(kk, acc):
            for u in range(SUPER):
                kb = kk * SUPER + u
                acc = acc + fold(jnp.where(pred(sk_ref[kb], kb), 1.0, 0.0))
            return acc
        acc = lax.fori_loop(0, nsb, step, jnp.zeros((8, tq), F32))
        return jnp.sum(acc, axis=0, keepdims=True)

    kf = float(topk)
    c0 = count(lambda s, kb: s >= 0)
    t0 = jnp.where(c0 >= kf, 0, INT_MIN).astype(jnp.int32)

    def bit_step(it, t):
        cand = t | jnp.left_shift(jnp.int32(1), 30 - it)
        return jnp.where(count(lambda s, kb: s >= cand) >= kf, cand, t)
    thr = lax.fori_loop(0, 31, bit_step, t0)
    thr = jnp.maximum(thr, INT_MIN + 1)

    c_ge = count(lambda s, kb: s >= thr)
    nbits = max(1, (nkeys - 1).bit_length())

    @pl.when(jnp.max(c_ge) > kf)
    def _():
        need = kf - count(lambda s, kb: s > thr)

        def cut_step(it, cut):
            cand = cut | jnp.left_shift(jnp.int32(1), nbits - 1 - it)
            c = count(lambda s, kb: (s == thr) & (kb * KEY_BLOCK + ksub < cand))
            return jnp.where(c < need, cand, cut)
        cut = lax.fori_loop(0, nbits, cut_step, jnp.zeros((1, tq), jnp.int32))

        def strike(kb, carry):
            key = sk_ref[kb]
            sk_ref[kb] = jnp.where((key == thr) & (kb * KEY_BLOCK + ksub > cut), INT_MIN, key)
            return carry
        lax.fori_loop(0, nsb * SUPER, strike, 0)

    acc_ref[...] = jnp.zeros(acc_ref.shape, F32)
    init = tuple(jnp.full((1, grp * tq), NEG, F32) for _ in range(C_KV_HEADS))

    span = ATTN_BLOCKS * KEY_BLOCK
    n_attn = nsb * (SUPER // ATTN_BLOCKS)

    s_bufs = (s0_ref, s1_ref)

    def qk_scores(kk, half):
        kblk = kc_ref[0, pl.ds(pl.multiple_of(kk * span, span), span), :]
        bias = jnp.concatenate(
            [jnp.where(sk_ref[kk * ATTN_BLOCKS + u] >= thr, 0.0, NEG) for u in range(ATTN_BLOCKS)], axis=0)
        bias = jnp.concatenate([bias] * grp, axis=1)
        for g in range(C_KV_HEADS):
            s = _dot(kblk, qct_ref[g]) + bias
            for r in range(grp):
                s_bufs[half][g, r] = s[:, r * tq:(r + 1) * tq]

    def softmax_pv(kk, half, m):
        m = list(m)
        for g in range(C_KV_HEADS):
            m_cols, alpha_cols = [], []
            for r in range(grp):
                cols = slice(r * tq, (r + 1) * tq)
                sb = s_bufs[half][g, r]
                m_old = m[g][:, cols]
                m_new = jnp.maximum(m_old, jnp.max(sb, axis=0, keepdims=True))
                p_ref[half, g, r] = jnp.exp(sb - m_new).astype(BF16)
                m_cols.append(m_new)
                alpha_cols.append(jnp.exp(m_old - m_new))
            rows = slice(g * VT_ROWS, (g + 1) * VT_ROWS)
            pv = None
            for u in range(ATTN_BLOCKS):
                keys_u = slice(u * KEY_BLOCK, (u + 1) * KEY_BLOCK)
                p_u = jnp.concatenate([p_ref[half, g, r, keys_u, :] for r in range(grp)], axis=1)
                d = _dot(vt_ref[0, kk * ATTN_BLOCKS + u, rows, :], p_u)
                pv = d if pv is None else pv + d
            for r in range(grp):
                acc_ref[g, r] = alpha_cols[r] * acc_ref[g, r] + pv[:, r * tq:(r + 1) * tq]
            m[g] = jnp.concatenate(m_cols, axis=1)
        return tuple(m)

    qk_scores(0, 0)

    def pair_step(j, m):
        qk_scores(2 * j + 1, 1)
        m = softmax_pv(2 * j, 0, m)
        qk_scores(jnp.minimum(2 * j + 2, n_attn - 2), 0)
        return softmax_pv(2 * j + 1, 1, m)

    lax.fori_loop(0, n_attn // 2, pair_step, init)
    for hq in range(C_HEADS):
        g, r = divmod(hq, grp)
        ot_ref[hq * C_HDIM:(hq + 1) * C_HDIM, :] = acc_ref[g, r, 0:C_HDIM, :] / acc_ref[g, r, C_HDIM:C_HDIM + 1, :]
    o_ref[...] = ot_ref[...].T


def _dsa_t(qc, qi, small, small_col, keys, vals_t, kidx, nseq, t, past, nkeys):
    tq = KEY_BLOCK
    nq = t // tq
    lp = keys.shape[1]
    assert t % tq == 0 and lp % (SUPER * KEY_BLOCK) == 0
    topk = min(TOPK_MAX, nkeys // 4)
    grp = C_HEADS // C_KV_HEADS
    qrow = lambda b, i: (b * nq + i, 0)
    return pl.pallas_call(
        functools.partial(_dsa_t_body, tq=tq, past=past, nkeys=nkeys, topk=topk),
        grid=(nseq, nq),
        in_specs=[
            pl.BlockSpec((tq, C_WIDTH), qrow),
            pl.BlockSpec((tq, IDX_HEADS * IDX_DIM), qrow),
            pl.BlockSpec((tq, LANES), lambda b, i: (b * nq + i, small_col)),
            pl.BlockSpec((1, lp, LANES), lambda b, i: (b, 0, 0)),
            pl.BlockSpec((1, lp // KEY_BLOCK, C_KV_HEADS * VT_ROWS, KEY_BLOCK), lambda b, i: (b, 0, 0, 0)),
            pl.BlockSpec((1, lp, IDX_DIM), lambda b, i: (b, 0, 0)),
        ],
        out_specs=pl.BlockSpec((tq, C_WIDTH), lambda b, i: (b * nq + i, 0)),
        out_shape=jax.ShapeDtypeStruct((nseq * t, C_WIDTH), F32),
        scratch_shapes=[
            pltpu.VMEM((lp // KEY_BLOCK, KEY_BLOCK, tq), jnp.int32),
            pltpu.VMEM((IDX_DIM, IDX_HEADS * tq), BF16),
            pltpu.VMEM((C_KV_HEADS, C_KV_HEADS * C_HDIM, grp * tq), BF16),
            pltpu.VMEM((LANES, tq), F32),
            pltpu.VMEM((C_KV_HEADS, grp, VT_ROWS, tq), F32),
            pltpu.VMEM((C_WIDTH, tq), F32),
            pltpu.VMEM((C_KV_HEADS, grp, ATTN_BLOCKS * KEY_BLOCK, tq), F32),
            pltpu.VMEM((C_KV_HEADS, grp, ATTN_BLOCKS * KEY_BLOCK, tq), F32),
            pltpu.VMEM((2, C_KV_HEADS, grp, ATTN_BLOCKS * KEY_BLOCK, tq), BF16),
        ],
        compiler_params=_params(("parallel", "arbitrary"), 48),
        name="dsa_mixer_t",
    )(qc, qi, small, keys, vals_t, kidx)


def _merge_body(h_ref, yap_ref, yas_ref, ybp_ref, ybs_ref, ocp_ref, ocs_ref, ga_ref, gb_ref, gc_ref,
                wa_ref, wb_ref, wc_ref, wo_ref, o_ref, *, prompt_tiles):
    is_prompt = pl.program_id(0) < prompt_tiles
    pick = lambda p_ref, s_ref: jnp.where(is_prompt, p_ref[...], s_ref[...]).astype(BF16)
    merged = (jax.nn.sigmoid(ga_ref[...]) * _dot(pick(yap_ref, yas_ref), wa_ref[...])
              + jax.nn.sigmoid(gb_ref[...]) * _dot(pick(ybp_ref, ybs_ref), wb_ref[...])
              + jax.nn.sigmoid(gc_ref[...]) * _dot(pick(ocp_ref, ocs_ref), wc_ref[...]))
    o_ref[...] = h_ref[...] + _dot(merged.astype(BF16), wo_ref[...])


def _merge(h, ya, yb, oc, z, wa, wb, wc, wo):
    n, d = h.shape
    tm = TOKEN_TILE
    pt = ya[0].shape[0] // tm
    assert ya[0].shape[0] % tm == 0 and ya[1].shape[0] % tm == 0 and n == ya[0].shape[0] + ya[1].shape[0]
    row = lambda i: (i, 0)
    const = lambda i: (0, 0)
    prow = lambda i: (jnp.minimum(i, pt - 1), 0)
    srow = lambda i: (jnp.maximum(i - pt, 0), 0)
    pair = [pl.BlockSpec((tm, MIX_W), prow), pl.BlockSpec((tm, MIX_W), srow)]
    return pl.pallas_call(
        functools.partial(_merge_body, prompt_tiles=pt),
        grid=(n // tm,),
        in_specs=[
            pl.BlockSpec((tm, d), row),
            *pair, *pair, *pair,
            pl.BlockSpec((tm, d), lambda i: (i, 0)),
            pl.BlockSpec((tm, d), lambda i: (i, 1)),
            pl.BlockSpec((tm, d), lambda i: (i, 2)),
            pl.BlockSpec((MIX_W, d), const),
            pl.BlockSpec((MIX_W, d), const),
            pl.BlockSpec((MIX_W, d), const),
            pl.BlockSpec((d, d), const),
        ],
        out_specs=pl.BlockSpec((tm, d), row),
        out_shape=jax.ShapeDtypeStruct((n, d), F32),
        compiler_params=_params(("parallel",), 48),
        name="gated_merge",
    )(h, *ya, *yb, *oc, z, z, z, wa, wb, wc, wo)


def _pad_heads(w, heads, dim):
    lead = w.shape[:-1]
    w = w.reshape(*lead, heads, dim)
    w = jnp.pad(w, [(0, 0)] * len(lead) + [(0, 0), (0, HEAD_W - dim)])
    return w.reshape(*lead, heads * HEAD_W)


def _layout_w_in(w):
    widths = (512, 512, 512, 512, 256, 256, 512, 16, 512, 512, 128, 128, 512, 64, 8, 1024, 1024, 1024)
    parts, o = [], 0
    for wd in widths:
        parts.append(w[:, o:o + wd])
        o += wd
    (a_q, a_f, a_i, a_g, b_q, b_k, b_v, b_r, b_g, c_q, c_k, c_v, i_q, i_k, i_w, g_a, g_b, g_c) = parts
    d = w.shape[0]
    small = jnp.concatenate([b_r, i_w, jnp.zeros((d, LANES - 24), w.dtype)], axis=1)
    i_k = jnp.pad(i_k, ((0, 0), (0, LANES - IDX_DIM)))
    cols = [g_a, g_b, g_c, a_q, a_f, a_i, a_g,
            _pad_heads(b_q, B_HEADS, B_KDIM), _pad_heads(b_k, B_HEADS, B_KDIM), b_v, b_g,
            c_q, i_q, c_k, c_v, i_k, small]
    out = jnp.concatenate(cols, axis=1)
    assert out.shape[1] == Z_WIDTH
    return out.astype(BF16)


def _rope_tables(pos):
    half = ROPE_DIMS // 2
    inv = ROPE_THETA ** (-jnp.arange(half, dtype=F32) * (2.0 / ROPE_DIMS))
    ang = pos.astype(F32)[:, None] * inv[None, :]
    cos, sin = jnp.cos(ang), jnp.sin(ang)
    n = pos.shape[0]
    ones = jnp.ones((n, C_HDIM - ROPE_DIMS), F32)
    zeros = jnp.zeros((n, C_HDIM - ROPE_DIMS), F32)
    zh = jnp.zeros((n, half), F32)
    c = jnp.concatenate([cos, cos, ones], axis=1)
    s_lo = jnp.concatenate([-sin, zh, zeros], axis=1)
    s_hi = jnp.concatenate([zh, sin, zeros], axis=1)
    two = lambda a: jnp.concatenate([a, a], axis=1)
    return two(c), two(s_lo), two(s_hi)


def kernel(x_prompt, x_sample, state_hgrn, state_gla, cache_k, cache_v, cache_kidx, hgrn_lb, w_in, w_gla_up, b_gla, norm_hgrn, norm_gla, w_br_a, w_br_b, w_br_c, w_out, norm_ffn1, norm_mix, norm_ffn2, ffn1_w1, ffn1_w3, ffn1_w2, ffn2_w1, ffn2_w3, ffn2_w2, norm_final):
    bp, tp, d = x_prompt.shape
    bs, ts, _ = x_sample.shape
    past = cache_k.shape[2]
    n_p, n_s = bp * tp, bs * ts

    lb_sm = jax.nn.softmax(hgrn_lb.astype(F32), axis=0)
    lb_all = jnp.concatenate([jnp.zeros_like(lb_sm[:1]), jnp.cumsum(lb_sm[1:], axis=0)], axis=0)

    pos = jnp.concatenate([jnp.tile(jnp.arange(tp, dtype=jnp.int32), bp),
                           jnp.tile(past + jnp.arange(ts, dtype=jnp.int32), bs)])
    cos_t, sin_lo, sin_hi = _rope_tables(pos)

    h = jnp.concatenate([x_prompt.reshape(n_p, d), x_sample.reshape(n_s, d)], axis=0)
    row2 = lambda a: a.reshape(1, -1)
    zero_state = jnp.zeros((bp, 4, HEAD_W, HEAD_W), F32)
    span_s = SUPER * KEY_BLOCK
    key_pad = -(-(past + ts) // span_s) * span_s - past - ts
    q_rep = KEY_BLOCK // ts

    outs = {k: [] for k in ("pa", "pb", "pk", "pv", "pki", "sa", "sb", "sk", "sv", "ski")}
    for l in range(DEPTH):
        bf = lambda a: a[l].astype(BF16)
        h = _ffn(h, row2(norm_ffn1[l]), bf(ffn1_w1), bf(ffn1_w3), bf(ffn1_w2), row2(norm_final), False)
        z = _inproj(h, row2(norm_mix[l]), _layout_w_in(w_in[l]))
        qc, qi, kc, vc, ki, kcb, vcb, kib, vt = _prep(z, cos_t, sin_lo, sin_hi)

        lb = row2(lb_all[l])
        nwa, nwb = row2(norm_hgrn[l]), row2(norm_gla[l])
        wup = jnp.pad(_pad_heads(w_gla_up[l], B_HEADS, B_KDIM), ((0, LANES - B_GATE_RANK), (0, 0)))
        bup = row2(_pad_heads(b_gla[l], B_HEADS, B_KDIM))
        sb0 = jnp.pad(state_gla[l], ((0, 0), (0, 0), (0, HEAD_W - B_KDIM), (0, 0)))

        ya_p, sa_p = _recurrent_mixer("hgrn", z, 0, bp, tp, zero_state, nwa, (lb,))
        ya_s, sa_s = _recurrent_mixer("hgrn", z, n_p, bs, ts, state_hgrn[l], nwa, (lb,))
        yb_p, sb_p = _recurrent_mixer("gla", z, 0, bp, tp, zero_state, nwb, (wup, bup))
        yb_s, sb_s = _recurrent_mixer("gla", z, n_p, bs, ts, sb0, nwb, (wup, bup))

        seqs = lambda a, n, t: a.reshape(n, t, a.shape[-1])
        vt_p = vt[:n_p // KEY_BLOCK].reshape(bp, tp // KEY_BLOCK, C_KV_HEADS * VT_ROWS, KEY_BLOCK)
        oc_p = _dsa_t(qc, qi, z, Z_SMALL, seqs(kcb[:n_p], bp, tp), vt_p, seqs(kib[:n_p], bp, tp), bp, tp, 0, tp)

        def with_cache(cache, new):
            full = jnp.concatenate([cache.reshape(bs, past, -1).astype(BF16), seqs(new[n_p:], bs, ts)], axis=1)
            return jnp.pad(full, ((0, 0), (0, key_pad), (0, 0)))

        def fill_block(a):
            return jnp.tile(seqs(a[n_p:], bs, ts), (1, q_rep, 1)).reshape(bs * KEY_BLOCK, a.shape[-1])

        vals_s = with_cache(cache_v[l], vcb)
        vt_s = jnp.swapaxes(vals_s.reshape(bs, -1, KEY_BLOCK, LANES), 2, 3)
        ones = jnp.ones(vt_s.shape[:2] + (VT_ROWS - C_HDIM, KEY_BLOCK), BF16)
        vt_s = jnp.concatenate([vt_s[:, :, :C_HDIM], ones, vt_s[:, :, C_HDIM:], ones], axis=2)
        small_s = fill_block(z[:, Z_SMALL * LANES:(Z_SMALL + 1) * LANES])
        oc_s = _dsa_t(fill_block(qc), fill_block(qi), small_s, 0, with_cache(cache_k[l], kcb), vt_s,
                      with_cache(cache_kidx[l], kib), bs, KEY_BLOCK, past, past + ts)
        oc_s = oc_s.reshape(bs, KEY_BLOCK, C_WIDTH)[:, :ts].reshape(n_s, C_WIDTH)

        h = _merge(h, (ya_p, ya_s), (yb_p, yb_s), (oc_p, oc_s), z,
                   bf(w_br_a), bf(w_br_b), bf(w_br_c), bf(w_out))
        h = _ffn(h, row2(norm_ffn2[l]), bf(ffn2_w1), bf(ffn2_w3), bf(ffn2_w2), row2(norm_final),
                 l == DEPTH - 1)

        outs["pa"].append(sa_p)
        outs["sa"].append(sa_s)
        outs["pb"].append(sb_p[:, :, :B_KDIM, :])
        outs["sb"].append(sb_s[:, :, :B_KDIM, :])
        outs["pk"].append(kc[:n_p].reshape(bp, tp, C_KV_HEADS, C_HDIM))
        outs["pv"].append(vc[:n_p].reshape(bp, tp, C_KV_HEADS, C_HDIM))
        outs["pki"].append(ki[:n_p].reshape(bp, tp, IDX_DIM))
        outs["sk"].append(kc[n_p:].reshape(bs, ts, C_KV_HEADS, C_HDIM))
        outs["sv"].append(vc[n_p:].reshape(bs, ts, C_KV_HEADS, C_HDIM))
        outs["ski"].append(ki[n_p:].reshape(bs, ts, IDX_DIM))

    st = {k: jnp.stack(v) for k, v in outs.items()}
    return (h[:n_p].reshape(bp, tp, d), h[n_p:].reshape(bs, ts, d),
            st["pa"], st["pb"], st["pk"], st["pv"], st["pki"],
            st["sa"], st["sb"], st["sk"], st["sv"], st["ski"])
```

```python
import functools

import jax
import jax.numpy as jnp
from jax import lax
from jax.experimental import pallas as pl
from jax.experimental.pallas import tpu as pltpu

F32 = jnp.float32
BF16 = jnp.bfloat16

D_MODEL = 1024
DEPTH = 2
CHUNK = 64
EPS = 1e-6
NEG = -1e30
F_MIN = 1e-30
A_HEADS = 4
A_KDIM = 128
A_VDIM = 128
B_HEADS = 4
B_KDIM = 64
B_VDIM = 128
B_GATE_RANK = 16
B_TAU = 16.0
C_HEADS = 8
C_KV_HEADS = 2
C_HDIM = 64
C_WIDTH = C_HEADS * C_HDIM
IDX_HEADS = 8
IDX_DIM = 64
IDX_SCALE = (IDX_HEADS * IDX_DIM) ** -0.5
TOPK_MAX = 256
ROPE_THETA = 500000.0
ROPE_DIMS = C_HDIM // 4
PAIR_BLOCK = 4
D_FF = 2816

LANES = 128
HEAD_W = 128
MIX_W = 4 * HEAD_W
TOKEN_TILE = 512
FF_TILE = D_FF // 2
KEY_BLOCK = 128
VT_ROWS = 80
INT_MIN = -(2 ** 31)

Z_GATES = 0
Z_A = 24
Z_B = 40
Z_CQ = 56
Z_IQ = 60
Z_CKVI = 64
Z_SMALL = 67
Z_WIDTH = 68 * LANES
SMALL_IW = B_GATE_RANK


def _params(sem, vmem_mb):
    return pltpu.CompilerParams(dimension_semantics=sem, vmem_limit_bytes=vmem_mb << 20)


def _dot(a, b):
    return jnp.dot(a, b, preferred_element_type=F32)


def _dot_nt(a, b):
    return lax.dot_general(a, b, (((1,), (1,)), ((), ())), preferred_element_type=F32)


def _dot_tn(a, b):
    return lax.dot_general(a, b, (((0,), (0,)), ((), ())), preferred_element_type=F32)


def _split3(x):
    h1 = x.astype(BF16)
    r1 = x - h1.astype(F32)
    h2 = r1.astype(BF16)
    h3 = (r1 - h2.astype(F32)).astype(BF16)
    return h1, h2, h3


def _dot_exact_lhs(m, x):
    h1, h2, h3 = _split3(x)
    return _dot(m, h1) + _dot(m, h2) + _dot(m, h3)


def _dot_hi(a, b):
    a1 = a.astype(BF16)
    a2 = (a - a1.astype(F32)).astype(BF16)
    b1 = b.astype(BF16)
    b2 = (b - b1.astype(F32)).astype(BF16)
    return _dot(a1, b1) + _dot(a1, b2) + _dot(a2, b1)


def _rmsnorm(x, g):
    return x * lax.rsqrt(jnp.mean(x * x, axis=-1, keepdims=True) + EPS) * g


def _ffn_body(x_ref, g_ref, w1_ref, w3_ref, w2_ref, gf_ref, o_ref, u_ref, acc_ref, *, final_norm):
    j = pl.program_id(1)

    @pl.when(j == 0)
    def _():
        u_ref[...] = _rmsnorm(x_ref[...], g_ref[...]).astype(BF16)
        acc_ref[...] = jnp.zeros_like(acc_ref)

    u = u_ref[...]
    a = _dot(u, w1_ref[...])
    b = _dot(u, w3_ref[...])
    hid = (a * jax.nn.sigmoid(a) * b).astype(BF16)
    acc_ref[...] += _dot(hid, w2_ref[...])

    @pl.when(j == pl.num_programs(1) - 1)
    def _():
        out = x_ref[...] + 0.5 * acc_ref[...]
        if final_norm:
            out = _rmsnorm(out, gf_ref[...])
        o_ref[...] = out


def _ffn(x, g, w1, w3, w2, gf, final_norm):
    n, d = x.shape
    dff = w1.shape[1]
    tm = min(TOKEN_TILE, n)
    tf = dff
    resident = pl.Buffered(1)
    return pl.pallas_call(
        functools.partial(_ffn_body, final_norm=final_norm),
        grid=(n // tm, dff // tf),
        in_specs=[
            pl.BlockSpec((tm, d), lambda i, j: (i, 0)),
            pl.BlockSpec((1, d), lambda i, j: (0, 0)),
            pl.BlockSpec((d, tf), lambda i, j: (0, j), pipeline_mode=resident),
            pl.BlockSpec((d, tf), lambda i, j: (0, j), pipeline_mode=resident),
            pl.BlockSpec((tf, d), lambda i, j: (j, 0), pipeline_mode=resident),
            pl.BlockSpec((1, d), lambda i, j: (0, 0)),
        ],
        out_specs=pl.BlockSpec((tm, d), lambda i, j: (i, 0)),
        out_shape=jax.ShapeDtypeStruct((n, d), F32),
        scratch_shapes=[pltpu.VMEM((tm, d), BF16), pltpu.VMEM((tm, d), F32)],
        compiler_params=_params(("parallel", "arbitrary"), 48),
        name="ffn_half_step",
    )(x, g, w1, w3, w2, gf)


def _inproj_body(x_ref, g_ref, w_ref, z_ref, u_ref):
    @pl.when(pl.program_id(1) == 0)
    def _():
        u_ref[...] = _rmsnorm(x_ref[...], g_ref[...]).astype(BF16)

    z_ref[...] = _dot(u_ref[...], w_ref[...])


def _inproj(h, g, w):
    n, d = h.shape
    zw = w.shape[1]
    tm = min(TOKEN_TILE, n)
    tn = zw // 4 if zw % (4 * LANES) == 0 else zw
    return pl.pallas_call(
        _inproj_body,
        grid=(n // tm, zw // tn),
        in_specs=[
            pl.BlockSpec((tm, d), lambda i, j: (i, 0)),
            pl.BlockSpec((1, d), lambda i, j: (0, 0)),
            pl.BlockSpec((d, tn), lambda i, j: (0, j)),
        ],
        out_specs=pl.BlockSpec((tm, tn), lambda i, j: (i, j)),
        out_shape=jax.ShapeDtypeStruct((n, zw), F32),
        scratch_shapes=[pltpu.VMEM((tm, d), BF16)],
        compiler_params=_params(("parallel", "arbitrary"), 32),
        name="in_projection",
    )(h, g, w)


def _rope(x, c, s_lo, s_hi):
    w = x.shape[1]
    rep = w // LANES
    if rep > 1:
        c = jnp.concatenate([c] * rep, axis=1)
        s_lo = jnp.concatenate([s_lo] * rep, axis=1)
        s_hi = jnp.concatenate([s_hi] * rep, axis=1)
    half = ROPE_DIMS // 2
    return x * c + pltpu.roll(x, half, 1) * s_hi + pltpu.roll(x, w - half, 1) * s_lo


def _prep_body(cq_ref, iq_ref, kv_ref, c_ref, slo_ref, shi_ref,
               qc_ref, qi_ref, kc_ref, vc_ref, ki_ref, kcb_ref, vcb_ref, kib_ref, vt_ref):
    c, s_lo, s_hi = c_ref[...], slo_ref[...], shi_ref[...]
    qc_ref[...] = (_rope(cq_ref[...], c, s_lo, s_hi) * (C_HDIM ** -0.5)).astype(BF16)
    qi_ref[...] = _rope(iq_ref[...], c, s_lo, s_hi).astype(BF16)
    kc = _rope(kv_ref[:, 0:LANES], c, s_lo, s_hi)
    vc = kv_ref[:, LANES:2 * LANES]
    ki = _rope(kv_ref[:, 2 * LANES:3 * LANES], c, s_lo, s_hi)[:, :IDX_DIM]
    kc_ref[...] = kc
    vc_ref[...] = vc
    ki_ref[...] = ki
    kcb_ref[...] = kc.astype(BF16)
    vcb_ref[...] = vc.astype(BF16)
    kib_ref[...] = ki.astype(BF16)
    ones = jnp.ones((VT_ROWS - C_HDIM, KEY_BLOCK), BF16)
    for kk in range(vt_ref.shape[0]):
        vt = vc[kk * KEY_BLOCK:(kk + 1) * KEY_BLOCK, :].T.astype(BF16)
        vt_ref[kk] = jnp.concatenate([vt[:C_HDIM], ones, vt[C_HDIM:], ones], axis=0)


def _prep(z, cos_t, sin_lo, sin_hi):
    n = z.shape[0]
    tm = min(TOKEN_TILE, n)
    row = lambda i: (i, 0)
    return pl.pallas_call(
        _prep_body,
        grid=(n // tm,),
        in_specs=[
            pl.BlockSpec((tm, MIX_W), lambda i: (i, Z_CQ // 4)),
            pl.BlockSpec((tm, MIX_W), lambda i: (i, Z_IQ // 4)),
            pl.BlockSpec((tm, MIX_W), lambda i: (i, Z_CKVI // 4)),
            pl.BlockSpec((tm, LANES), row),
            pl.BlockSpec((tm, LANES), row),
            pl.BlockSpec((tm, LANES), row),
        ],
        out_specs=[
            pl.BlockSpec((tm, C_WIDTH), row),
            pl.BlockSpec((tm, IDX_HEADS * IDX_DIM), row),
            pl.BlockSpec((tm, LANES), row),
            pl.BlockSpec((tm, LANES), row),
            pl.BlockSpec((tm, IDX_DIM), row),
            pl.BlockSpec((tm, LANES), row),
            pl.BlockSpec((tm, LANES), row),
            pl.BlockSpec((tm, IDX_DIM), row),
            pl.BlockSpec((tm // KEY_BLOCK, C_KV_HEADS * VT_ROWS, KEY_BLOCK), lambda i: (i, 0, 0)),
        ],
        out_shape=[
            jax.ShapeDtypeStruct((n, C_WIDTH), BF16),
            jax.ShapeDtypeStruct((n, IDX_HEADS * IDX_DIM), BF16),
            jax.ShapeDtypeStruct((n, LANES), F32),
            jax.ShapeDtypeStruct((n, LANES), F32),
            jax.ShapeDtypeStruct((n, IDX_DIM), F32),
            jax.ShapeDtypeStruct((n, LANES), BF16),
            jax.ShapeDtypeStruct((n, LANES), BF16),
            jax.ShapeDtypeStruct((n, IDX_DIM), BF16),
            jax.ShapeDtypeStruct((n // KEY_BLOCK, C_KV_HEADS * VT_ROWS, KEY_BLOCK), BF16),
        ],
        compiler_params=_params(("parallel",), 32),
        name="rotary_kv_staging",
    )(z, z, z, cos_t, sin_lo, sin_hi)


def _gla_head(q, k, v, logf, st_ref, h, tc):
    row = lax.broadcasted_iota(jnp.int32, (tc, tc), 0)
    col = lax.broadcasted_iota(jnp.int32, (tc, tc), 1)
    tril = (col <= row).astype(BF16)
    cum = _dot_exact_lhs(tril, logf)

    tile = 2 * PAIR_BLOCK
    nt = tc // tile
    b3 = cum.reshape(nt, tile, HEAD_W)
    q3 = q.reshape(nt, tile, HEAD_W)
    k3 = k.reshape(nt, tile, HEAD_W)
    v3 = v.reshape(nt, tile, HEAD_W)
    srow = lax.broadcasted_iota(jnp.int32, (nt, tile, 1), 1)
    low = srow < PAIR_BLOCK
    tloc = srow & (PAIR_BLOCK - 1)

    def block_row(x3, j):
        return jnp.where(low, x3[:, j:j + 1, :], x3[:, PAIR_BLOCK + j:PAIR_BLOCK + j + 1, :])

    o3 = jnp.zeros((nt, tile, HEAD_W), F32)
    for j in range(PAIR_BLOCK):
        causal = tloc >= j
        decay = jnp.exp(jnp.where(causal, b3 - block_row(b3, j), 0.0))
        w = jnp.sum(q3 * block_row(k3, j) * decay, axis=-1, keepdims=True)
        o3 = o3 + jnp.where(causal, w, 0.0) * block_row(v3, j)
    o = o3.reshape(tc, HEAD_W)

    vb = v.astype(BF16)
    attn = jnp.zeros((tc, tc), F32)
    half = tc // 2
    while half >= PAIR_BLOCK:
        blk = 2 * half
        nblk = tc // blk
        bl = cum.reshape(nblk, blk, HEAD_W)
        x = bl - bl[:, half - 1:half, :]
        second = lax.broadcasted_iota(jnp.int32, (nblk, blk, 1), 1) >= half
        e = jnp.exp(jnp.where(second, x, -x))
        qt = jnp.where(second, q.reshape(nblk, blk, HEAD_W) * e, 0.0).reshape(tc, HEAD_W).astype(BF16)
        kt = jnp.where(second, 0.0, k.reshape(nblk, blk, HEAD_W) * e).reshape(tc, HEAD_W).astype(BF16)
        shift = blk.bit_length() - 1
        same = (row >> shift) == (col >> shift)
        attn = attn + jnp.where(same, _dot_nt(qt, kt), 0.0)
        half //= 2
    o = o + _dot(attn.astype(BF16), vb)

    st = st_ref[h]
    o = o + _dot_nt((q * jnp.exp(cum)).astype(BF16), st.astype(BF16))
    last = cum[tc - 1:tc, :]
    kd = (k * jnp.exp(last - cum)).astype(BF16)
    st_ref[h] = st * jnp.exp(last) + _dot_tn(vb, kd)
    return o


def _gla_finish(o, nw, gate):
    return _rmsnorm(o, nw) * (gate * jax.nn.sigmoid(gate))


def _gla_state_io(c, s0_ref, st_ref, heads):
    @pl.when(c == 0)
    def _():
        for h in range(heads):
            st_ref[h] = s0_ref[0, h].T


def _gla_state_out(c, sout_ref, st_ref, heads):
    @pl.when(c == pl.num_programs(1) - 1)
    def _():
        for h in range(heads):
            sout_ref[0, h] = st_ref[h].T


def _hgrn_body(q_ref, f_ref, v_ref, g_ref, lb_ref, nw_ref, s0_ref, y_ref, sout_ref, st_ref, *, tc):
    c = pl.program_id(1)
    _gla_state_io(c, s0_ref, st_ref, A_HEADS)
    for h in range(A_HEADS):
        hs = slice(h * HEAD_W, (h + 1) * HEAD_W)
        zf = f_ref[:, hs]
        lb = lb_ref[:, hs]
        f = lb + (1.0 - lb) * jax.nn.sigmoid(zf)
        logf = jnp.log(jnp.maximum(f, F_MIN))
        k = (1.0 - lb) * jax.nn.sigmoid(-zf)
        zq = q_ref[:, hs]
        q = zq * jax.nn.sigmoid(zq) * (A_KDIM ** -0.5)
        o = _gla_head(q, k, v_ref[:, hs], logf, st_ref, h, tc)
        y_ref[:, hs] = _gla_finish(o, nw_ref[...], g_ref[:, hs])
    _gla_state_out(c, sout_ref, st_ref, A_HEADS)


def _gla_body(q_ref, k_ref, v_ref, g_ref, r_ref, wup_ref, bup_ref, nw_ref, s0_ref,
              y_ref, sout_ref, st_ref, *, tc):
    c = pl.program_id(1)
    _gla_state_io(c, s0_ref, st_ref, B_HEADS)
    r = _dot_hi(r_ref[...], wup_ref[...]) + bup_ref[...]
    logf_all = (jnp.minimum(r, 0.0) - jnp.log1p(jnp.exp(-jnp.abs(r)))) / B_TAU
    for h in range(B_HEADS):
        hs = slice(h * HEAD_W, (h + 1) * HEAD_W)
        q = q_ref[:, hs] * (B_KDIM ** -0.5)
        o = _gla_head(q, k_ref[:, hs], v_ref[:, hs], logf_all[:, hs], st_ref, h, tc)
        y_ref[:, hs] = _gla_finish(o, nw_ref[...], g_ref[:, hs])
    _gla_state_out(c, sout_ref, st_ref, B_HEADS)


def _recurrent_mixer(mode, z, row0, nseq, t, s0, nw, extra):
    tc = min(128, t)
    nc = t // tc
    rb0 = row0 // tc
    zcol = (Z_A if mode == "hgrn" else Z_B) // 4

    def zspec(k):
        return pl.BlockSpec((tc, MIX_W), lambda b, c: (rb0 + b * nc + c, zcol + k))

    const = lambda b, c: (0, 0)
    state_spec = pl.BlockSpec((1, 4, HEAD_W, HEAD_W), lambda b, c: (b, 0, 0, 0))
    if mode == "hgrn":
        body = functools.partial(_hgrn_body, tc=tc)
        in_specs = [zspec(0), zspec(1), zspec(2), zspec(3),
                    pl.BlockSpec((1, MIX_W), const), pl.BlockSpec((1, HEAD_W), const), state_spec]
        args = (z, z, z, z, extra[0], nw, s0)
    else:
        body = functools.partial(_gla_body, tc=tc)
        in_specs = [zspec(0), zspec(1), zspec(2), zspec(3),
                    pl.BlockSpec((tc, LANES), lambda b, c: (rb0 + b * nc + c, Z_SMALL)),
                    pl.BlockSpec((LANES, MIX_W), const), pl.BlockSpec((1, MIX_W), const),
                    pl.BlockSpec((1, HEAD_W), const), state_spec]
        args = (z, z, z, z, z, extra[0], extra[1], nw, s0)
    return pl.pallas_call(
        body,
        grid=(nseq, nc),
        in_specs=in_specs,
        out_specs=[pl.BlockSpec((tc, MIX_W), lambda b, c: (b * nc + c, 0)), state_spec],
        out_shape=[jax.ShapeDtypeStruct((nseq * t, MIX_W), F32),
                   jax.ShapeDtypeStruct((nseq, 4, HEAD_W, HEAD_W), F32)],
        scratch_shapes=[pltpu.VMEM((4, HEAD_W, HEAD_W), F32)],
        compiler_params=_params(("parallel", "arbitrary"), 32),
        name=mode + "_mixer",
    )(*args)


SUPER = 4
ATTN_BLOCKS = 2


def _dsa_t_body(qc_ref, qi_ref, sm_ref, kc_ref, vt_ref, ki_ref, o_ref,
                sk_ref, qit_ref, qct_ref, wrow_ref, acc_ref, ot_ref, s0_ref, s1_ref, p_ref,
                *, tq, past, nkeys, topk):
    i = pl.program_id(1)
    qpos0 = past + i * tq
    last_chunk = (qpos0 + tq - 1) // CHUNK
    n_adm = jnp.minimum((last_chunk + 1) * CHUNK, nkeys)
    nkb = (n_adm + KEY_BLOCK - 1) // KEY_BLOCK
    nsb = (nkb + SUPER - 1) // SUPER
    grp = C_HEADS // C_KV_HEADS

    qit = qi_ref[...].astype(F32).T
    for j in range(IDX_HEADS):
        qit_ref[:, j * tq:(j + 1) * tq] = qit[j * IDX_DIM:(j + 1) * IDX_DIM, :].astype(BF16)
    qct = qc_ref[...].astype(F32).T
    zeros = jnp.zeros((C_HDIM, tq), BF16)
    for hq in range(C_HEADS):
        g, r = divmod(hq, grp)
        for gg in range(C_KV_HEADS):
            val = qct[hq * C_HDIM:(hq + 1) * C_HDIM, :].astype(BF16) if gg == g else zeros
            qct_ref[g, gg * C_HDIM:(gg + 1) * C_HDIM, r * tq:(r + 1) * tq] = val
    wrow_ref[...] = sm_ref[...].T

    qchunk = (qpos0 + lax.broadcasted_iota(jnp.int32, (1, tq), 1)) >> 6
    ksub = lax.broadcasted_iota(jnp.int32, (KEY_BLOCK, 1), 0)

    def admissible(kb):
        kpos = kb * KEY_BLOCK + ksub
        return ((kpos >> 6) <= qchunk) & (kpos < nkeys)

    def score_step(kk, carry):
        for u in range(SUPER):
            kb = kk * SUPER + u
            kib = ki_ref[0, pl.ds(pl.multiple_of(kb * KEY_BLOCK, KEY_BLOCK), KEY_BLOCK), :]
            d = _dot(kib, qit_ref[...])
            sc = jnp.maximum(d[:, 0:tq], 0.0) * wrow_ref[SMALL_IW:SMALL_IW + 1, :]
            for j in range(1, IDX_HEADS):
                sc = sc + jnp.maximum(d[:, j * tq:(j + 1) * tq], 0.0) * wrow_ref[SMALL_IW + j:SMALL_IW + j + 1, :]
            bits = pltpu.bitcast(sc * IDX_SCALE + 0.0, jnp.int32)
            key = jnp.where(admissible(kb), jnp.where(bits < 0, bits ^ 0x7FFFFFFF, bits), INT_MIN)
            sk_ref[kb] = key
        return carry

    lax.fori_loop(0, nsb, score_step, 0)

    def fold(hit):
        return jnp.sum(hit.reshape(KEY_BLOCK // 8, 8, tq), axis=0)

    def count(pred):
        def step(kk, acc):
            for u in range(SUPER):
                kb = kk * SUPER + u
                acc = acc + fold(jnp.where(pred(sk_ref[kb], kb), 1.0, 0.0))
            return acc
        acc = lax.fori_loop(0, nsb, step, jnp.zeros((8, tq), F32))
        return jnp.sum(acc, axis=0, keepdims=True)

    kf = float(topk)
    c0 = count(lambda s, kb: s >= 0)
    t0 = jnp.where(c0 >= kf, 0, INT_MIN).astype(jnp.int32)

    def bit_step(it, t):
        cand = t | jnp.left_shift(jnp.int32(1), 30 - it)
        return jnp.where(count(lambda s, kb: s >= cand) >= kf, cand, t)
    thr = lax.fori_loop(0, 31, bit_step, t0)
    thr = jnp.maximum(thr, INT_MIN + 1)

    c_ge = count(lambda s, kb: s >= thr)
    nbits = max(1, (nkeys - 1).bit_length())

    @pl.when(jnp.max(c_ge) > kf)
    def _():
        need = kf - count(lambda s, kb: s > thr)

        def cut_step(it, cut):
            cand = cut | jnp.left_shift(jnp.int32(1), nbits - 1 - it)
            c = count(lambda s, kb: (s == thr) & (kb * KEY_BLOCK + ksub < cand))
            return jnp.where(c < need, cand, cut)
        cut = lax.fori_loop(0, nbits, cut_step, jnp.zeros((1, tq), jnp.int32))

        def strike(kb, carry):
            key = sk_ref[kb]
            sk_ref[kb] = jnp.where((key == thr) & (kb * KEY_BLOCK + ksub > cut), INT_MIN, key)
            return carry
        lax.fori_loop(0, nsb * SUPER, strike, 0)

    acc_ref[...] = jnp.zeros(acc_ref.shape, F32)
    init = tuple(jnp.full((1, grp * tq), NEG, F32) for _ in range(C_KV_HEADS))

    span = ATTN_BLOCKS * KEY_BLOCK
    n_attn = nsb * (SUPER // ATTN_BLOCKS)

    s_bufs = (s0_ref, s1_ref)

    def qk_scores(kk, half):
        kblk = kc_ref[0, pl.ds(pl.multiple_of(kk * span, span), span), :]
        bias = jnp.concatenate(
            [jnp.where(sk_ref[kk * ATTN_BLOCKS + u] >= thr, 0.0, NEG) for u in range(ATTN_BLOCKS)], axis=0)
        bias = jnp.concatenate([bias] * grp, axis=1)
        for g in range(C_KV_HEADS):
            s = _dot(kblk, qct_ref[g]) + bias
            for r in range(grp):
                s_bufs[half][g, r] = s[:, r * tq:(r + 1) * tq]

    def softmax_pv(kk, half, m):
        m = list(m)
        for g in range(C_KV_HEADS):
            m_cols, alpha_cols = [], []
            for r in range(grp):
                cols = slice(r * tq, (r + 1) * tq)
                sb = s_bufs[half][g, r]
                m_old = m[g][:, cols]
                m_new = jnp.maximum(m_old, jnp.max(sb, axis=0, keepdims=True))
                p_ref[half, g, r] = jnp.exp(sb - m_new).astype(BF16)
                m_cols.append(m_new)
                alpha_cols.append(jnp.exp(m_old - m_new))
            rows = slice(g * VT_ROWS, (g + 1) * VT_ROWS)
            pv = None
            for u in range(ATTN_BLOCKS):
                keys_u = slice(u * KEY_BLOCK, (u + 1) * KEY_BLOCK)
                p_u = jnp.concatenate([p_ref[half, g, r, keys_u, :] for r in range(grp)], axis=1)
                d = _dot(vt_ref[0, kk * ATTN_BLOCKS + u, rows, :], p_u)
                pv = d if pv is None else pv + d
            for r in range(grp):
                acc_ref[g, r] = alpha_cols[r] * acc_ref[g, r] + pv[:, r * tq:(r + 1) * tq]
            m[g] = jnp.concatenate(m_cols, axis=1)
        return tuple(m)

    qk_scores(0, 0)

    def pair_step(j, m):
        qk_scores(2 * j + 1, 1)
        m = softmax_pv(2 * j, 0, m)
        qk_scores(jnp.minimum(2 * j + 2, n_attn - 2), 0)
        return softmax_pv(2 * j + 1, 1, m)

    lax.fori_loop(0, n_attn // 2, pair_step, init)
    for hq in range(C_HEADS):
        g, r = divmod(hq, grp)
        ot_ref[hq * C_HDIM:(hq + 1) * C_HDIM, :] = acc_ref[g, r, 0:C_HDIM, :] / acc_ref[g, r, C_HDIM:C_HDIM + 1, :]
    o_ref[...] = ot_ref[...].T


def _dsa_t(qc, qi, small, small_col, keys, vals_t, kidx, nseq, t, past, nkeys):
    tq = KEY_BLOCK
    nq = t // tq
    lp = keys.shape[1]
    assert t % tq == 0 and lp % (SUPER * KEY_BLOCK) == 0
    topk = min(TOPK_MAX, nkeys // 4)
    grp = C_HEADS // C_KV_HEADS
    qrow = lambda b, i: (b * nq + i, 0)
    return pl.pallas_call(
        functools.partial(_dsa_t_body, tq=tq, past=past, nkeys=nkeys, topk=topk),
        grid=(nseq, nq),
        in_specs=[
            pl.BlockSpec((tq, C_WIDTH), qrow),
            pl.BlockSpec((tq, IDX_HEADS * IDX_DIM), qrow),
            pl.BlockSpec((tq, LANES), lambda b, i: (b * nq + i, small_col)),
            pl.BlockSpec((1, lp, LANES), lambda b, i: (b, 0, 0)),
            pl.BlockSpec((1, lp // KEY_BLOCK, C_KV_HEADS * VT_ROWS, KEY_BLOCK), lambda b, i: (b, 0, 0, 0)),
            pl.BlockSpec((1, lp, IDX_DIM), lambda b, i: (b, 0, 0)),
        ],
        out_specs=pl.BlockSpec((tq, C_WIDTH), lambda b, i: (b * nq + i, 0)),
        out_shape=jax.ShapeDtypeStruct((nseq * t, C_WIDTH), F32),
        scratch_shapes=[
            pltpu.VMEM((lp // KEY_BLOCK, KEY_BLOCK, tq), jnp.int32),
            pltpu.VMEM((IDX_DIM, IDX_HEADS * tq), BF16),
            pltpu.VMEM((C_KV_HEADS, C_KV_HEADS * C_HDIM, grp * tq), BF16),
            pltpu.VMEM((LANES, tq), F32),
            pltpu.VMEM((C_KV_HEADS, grp, VT_ROWS, tq), F32),
            pltpu.VMEM((C_WIDTH, tq), F32),
            pltpu.VMEM((C_KV_HEADS, grp, ATTN_BLOCKS * KEY_BLOCK, tq), F32),
            pltpu.VMEM((C_KV_HEADS, grp, ATTN_BLOCKS * KEY_BLOCK, tq), F32),
            pltpu.VMEM((2, C_KV_HEADS, grp, ATTN_BLOCKS * KEY_BLOCK, tq), BF16),
        ],
        compiler_params=_params(("parallel", "arbitrary"), 48),
        name="dsa_mixer_t",
    )(qc, qi, small, keys, vals_t, kidx)


def _merge_body(h_ref, yap_ref, yas_ref, ybp_ref, ybs_ref, ocp_ref, ocs_ref, ga_ref, gb_ref, gc_ref,
                wa_ref, wb_ref, wc_ref, wo_ref, o_ref, *, prompt_tiles):
    is_prompt = pl.program_id(0) < prompt_tiles
    pick = lambda p_ref, s_ref: jnp.where(is_prompt, p_ref[...], s_ref[...]).astype(BF16)
    merged = (jax.nn.sigmoid(ga_ref[...]) * _dot(pick(yap_ref, yas_ref), wa_ref[...])
              + jax.nn.sigmoid(gb_ref[...]) * _dot(pick(ybp_ref, ybs_ref), wb_ref[...])
              + jax.nn.sigmoid(gc_ref[...]) * _dot(pick(ocp_ref, ocs_ref), wc_ref[...]))
    o_ref[...] = h_ref[...] + _dot(merged.astype(BF16), wo_ref[...])


def _merge(h, ya, yb, oc, z, wa, wb, wc, wo):
    n, d = h.shape
    tm = TOKEN_TILE
    pt = ya[0].shape[0] // tm
    assert ya[0].shape[0] % tm == 0 and ya[1].shape[0] % tm == 0 and n == ya[0].shape[0] + ya[1].shape[0]
    row = lambda i: (i, 0)
    const = lambda i: (0, 0)
    prow = lambda i: (jnp.minimum(i, pt - 1), 0)
    srow = lambda i: (jnp.maximum(i - pt, 0), 0)
    pair = [pl.BlockSpec((tm, MIX_W), prow), pl.BlockSpec((tm, MIX_W), srow)]
    return pl.pallas_call(
        functools.partial(_merge_body, prompt_tiles=pt),
        grid=(n // tm,),
        in_specs=[
            pl.BlockSpec((tm, d), row),
            *pair, *pair, *pair,
            pl.BlockSpec((tm, d), lambda i: (i, 0)),
            pl.BlockSpec((tm, d), lambda i: (i, 1)),
            pl.BlockSpec((tm, d), lambda i: (i, 2)),
            pl.BlockSpec((MIX_W, d), const),
            pl.BlockSpec((MIX_W, d), const),
            pl.BlockSpec((MIX_W, d), const),
            pl.BlockSpec((d, d), const),
        ],
        out_specs=pl.BlockSpec((tm, d), row),
        out_shape=jax.ShapeDtypeStruct((n, d), F32),
        compiler_params=_params(("parallel",), 48),
        name="gated_merge",
    )(h, *ya, *yb, *oc, z, z, z, wa, wb, wc, wo)


def _pad_heads(w, heads, dim):
    lead = w.shape[:-1]
    w = w.reshape(*lead, heads, dim)
    w = jnp.pad(w, [(0, 0)] * len(lead) + [(0, 0), (0, HEAD_W - dim)])
    return w.reshape(*lead, heads * HEAD_W)


def _layout_w_in(w):
    widths = (512, 512, 512, 512, 256, 256, 512, 16, 512, 512, 128, 128, 512, 64, 8, 1024, 1024, 1024)
    parts, o = [], 0
    for wd in widths:
        parts.append(w[:, o:o + wd])
        o += wd
    (a_q, a_f, a_i, a_g, b_q, b_k, b_v, b_r, b_g, c_q, c_k, c_v, i_q, i_k, i_w, g_a, g_b, g_c) = parts
    d = w.shape[0]
    small = jnp.concatenate([b_r, i_w, jnp.zeros((d, LANES - 24), w.dtype)], axis=1)
    i_k = jnp.pad(i_k, ((0, 0), (0, LANES - IDX_DIM)))
    cols = [g_a, g_b, g_c, a_q, a_f, a_i, a_g,
            _pad_heads(b_q, B_HEADS, B_KDIM), _pad_heads(b_k, B_HEADS, B_KDIM), b_v, b_g,
            c_q, i_q, c_k, c_v, i_k, small]
    out = jnp.concatenate(cols, axis=1)
    assert out.shape[1] == Z_WIDTH
    return out.astype(BF16)


def _rope_tables(pos):
    half = ROPE_DIMS // 2
    inv = ROPE_THETA ** (-jnp.arange(half, dtype=F32) * (2.0 / ROPE_DIMS))
    ang = pos.astype(F32)[:, None] * inv[None, :]
    cos, sin = jnp.cos(ang), jnp.sin(ang)
    n = pos.shape[0]
    ones = jnp.ones((n, C_HDIM - ROPE_DIMS), F32)
    zeros = jnp.zeros((n, C_HDIM - ROPE_DIMS), F32)
    zh = jnp.zeros((n, half), F32)
    c = jnp.concatenate([cos, cos, ones], axis=1)
    s_lo = jnp.concatenate([-sin, zh, zeros], axis=1)
    s_hi = jnp.concatenate([zh, sin, zeros], axis=1)
    two = lambda a: jnp.concatenate([a, a], axis=1)
    return two(c), two(s_lo), two(s_hi)


def kernel(x_prompt, x_sample, state_hgrn, state_gla, cache_k, cache_v, cache_kidx, hgrn_lb, w_in, w_gla_up, b_gla, norm_hgrn, norm_gla, w_br_a, w_br_b, w_br_c, w_out, norm_ffn1, norm_mix, norm_ffn2, ffn1_w1, ffn1_w3, ffn1_w2, ffn2_w1, ffn2_w3, ffn2_w2, norm_final):
    bp, tp, d = x_prompt.shape
    bs, ts, _ = x_sample.shape
    past = cache_k.shape[2]
    n_p, n_s = bp * tp, bs * ts

    lb_sm = jax.nn.softmax(hgrn_lb.astype(F32), axis=0)
    lb_all = jnp.concatenate([jnp.zeros_like(lb_sm[:1]), jnp.cumsum(lb_sm[1:], axis=0)], axis=0)

    pos = jnp.concatenate([jnp.tile(jnp.arange(tp, dtype=jnp.int32), bp),
                           jnp.tile(past + jnp.arange(ts, dtype=jnp.int32), bs)])
    cos_t, sin_lo, sin_hi = _rope_tables(pos)

    h = jnp.concatenate([x_prompt.reshape(n_p, d), x_sample.reshape(n_s, d)], axis=0)
    row2 = lambda a: a.reshape(1, -1)
    zero_state = jnp.zeros((bp, 4, HEAD_W, HEAD_W), F32)
    span_s = SUPER * KEY_BLOCK
    key_pad = -(-(past + ts) // span_s) * span_s - past - ts
    q_rep = KEY_BLOCK // ts

    outs = {k: [] for k in ("pa", "pb", "pk", "pv", "pki", "sa", "sb", "sk", "sv", "ski")}
    for l in range(DEPTH):
        bf = lambda a: a[l].astype(BF16)
        h = _ffn(h, row2(norm_ffn1[l]), bf(ffn1_w1), bf(ffn1_w3), bf(ffn1_w2), row2(norm_final), False)
        z = _inproj(h, row2(norm_mix[l]), _layout_w_in(w_in[l]))
        qc, qi, kc, vc, ki, kcb, vcb, kib, vt = _prep(z, cos_t, sin_lo, sin_hi)

        lb = row2(lb_all[l])
        nwa, nwb = row2(norm_hgrn[l]), row2(norm_gla[l])
        wup = jnp.pad(_pad_heads(w_gla_up[l], B_HEADS, B_KDIM), ((0, LANES - B_GATE_RANK), (0, 0)))
        bup = row2(_pad_heads(b_gla[l], B_HEADS, B_KDIM))
        sb0 = jnp.pad(state_gla[l], ((0, 0), (0, 0), (0, HEAD_W - B_KDIM), (0, 0)))

        ya_p, sa_p = _recurrent_mixer("hgrn", z, 0, bp, tp, zero_state, nwa, (lb,))
        ya_s, sa_s = _recurrent_mixer("hgrn", z, n_p, bs, ts, state_hgrn[l], nwa, (lb,))
        yb_p, sb_p = _recurrent_mixer("gla", z, 0, bp, tp, zero_state, nwb, (wup, bup))
        yb_s, sb_s = _recurrent_mixer("gla", z, n_p, bs, ts, sb0, nwb, (wup, bup))

        seqs = lambda a, n, t: a.reshape(n, t, a.shape[-1])
        vt_p = vt[:n_p // KEY_BLOCK].reshape(bp, tp // KEY_BLOCK, C_KV_HEADS * VT_ROWS, KEY_BLOCK)
        oc_p = _dsa_t(qc, qi, z, Z_SMALL, seqs(kcb[:n_p], bp, tp), vt_p, seqs(kib[:n_p], bp, tp), bp, tp, 0, tp)

        def with_cache(cache, new):
            full = jnp.concatenate([cache.reshape(bs, past, -1).astype(BF16), seqs(new[n_p:], bs, ts)], axis=1)
            return jnp.pad(full, ((0, 0), (0, key_pad), (0, 0)))

        def fill_block(a):
            return jnp.tile(seqs(a[n_p:], bs, ts), (1, q_rep, 1)).reshape(bs * KEY_BLOCK, a.shape[-1])

        vals_s = with_cache(cache_v[l], vcb)
        vt_s = jnp.swapaxes(vals_s.reshape(bs, -1, KEY_BLOCK, LANES), 2, 3)
        ones = jnp.ones(vt_s.shape[:2] + (VT_ROWS - C_HDIM, KEY_BLOCK), BF16)
        vt_s = jnp.concatenate([vt_s[:, :, :C_HDIM], ones, vt_s[:, :, C_HDIM:], ones], axis=2)
        small_s = fill_block(z[:, Z_SMALL * LANES:(Z_SMALL + 1) * LANES])
        oc_s = _dsa_t(fill_block(qc), fill_block(qi), small_s, 0, with_cache(cache_k[l], kcb), vt_s,
                      with_cache(cache_kidx[l], kib), bs, KEY_BLOCK, past, past + ts)
        oc_s = oc_s.reshape(bs, KEY_BLOCK, C_WIDTH)[:, :ts].reshape(n_s, C_WIDTH)

        h = _merge(h, (ya_p, ya_s), (yb_p, yb_s), (oc_p, oc_s), z,
                   bf(w_br_a), bf(w_br_b), bf(w_br_c), bf(w_out))
        h = _ffn(h, row2(norm_ffn2[l]), bf(ffn2_w1), bf(ffn2_w3), bf(ffn2_w2), row2(norm_final),
                 l == DEPTH - 1)

        outs["pa"].append(sa_p)
        outs["sa"].append(sa_s)
        outs["pb"].append(sb_p[:, :, :B_KDIM, :])
        outs["sb"].append(sb_s[:, :, :B_KDIM, :])
        outs["pk"].append(kc[:n_p].reshape(bp, tp, C_KV_HEADS, C_HDIM))
        outs["pv"].append(vc[:n_p].reshape(bp, tp, C_KV_HEADS, C_HDIM))
        outs["pki"].append(ki[:n_p].reshape(bp, tp, IDX_DIM))
        outs["sk"].append(kc[n_p:].reshape(bs, ts, C_KV_HEADS, C_HDIM))
        outs["sv"].append(vc[n_p:].reshape(bs, ts, C_KV_HEADS, C_HDIM))
        outs["ski"].append(ki[n_p:].reshape(bs, ts, IDX_DIM))

    st = {k: jnp.stack(v) for k, v in outs.items()}
    return (h[:n_p].reshape(bp, tp, d), h[n_p:].reshape(bs, ts, d),
            st["pa"], st["pb"], st["pk"], st["pv"], st["pki"],
            st["sa"], st["sb"], st["sk"], st["sv"], st["ski"])
```

```python
import functools

import jax
import jax.numpy as jnp
from jax import lax
from jax.experimental import pallas as pl
from jax.experimental.pallas import tpu as pltpu

F32 = jnp.float32
BF16 = jnp.bfloat16

D_MODEL = 1024
DEPTH = 2
CHUNK = 64
EPS = 1e-6
NEG = -1e30
F_MIN = 1e-30
A_HEADS = 4
A_KDIM = 128
A_VDIM = 128
B_HEADS = 4
B_KDIM = 64
B_VDIM = 128
B_GATE_RANK = 16
B_TAU = 16.0
C_HEADS = 8
C_KV_HEADS = 2
C_HDIM = 64
C_WIDTH = C_HEADS * C_HDIM
IDX_HEADS = 8
IDX_DIM = 64
IDX_SCALE = (IDX_HEADS * IDX_DIM) ** -0.5
TOPK_MAX = 256
ROPE_THETA = 500000.0
ROPE_DIMS = C_HDIM // 4
PAIR_BLOCK = 4
D_FF = 2816

LANES = 128
HEAD_W = 128
MIX_W = 4 * HEAD_W
TOKEN_TILE = 512
KEY_BLOCK = 128
VT_ROWS = 80
INT_MIN = -(2 ** 31)

Z_GATES = 0
Z_A = 24
Z_B = 40
Z_SMALL = 56
Z_OUT_WIDTH = 57 * LANES
Z_CQ = 57
Z_IQ = 61
Z_CK = 65
Z_CV = 66
Z_IK = 67
Z_WIDTH = 68 * LANES
SMALL_IW = B_GATE_RANK


def _params(sem, vmem_mb):
    return pltpu.CompilerParams(dimension_semantics=sem, vmem_limit_bytes=vmem_mb << 20)


def _dot(a, b):
    return jnp.dot(a, b, preferred_element_type=F32)


def _dot_nt(a, b):
    return lax.dot_general(a, b, (((1,), (1,)), ((), ())), preferred_element_type=F32)


def _dot_tn(a, b):
    return lax.dot_general(a, b, (((0,), (0,)), ((), ())), preferred_element_type=F32)


def _split3(x):
    h1 = x.astype(BF16)
    r1 = x - h1.astype(F32)
    h2 = r1.astype(BF16)
    h3 = (r1 - h2.astype(F32)).astype(BF16)
    return h1, h2, h3


def _dot_exact_lhs(m, x):
    h1, h2, h3 = _split3(x)
    return _dot(m, h1) + _dot(m, h2) + _dot(m, h3)


def _dot_hi(a, b):
    a1 = a.astype(BF16)
    a2 = (a - a1.astype(F32)).astype(BF16)
    b1 = b.astype(BF16)
    b2 = (b - b1.astype(F32)).astype(BF16)
    return _dot(a1, b1) + _dot(a1, b2) + _dot(a2, b1)


def _rmsnorm(x, g):
    return x * lax.rsqrt(jnp.mean(x * x, axis=-1, keepdims=True) + EPS) * g


def _ffn_body(x_ref, g_ref, w1_ref, w3_ref, w2_ref, gf_ref, o_ref, u_ref, acc_ref, *, final_norm):
    j = pl.program_id(1)

    @pl.when(j == 0)
    def _():
        u_ref[...] = _rmsnorm(x_ref[...], g_ref[...]).astype(BF16)
        acc_ref[...] = jnp.zeros_like(acc_ref)

    u = u_ref[...]
    a = _dot(u, w1_ref[...])
    b = _dot(u, w3_ref[...])
    hid = (a * jax.nn.sigmoid(a) * b).astype(BF16)
    acc_ref[...] += _dot(hid, w2_ref[...])

    @pl.when(j == pl.num_programs(1) - 1)
    def _():
        out = x_ref[...] + 0.5 * acc_ref[...]
        if final_norm:
            out = _rmsnorm(out, gf_ref[...])
        o_ref[...] = out


def _ffn(x, g, w1, w3, w2, gf, final_norm):
    n, d = x.shape
    dff = w1.shape[1]
    tm = min(TOKEN_TILE, n)
    tf = dff
    resident = pl.Buffered(1)
    return pl.pallas_call(
        functools.partial(_ffn_body, final_norm=final_norm),
        grid=(n // tm, dff // tf),
        in_specs=[
            pl.BlockSpec((tm, d), lambda i, j: (i, 0)),
            pl.BlockSpec((1, d), lambda i, j: (0, 0)),
            pl.BlockSpec((d, tf), lambda i, j: (0, j), pipeline_mode=resident),
            pl.BlockSpec((d, tf), lambda i, j: (0, j), pipeline_mode=resident),
            pl.BlockSpec((tf, d), lambda i, j: (j, 0), pipeline_mode=resident),
            pl.BlockSpec((1, d), lambda i, j: (0, 0)),
        ],
        out_specs=pl.BlockSpec((tm, d), lambda i, j: (i, 0)),
        out_shape=jax.ShapeDtypeStruct((n, d), F32),
        scratch_shapes=[pltpu.VMEM((tm, d), BF16), pltpu.VMEM((tm, d), F32)],
        compiler_params=_params(("parallel", "arbitrary"), 48),
        name="ffn_half_step",
    )(x, g, w1, w3, w2, gf)


def _rope(x, c, s_lo, s_hi):
    w = x.shape[1]
    rep = w // LANES
    if rep > 1:
        c = jnp.concatenate([c] * rep, axis=1)
        s_lo = jnp.concatenate([s_lo] * rep, axis=1)
        s_hi = jnp.concatenate([s_hi] * rep, axis=1)
    half = ROPE_DIMS // 2
    return x * c + pltpu.roll(x, half, 1) * s_hi + pltpu.roll(x, w - half, 1) * s_lo


def _inproj_body(x_ref, g_ref, w_ref, c_ref, slo_ref, shi_ref,
                 z_ref, qc_ref, qi_ref, kc_ref, vc_ref, ki_ref, kcb_ref, vcb_ref, kib_ref, vt_ref):
    u = _rmsnorm(x_ref[...], g_ref[...]).astype(BF16)
    zf = _dot(u, w_ref[...])
    z_ref[...] = zf[:, :Z_OUT_WIDTH]

    unit = lambda first, count: zf[:, first * LANES:(first + count) * LANES]
    c, s_lo, s_hi = c_ref[...], slo_ref[...], shi_ref[...]
    qc_ref[...] = (_rope(unit(Z_CQ, 4), c, s_lo, s_hi) * (C_HDIM ** -0.5)).astype(BF16)
    qi_ref[...] = _rope(unit(Z_IQ, 4), c, s_lo, s_hi).astype(BF16)
    kc = _rope(unit(Z_CK, 1), c, s_lo, s_hi)
    vc = unit(Z_CV, 1)
    ki = _rope(unit(Z_IK, 1), c, s_lo, s_hi)[:, :IDX_DIM]
    kc_ref[...] = kc
    vc_ref[...] = vc
    ki_ref[...] = ki
    kcb_ref[...] = kc.astype(BF16)
    vcb_ref[...] = vc.astype(BF16)
    kib_ref[...] = ki.astype(BF16)
    ones = jnp.ones((VT_ROWS - C_HDIM, KEY_BLOCK), BF16)
    for kk in range(vt_ref.shape[0]):
        vt = vc[kk * KEY_BLOCK:(kk + 1) * KEY_BLOCK, :].T.astype(BF16)
        vt_ref[kk] = jnp.concatenate([vt[:C_HDIM], ones, vt[C_HDIM:], ones], axis=0)


def _inproj(h, g, w, cos_t, sin_lo, sin_hi):
    n, d = h.shape
    tm = min(TOKEN_TILE // 2, n)
    row = lambda i: (i, 0)
    return pl.pallas_call(
        _inproj_body,
        grid=(n // tm,),
        in_specs=[
            pl.BlockSpec((tm, d), row),
            pl.BlockSpec((1, d), lambda i: (0, 0)),
            pl.BlockSpec((d, Z_WIDTH), lambda i: (0, 0), pipeline_mode=pl.Buffered(1)),
            pl.BlockSpec((tm, LANES), row),
            pl.BlockSpec((tm, LANES), row),
            pl.BlockSpec((tm, LANES), row),
        ],
        out_specs=[
            pl.BlockSpec((tm, Z_OUT_WIDTH), row),
            pl.BlockSpec((tm, C_WIDTH), row),
            pl.BlockSpec((tm, IDX_HEADS * IDX_DIM), row),
            pl.BlockSpec((tm, LANES), row),
            pl.BlockSpec((tm, LANES), row),
            pl.BlockSpec((tm, IDX_DIM), row),
            pl.BlockSpec((tm, LANES), row),
            pl.BlockSpec((tm, LANES), row),
            pl.BlockSpec((tm, IDX_DIM), row),
            pl.BlockSpec((tm // KEY_BLOCK, C_KV_HEADS * VT_ROWS, KEY_BLOCK), lambda i: (i, 0, 0)),
        ],
        out_shape=[
            jax.ShapeDtypeStruct((n, Z_OUT_WIDTH), F32),
            jax.ShapeDtypeStruct((n, C_WIDTH), BF16),
            jax.ShapeDtypeStruct((n, IDX_HEADS * IDX_DIM), BF16),
            jax.ShapeDtypeStruct((n, LANES), F32),
            jax.ShapeDtypeStruct((n, LANES), F32),
            jax.ShapeDtypeStruct((n, IDX_DIM), F32),
            jax.ShapeDtypeStruct((n, LANES), BF16),
            jax.ShapeDtypeStruct((n, LANES), BF16),
            jax.ShapeDtypeStruct((n, IDX_DIM), BF16),
            jax.ShapeDtypeStruct((n // KEY_BLOCK, C_KV_HEADS * VT_ROWS, KEY_BLOCK), BF16),
        ],
        compiler_params=_params(("parallel",), 48),
        name="in_projection",
    )(h, g, w, cos_t, sin_lo, sin_hi)


def _gla_head(q, k, v, logf, st_ref, h, tc):
    row = lax.broadcasted_iota(jnp.int32, (tc, tc), 0)
    col = lax.broadcasted_iota(jnp.int32, (tc, tc), 1)
    tril = (col <= row).astype(BF16)
    cum = _dot_exact_lhs(tril, logf)

    tile = 2 * PAIR_BLOCK
    nt = tc // tile
    b3 = cum.reshape(nt, tile, HEAD_W)
    q3 = q.reshape(nt, tile, HEAD_W)
    k3 = k.reshape(nt, tile, HEAD_W)
    v3 = v.reshape(nt, tile, HEAD_W)
    srow = lax.broadcasted_iota(jnp.int32, (nt, tile, 1), 1)
    low = srow < PAIR_BLOCK
    tloc = srow & (PAIR_BLOCK - 1)

    def block_row(x3, j):
        return jnp.where(low, x3[:, j:j + 1, :], x3[:, PAIR_BLOCK + j:PAIR_BLOCK + j + 1, :])

    o3 = jnp.zeros((nt, tile, HEAD_W), F32)
    for j in range(PAIR_BLOCK):
        causal = tloc >= j
        decay = jnp.exp(jnp.where(causal, b3 - block_row(b3, j), 0.0))
        w = jnp.sum(q3 * block_row(k3, j) * decay, axis=-1, keepdims=True)
        o3 = o3 + jnp.where(causal, w, 0.0) * block_row(v3, j)
    o = o3.reshape(tc, HEAD_W)

    vb = v.astype(BF16)
    attn = jnp.zeros((tc, tc), F32)
    half = tc // 2
    while half >= PAIR_BLOCK:
        blk = 2 * half
        nblk = tc // blk
        bl = cum.reshape(nblk, blk, HEAD_W)
        x = bl - bl[:, half - 1:half, :]
        second = lax.broadcasted_iota(jnp.int32, (nblk, blk, 1), 1) >= half
        e = jnp.exp(jnp.where(second, x, -x))
        qt = jnp.where(second, q.reshape(nblk, blk, HEAD_W) * e, 0.0).reshape(tc, HEAD_W).astype(BF16)
        kt = jnp.where(second, 0.0, k.reshape(nblk, blk, HEAD_W) * e).reshape(tc, HEAD_W).astype(BF16)
        shift = blk.bit_length() - 1
        same = (row >> shift) == (col >> shift)
        attn = attn + jnp.where(same, _dot_nt(qt, kt), 0.0)
        half //= 2
    o = o + _dot(attn.astype(BF16), vb)

    st = st_ref[h]
    o = o + _dot_nt((q * jnp.exp(cum)).astype(BF16), st.astype(BF16))
    last = cum[tc - 1:tc, :]
    kd = (k * jnp.exp(last - cum)).astype(BF16)
    st_ref[h] = st * jnp.exp(last) + _dot_tn(vb, kd)
    return o


def _gla_finish(o, nw, gate):
    return _rmsnorm(o, nw) * (gate * jax.nn.sigmoid(gate))


def _gla_state_io(c, s0_ref, st_ref, heads):
    @pl.when(c == 0)
    def _():
        for h in range(heads):
            st_ref[h] = s0_ref[0, h].T


def _gla_state_out(c, sout_ref, st_ref, heads):
    @pl.when(c == pl.num_programs(1) - 1)
    def _():
        for h in range(heads):
            sout_ref[0, h] = st_ref[h].T


def _hgrn_body(q_ref, f_ref, v_ref, g_ref, lb_ref, nw_ref, s0_ref, y_ref, sout_ref, st_ref, *, tc):
    c = pl.program_id(1)
    _gla_state_io(c, s0_ref, st_ref, A_HEADS)
    for h in range(A_HEADS):
        hs = slice(h * HEAD_W, (h + 1) * HEAD_W)
        zf = f_ref[:, hs]
        lb = lb_ref[:, hs]
        f = lb + (1.0 - lb) * jax.nn.sigmoid(zf)
        logf = jnp.log(jnp.maximum(f, F_MIN))
        k = (1.0 - lb) * jax.nn.sigmoid(-zf)
        zq = q_ref[:, hs]
        q = zq * jax.nn.sigmoid(zq) * (A_KDIM ** -0.5)
        o = _gla_head(q, k, v_ref[:, hs], logf, st_ref, h, tc)
        y_ref[:, hs] = _gla_finish(o, nw_ref[...], g_ref[:, hs])
    _gla_state_out(c, sout_ref, st_ref, A_HEADS)


def _gla_body(q_ref, k_ref, v_ref, g_ref, r_ref, wup_ref, bup_ref, nw_ref, s0_ref,
              y_ref, sout_ref, st_ref, *, tc):
    c = pl.program_id(1)
    _gla_state_io(c, s0_ref, st_ref, B_HEADS)
    r = _dot_hi(r_ref[...], wup_ref[...]) + bup_ref[...]
    logf_all = (jnp.minimum(r, 0.0) - jnp.log1p(jnp.exp(-jnp.abs(r)))) / B_TAU
    for h in range(B_HEADS):
        hs = slice(h * HEAD_W, (h + 1) * HEAD_W)
        q = q_ref[:, hs] * (B_KDIM ** -0.5)
        o = _gla_head(q, k_ref[:, hs], v_ref[:, hs], logf_all[:, hs], st_ref, h, tc)
        y_ref[:, hs] = _gla_finish(o, nw_ref[...], g_ref[:, hs])
    _gla_state_out(c, sout_ref, st_ref, B_HEADS)


def _recurrent_mixer(mode, z, row0, nseq, t, s0, nw, extra):
    tc = min(128, t)
    nc = t // tc
    rb0 = row0 // tc
    zcol = (Z_A if mode == "hgrn" else Z_B) // 4

    def zspec(k):
        return pl.BlockSpec((tc, MIX_W), lambda b, c: (rb0 + b * nc + c, zcol + k))

    const = lambda b, c: (0, 0)
    state_spec = pl.BlockSpec((1, 4, HEAD_W, HEAD_W), lambda b, c: (b, 0, 0, 0))
    if mode == "hgrn":
        body = functools.partial(_hgrn_body, tc=tc)
        in_specs = [zspec(0), zspec(1), zspec(2), zspec(3),
                    pl.BlockSpec((1, MIX_W), const), pl.BlockSpec((1, HEAD_W), const), state_spec]
        args = (z, z, z, z, extra[0], nw, s0)
    else:
        body = functools.partial(_gla_body, tc=tc)
        in_specs = [zspec(0), zspec(1), zspec(2), zspec(3),
                    pl.BlockSpec((tc, LANES), lambda b, c: (rb0 + b * nc + c, Z_SMALL)),
                    pl.BlockSpec((LANES, MIX_W), const), pl.BlockSpec((1, MIX_W), const),
                    pl.BlockSpec((1, HEAD_W), const), state_spec]
        args = (z, z, z, z, z, extra[0], extra[1], nw, s0)
    return pl.pallas_call(
        body,
        grid=(nseq, nc),
        in_specs=in_specs,
        out_specs=[pl.BlockSpec((tc, MIX_W), lambda b, c: (b * nc + c, 0)), state_spec],
        out_shape=[jax.ShapeDtypeStruct((nseq * t, MIX_W), F32),
                   jax.ShapeDtypeStruct((nseq, 4, HEAD_W, HEAD_W), F32)],
        scratch_shapes=[pltpu.VMEM((4, HEAD_W, HEAD_W), F32)],
        compiler_params=_params(("parallel", "arbitrary"), 32),
        name=mode + "_mixer",
    )(*args)


SUPER = 4
ATTN_BLOCKS = 2


def _dsa_t_body(qc_ref, qi_ref, sm_ref, kc_ref, vt_ref, ki_ref, o_ref,
                sk_ref, qit_ref, qct_ref, wrow_ref, acc_ref, ot_ref, s0_ref, s1_ref, p_ref,
                *, tq, past, nkeys, topk):
    i = pl.program_id(1)
    qpos0 = past + i * tq
    last_chunk = (qpos0 + tq - 1) // CHUNK
    n_adm = jnp.minimum((last_chunk + 1) * CHUNK, nkeys)
    nkb = (n_adm + KEY_BLOCK - 1) // KEY_BLOCK
    nsb = (nkb + SUPER - 1) // SUPER
    grp = C_HEADS // C_KV_HEADS

    qit = qi_ref[...].astype(F32).T
    for j in range(IDX_HEADS):
        qit_ref[:, j * tq:(j + 1) * tq] = qit[j * IDX_DIM:(j + 1) * IDX_DIM, :].astype(BF16)
    qct = qc_ref[...].astype(F32).T
    zeros = jnp.zeros((C_HDIM, tq), BF16)
    for hq in range(C_HEADS):
        g, r = divmod(hq, grp)
        for gg in range(C_KV_HEADS):
            val = qct[hq * C_HDIM:(hq + 1) * C_HDIM, :].astype(BF16) if gg == g else zeros
            qct_ref[g, gg * C_HDIM:(gg + 1) * C_HDIM, r * tq:(r + 1) * tq] = val
    wrow_ref[...] = sm_ref[...].T

    qchunk = (qpos0 + lax.broadcasted_iota(jnp.int32, (1, tq), 1)) >> 6
    ksub = lax.broadcasted_iota(jnp.int32, (KEY_BLOCK, 1), 0)

    def admissible(kb):
        kpos = kb * KEY_BLOCK + ksub
        return ((kpos >> 6) <= qchunk) & (kpos < nkeys)

    def score_step(kk, carry):
        for u in range(SUPER):
            kb = kk * SUPER + u
            kib = ki_ref[0, pl.ds(pl.multiple_of(kb * KEY_BLOCK, KEY_BLOCK), KEY_BLOCK), :]
            d = _dot(kib, qit_ref[...])
            sc = jnp.maximum(d[:, 0:tq], 0.0) * wrow_ref[SMALL_IW:SMALL_IW + 1, :]
            for j in range(1, IDX_HEADS):
                sc = sc + jnp.maximum(d[:, j * tq:(j + 1) * tq], 0.0) * wrow_ref[SMALL_IW + j:SMALL_IW + j + 1, :]
            bits = pltpu.bitcast(sc * IDX_SCALE + 0.0, jnp.int32)
            key = jnp.where(admissible(kb), jnp.where(bits < 0, bits ^ 0x7FFFFFFF, bits), INT_MIN)
            sk_ref[kb] = key
        return carry

    lax.fori_loop(0, nsb, score_step, 0)

    def fold(hit):
        return jnp.sum(hit.reshape(KEY_BLOCK // 8, 8, tq), axis=0)

    def count(pred):
        def step(kk, acc):
            for u in range(SUPER):
                kb = kk * SUPER + u
                acc = acc + fold(jnp.where(pred(sk_ref[kb], kb), 1.0, 0.0))
            return acc
        acc = lax.fori_loop(0, nsb, step, jnp.zeros((8, tq), F32))
        return jnp.sum(acc, axis=0, keepdims=True)

    kf = float(topk)
    c0 = count(lambda s, kb: s >= 0)
    t0 = jnp.where(c0 >= kf, 0, INT_MIN).astype(jnp.int32)

    def bit_step(it, t):
        cand = t | jnp.left_shift(jnp.int32(1), 30 - it)
        return jnp.where(count(lambda s, kb: s >= cand) >= kf, cand, t)
    thr = lax.fori_loop(0, 31, bit_step, t0)
    thr = jnp.maximum(thr, INT_MIN + 1)

    c_ge = count(lambda s, kb: s >= thr)
    nbits = max(1, (nkeys - 1).bit_length())

    @pl.when(jnp.max(c_ge) > kf)
    def _():
        need = kf - count(lambda s, kb: s > thr)

        def cut_step(it, cut):
            cand = cut | jnp.left_shift(jnp.int32(1), nbits - 1 - it)
            c = count(lambda s, kb: (s == thr) & (kb * KEY_BLOCK + ksub < cand))
            return jnp.where(c < need, cand, cut)
        cut = lax.fori_loop(0, nbits, cut_step, jnp.zeros((1, tq), jnp.int32))

        def strike(kb, carry):
            key = sk_ref[kb]
            sk_ref[kb] = jnp.where((key == thr) & (kb * KEY_BLOCK + ksub > cut), INT_MIN, key)
            return carry
        lax.fori_loop(0, nsb * SUPER, strike, 0)

    acc_ref[...] = jnp.zeros(acc_ref.shape, F32)
    init = tuple(jnp.full((1, grp * tq), NEG, F32) for _ in range(C_KV_HEADS))

    span = ATTN_BLOCKS * KEY_BLOCK
    n_attn = nsb * (SUPER // ATTN_BLOCKS)

    s_bufs = (s0_ref, s1_ref)

    def qk_scores(kk, half):
        kblk = kc_ref[0, pl.ds(pl.multiple_of(kk * span, span), span), :]
        bias = jnp.concatenate(
            [jnp.where(sk_ref[kk * ATTN_BLOCKS + u] >= thr, 0.0, NEG) for u in range(ATTN_BLOCKS)], axis=0)
        bias = jnp.concatenate([bias] * grp, axis=1)
        for g in range(C_KV_HEADS):
            s = _dot(kblk, qct_ref[g]) + bias
            for r in range(grp):
                s_bufs[half][g, r] = s[:, r * tq:(r + 1) * tq]

    def softmax_pv(kk, half, m):
        m = list(m)
        for g in range(C_KV_HEADS):
            m_cols, alpha_cols = [], []
            for r in range(grp):
                cols = slice(r * tq, (r + 1) * tq)
                sb = s_bufs[half][g, r]
                m_old = m[g][:, cols]
                m_new = jnp.maximum(m_old, jnp.max(sb, axis=0, keepdims=True))
                p_ref[half, g, r] = jnp.exp(sb - m_new).astype(BF16)
                m_cols.append(m_new)
                alpha_cols.append(jnp.exp(m_old - m_new))
            rows = slice(g * VT_ROWS, (g + 1) * VT_ROWS)
            pv = None
            for u in range(ATTN_BLOCKS):
                keys_u = slice(u * KEY_BLOCK, (u + 1) * KEY_BLOCK)
                p_u = jnp.concatenate([p_ref[half, g, r, keys_u, :] for r in range(grp)], axis=1)
                d = _dot(vt_ref[0, kk * ATTN_BLOCKS + u, rows, :], p_u)
                pv = d if pv is None else pv + d
            for r in range(grp):
                acc_ref[g, r] = alpha_cols[r] * acc_ref[g, r] + pv[:, r * tq:(r + 1) * tq]
            m[g] = jnp.concatenate(m_cols, axis=1)
        return tuple(m)

    qk_scores(0, 0)

    def pair_step(j, m):
        qk_scores(2 * j + 1, 1)
        m = softmax_pv(2 * j, 0, m)
        qk_scores(jnp.minimum(2 * j + 2, n_attn - 2), 0)
        return softmax_pv(2 * j + 1, 1, m)

    lax.fori_loop(0, n_attn // 2, pair_step, init)
    for hq in range(C_HEADS):
        g, r = divmod(hq, grp)
        ot_ref[hq * C_HDIM:(hq + 1) * C_HDIM, :] = acc_ref[g, r, 0:C_HDIM, :] / acc_ref[g, r, C_HDIM:C_HDIM + 1, :]
    o_ref[...] = ot_ref[...].T


def _dsa_t(qc, qi, small, small_col, keys, vals_t, kidx, nseq, t, past, nkeys):
    tq = KEY_BLOCK
    nq = t // tq
    lp = keys.shape[1]
    assert t % tq == 0 and lp % (SUPER * KEY_BLOCK) == 0
    topk = min(TOPK_MAX, nkeys // 4)
    grp = C_HEADS // C_KV_HEADS
    qrow = lambda b, i: (b * nq + i, 0)
    return pl.pallas_call(
        functools.partial(_dsa_t_body, tq=tq, past=past, nkeys=nkeys, topk=topk),
        grid=(nseq, nq),
        in_specs=[
            pl.BlockSpec((tq, C_WIDTH), qrow),
            pl.BlockSpec((tq, IDX_HEADS * IDX_DIM), qrow),
            pl.BlockSpec((tq, LANES), lambda b, i: (b * nq + i, small_col)),
            pl.BlockSpec((1, lp, LANES), lambda b, i: (b, 0, 0)),
            pl.BlockSpec((1, lp // KEY_BLOCK, C_KV_HEADS * VT_ROWS, KEY_BLOCK), lambda b, i: (b, 0, 0, 0)),
            pl.BlockSpec((1, lp, IDX_DIM), lambda b, i: (b, 0, 0)),
        ],
        out_specs=pl.BlockSpec((tq, C_WIDTH), lambda b, i: (b * nq + i, 0)),
        out_shape=jax.ShapeDtypeStruct((nseq * t, C_WIDTH), F32),
        scratch_shapes=[
            pltpu.VMEM((lp // KEY_BLOCK, KEY_BLOCK, tq), jnp.int32),
            pltpu.VMEM((IDX_DIM, IDX_HEADS * tq), BF16),
            pltpu.VMEM((C_KV_HEADS, C_KV_HEADS * C_HDIM, grp * tq), BF16),
            pltpu.VMEM((LANES, tq), F32),
            pltpu.VMEM((C_KV_HEADS, grp, VT_ROWS, tq), F32),
            pltpu.VMEM((C_WIDTH, tq), F32),
            pltpu.VMEM((C_KV_HEADS, grp, ATTN_BLOCKS * KEY_BLOCK, tq), F32),
            pltpu.VMEM((C_KV_HEADS, grp, ATTN_BLOCKS * KEY_BLOCK, tq), F32),
            pltpu.VMEM((2, C_KV_HEADS, grp, ATTN_BLOCKS * KEY_BLOCK, tq), BF16),
        ],
        compiler_params=_params(("parallel", "arbitrary"), 48),
        name="dsa_mixer_t",
    )(qc, qi, small, keys, vals_t, kidx)


def _merge_body(h_ref, yap_ref, yas_ref, ybp_ref, ybs_ref, ocp_ref, ocs_ref, ga_ref, gb_ref, gc_ref,
                wa_ref, wb_ref, wc_ref, wo_ref, o_ref, *, prompt_tiles):
    is_prompt = pl.program_id(0) < prompt_tiles
    pick = lambda p_ref, s_ref: jnp.where(is_prompt, p_ref[...], s_ref[...]).astype(BF16)
    merged = (jax.nn.sigmoid(ga_ref[...]) * _dot(pick(yap_ref, yas_ref), wa_ref[...])
              + jax.nn.sigmoid(gb_ref[...]) * _dot(pick(ybp_ref, ybs_ref), wb_ref[...])
              + jax.nn.sigmoid(gc_ref[...]) * _dot(pick(ocp_ref, ocs_ref), wc_ref[...]))
    o_ref[...] = h_ref[...] + _dot(merged.astype(BF16), wo_ref[...])


def _merge(h, ya, yb, oc, z, wa, wb, wc, wo):
    n, d = h.shape
    tm = TOKEN_TILE
    pt = ya[0].shape[0] // tm
    assert ya[0].shape[0] % tm == 0 and ya[1].shape[0] % tm == 0 and n == ya[0].shape[0] + ya[1].shape[0]
    row = lambda i: (i, 0)
    const = lambda i: (0, 0)
    prow = lambda i: (jnp.minimum(i, pt - 1), 0)
    srow = lambda i: (jnp.maximum(i - pt, 0), 0)
    pair = [pl.BlockSpec((tm, MIX_W), prow), pl.BlockSpec((tm, MIX_W), srow)]
    return pl.pallas_call(
        functools.partial(_merge_body, prompt_tiles=pt),
        grid=(n // tm,),
        in_specs=[
            pl.BlockSpec((tm, d), row),
            *pair, *pair, *pair,
            pl.BlockSpec((tm, d), lambda i: (i, 0)),
            pl.BlockSpec((tm, d), lambda i: (i, 1)),
            pl.BlockSpec((tm, d), lambda i: (i, 2)),
            pl.BlockSpec((MIX_W, d), const),
            pl.BlockSpec((MIX_W, d), const),
            pl.BlockSpec((MIX_W, d), const),
            pl.BlockSpec((d, d), const),
        ],
        out_specs=pl.BlockSpec((tm, d), row),
        out_shape=jax.ShapeDtypeStruct((n, d), F32),
        compiler_params=_params(("parallel",), 48),
        name="gated_merge",
    )(h, *ya, *yb, *oc, z, z, z, wa, wb, wc, wo)


def _pad_heads(w, heads, dim):
    lead = w.shape[:-1]
    w = w.reshape(*lead, heads, dim)
    w = jnp.pad(w, [(0, 0)] * len(lead) + [(0, 0), (0, HEAD_W - dim)])
    return w.reshape(*lead, heads * HEAD_W)


def _layout_w_in(w):
    widths = (512, 512, 512, 512, 256, 256, 512, 16, 512, 512, 128, 128, 512, 64, 8, 1024, 1024, 1024)
    parts, o = [], 0
    for wd in widths:
        parts.append(w[:, o:o + wd])
        o += wd
    (a_q, a_f, a_i, a_g, b_q, b_k, b_v, b_r, b_g, c_q, c_k, c_v, i_q, i_k, i_w, g_a, g_b, g_c) = parts
    d = w.shape[0]
    small = jnp.concatenate([b_r, i_w, jnp.zeros((d, LANES - 24), w.dtype)], axis=1)
    i_k = jnp.pad(i_k, ((0, 0), (0, LANES - IDX_DIM)))
    cols = [g_a, g_b, g_c, a_q, a_f, a_i, a_g,
            _pad_heads(b_q, B_HEADS, B_KDIM), _pad_heads(b_k, B_HEADS, B_KDIM), b_v, b_g,
            small, c_q, i_q, c_k, c_v, i_k]
    out = jnp.concatenate(cols, axis=1)
    assert out.shape[1] == Z_WIDTH
    return out.astype(BF16)


def _rope_tables(pos):
    half = ROPE_DIMS // 2
    inv = ROPE_THETA ** (-jnp.arange(half, dtype=F32) * (2.0 / ROPE_DIMS))
    ang = pos.astype(F32)[:, None] * inv[None, :]
    cos, sin = jnp.cos(ang), jnp.sin(ang)
    n = pos.shape[0]
    ones = jnp.ones((n, C_HDIM - ROPE_DIMS), F32)
    zeros = jnp.zeros((n, C_HDIM - ROPE_DIMS), F32)
    zh = jnp.zeros((n, half), F32)
    c = jnp.concatenate([cos, cos, ones], axis=1)
    s_lo = jnp.concatenate([-sin, zh, zeros], axis=1)
    s_hi = jnp.concatenate([zh, sin, zeros], axis=1)
    two = lambda a: jnp.concatenate([a, a], axis=1)
    return two(c), two(s_lo), two(s_hi)


def kernel(x_prompt, x_sample, state_hgrn, state_gla, cache_k, cache_v, cache_kidx, hgrn_lb, w_in, w_gla_up, b_gla, norm_hgrn, norm_gla, w_br_a, w_br_b, w_br_c, w_out, norm_ffn1, norm_mix, norm_ffn2, ffn1_w1, ffn1_w3, ffn1_w2, ffn2_w1, ffn2_w3, ffn2_w2, norm_final):
    bp, tp, d = x_prompt.shape
    bs, ts, _ = x_sample.shape
    past = cache_k.shape[2]
    n_p, n_s = bp * tp, bs * ts

    lb_sm = jax.nn.softmax(hgrn_lb.astype(F32), axis=0)
    lb_all = jnp.concatenate([jnp.zeros_like(lb_sm[:1]), jnp.cumsum(lb_sm[1:], axis=0)], axis=0)

    pos = jnp.concatenate([jnp.tile(jnp.arange(tp, dtype=jnp.int32), bp),
                           jnp.tile(past + jnp.arange(ts, dtype=jnp.int32), bs)])
    cos_t, sin_lo, sin_hi = _rope_tables(pos)

    h = jnp.concatenate([x_prompt.reshape(n_p, d), x_sample.reshape(n_s, d)], axis=0)
    row2 = lambda a: a.reshape(1, -1)
    zero_state = jnp.zeros((bp, 4, HEAD_W, HEAD_W), F32)
    span_s = SUPER * KEY_BLOCK
    key_pad = -(-(past + ts) // span_s) * span_s - past - ts
    q_rep = KEY_BLOCK // ts

    outs = {k: [] for k in ("pa", "pb", "pk", "pv", "pki", "sa", "sb", "sk", "sv", "ski")}
    for l in range(DEPTH):
        bf = lambda a: a[l].astype(BF16)
        h = _ffn(h, row2(norm_ffn1[l]), bf(ffn1_w1), bf(ffn1_w3), bf(ffn1_w2), row2(norm_final), False)
        z, qc, qi, kc, vc, ki, kcb, vcb, kib, vt = _inproj(
            h, row2(norm_mix[l]), _layout_w_in(w_in[l]), cos_t, sin_lo, sin_hi)

        lb = row2(lb_all[l])
        nwa, nwb = row2(norm_hgrn[l]), row2(norm_gla[l])
        wup = jnp.pad(_pad_heads(w_gla_up[l], B_HEADS, B_KDIM), ((0, LANES - B_GATE_RANK), (0, 0)))
        bup = row2(_pad_heads(b_gla[l], B_HEADS, B_KDIM))
        sb0 = jnp.pad(state_gla[l], ((0, 0), (0, 0), (0, HEAD_W - B_KDIM), (0, 0)))

        ya_p, sa_p = _recurrent_mixer("hgrn", z, 0, bp, tp, zero_state, nwa, (lb,))
        ya_s, sa_s = _recurrent_mixer("hgrn", z, n_p, bs, ts, state_hgrn[l], nwa, (lb,))
        yb_p, sb_p = _recurrent_mixer("gla", z, 0, bp, tp, zero_state, nwb, (wup, bup))
        yb_s, sb_s = _recurrent_mixer("gla", z, n_p, bs, ts, sb0, nwb, (wup, bup))

        seqs = lambda a, n, t: a.reshape(n, t, a.shape[-1])
        vt_p = vt[:n_p // KEY_BLOCK].reshape(bp, tp // KEY_BLOCK, C_KV_HEADS * VT_ROWS, KEY_BLOCK)
        oc_p = _dsa_t(qc, qi, z, Z_SMALL, seqs(kcb[:n_p], bp, tp), vt_p, seqs(kib[:n_p], bp, tp), bp, tp, 0, tp)

        def with_cache(cache, new):
            full = jnp.concatenate([cache.reshape(bs, past, -1).astype(BF16), seqs(new[n_p:], bs, ts)], axis=1)
            return jnp.pad(full, ((0, 0), (0, key_pad), (0, 0)))

        def fill_block(a):
            return jnp.tile(seqs(a[n_p:], bs, ts), (1, q_rep, 1)).reshape(bs * KEY_BLOCK, a.shape[-1])

        vals_s = with_cache(cache_v[l], vcb)
        vt_s = jnp.swapaxes(vals_s.reshape(bs, -1, KEY_BLOCK, LANES), 2, 3)
        ones = jnp.ones(vt_s.shape[:2] + (VT_ROWS - C_HDIM, KEY_BLOCK), BF16)
        vt_s = jnp.concatenate([vt_s[:, :, :C_HDIM], ones, vt_s[:, :, C_HDIM:], ones], axis=2)
        small_s = fill_block(z[:, Z_SMALL * LANES:(Z_SMALL + 1) * LANES])
        oc_s = _dsa_t(fill_block(qc), fill_block(qi), small_s, 0, with_cache(cache_k[l], kcb), vt_s,
                      with_cache(cache_kidx[l], kib), bs, KEY_BLOCK, past, past + ts)
        oc_s = oc_s.reshape(bs, KEY_BLOCK, C_WIDTH)[:, :ts].reshape(n_s, C_WIDTH)

        h = _merge(h, (ya_p, ya_s), (yb_p, yb_s), (oc_p, oc_s), z,
                   bf(w_br_a), bf(w_br_b), bf(w_br_c), bf(w_out))
        h = _ffn(h, row2(norm_ffn2[l]), bf(ffn2_w1), bf(ffn2_w3), bf(ffn2_w2), row2(norm_final),
                 l == DEPTH - 1)

        outs["pa"].append(sa_p)
        outs["sa"].append(sa_s)
        outs["pb"].append(sb_p[:, :, :B_KDIM, :])
        outs["sb"].append(sb_s[:, :, :B_KDIM, :])
        outs["pk"].append(kc[:n_p].reshape(bp, tp, C_KV_HEADS, C_HDIM))
        outs["pv"].append(vc[:n_p].reshape(bp, tp, C_KV_HEADS, C_HDIM))
        outs["pki"].append(ki[:n_p].reshape(bp, tp, IDX_DIM))
        outs["sk"].append(kc[n_p:].reshape(bs, ts, C_KV_HEADS, C_HDIM))
        outs["sv"].append(vc[n_p:].reshape(bs, ts, C_KV_HEADS, C_HDIM))
        outs["ski"].append(ki[n_p:].reshape(bs, ts, IDX_DIM))

    st = {k: jnp.stack(v) for k, v in outs.items()}
    return (h[:n_p].reshape(bp, tp, d), h[n_p:].reshape(bs, ts, d),
            st["pa"], st["pb"], st["pk"], st["pv"], st["pki"],
            st["sa"], st["sb"], st["sk"], st["sv"], st["ski"])
```

```python
import functools

import jax
import jax.numpy as jnp
from jax import lax
from jax.experimental import pallas as pl
from jax.experimental.pallas import tpu as pltpu

F32 = jnp.float32
BF16 = jnp.bfloat16

D_MODEL = 1024
DEPTH = 2
CHUNK = 64
EPS = 1e-6
NEG = -1e30
F_MIN = 1e-30
A_HEADS = 4
A_KDIM = 128
A_VDIM = 128
B_HEADS = 4
B_KDIM = 64
B_VDIM = 128
B_GATE_RANK = 16
B_TAU = 16.0
C_HEADS = 8
C_KV_HEADS = 2
C_HDIM = 64
C_WIDTH = C_HEADS * C_HDIM
IDX_HEADS = 8
IDX_DIM = 64
IDX_SCALE = (IDX_HEADS * IDX_DIM) ** -0.5
TOPK_MAX = 256
ROPE_THETA = 500000.0
ROPE_DIMS = C_HDIM // 4
PAIR_BLOCK = 4
D_FF = 2816

LANES = 128
HEAD_W = 128
MIX_W = 4 * HEAD_W
TOKEN_TILE = 512
KEY_BLOCK = 128
VT_ROWS = 80
INT_MIN = -(2 ** 31)

GATE_UNITS = 24
Z_A = 0
Z_B = 16
Z_SMALL = 32
Z_UNITS = 33
Z_CQ = GATE_UNITS + Z_UNITS
Z_IQ = Z_CQ + 4
Z_CK = Z_IQ + 4
Z_CV = Z_CK + 1
Z_IK = Z_CV + 1
Z_WIDTH = (Z_IK + 1) * LANES
SMALL_IW = B_GATE_RANK


def _params(sem, vmem_mb):
    return pltpu.CompilerParams(dimension_semantics=sem, vmem_limit_bytes=vmem_mb << 20)


def _dot(a, b):
    return jnp.dot(a, b, preferred_element_type=F32)


def _dot_nt(a, b):
    return lax.dot_general(a, b, (((1,), (1,)), ((), ())), preferred_element_type=F32)


def _dot_tn(a, b):
    return lax.dot_general(a, b, (((0,), (0,)), ((), ())), preferred_element_type=F32)


def _split3(x):
    h1 = x.astype(BF16)
    r1 = x - h1.astype(F32)
    h2 = r1.astype(BF16)
    h3 = (r1 - h2.astype(F32)).astype(BF16)
    return h1, h2, h3


def _dot_exact_lhs(m, x):
    h1, h2, h3 = _split3(x)
    return _dot(m, h1) + _dot(m, h2) + _dot(m, h3)


def _dot_hi(a, b):
    a1 = a.astype(BF16)
    a2 = (a - a1.astype(F32)).astype(BF16)
    b1 = b.astype(BF16)
    b2 = (b - b1.astype(F32)).astype(BF16)
    return _dot(a1, b1) + _dot(a1, b2) + _dot(a2, b1)


def _rmsnorm(x, g):
    return x * lax.rsqrt(jnp.mean(x * x, axis=-1, keepdims=True) + EPS) * g


def _ffn_body(x_ref, g_ref, w1_ref, w3_ref, w2_ref, gf_ref, o_ref, u_ref, acc_ref, *, final_norm):
    j = pl.program_id(1)

    @pl.when(j == 0)
    def _():
        u_ref[...] = _rmsnorm(x_ref[...], g_ref[...]).astype(BF16)
        acc_ref[...] = jnp.zeros_like(acc_ref)

    u = u_ref[...]
    a = _dot(u, w1_ref[...])
    b = _dot(u, w3_ref[...])
    hid = (a * jax.nn.sigmoid(a) * b).astype(BF16)
    acc_ref[...] += _dot(hid, w2_ref[...])

    @pl.when(j == pl.num_programs(1) - 1)
    def _():
        out = x_ref[...] + 0.5 * acc_ref[...]
        if final_norm:
            out = _rmsnorm(out, gf_ref[...])
        o_ref[...] = out


def _ffn(x, g, w1, w3, w2, gf, final_norm):
    n, d = x.shape
    dff = w1.shape[1]
    tm = min(TOKEN_TILE, n)
    tf = dff
    resident = pl.Buffered(1)
    return pl.pallas_call(
        functools.partial(_ffn_body, final_norm=final_norm),
        grid=(n // tm, dff // tf),
        in_specs=[
            pl.BlockSpec((tm, d), lambda i, j: (i, 0)),
            pl.BlockSpec((1, d), lambda i, j: (0, 0)),
            pl.BlockSpec((d, tf), lambda i, j: (0, j), pipeline_mode=resident),
            pl.BlockSpec((d, tf), lambda i, j: (0, j), pipeline_mode=resident),
            pl.BlockSpec((tf, d), lambda i, j: (j, 0), pipeline_mode=resident),
            pl.BlockSpec((1, d), lambda i, j: (0, 0)),
        ],
        out_specs=pl.BlockSpec((tm, d), lambda i, j: (i, 0)),
        out_shape=jax.ShapeDtypeStruct((n, d), F32),
        scratch_shapes=[pltpu.VMEM((tm, d), BF16), pltpu.VMEM((tm, d), F32)],
        compiler_params=_params(("parallel", "arbitrary"), 48),
        name="ffn_half_step",
    )(x, g, w1, w3, w2, gf)


def _rope(x, c, s_lo, s_hi):
    w = x.shape[1]
    rep = w // LANES
    if rep > 1:
        c = jnp.concatenate([c] * rep, axis=1)
        s_lo = jnp.concatenate([s_lo] * rep, axis=1)
        s_hi = jnp.concatenate([s_hi] * rep, axis=1)
    half = ROPE_DIMS // 2
    return x * c + pltpu.roll(x, half, 1) * s_hi + pltpu.roll(x, w - half, 1) * s_lo


def _inproj_body(x_ref, g_ref, w_ref, c_ref, slo_ref, shi_ref,
                 gates_ref, z_ref, qc_ref, qi_ref, kc_ref, vc_ref, ki_ref, kcb_ref, vcb_ref, kib_ref, vt_ref):
    u = _rmsnorm(x_ref[...], g_ref[...]).astype(BF16)
    zf = _dot(u, w_ref[...])
    gates_ref[...] = jax.nn.sigmoid(zf[:, :GATE_UNITS * LANES]).astype(BF16)
    z_ref[...] = zf[:, GATE_UNITS * LANES:(GATE_UNITS + Z_UNITS) * LANES]

    unit = lambda first, count: zf[:, first * LANES:(first + count) * LANES]
    c, s_lo, s_hi = c_ref[...], slo_ref[...], shi_ref[...]
    qc_ref[...] = (_rope(unit(Z_CQ, 4), c, s_lo, s_hi) * (C_HDIM ** -0.5)).astype(BF16)
    qi_ref[...] = _rope(unit(Z_IQ, 4), c, s_lo, s_hi).astype(BF16)
    kc = _rope(unit(Z_CK, 1), c, s_lo, s_hi)
    vc = unit(Z_CV, 1)
    ki = _rope(unit(Z_IK, 1), c, s_lo, s_hi)[:, :IDX_DIM]
    kc_ref[...] = kc
    vc_ref[...] = vc
    ki_ref[...] = ki
    kcb_ref[...] = kc.astype(BF16)
    vcb_ref[...] = vc.astype(BF16)
    kib_ref[...] = ki.astype(BF16)
    ones = jnp.ones((VT_ROWS - C_HDIM, KEY_BLOCK), BF16)
    for kk in range(vt_ref.shape[0]):
        vt = vc[kk * KEY_BLOCK:(kk + 1) * KEY_BLOCK, :].T.astype(BF16)
        vt_ref[kk] = jnp.concatenate([vt[:C_HDIM], ones, vt[C_HDIM:], ones], axis=0)


def _inproj(h, g, w, cos_t, sin_lo, sin_hi):
    n, d = h.shape
    tm = min(TOKEN_TILE // 2, n)
    row = lambda i: (i, 0)
    return pl.pallas_call(
        _inproj_body,
        grid=(n // tm,),
        in_specs=[
            pl.BlockSpec((tm, d), row),
            pl.BlockSpec((1, d), lambda i: (0, 0)),
            pl.BlockSpec((d, Z_WIDTH), lambda i: (0, 0), pipeline_mode=pl.Buffered(1)),
            pl.BlockSpec((tm, LANES), row),
            pl.BlockSpec((tm, LANES), row),
            pl.BlockSpec((tm, LANES), row),
        ],
        out_specs=[
            pl.BlockSpec((tm, GATE_UNITS * LANES), row),
            pl.BlockSpec((tm, Z_UNITS * LANES), row),
            pl.BlockSpec((tm, C_WIDTH), row),
            pl.BlockSpec((tm, IDX_HEADS * IDX_DIM), row),
            pl.BlockSpec((tm, LANES), row),
            pl.BlockSpec((tm, LANES), row),
            pl.BlockSpec((tm, IDX_DIM), row),
            pl.BlockSpec((tm, LANES), row),
            pl.BlockSpec((tm, LANES), row),
            pl.BlockSpec((tm, IDX_DIM), row),
            pl.BlockSpec((tm // KEY_BLOCK, C_KV_HEADS * VT_ROWS, KEY_BLOCK), lambda i: (i, 0, 0)),
        ],
        out_shape=[
            jax.ShapeDtypeStruct((n, GATE_UNITS * LANES), BF16),
            jax.ShapeDtypeStruct((n, Z_UNITS * LANES), F32),
            jax.ShapeDtypeStruct((n, C_WIDTH), BF16),
            jax.ShapeDtypeStruct((n, IDX_HEADS * IDX_DIM), BF16),
            jax.ShapeDtypeStruct((n, LANES), F32),
            jax.ShapeDtypeStruct((n, LANES), F32),
            jax.ShapeDtypeStruct((n, IDX_DIM), F32),
            jax.ShapeDtypeStruct((n, LANES), BF16),
            jax.ShapeDtypeStruct((n, LANES), BF16),
            jax.ShapeDtypeStruct((n, IDX_DIM), BF16),
            jax.ShapeDtypeStruct((n // KEY_BLOCK, C_KV_HEADS * VT_ROWS, KEY_BLOCK), BF16),
        ],
        compiler_params=_params(("parallel",), 48),
        name="in_projection",
    )(h, g, w, cos_t, sin_lo, sin_hi)


def _gla_head(q, k, v, logf, st_ref, h, tc):
    row = lax.broadcasted_iota(jnp.int32, (tc, tc), 0)
    col = lax.broadcasted_iota(jnp.int32, (tc, tc), 1)
    tril = (col <= row).astype(BF16)
    cum = _dot_exact_lhs(tril, logf)

    tile = 2 * PAIR_BLOCK
    nt = tc // tile
    b3 = cum.reshape(nt, tile, HEAD_W)
    q3 = q.reshape(nt, tile, HEAD_W)
    k3 = k.reshape(nt, tile, HEAD_W)
    v3 = v.reshape(nt, tile, HEAD_W)
    srow = lax.broadcasted_iota(jnp.int32, (nt, tile, 1), 1)
    low = srow < PAIR_BLOCK
    tloc = srow & (PAIR_BLOCK - 1)

    def block_row(x3, j):
        return jnp.where(low, x3[:, j:j + 1, :], x3[:, PAIR_BLOCK + j:PAIR_BLOCK + j + 1, :])

    o3 = jnp.zeros((nt, tile, HEAD_W), F32)
    for j in range(PAIR_BLOCK):
        causal = tloc >= j
        decay = jnp.exp(jnp.where(causal, b3 - block_row(b3, j), 0.0))
        w = jnp.sum(q3 * block_row(k3, j) * decay, axis=-1, keepdims=True)
        o3 = o3 + jnp.where(causal, w, 0.0) * block_row(v3, j)
    o = o3.reshape(tc, HEAD_W)

    vb = v.astype(BF16)
    attn = jnp.zeros((tc, tc), F32)
    half = tc // 2
    while half >= PAIR_BLOCK:
        blk = 2 * half
        nblk = tc // blk
        bl = cum.reshape(nblk, blk, HEAD_W)
        x = bl - bl[:, half - 1:half, :]
        second = lax.broadcasted_iota(jnp.int32, (nblk, blk, 1), 1) >= half
        e = jnp.exp(jnp.where(second, x, -x))
        qt = jnp.where(second, q.reshape(nblk, blk, HEAD_W) * e, 0.0).reshape(tc, HEAD_W).astype(BF16)
        kt = jnp.where(second, 0.0, k.reshape(nblk, blk, HEAD_W) * e).reshape(tc, HEAD_W).astype(BF16)
        shift = blk.bit_length() - 1
        same = (row >> shift) == (col >> shift)
        attn = attn + jnp.where(same, _dot_nt(qt, kt), 0.0)
        half //= 2
    o = o + _dot(attn.astype(BF16), vb)

    st = st_ref[h]
    o = o + _dot_nt((q * jnp.exp(cum)).astype(BF16), st.astype(BF16))
    last = cum[tc - 1:tc, :]
    kd = (k * jnp.exp(last - cum)).astype(BF16)
    st_ref[h] = st * jnp.exp(last) + _dot_tn(vb, kd)
    return o


def _gla_finish(o, nw, gate):
    return _rmsnorm(o, nw) * (gate * jax.nn.sigmoid(gate))


def _gla_state_io(c, s0_ref, st_ref, heads):
    @pl.when(c == 0)
    def _():
        for h in range(heads):
            st_ref[h] = s0_ref[0, h].T


def _gla_state_out(c, sout_ref, st_ref, heads):
    @pl.when(c == pl.num_programs(1) - 1)
    def _():
        for h in range(heads):
            sout_ref[0, h] = st_ref[h].T


def _hgrn_body(q_ref, f_ref, v_ref, g_ref, lb_ref, nw_ref, s0_ref, y_ref, sout_ref, st_ref, *, tc):
    c = pl.program_id(1)
    _gla_state_io(c, s0_ref, st_ref, A_HEADS)
    for h in range(A_HEADS):
        hs = slice(h * HEAD_W, (h + 1) * HEAD_W)
        zf = f_ref[:, hs]
        lb = lb_ref[:, hs]
        f = lb + (1.0 - lb) * jax.nn.sigmoid(zf)
        logf = jnp.log(jnp.maximum(f, F_MIN))
        k = (1.0 - lb) * jax.nn.sigmoid(-zf)
        zq = q_ref[:, hs]
        q = zq * jax.nn.sigmoid(zq) * (A_KDIM ** -0.5)
        o = _gla_head(q, k, v_ref[:, hs], logf, st_ref, h, tc)
        y_ref[:, hs] = _gla_finish(o, nw_ref[...], g_ref[:, hs]).astype(BF16)
    _gla_state_out(c, sout_ref, st_ref, A_HEADS)


def _gla_body(q_ref, k_ref, v_ref, g_ref, r_ref, wup_ref, bup_ref, nw_ref, s0_ref,
              y_ref, sout_ref, st_ref, *, tc):
    c = pl.program_id(1)
    _gla_state_io(c, s0_ref, st_ref, B_HEADS)
    r = _dot_hi(r_ref[...], wup_ref[...]) + bup_ref[...]
    logf_all = (jnp.minimum(r, 0.0) - jnp.log1p(jnp.exp(-jnp.abs(r)))) / B_TAU
    for h in range(B_HEADS):
        hs = slice(h * HEAD_W, (h + 1) * HEAD_W)
        q = q_ref[:, hs] * (B_KDIM ** -0.5)
        o = _gla_head(q, k_ref[:, hs], v_ref[:, hs], logf_all[:, hs], st_ref, h, tc)
        y_ref[:, hs] = _gla_finish(o, nw_ref[...], g_ref[:, hs]).astype(BF16)
    _gla_state_out(c, sout_ref, st_ref, B_HEADS)


def _recurrent_mixer(mode, z, row0, nseq, t, s0, nw, extra):
    tc = min(128, t)
    nc = t // tc
    rb0 = row0 // tc
    zcol = (Z_A if mode == "hgrn" else Z_B) // 4

    def zspec(k):
        return pl.BlockSpec((tc, MIX_W), lambda b, c: (rb0 + b * nc + c, zcol + k))

    const = lambda b, c: (0, 0)
    state_spec = pl.BlockSpec((1, 4, HEAD_W, HEAD_W), lambda b, c: (b, 0, 0, 0))
    if mode == "hgrn":
        body = functools.partial(_hgrn_body, tc=tc)
        in_specs = [zspec(0), zspec(1), zspec(2), zspec(3),
                    pl.BlockSpec((1, MIX_W), const), pl.BlockSpec((1, HEAD_W), const), state_spec]
        args = (z, z, z, z, extra[0], nw, s0)
    else:
        body = functools.partial(_gla_body, tc=tc)
        in_specs = [zspec(0), zspec(1), zspec(2), zspec(3),
                    pl.BlockSpec((tc, LANES), lambda b, c: (rb0 + b * nc + c, Z_SMALL)),
                    pl.BlockSpec((LANES, MIX_W), const), pl.BlockSpec((1, MIX_W), const),
                    pl.BlockSpec((1, HEAD_W), const), state_spec]
        args = (z, z, z, z, z, extra[0], extra[1], nw, s0)
    return pl.pallas_call(
        body,
        grid=(nseq, nc),
        in_specs=in_specs,
        out_specs=[pl.BlockSpec((tc, MIX_W), lambda b, c: (b * nc + c, 0)), state_spec],
        out_shape=[jax.ShapeDtypeStruct((nseq * t, MIX_W), BF16),
                   jax.ShapeDtypeStruct((nseq, 4, HEAD_W, HEAD_W), F32)],
        scratch_shapes=[pltpu.VMEM((4, HEAD_W, HEAD_W), F32)],
        compiler_params=_params(("parallel", "arbitrary"), 32),
        name=mode + "_mixer",
    )(*args)


SUPER = 4
ATTN_BLOCKS = 2


def _dsa_t_body(qc_ref, qi_ref, sm_ref, kc_ref, vt_ref, ki_ref, o_ref,
                sk_ref, qit_ref, qct_ref, wrow_ref, acc_ref, ot_ref, s0_ref, s1_ref, p_ref,
                *, tq, past, nkeys, topk):
    i = pl.program_id(1)
    qpos0 = past + i * tq
    last_chunk = (qpos0 + tq - 1) // CHUNK
    n_adm = jnp.minimum((last_chunk + 1) * CHUNK, nkeys)
    nkb = (n_adm + KEY_BLOCK - 1) // KEY_BLOCK
    nsb = (nkb + SUPER - 1) // SUPER
    grp = C_HEADS // C_KV_HEADS

    qit = qi_ref[...].astype(F32).T
    for j in range(IDX_HEADS):
        qit_ref[:, j * tq:(j + 1) * tq] = qit[j * IDX_DIM:(j + 1) * IDX_DIM, :].astype(BF16)
    qct = qc_ref[...].astype(F32).T
    zeros = jnp.zeros((C_HDIM, tq), BF16)
    for hq in range(C_HEADS):
        g, r = divmod(hq, grp)
        for gg in range(C_KV_HEADS):
            val = qct[hq * C_HDIM:(hq + 1) * C_HDIM, :].astype(BF16) if gg == g else zeros
            qct_ref[g, gg * C_HDIM:(gg + 1) * C_HDIM, r * tq:(r + 1) * tq] = val
    wrow_ref[...] = sm_ref[...].T

    qchunk = (qpos0 + lax.broadcasted_iota(jnp.int32, (1, tq), 1)) >> 6
    ksub = lax.broadcasted_iota(jnp.int32, (KEY_BLOCK, 1), 0)

    def admissible(kb):
        kpos = kb * KEY_BLOCK + ksub
        return ((kpos >> 6) <= qchunk) & (kpos < nkeys)

    def score_step(kk, carry):
        for u in range(SUPER):
            kb = kk * SUPER + u
            kib = ki_ref[0, pl.ds(pl.multiple_of(kb * KEY_BLOCK, KEY_BLOCK), KEY_BLOCK), :]
            d = _dot(kib, qit_ref[...])
            sc = jnp.maximum(d[:, 0:tq], 0.0) * wrow_ref[SMALL_IW:SMALL_IW + 1, :]
            for j in range(1, IDX_HEADS):
                sc = sc + jnp.maximum(d[:, j * tq:(j + 1) * tq], 0.0) * wrow_ref[SMALL_IW + j:SMALL_IW + j + 1, :]
            bits = pltpu.bitcast(sc * IDX_SCALE + 0.0, jnp.int32)
            key = jnp.where(admissible(kb), jnp.where(bits < 0, bits ^ 0x7FFFFFFF, bits), INT_MIN)
            sk_ref[kb] = key
        return carry

    lax.fori_loop(0, nsb, score_step, 0)

    def fold(hit):
        return jnp.sum(hit.reshape(KEY_BLOCK // 8, 8, tq), axis=0)

    def count(pred):
        def step(kk, acc):
            for u in range(SUPER):
                kb = kk * SUPER + u
                acc = acc + fold(jnp.where(pred(sk_ref[kb], kb), 1.0, 0.0))
            return acc
        acc = lax.fori_loop(0, nsb, step, jnp.zeros((8, tq), F32))
        return jnp.sum(acc, axis=0, keepdims=True)

    kf = float(topk)
    c0 = count(lambda s, kb: s >= 0)
    t0 = jnp.where(c0 >= kf, 0, INT_MIN).astype(jnp.int32)

    def bit_step(it, t):
        cand = t | jnp.left_shift(jnp.int32(1), 30 - it)
        return jnp.where(count(lambda s, kb: s >= cand) >= kf, cand, t)
    thr = lax.fori_loop(0, 31, bit_step, t0)
    thr = jnp.maximum(thr, INT_MIN + 1)

    c_ge = count(lambda s, kb: s >= thr)
    nbits = max(1, (nkeys - 1).bit_length())

    @pl.when(jnp.max(c_ge) > kf)
    def _():
        need = kf - count(lambda s, kb: s > thr)

        def cut_step(it, cut):
            cand = cut | jnp.left_shift(jnp.int32(1), nbits - 1 - it)
            c = count(lambda s, kb: (s == thr) & (kb * KEY_BLOCK + ksub < cand))
            return jnp.where(c < need, cand, cut)
        cut = lax.fori_loop(0, nbits, cut_step, jnp.zeros((1, tq), jnp.int32))

        def strike(kb, carry):
            key = sk_ref[kb]
            sk_ref[kb] = jnp.where((key == thr) & (kb * KEY_BLOCK + ksub > cut), INT_MIN, key)
            return carry
        lax.fori_loop(0, nsb * SUPER, strike, 0)

    acc_ref[...] = jnp.zeros(acc_ref.shape, F32)
    init = tuple(jnp.full((1, grp * tq), NEG, F32) for _ in range(C_KV_HEADS))

    span = ATTN_BLOCKS * KEY_BLOCK
    n_attn = nsb * (SUPER // ATTN_BLOCKS)

    s_bufs = (s0_ref, s1_ref)

    def qk_scores(kk, half):
        kblk = kc_ref[0, pl.ds(pl.multiple_of(kk * span, span), span), :]
        bias = jnp.concatenate(
            [jnp.where(sk_ref[kk * ATTN_BLOCKS + u] >= thr, 0.0, NEG) for u in range(ATTN_BLOCKS)], axis=0)
        bias = jnp.concatenate([bias] * grp, axis=1)
        for g in range(C_KV_HEADS):
            s = _dot(kblk, qct_ref[g]) + bias
            for r in range(grp):
                s_bufs[half][g, r] = s[:, r * tq:(r + 1) * tq]

    def softmax_pv(kk, half, m):
        m = list(m)
        for g in range(C_KV_HEADS):
            m_cols, alpha_cols = [], []
            for r in range(grp):
                cols = slice(r * tq, (r + 1) * tq)
                sb = s_bufs[half][g, r]
                m_old = m[g][:, cols]
                m_new = jnp.maximum(m_old, jnp.max(sb, axis=0, keepdims=True))
                p_ref[half, g, r] = jnp.exp(sb - m_new).astype(BF16)
                m_cols.append(m_new)
                alpha_cols.append(jnp.exp(m_old - m_new))
            rows = slice(g * VT_ROWS, (g + 1) * VT_ROWS)
            pv = None
            for u in range(ATTN_BLOCKS):
                keys_u = slice(u * KEY_BLOCK, (u + 1) * KEY_BLOCK)
                p_u = jnp.concatenate([p_ref[half, g, r, keys_u, :] for r in range(grp)], axis=1)
                d = _dot(vt_ref[0, kk * ATTN_BLOCKS + u, rows, :], p_u)
                pv = d if pv is None else pv + d
            for r in range(grp):
                acc_ref[g, r] = alpha_cols[r] * acc_ref[g, r] + pv[:, r * tq:(r + 1) * tq]
            m[g] = jnp.concatenate(m_cols, axis=1)
        return tuple(m)

    qk_scores(0, 0)

    def pair_step(j, m):
        qk_scores(2 * j + 1, 1)
        m = softmax_pv(2 * j, 0, m)
        qk_scores(jnp.minimum(2 * j + 2, n_attn - 2), 0)
        return softmax_pv(2 * j + 1, 1, m)

    lax.fori_loop(0, n_attn // 2, pair_step, init)
    for hq in range(C_HEADS):
        g, r = divmod(hq, grp)
        ot_ref[hq * C_HDIM:(hq + 1) * C_HDIM, :] = acc_ref[g, r, 0:C_HDIM, :] / acc_ref[g, r, C_HDIM:C_HDIM + 1, :]
    o_ref[...] = ot_ref[...].T.astype(BF16)


def _dsa_t(qc, qi, small, small_col, keys, vals_t, kidx, nseq, t, past, nkeys):
    tq = KEY_BLOCK
    nq = t // tq
    lp = keys.shape[1]
    assert t % tq == 0 and lp % (SUPER * KEY_BLOCK) == 0
    topk = min(TOPK_MAX, nkeys // 4)
    grp = C_HEADS // C_KV_HEADS
    qrow = lambda b, i: (b * nq + i, 0)
    return pl.pallas_call(
        functools.partial(_dsa_t_body, tq=tq, past=past, nkeys=nkeys, topk=topk),
        grid=(nseq, nq),
        in_specs=[
            pl.BlockSpec((tq, C_WIDTH), qrow),
            pl.BlockSpec((tq, IDX_HEADS * IDX_DIM), qrow),
            pl.BlockSpec((tq, LANES), lambda b, i: (b * nq + i, small_col)),
            pl.BlockSpec((1, lp, LANES), lambda b, i: (b, 0, 0)),
            pl.BlockSpec((1, lp // KEY_BLOCK, C_KV_HEADS * VT_ROWS, KEY_BLOCK), lambda b, i: (b, 0, 0, 0)),
            pl.BlockSpec((1, lp, IDX_DIM), lambda b, i: (b, 0, 0)),
        ],
        out_specs=pl.BlockSpec((tq, C_WIDTH), lambda b, i: (b * nq + i, 0)),
        out_shape=jax.ShapeDtypeStruct((nseq * t, C_WIDTH), BF16),
        scratch_shapes=[
            pltpu.VMEM((lp // KEY_BLOCK, KEY_BLOCK, tq), jnp.int32),
            pltpu.VMEM((IDX_DIM, IDX_HEADS * tq), BF16),
            pltpu.VMEM((C_KV_HEADS, C_KV_HEADS * C_HDIM, grp * tq), BF16),
            pltpu.VMEM((LANES, tq), F32),
            pltpu.VMEM((C_KV_HEADS, grp, VT_ROWS, tq), F32),
            pltpu.VMEM((C_WIDTH, tq), F32),
            pltpu.VMEM((C_KV_HEADS, grp, ATTN_BLOCKS * KEY_BLOCK, tq), F32),
            pltpu.VMEM((C_KV_HEADS, grp, ATTN_BLOCKS * KEY_BLOCK, tq), F32),
            pltpu.VMEM((2, C_KV_HEADS, grp, ATTN_BLOCKS * KEY_BLOCK, tq), BF16),
        ],
        compiler_params=_params(("parallel", "arbitrary"), 48),
        name="dsa_mixer_t",
    )(qc, qi, small, keys, vals_t, kidx)


def _merge_body(h_ref, yap_ref, yas_ref, ybp_ref, ybs_ref, ocp_ref, ocs_ref, ga_ref, gb_ref, gc_ref,
                wa_ref, wb_ref, wc_ref, wo_ref, o_ref, *, prompt_tiles):
    is_prompt = pl.program_id(0) < prompt_tiles
    pick = lambda p_ref, s_ref: jnp.where(is_prompt, p_ref[...], s_ref[...])
    merged = (ga_ref[...].astype(F32) * _dot(pick(yap_ref, yas_ref), wa_ref[...])
              + gb_ref[...].astype(F32) * _dot(pick(ybp_ref, ybs_ref), wb_ref[...])
              + gc_ref[...].astype(F32) * _dot(pick(ocp_ref, ocs_ref), wc_ref[...]))
    o_ref[...] = h_ref[...] + _dot(merged.astype(BF16), wo_ref[...])


def _merge(h, ya, yb, oc, gates, wa, wb, wc, wo):
    n, d = h.shape
    tm = TOKEN_TILE
    pt = ya[0].shape[0] // tm
    assert ya[0].shape[0] % tm == 0 and ya[1].shape[0] % tm == 0 and n == ya[0].shape[0] + ya[1].shape[0]
    row = lambda i: (i, 0)
    const = lambda i: (0, 0)
    prow = lambda i: (jnp.minimum(i, pt - 1), 0)
    srow = lambda i: (jnp.maximum(i - pt, 0), 0)
    pair = [pl.BlockSpec((tm, MIX_W), prow), pl.BlockSpec((tm, MIX_W), srow)]
    return pl.pallas_call(
        functools.partial(_merge_body, prompt_tiles=pt),
        grid=(n // tm,),
        in_specs=[
            pl.BlockSpec((tm, d), row),
            *pair, *pair, *pair,
            pl.BlockSpec((tm, d), lambda i: (i, 0)),
            pl.BlockSpec((tm, d), lambda i: (i, 1)),
            pl.BlockSpec((tm, d), lambda i: (i, 2)),
            pl.BlockSpec((MIX_W, d), const),
            pl.BlockSpec((MIX_W, d), const),
            pl.BlockSpec((MIX_W, d), const),
            pl.BlockSpec((d, d), const),
        ],
        out_specs=pl.BlockSpec((tm, d), row),
        out_shape=jax.ShapeDtypeStruct((n, d), F32),
        compiler_params=_params(("parallel",), 48),
        name="gated_merge",
    )(h, *ya, *yb, *oc, gates, gates, gates, wa, wb, wc, wo)


def _pad_heads(w, heads, dim):
    lead = w.shape[:-1]
    w = w.reshape(*lead, heads, dim)
    w = jnp.pad(w, [(0, 0)] * len(lead) + [(0, 0), (0, HEAD_W - dim)])
    return w.reshape(*lead, heads * HEAD_W)


def _layout_w_in(w):
    widths = (512, 512, 512, 512, 256, 256, 512, 16, 512, 512, 128, 128, 512, 64, 8, 1024, 1024, 1024)
    parts, o = [], 0
    for wd in widths:
        parts.append(w[:, o:o + wd])
        o += wd
    (a_q, a_f, a_i, a_g, b_q, b_k, b_v, b_r, b_g, c_q, c_k, c_v, i_q, i_k, i_w, g_a, g_b, g_c) = parts
    d = w.shape[0]
    small = jnp.concatenate([b_r, i_w, jnp.zeros((d, LANES - 24), w.dtype)], axis=1)
    i_k = jnp.pad(i_k, ((0, 0), (0, LANES - IDX_DIM)))
    cols = [g_a, g_b, g_c, a_q, a_f, a_i, a_g,
            _pad_heads(b_q, B_HEADS, B_KDIM), _pad_heads(b_k, B_HEADS, B_KDIM), b_v, b_g,
            small, c_q, i_q, c_k, c_v, i_k]
    out = jnp.concatenate(cols, axis=1)
    assert out.shape[1] == Z_WIDTH
    return out.astype(BF16)


def _rope_tables(pos):
    half = ROPE_DIMS // 2
    inv = ROPE_THETA ** (-jnp.arange(half, dtype=F32) * (2.0 / ROPE_DIMS))
    ang = pos.astype(F32)[:, None] * inv[None, :]
    cos, sin = jnp.cos(ang), jnp.sin(ang)
    n = pos.shape[0]
    ones = jnp.ones((n, C_HDIM - ROPE_DIMS), F32)
    zeros = jnp.zeros((n, C_HDIM - ROPE_DIMS), F32)
    zh = jnp.zeros((n, half), F32)
    c = jnp.concatenate([cos, cos, ones], axis=1)
    s_lo = jnp.concatenate([-sin, zh, zeros], axis=1)
    s_hi = jnp.concatenate([zh, sin, zeros], axis=1)
    two = lambda a: jnp.concatenate([a, a], axis=1)
    return two(c), two(s_lo), two(s_hi)


def kernel(x_prompt, x_sample, state_hgrn, state_gla, cache_k, cache_v, cache_kidx, hgrn_lb, w_in, w_gla_up, b_gla, norm_hgrn, norm_gla, w_br_a, w_br_b, w_br_c, w_out, norm_ffn1, norm_mix, norm_ffn2, ffn1_w1, ffn1_w3, ffn1_w2, ffn2_w1, ffn2_w3, ffn2_w2, norm_final):
    bp, tp, d = x_prompt.shape
    bs, ts, _ = x_sample.shape
    past = cache_k.shape[2]
    n_p, n_s = bp * tp, bs * ts

    lb_sm = jax.nn.softmax(hgrn_lb.astype(F32), axis=0)
    lb_all = jnp.concatenate([jnp.zeros_like(lb_sm[:1]), jnp.cumsum(lb_sm[1:], axis=0)], axis=0)

    pos = jnp.concatenate([jnp.tile(jnp.arange(tp, dtype=jnp.int32), bp),
                           jnp.tile(past + jnp.arange(ts, dtype=jnp.int32), bs)])
    cos_t, sin_lo, sin_hi = _rope_tables(pos)

    h = jnp.concatenate([x_prompt.reshape(n_p, d), x_sample.reshape(n_s, d)], axis=0)
    row2 = lambda a: a.reshape(1, -1)
    zero_state = jnp.zeros((bp, 4, HEAD_W, HEAD_W), F32)
    span_s = SUPER * KEY_BLOCK
    key_pad = -(-(past + ts) // span_s) * span_s - past - ts
    q_rep = KEY_BLOCK // ts

    outs = {k: [] for k in ("pa", "pb", "pk", "pv", "pki", "sa", "sb", "sk", "sv", "ski")}
    for l in range(DEPTH):
        bf = lambda a: a[l].astype(BF16)
        h = _ffn(h, row2(norm_ffn1[l]), bf(ffn1_w1), bf(ffn1_w3), bf(ffn1_w2), row2(norm_final), False)
        gates, z, qc, qi, kc, vc, ki, kcb, vcb, kib, vt = _inproj(
            h, row2(norm_mix[l]), _layout_w_in(w_in[l]), cos_t, sin_lo, sin_hi)

        lb = row2(lb_all[l])
        nwa, nwb = row2(norm_hgrn[l]), row2(norm_gla[l])
        wup = jnp.pad(_pad_heads(w_gla_up[l], B_HEADS, B_KDIM), ((0, LANES - B_GATE_RANK), (0, 0)))
        bup = row2(_pad_heads(b_gla[l], B_HEADS, B_KDIM))
        sb0 = jnp.pad(state_gla[l], ((0, 0), (0, 0), (0, HEAD_W - B_KDIM), (0, 0)))

        ya_p, sa_p = _recurrent_mixer("hgrn", z, 0, bp, tp, zero_state, nwa, (lb,))
        ya_s, sa_s = _recurrent_mixer("hgrn", z, n_p, bs, ts, state_hgrn[l], nwa, (lb,))
        yb_p, sb_p = _recurrent_mixer("gla", z, 0, bp, tp, zero_state, nwb, (wup, bup))
        yb_s, sb_s = _recurrent_mixer("gla", z, n_p, bs, ts, sb0, nwb, (wup, bup))

        seqs = lambda a, n, t: a.reshape(n, t, a.shape[-1])
        vt_p = vt[:n_p // KEY_BLOCK].reshape(bp, tp // KEY_BLOCK, C_KV_HEADS * VT_ROWS, KEY_BLOCK)
        oc_p = _dsa_t(qc, qi, z, Z_SMALL, seqs(kcb[:n_p], bp, tp), vt_p, seqs(kib[:n_p], bp, tp), bp, tp, 0, tp)

        def with_cache(cache, new):
            full = jnp.concatenate([cache.reshape(bs, past, -1).astype(BF16), seqs(new[n_p:], bs, ts)], axis=1)
            return jnp.pad(full, ((0, 0), (0, key_pad), (0, 0)))

        def fill_block(a):
            return jnp.tile(seqs(a[n_p:], bs, ts), (1, q_rep, 1)).reshape(bs * KEY_BLOCK, a.shape[-1])

        vals_s = with_cache(cache_v[l], vcb)
        vt_s = jnp.swapaxes(vals_s.reshape(bs, -1, KEY_BLOCK, LANES), 2, 3)
        ones = jnp.ones(vt_s.shape[:2] + (VT_ROWS - C_HDIM, KEY_BLOCK), BF16)
        vt_s = jnp.concatenate([vt_s[:, :, :C_HDIM], ones, vt_s[:, :, C_HDIM:], ones], axis=2)
        small_s = fill_block(z[:, Z_SMALL * LANES:(Z_SMALL + 1) * LANES])
        oc_s = _dsa_t(fill_block(qc), fill_block(qi), small_s, 0, with_cache(cache_k[l], kcb), vt_s,
                      with_cache(cache_kidx[l], kib), bs, KEY_BLOCK, past, past + ts)
        oc_s = oc_s.reshape(bs, KEY_BLOCK, C_WIDTH)[:, :ts].reshape(n_s, C_WIDTH)

        h = _merge(h, (ya_p, ya_s), (yb_p, yb_s), (oc_p, oc_s), gates,
                   bf(w_br_a), bf(w_br_b), bf(w_br_c), bf(w_out))
        h = _ffn(h, row2(norm_ffn2[l]), bf(ffn2_w1), bf(ffn2_w3), bf(ffn2_w2), row2(norm_final),
                 l == DEPTH - 1)

        outs["pa"].append(sa_p)
        outs["sa"].append(sa_s)
        outs["pb"].append(sb_p[:, :, :B_KDIM, :])
        outs["sb"].append(sb_s[:, :, :B_KDIM, :])
        outs["pk"].append(kc[:n_p].reshape(bp, tp, C_KV_HEADS, C_HDIM))
        outs["pv"].append(vc[:n_p].reshape(bp, tp, C_KV_HEADS, C_HDIM))
        outs["pki"].append(ki[:n_p].reshape(bp, tp, IDX_DIM))
        outs["sk"].append(kc[n_p:].reshape(bs, ts, C_KV_HEADS, C_HDIM))
        outs["sv"].append(vc[n_p:].reshape(bs, ts, C_KV_HEADS, C_HDIM))
        outs["ski"].append(ki[n_p:].reshape(bs, ts, IDX_DIM))

    st = {k: jnp.stack(v) for k, v in outs.items()}
    return (h[:n_p].reshape(bp, tp, d), h[n_p:].reshape(bs, ts, d),
            st["pa"], st["pb"], st["pk"], st["pv"], st["pki"],
            st["sa"], st["sb"], st["sk"], st["sv"], st["ski"])
```

```python
import functools

import jax
import jax.numpy as jnp
from jax import lax
from jax.experimental import pallas as pl
from jax.experimental.pallas import tpu as pltpu

F32 = jnp.float32
BF16 = jnp.bfloat16

D_MODEL = 1024
DEPTH = 2
CHUNK = 64
EPS = 1e-6
NEG = -1e30
F_MIN = 1e-30
A_HEADS = 4
A_KDIM = 128
A_VDIM = 128
B_HEADS = 4
B_KDIM = 64
B_VDIM = 128
B_GATE_RANK = 16
B_TAU = 16.0
C_HEADS = 8
C_KV_HEADS = 2
C_HDIM = 64
C_WIDTH = C_HEADS * C_HDIM
IDX_HEADS = 8
IDX_DIM = 64
IDX_SCALE = (IDX_HEADS * IDX_DIM) ** -0.5
TOPK_MAX = 256
ROPE_THETA = 500000.0
ROPE_DIMS = C_HDIM // 4
PAIR_BLOCK = 4
D_FF = 2816

LANES = 128
HEAD_W = 128
MIX_W = 4 * HEAD_W
TOKEN_TILE = 512
KEY_BLOCK = 128
VT_ROWS = 80
INT_MIN = -(2 ** 31)

GATE_UNITS = 24
Z_A = 0
Z_B = 16
Z_SMALL = 32
Z_UNITS = 33
Z_CQ = GATE_UNITS + Z_UNITS
Z_IQ = Z_CQ + 4
Z_CK = Z_IQ + 4
Z_CV = Z_CK + 1
Z_IK = Z_CV + 1
Z_WIDTH = (Z_IK + 1) * LANES
SMALL_IW = B_GATE_RANK


def _params(sem, vmem_mb):
    return pltpu.CompilerParams(dimension_semantics=sem, vmem_limit_bytes=vmem_mb << 20)


def _dot(a, b):
    return jnp.dot(a, b, preferred_element_type=F32)


def _dot_nt(a, b):
    return lax.dot_general(a, b, (((1,), (1,)), ((), ())), preferred_element_type=F32)


def _dot_tn(a, b):
    return lax.dot_general(a, b, (((0,), (0,)), ((), ())), preferred_element_type=F32)


def _split3(x):
    h1 = x.astype(BF16)
    r1 = x - h1.astype(F32)
    h2 = r1.astype(BF16)
    h3 = (r1 - h2.astype(F32)).astype(BF16)
    return h1, h2, h3


def _dot_exact_lhs(m, x):
    h1, h2, h3 = _split3(x)
    return _dot(m, h1) + _dot(m, h2) + _dot(m, h3)


def _dot_hi(a, b):
    a1 = a.astype(BF16)
    a2 = (a - a1.astype(F32)).astype(BF16)
    b1 = b.astype(BF16)
    b2 = (b - b1.astype(F32)).astype(BF16)
    return _dot(a1, b1) + _dot(a1, b2) + _dot(a2, b1)


def _rmsnorm(x, g):
    return x * lax.rsqrt(jnp.mean(x * x, axis=-1, keepdims=True) + EPS) * g


def _ffn_body(*refs, n_in, n_out, prompt_tiles, final_norm):
    x_refs, (g_ref, w1_ref, w3_ref, w2_ref, gf_ref), o_refs = refs[:n_in], refs[n_in:n_in + 5], refs[n_in + 5:]
    is_prompt = pl.program_id(0) < prompt_tiles
    x = x_refs[0][...] if n_in == 1 else jnp.where(is_prompt, x_refs[0][...], x_refs[1][...])
    u = _rmsnorm(x, g_ref[...]).astype(BF16)
    a = _dot(u, w1_ref[...])
    b = _dot(u, w3_ref[...])
    hid = (a * jax.nn.sigmoid(a) * b).astype(BF16)
    out = x + 0.5 * _dot(hid, w2_ref[...])
    if final_norm:
        out = _rmsnorm(out, gf_ref[...])
    if n_out == 1:
        o_refs[0][...] = out
    else:
        @pl.when(is_prompt)
        def _():
            o_refs[0][...] = out

        @pl.when(jnp.logical_not(is_prompt))
        def _():
            o_refs[1][...] = out


def _ffn(xs, g, w1, w3, w2, gf, final_norm, split_out):
    d = xs[0].shape[1]
    tm = TOKEN_TILE
    sizes = [x.shape[0] for x in xs]
    n = sum(sizes)
    assert all(sz % tm == 0 for sz in sizes)
    pt = sizes[0] // tm if len(xs) == 2 else split_out // tm
    prow = lambda i: (jnp.minimum(i, pt - 1), 0)
    srow = lambda i: (jnp.maximum(i - pt, 0), 0)
    row = lambda i: (i, 0)
    const = lambda i: (0, 0)
    resident = pl.Buffered(1)
    x_specs = [pl.BlockSpec((tm, d), row)] if len(xs) == 1 else \
        [pl.BlockSpec((tm, d), prow), pl.BlockSpec((tm, d), srow)]
    if split_out:
        out_specs = [pl.BlockSpec((tm, d), prow), pl.BlockSpec((tm, d), srow)]
        out_shape = [jax.ShapeDtypeStruct((split_out, d), F32), jax.ShapeDtypeStruct((n - split_out, d), F32)]
    else:
        out_specs = pl.BlockSpec((tm, d), row)
        out_shape = jax.ShapeDtypeStruct((n, d), F32)
    return pl.pallas_call(
        functools.partial(_ffn_body, n_in=len(xs), n_out=2 if split_out else 1, prompt_tiles=pt,
                          final_norm=final_norm),
        grid=(n // tm,),
        in_specs=x_specs + [
            pl.BlockSpec((1, d), const),
            pl.BlockSpec(w1.shape, const, pipeline_mode=resident),
            pl.BlockSpec(w3.shape, const, pipeline_mode=resident),
            pl.BlockSpec(w2.shape, const, pipeline_mode=resident),
            pl.BlockSpec((1, d), const),
        ],
        out_specs=out_specs,
        out_shape=out_shape,
        compiler_params=_params(("arbitrary",), 48),
        name="ffn_half_step",
    )(*xs, g, w1, w3, w2, gf)


def _rope(x, c, s_lo, s_hi):
    w = x.shape[1]
    rep = w // LANES
    if rep > 1:
        c = jnp.concatenate([c] * rep, axis=1)
        s_lo = jnp.concatenate([s_lo] * rep, axis=1)
        s_hi = jnp.concatenate([s_hi] * rep, axis=1)
    half = ROPE_DIMS // 2
    return x * c + pltpu.roll(x, half, 1) * s_hi + pltpu.roll(x, w - half, 1) * s_lo


def _inproj_body(x_ref, g_ref, w_ref, c_ref, slo_ref, shi_ref,
                 gates_ref, z_ref, qc_ref, qi_ref, kc_ref, vc_ref, ki_ref, kcb_ref, vcb_ref, kib_ref, vt_ref):
    u = _rmsnorm(x_ref[...], g_ref[...]).astype(BF16)
    zf = _dot(u, w_ref[...])
    gates_ref[...] = jax.nn.sigmoid(zf[:, :GATE_UNITS * LANES]).astype(BF16)
    z_ref[...] = zf[:, GATE_UNITS * LANES:(GATE_UNITS + Z_UNITS) * LANES]

    unit = lambda first, count: zf[:, first * LANES:(first + count) * LANES]
    c, s_lo, s_hi = c_ref[...], slo_ref[...], shi_ref[...]
    qc_ref[...] = (_rope(unit(Z_CQ, 4), c, s_lo, s_hi) * (C_HDIM ** -0.5)).astype(BF16)
    qi_ref[...] = _rope(unit(Z_IQ, 4), c, s_lo, s_hi).astype(BF16)
    kc = _rope(unit(Z_CK, 1), c, s_lo, s_hi)
    vc = unit(Z_CV, 1)
    ki = _rope(unit(Z_IK, 1), c, s_lo, s_hi)[:, :IDX_DIM]
    kc_ref[...] = kc
    vc_ref[...] = vc
    ki_ref[...] = ki
    kcb_ref[...] = kc.astype(BF16)
    vcb_ref[...] = vc.astype(BF16)
    kib_ref[...] = ki.astype(BF16)
    ones = jnp.ones((VT_ROWS - C_HDIM, KEY_BLOCK), BF16)
    for kk in range(vt_ref.shape[0]):
        vt = vc[kk * KEY_BLOCK:(kk + 1) * KEY_BLOCK, :].T.astype(BF16)
        vt_ref[kk] = jnp.concatenate([vt[:C_HDIM], ones, vt[C_HDIM:], ones], axis=0)


def _inproj(h, g, w, cos_t, sin_lo, sin_hi):
    n, d = h.shape
    tm = min(TOKEN_TILE // 2, n)
    row = lambda i: (i, 0)
    return pl.pallas_call(
        _inproj_body,
        grid=(n // tm,),
        in_specs=[
            pl.BlockSpec((tm, d), row),
            pl.BlockSpec((1, d), lambda i: (0, 0)),
            pl.BlockSpec((d, Z_WIDTH), lambda i: (0, 0), pipeline_mode=pl.Buffered(1)),
            pl.BlockSpec((tm, LANES), row),
            pl.BlockSpec((tm, LANES), row),
            pl.BlockSpec((tm, LANES), row),
        ],
        out_specs=[
            pl.BlockSpec((tm, GATE_UNITS * LANES), row),
            pl.BlockSpec((tm, Z_UNITS * LANES), row),
            pl.BlockSpec((tm, C_WIDTH), row),
            pl.BlockSpec((tm, IDX_HEADS * IDX_DIM), row),
            pl.BlockSpec((tm, LANES), row),
            pl.BlockSpec((tm, LANES), row),
            pl.BlockSpec((tm, IDX_DIM), row),
            pl.BlockSpec((tm, LANES), row),
            pl.BlockSpec((tm, LANES), row),
            pl.BlockSpec((tm, IDX_DIM), row),
            pl.BlockSpec((tm // KEY_BLOCK, C_KV_HEADS * VT_ROWS, KEY_BLOCK), lambda i: (i, 0, 0)),
        ],
        out_shape=[
            jax.ShapeDtypeStruct((n, GATE_UNITS * LANES), BF16),
            jax.ShapeDtypeStruct((n, Z_UNITS * LANES), F32),
            jax.ShapeDtypeStruct((n, C_WIDTH), BF16),
            jax.ShapeDtypeStruct((n, IDX_HEADS * IDX_DIM), BF16),
            jax.ShapeDtypeStruct((n, LANES), F32),
            jax.ShapeDtypeStruct((n, LANES), F32),
            jax.ShapeDtypeStruct((n, IDX_DIM), F32),
            jax.ShapeDtypeStruct((n, LANES), BF16),
            jax.ShapeDtypeStruct((n, LANES), BF16),
            jax.ShapeDtypeStruct((n, IDX_DIM), BF16),
            jax.ShapeDtypeStruct((n // KEY_BLOCK, C_KV_HEADS * VT_ROWS, KEY_BLOCK), BF16),
        ],
        compiler_params=_params(("parallel",), 48),
        name="in_projection",
    )(h, g, w, cos_t, sin_lo, sin_hi)


def _gla_head(q, k, v, logf, st_ref, h, tc):
    row = lax.broadcasted_iota(jnp.int32, (tc, tc), 0)
    col = lax.broadcasted_iota(jnp.int32, (tc, tc), 1)
    tril = (col <= row).astype(BF16)
    cum = _dot_exact_lhs(tril, logf)

    tile = 2 * PAIR_BLOCK
    nt = tc // tile
    b3 = cum.reshape(nt, tile, HEAD_W)
    q3 = q.reshape(nt, tile, HEAD_W)
    k3 = k.reshape(nt, tile, HEAD_W)
    v3 = v.reshape(nt, tile, HEAD_W)
    srow = lax.broadcasted_iota(jnp.int32, (nt, tile, 1), 1)
    low = srow < PAIR_BLOCK
    tloc = srow & (PAIR_BLOCK - 1)

    def block_row(x3, j):
        return jnp.where(low, x3[:, j:j + 1, :], x3[:, PAIR_BLOCK + j:PAIR_BLOCK + j + 1, :])

    o3 = jnp.zeros((nt, tile, HEAD_W), F32)
    for j in range(PAIR_BLOCK):
        causal = tloc >= j
        decay = jnp.exp(jnp.where(causal, b3 - block_row(b3, j), 0.0))
        w = jnp.sum(q3 * block_row(k3, j) * decay, axis=-1, keepdims=True)
        o3 = o3 + jnp.where(causal, w, 0.0) * block_row(v3, j)
    o = o3.reshape(tc, HEAD_W)

    vb = v.astype(BF16)
    attn = jnp.zeros((tc, tc), F32)
    half = tc // 2
    while half >= PAIR_BLOCK:
        blk = 2 * half
        nblk = tc // blk
        bl = cum.reshape(nblk, blk, HEAD_W)
        x = bl - bl[:, half - 1:half, :]
        second = lax.broadcasted_iota(jnp.int32, (nblk, blk, 1), 1) >= half
        e = jnp.exp(jnp.where(second, x, -x))
        qt = jnp.where(second, q.reshape(nblk, blk, HEAD_W) * e, 0.0).reshape(tc, HEAD_W).astype(BF16)
        kt = jnp.where(second, 0.0, k.reshape(nblk, blk, HEAD_W) * e).reshape(tc, HEAD_W).astype(BF16)
        shift = blk.bit_length() - 1
        same = (row >> shift) == (col >> shift)
        attn = attn + jnp.where(same, _dot_nt(qt, kt), 0.0)
        half //= 2
    o = o + _dot(attn.astype(BF16), vb)

    st = st_ref[h]
    o = o + _dot_nt((q * jnp.exp(cum)).astype(BF16), st.astype(BF16))
    last = cum[tc - 1:tc, :]
    kd = (k * jnp.exp(last - cum)).astype(BF16)
    st_ref[h] = st * jnp.exp(last) + _dot_tn(vb, kd)
    return o


def _gla_finish(o, nw, gate):
    return _rmsnorm(o, nw) * (gate * jax.nn.sigmoid(gate))


def _gla_state_io(c, s0_ref, st_ref, heads):
    @pl.when(c == 0)
    def _():
        for h in range(heads):
            st_ref[h] = s0_ref[0, h].T


def _gla_state_out(c, sout_ref, st_ref, heads):
    @pl.when(c == pl.num_programs(1) - 1)
    def _():
        for h in range(heads):
            sout_ref[0, h] = st_ref[h].T


def _hgrn_body(q_ref, f_ref, v_ref, g_ref, lb_ref, nw_ref, s0_ref, y_ref, sout_ref, st_ref, *, tc):
    c = pl.program_id(1)
    _gla_state_io(c, s0_ref, st_ref, A_HEADS)
    for h in range(A_HEADS):
        hs = slice(h * HEAD_W, (h + 1) * HEAD_W)
        zf = f_ref[:, hs]
        lb = lb_ref[:, hs]
        f = lb + (1.0 - lb) * jax.nn.sigmoid(zf)
        logf = jnp.log(jnp.maximum(f, F_MIN))
        k = (1.0 - lb) * jax.nn.sigmoid(-zf)
        zq = q_ref[:, hs]
        q = zq * jax.nn.sigmoid(zq) * (A_KDIM ** -0.5)
        o = _gla_head(q, k, v_ref[:, hs], logf, st_ref, h, tc)
        y_ref[:, hs] = _gla_finish(o, nw_ref[...], g_ref[:, hs]).astype(BF16)
    _gla_state_out(c, sout_ref, st_ref, A_HEADS)


def _gla_body(q_ref, k_ref, v_ref, g_ref, r_ref, wup_ref, bup_ref, nw_ref, s0_ref,
              y_ref, sout_ref, st_ref, *, tc):
    c = pl.program_id(1)
    _gla_state_io(c, s0_ref, st_ref, B_HEADS)
    r = _dot_hi(r_ref[...], wup_ref[...]) + bup_ref[...]
    logf_all = (jnp.minimum(r, 0.0) - jnp.log1p(jnp.exp(-jnp.abs(r)))) / B_TAU
    for h in range(B_HEADS):
        hs = slice(h * HEAD_W, (h + 1) * HEAD_W)
        q = q_ref[:, hs] * (B_KDIM ** -0.5)
        o = _gla_head(q, k_ref[:, hs], v_ref[:, hs], logf_all[:, hs], st_ref, h, tc)
        y_ref[:, hs] = _gla_finish(o, nw_ref[...], g_ref[:, hs]).astype(BF16)
    _gla_state_out(c, sout_ref, st_ref, B_HEADS)


def _recurrent_mixer(mode, z, row0, nseq, t, s0, nw, extra):
    tc = min(128, t)
    nc = t // tc
    rb0 = row0 // tc
    zcol = (Z_A if mode == "hgrn" else Z_B) // 4

    def zspec(k):
        return pl.BlockSpec((tc, MIX_W), lambda b, c: (rb0 + b * nc + c, zcol + k))

    const = lambda b, c: (0, 0)
    state_spec = pl.BlockSpec((1, 4, HEAD_W, HEAD_W), lambda b, c: (b, 0, 0, 0))
    if mode == "hgrn":
        body = functools.partial(_hgrn_body, tc=tc)
        in_specs = [zspec(0), zspec(1), zspec(2), zspec(3),
                    pl.BlockSpec((1, MIX_W), const), pl.BlockSpec((1, HEAD_W), const), state_spec]
        args = (z, z, z, z, extra[0], nw, s0)
    else:
        body = functools.partial(_gla_body, tc=tc)
        in_specs = [zspec(0), zspec(1), zspec(2), zspec(3),
                    pl.BlockSpec((tc, LANES), lambda b, c: (rb0 + b * nc + c, Z_SMALL)),
                    pl.BlockSpec((LANES, MIX_W), const), pl.BlockSpec((1, MIX_W), const),
                    pl.BlockSpec((1, HEAD_W), const), state_spec]
        args = (z, z, z, z, z, extra[0], extra[1], nw, s0)
    return pl.pallas_call(
        body,
        grid=(nseq, nc),
        in_specs=in_specs,
        out_specs=[pl.BlockSpec((tc, MIX_W), lambda b, c: (b * nc + c, 0)), state_spec],
        out_shape=[jax.ShapeDtypeStruct((nseq * t, MIX_W), BF16),
                   jax.ShapeDtypeStruct((nseq, 4, HEAD_W, HEAD_W), F32)],
        scratch_shapes=[pltpu.VMEM((4, HEAD_W, HEAD_W), F32)],
        compiler_params=_params(("parallel", "arbitrary"), 32),
        name=mode + "_mixer",
    )(*args)


SUPER = 4
ATTN_BLOCKS = 2


def _dsa_t_body(qc_ref, qi_ref, sm_ref, kc_ref, vt_ref, ki_ref, o_ref,
                sk_ref, qit_ref, qct_ref, wrow_ref, acc_ref, ot_ref, s0_ref, s1_ref, p_ref,
                *, tq, past, nkeys, topk):
    i = pl.program_id(1)
    qpos0 = past + i * tq
    last_chunk = (qpos0 + tq - 1) // CHUNK
    n_adm = jnp.minimum((last_chunk + 1) * CHUNK, nkeys)
    nkb = (n_adm + KEY_BLOCK - 1) // KEY_BLOCK
    nsb = (nkb + SUPER - 1) // SUPER
    grp = C_HEADS // C_KV_HEADS

    qit = qi_ref[...].astype(F32).T
    for j in range(IDX_HEADS):
        qit_ref[:, j * tq:(j + 1) * tq] = qit[j * IDX_DIM:(j + 1) * IDX_DIM, :].astype(BF16)
    qct = qc_ref[...].astype(F32).T
    zeros = jnp.zeros((C_HDIM, tq), BF16)
    for hq in range(C_HEADS):
        g, r = divmod(hq, grp)
        for gg in range(C_KV_HEADS):
            val = qct[hq * C_HDIM:(hq + 1) * C_HDIM, :].astype(BF16) if gg == g else zeros
            qct_ref[g, gg * C_HDIM:(gg + 1) * C_HDIM, r * tq:(r + 1) * tq] = val
    wrow_ref[...] = sm_ref[...].T

    qchunk = (qpos0 + lax.broadcasted_iota(jnp.int32, (1, tq), 1)) >> 6
    ksub = lax.broadcasted_iota(jnp.int32, (KEY_BLOCK, 1), 0)

    def admissible(kb):
        kpos = kb * KEY_BLOCK + ksub
        return ((kpos >> 6) <= qchunk) & (kpos < nkeys)

    def score_step(kk, carry):
        for u in range(SUPER):
            kb = kk * SUPER + u
            kib = ki_ref[0, pl.ds(pl.multiple_of(kb * KEY_BLOCK, KEY_BLOCK), KEY_BLOCK), :]
            d = _dot(kib, qit_ref[...])
            sc = jnp.maximum(d[:, 0:tq], 0.0) * wrow_ref[SMALL_IW:SMALL_IW + 1, :]
            for j in range(1, IDX_HEADS):
                sc = sc + jnp.maximum(d[:, j * tq:(j + 1) * tq], 0.0) * wrow_ref[SMALL_IW + j:SMALL_IW + j + 1, :]
            bits = pltpu.bitcast(sc * IDX_SCALE + 0.0, jnp.int32)
            key = jnp.where(admissible(kb), jnp.where(bits < 0, bits ^ 0x7FFFFFFF, bits), INT_MIN)
            sk_ref[kb] = key
        return carry

    lax.fori_loop(0, nsb, score_step, 0)

    def fold(hit):
        return jnp.sum(hit.reshape(KEY_BLOCK // 8, 8, tq), axis=0)

    def count(pred):
        def step(kk, acc):
            for u in range(SUPER):
                kb = kk * SUPER + u
                acc = acc + fold(jnp.where(pred(sk_ref[kb], kb), 1.0, 0.0))
            return acc
        acc = lax.fori_loop(0, nsb, step, jnp.zeros((8, tq), F32))
        return jnp.sum(acc, axis=0, keepdims=True)

    kf = float(topk)
    c0 = count(lambda s, kb: s >= 0)
    t0 = jnp.where(c0 >= kf, 0, INT_MIN).astype(jnp.int32)

    def bit_step(it, t):
        cand = t | jnp.left_shift(jnp.int32(1), 30 - it)
        return jnp.where(count(lambda s, kb: s >= cand) >= kf, cand, t)
    thr = lax.fori_loop(0, 31, bit_step, t0)
    thr = jnp.maximum(thr, INT_MIN + 1)

    c_ge = count(lambda s, kb: s >= thr)
    nbits = max(1, (nkeys - 1).bit_length())

    @pl.when(jnp.max(c_ge) > kf)
    def _():
        need = kf - count(lambda s, kb: s > thr)

        def cut_step(it, cut):
            cand = cut | jnp.left_shift(jnp.int32(1), nbits - 1 - it)
            c = count(lambda s, kb: (s == thr) & (kb * KEY_BLOCK + ksub < cand))
            return jnp.where(c < need, cand, cut)
        cut = lax.fori_loop(0, nbits, cut_step, jnp.zeros((1, tq), jnp.int32))

        def strike(kb, carry):
            key = sk_ref[kb]
            sk_ref[kb] = jnp.where((key == thr) & (kb * KEY_BLOCK + ksub > cut), INT_MIN, key)
            return carry
        lax.fori_loop(0, nsb * SUPER, strike, 0)

    acc_ref[...] = jnp.zeros(acc_ref.shape, F32)
    init = tuple(jnp.full((1, grp * tq), NEG, F32) for _ in range(C_KV_HEADS))

    span = ATTN_BLOCKS * KEY_BLOCK
    n_attn = nsb * (SUPER // ATTN_BLOCKS)

    s_bufs = (s0_ref, s1_ref)

    def qk_scores(kk, half):
        kblk = kc_ref[0, pl.ds(pl.multiple_of(kk * span, span), span), :]
        bias = jnp.concatenate(
            [jnp.where(sk_ref[kk * ATTN_BLOCKS + u] >= thr, 0.0, NEG) for u in range(ATTN_BLOCKS)], axis=0)
        bias = jnp.concatenate([bias] * grp, axis=1)
        for g in range(C_KV_HEADS):
            s = _dot(kblk, qct_ref[g]) + bias
            for r in range(grp):
                s_bufs[half][g, r] = s[:, r * tq:(r + 1) * tq]

    def softmax_pv(kk, half, m):
        m = list(m)
        for g in range(C_KV_HEADS):
            m_cols, alpha_cols = [], []
            for r in range(grp):
                cols = slice(r * tq, (r + 1) * tq)
                sb = s_bufs[half][g, r]
                m_old = m[g][:, cols]
                m_new = jnp.maximum(m_old, jnp.max(sb, axis=0, keepdims=True))
                p_ref[half, g, r] = jnp.exp(sb - m_new).astype(BF16)
                m_cols.append(m_new)
                alpha_cols.append(jnp.exp(m_old - m_new))
            rows = slice(g * VT_ROWS, (g + 1) * VT_ROWS)
            pv = None
            for u in range(ATTN_BLOCKS):
                keys_u = slice(u * KEY_BLOCK, (u + 1) * KEY_BLOCK)
                p_u = jnp.concatenate([p_ref[half, g, r, keys_u, :] for r in range(grp)], axis=1)
                d = _dot(vt_ref[0, kk * ATTN_BLOCKS + u, rows, :], p_u)
                pv = d if pv is None else pv + d
            for r in range(grp):
                acc_ref[g, r] = alpha_cols[r] * acc_ref[g, r] + pv[:, r * tq:(r + 1) * tq]
            m[g] = jnp.concatenate(m_cols, axis=1)
        return tuple(m)

    qk_scores(0, 0)

    def pair_step(j, m):
        qk_scores(2 * j + 1, 1)
        m = softmax_pv(2 * j, 0, m)
        qk_scores(jnp.minimum(2 * j + 2, n_attn - 2), 0)
        return softmax_pv(2 * j + 1, 1, m)

    lax.fori_loop(0, n_attn // 2, pair_step, init)
    for hq in range(C_HEADS):
        g, r = divmod(hq, grp)
        ot_ref[hq * C_HDIM:(hq + 1) * C_HDIM, :] = acc_ref[g, r, 0:C_HDIM, :] / acc_ref[g, r, C_HDIM:C_HDIM + 1, :]
    o_ref[...] = ot_ref[...].T.astype(BF16)


def _dsa_t(qc, qi, small, small_col, keys, vals_t, kidx, nseq, t, past, nkeys):
    tq = KEY_BLOCK
    nq = t // tq
    lp = keys.shape[1]
    assert t % tq == 0 and lp % (SUPER * KEY_BLOCK) == 0
    topk = min(TOPK_MAX, nkeys // 4)
    grp = C_HEADS // C_KV_HEADS
    qrow = lambda b, i: (b * nq + i, 0)
    return pl.pallas_call(
        functools.partial(_dsa_t_body, tq=tq, past=past, nkeys=nkeys, topk=topk),
        grid=(nseq, nq),
        in_specs=[
            pl.BlockSpec((tq, C_WIDTH), qrow),
            pl.BlockSpec((tq, IDX_HEADS * IDX_DIM), qrow),
            pl.BlockSpec((tq, LANES), lambda b, i: (b * nq + i, small_col)),
            pl.BlockSpec((1, lp, LANES), lambda b, i: (b, 0, 0)),
            pl.BlockSpec((1, lp // KEY_BLOCK, C_KV_HEADS * VT_ROWS, KEY_BLOCK), lambda b, i: (b, 0, 0, 0)),
            pl.BlockSpec((1, lp, IDX_DIM), lambda b, i: (b, 0, 0)),
        ],
        out_specs=pl.BlockSpec((tq, C_WIDTH), lambda b, i: (b * nq + i, 0)),
        out_shape=jax.ShapeDtypeStruct((nseq * t, C_WIDTH), BF16),
        scratch_shapes=[
            pltpu.VMEM((lp // KEY_BLOCK, KEY_BLOCK, tq), jnp.int32),
            pltpu.VMEM((IDX_DIM, IDX_HEADS * tq), BF16),
            pltpu.VMEM((C_KV_HEADS, C_KV_HEADS * C_HDIM, grp * tq), BF16),
            pltpu.VMEM((LANES, tq), F32),
            pltpu.VMEM((C_KV_HEADS, grp, VT_ROWS, tq), F32),
            pltpu.VMEM((C_WIDTH, tq), F32),
            pltpu.VMEM((C_KV_HEADS, grp, ATTN_BLOCKS * KEY_BLOCK, tq), F32),
            pltpu.VMEM((C_KV_HEADS, grp, ATTN_BLOCKS * KEY_BLOCK, tq), F32),
            pltpu.VMEM((2, C_KV_HEADS, grp, ATTN_BLOCKS * KEY_BLOCK, tq), BF16),
        ],
        compiler_params=_params(("parallel", "arbitrary"), 48),
        name="dsa_mixer_t",
    )(qc, qi, small, keys, vals_t, kidx)


def _merge_body(h_ref, yap_ref, yas_ref, ybp_ref, ybs_ref, ocp_ref, ocs_ref, ga_ref, gb_ref, gc_ref,
                wa_ref, wb_ref, wc_ref, wo_ref, o_ref, *, prompt_tiles):
    is_prompt = pl.program_id(0) < prompt_tiles
    pick = lambda p_ref, s_ref: jnp.where(is_prompt, p_ref[...], s_ref[...])
    merged = (ga_ref[...].astype(F32) * _dot(pick(yap_ref, yas_ref), wa_ref[...])
              + gb_ref[...].astype(F32) * _dot(pick(ybp_ref, ybs_ref), wb_ref[...])
              + gc_ref[...].astype(F32) * _dot(pick(ocp_ref, ocs_ref), wc_ref[...]))
    o_ref[...] = h_ref[...] + _dot(merged.astype(BF16), wo_ref[...])


def _merge(h, ya, yb, oc, gates, wa, wb, wc, wo):
    n, d = h.shape
    tm = TOKEN_TILE
    pt = ya[0].shape[0] // tm
    assert ya[0].shape[0] % tm == 0 and ya[1].shape[0] % tm == 0 and n == ya[0].shape[0] + ya[1].shape[0]
    row = lambda i: (i, 0)
    const = lambda i: (0, 0)
    prow = lambda i: (jnp.minimum(i, pt - 1), 0)
    srow = lambda i: (jnp.maximum(i - pt, 0), 0)
    pair = [pl.BlockSpec((tm, MIX_W), prow), pl.BlockSpec((tm, MIX_W), srow)]
    return pl.pallas_call(
        functools.partial(_merge_body, prompt_tiles=pt),
        grid=(n // tm,),
        in_specs=[
            pl.BlockSpec((tm, d), row),
            *pair, *pair, *pair,
            pl.BlockSpec((tm, d), lambda i: (i, 0)),
            pl.BlockSpec((tm, d), lambda i: (i, 1)),
            pl.BlockSpec((tm, d), lambda i: (i, 2)),
            pl.BlockSpec((MIX_W, d), const),
            pl.BlockSpec((MIX_W, d), const),
            pl.BlockSpec((MIX_W, d), const),
            pl.BlockSpec((d, d), const),
        ],
        out_specs=pl.BlockSpec((tm, d), row),
        out_shape=jax.ShapeDtypeStruct((n, d), F32),
        compiler_params=_params(("parallel",), 48),
        name="gated_merge",
    )(h, *ya, *yb, *oc, gates, gates, gates, wa, wb, wc, wo)


def _pad_heads(w, heads, dim):
    lead = w.shape[:-1]
    w = w.reshape(*lead, heads, dim)
    w = jnp.pad(w, [(0, 0)] * len(lead) + [(0, 0), (0, HEAD_W - dim)])
    return w.reshape(*lead, heads * HEAD_W)


def _layout_w_in(w):
    widths = (512, 512, 512, 512, 256, 256, 512, 16, 512, 512, 128, 128, 512, 64, 8, 1024, 1024, 1024)
    parts, o = [], 0
    for wd in widths:
        parts.append(w[:, o:o + wd])
        o += wd
    (a_q, a_f, a_i, a_g, b_q, b_k, b_v, b_r, b_g, c_q, c_k, c_v, i_q, i_k, i_w, g_a, g_b, g_c) = parts
    d = w.shape[0]
    small = jnp.concatenate([b_r, i_w, jnp.zeros((d, LANES - 24), w.dtype)], axis=1)
    i_k = jnp.pad(i_k, ((0, 0), (0, LANES - IDX_DIM)))
    cols = [g_a, g_b, g_c, a_q, a_f, a_i, a_g,
            _pad_heads(b_q, B_HEADS, B_KDIM), _pad_heads(b_k, B_HEADS, B_KDIM), b_v, b_g,
            small, c_q, i_q, c_k, c_v, i_k]
    out = jnp.concatenate(cols, axis=1)
    assert out.shape[1] == Z_WIDTH
    return out.astype(BF16)


def _rope_tables(pos):
    half = ROPE_DIMS // 2
    inv = ROPE_THETA ** (-jnp.arange(half, dtype=F32) * (2.0 / ROPE_DIMS))
    ang = pos.astype(F32)[:, None] * inv[None, :]
    cos, sin = jnp.cos(ang), jnp.sin(ang)
    n = pos.shape[0]
    ones = jnp.ones((n, C_HDIM - ROPE_DIMS), F32)
    zeros = jnp.zeros((n, C_HDIM - ROPE_DIMS), F32)
    zh = jnp.zeros((n, half), F32)
    c = jnp.concatenate([cos, cos, ones], axis=1)
    s_lo = jnp.concatenate([-sin, zh, zeros], axis=1)
    s_hi = jnp.concatenate([zh, sin, zeros], axis=1)
    two = lambda a: jnp.concatenate([a, a], axis=1)
    return two(c), two(s_lo), two(s_hi)


def kernel(x_prompt, x_sample, state_hgrn, state_gla, cache_k, cache_v, cache_kidx, hgrn_lb, w_in, w_gla_up, b_gla, norm_hgrn, norm_gla, w_br_a, w_br_b, w_br_c, w_out, norm_ffn1, norm_mix, norm_ffn2, ffn1_w1, ffn1_w3, ffn1_w2, ffn2_w1, ffn2_w3, ffn2_w2, norm_final):
    bp, tp, d = x_prompt.shape
    bs, ts, _ = x_sample.shape
    past = cache_k.shape[2]
    n_p, n_s = bp * tp, bs * ts

    lb_sm = jax.nn.softmax(hgrn_lb.astype(F32), axis=0)
    lb_all = jnp.concatenate([jnp.zeros_like(lb_sm[:1]), jnp.cumsum(lb_sm[1:], axis=0)], axis=0)

    pos = jnp.concatenate([jnp.tile(jnp.arange(tp, dtype=jnp.int32), bp),
                           jnp.tile(past + jnp.arange(ts, dtype=jnp.int32), bs)])
    cos_t, sin_lo, sin_hi = _rope_tables(pos)

    hs = (x_prompt.reshape(n_p, d), x_sample.reshape(n_s, d))
    row2 = lambda a: a.reshape(1, -1)
    zero_state = jnp.zeros((bp, 4, HEAD_W, HEAD_W), F32)
    span_s = SUPER * KEY_BLOCK
    key_pad = -(-(past + ts) // span_s) * span_s - past - ts
    q_rep = KEY_BLOCK // ts

    outs = {k: [] for k in ("pa", "pb", "pk", "pv", "pki", "sa", "sb", "sk", "sv", "ski")}
    for l in range(DEPTH):
        bf = lambda a: a[l].astype(BF16)
        h = _ffn(hs, row2(norm_ffn1[l]), bf(ffn1_w1), bf(ffn1_w3), bf(ffn1_w2), row2(norm_final), False, 0)
        gates, z, qc, qi, kc, vc, ki, kcb, vcb, kib, vt = _inproj(
            h, row2(norm_mix[l]), _layout_w_in(w_in[l]), cos_t, sin_lo, sin_hi)

        lb = row2(lb_all[l])
        nwa, nwb = row2(norm_hgrn[l]), row2(norm_gla[l])
        wup = jnp.pad(_pad_heads(w_gla_up[l], B_HEADS, B_KDIM), ((0, LANES - B_GATE_RANK), (0, 0)))
        bup = row2(_pad_heads(b_gla[l], B_HEADS, B_KDIM))
        sb0 = jnp.pad(state_gla[l], ((0, 0), (0, 0), (0, HEAD_W - B_KDIM), (0, 0)))

        ya_p, sa_p = _recurrent_mixer("hgrn", z, 0, bp, tp, zero_state, nwa, (lb,))
        ya_s, sa_s = _recurrent_mixer("hgrn", z, n_p, bs, ts, state_hgrn[l], nwa, (lb,))
        yb_p, sb_p = _recurrent_mixer("gla", z, 0, bp, tp, zero_state, nwb, (wup, bup))
        yb_s, sb_s = _recurrent_mixer("gla", z, n_p, bs, ts, sb0, nwb, (wup, bup))

        seqs = lambda a, n, t: a.reshape(n, t, a.shape[-1])
        vt_p = vt[:n_p // KEY_BLOCK].reshape(bp, tp // KEY_BLOCK, C_KV_HEADS * VT_ROWS, KEY_BLOCK)
        oc_p = _dsa_t(qc, qi, z, Z_SMALL, seqs(kcb[:n_p], bp, tp), vt_p, seqs(kib[:n_p], bp, tp), bp, tp, 0, tp)

        def with_cache(cache, new):
            full = jnp.concatenate([cache.reshape(bs, past, -1).astype(BF16), seqs(new[n_p:], bs, ts)], axis=1)
            return jnp.pad(full, ((0, 0), (0, key_pad), (0, 0)))

        def fill_block(a):
            return jnp.tile(seqs(a[n_p:], bs, ts), (1, q_rep, 1)).reshape(bs * KEY_BLOCK, a.shape[-1])

        vals_s = with_cache(cache_v[l], vcb)
        vt_s = jnp.swapaxes(vals_s.reshape(bs, -1, KEY_BLOCK, LANES), 2, 3)
        ones = jnp.ones(vt_s.shape[:2] + (VT_ROWS - C_HDIM, KEY_BLOCK), BF16)
        vt_s = jnp.concatenate([vt_s[:, :, :C_HDIM], ones, vt_s[:, :, C_HDIM:], ones], axis=2)
        small_s = fill_block(z[:, Z_SMALL * LANES:(Z_SMALL + 1) * LANES])
        oc_s = _dsa_t(fill_block(qc), fill_block(qi), small_s, 0, with_cache(cache_k[l], kcb), vt_s,
                      with_cache(cache_kidx[l], kib), bs, KEY_BLOCK, past, past + ts)
        oc_s = oc_s.reshape(bs, KEY_BLOCK, C_WIDTH)[:, :ts].reshape(n_s, C_WIDTH)

        h = _merge(h, (ya_p, ya_s), (yb_p, yb_s), (oc_p, oc_s), gates,
                   bf(w_br_a), bf(w_br_b), bf(w_br_c), bf(w_out))
        last = l == DEPTH - 1
        hs = _ffn((h,), row2(norm_ffn2[l]), bf(ffn2_w1), bf(ffn2_w3), bf(ffn2_w2), row2(norm_final),
                  last, n_p if last else 0)
        hs = hs if last else (hs,)

        outs["pa"].append(sa_p)
        outs["sa"].append(sa_s)
        outs["pb"].append(sb_p[:, :, :B_KDIM, :])
        outs["sb"].append(sb_s[:, :, :B_KDIM, :])
        outs["pk"].append(kc[:n_p].reshape(bp, tp, C_KV_HEADS, C_HDIM))
        outs["pv"].append(vc[:n_p].reshape(bp, tp, C_KV_HEADS, C_HDIM))
        outs["pki"].append(ki[:n_p].reshape(bp, tp, IDX_DIM))
        outs["sk"].append(kc[n_p:].reshape(bs, ts, C_KV_HEADS, C_HDIM))
        outs["sv"].append(vc[n_p:].reshape(bs, ts, C_KV_HEADS, C_HDIM))
        outs["ski"].append(ki[n_p:].reshape(bs, ts, IDX_DIM))

    st = {k: jnp.stack(v) for k, v in outs.items()}
    return (hs[0].reshape(bp, tp, d), hs[1].reshape(bs, ts, d),
            st["pa"], st["pb"], st["pk"], st["pv"], st["pki"],
            st["sa"], st["sb"], st["sk"], st["sv"], st["ski"])
```

```python
import functools

import jax
import jax.numpy as jnp
from jax import lax
from jax.experimental import pallas as pl
from jax.experimental.pallas import tpu as pltpu

F32 = jnp.float32
BF16 = jnp.bfloat16

D_MODEL = 1024
DEPTH = 2
CHUNK = 64
EPS = 1e-6
NEG = -1e30
F_MIN = 1e-30
A_HEADS = 4
A_KDIM = 128
A_VDIM = 128
B_HEADS = 4
B_KDIM = 64
B_VDIM = 128
B_GATE_RANK = 16
B_TAU = 16.0
C_HEADS = 8
C_KV_HEADS = 2
C_HDIM = 64
C_WIDTH = C_HEADS * C_HDIM
IDX_HEADS = 8
IDX_DIM = 64
IDX_SCALE = (IDX_HEADS * IDX_DIM) ** -0.5
TOPK_MAX = 256
ROPE_THETA = 500000.0
ROPE_DIMS = C_HDIM // 4
PAIR_BLOCK = 4
D_FF = 2816

LANES = 128
HEAD_W = 128
MIX_W = 4 * HEAD_W
TOKEN_TILE = 512
PROJ_TILE = 256
KEY_BLOCK = 128
VT_ROWS = 80
INT_MIN = -(2 ** 31)

GATE_UNITS = 24
Z_A = 0
Z_B = 16
Z_SMALL = 32
Z_UNITS = 33
Z_CQ = GATE_UNITS + Z_UNITS
Z_IQ = Z_CQ + 4
Z_CK = Z_IQ + 4
Z_CV = Z_CK + 1
Z_IK = Z_CV + 1
Z_WIDTH = (Z_IK + 1) * LANES
SMALL_IW = B_GATE_RANK


def _params(sem, vmem_mb):
    return pltpu.CompilerParams(dimension_semantics=sem, vmem_limit_bytes=vmem_mb << 20)


def _dot(a, b):
    return jnp.dot(a, b, preferred_element_type=F32)


def _dot_nt(a, b):
    return lax.dot_general(a, b, (((1,), (1,)), ((), ())), preferred_element_type=F32)


def _dot_tn(a, b):
    return lax.dot_general(a, b, (((0,), (0,)), ((), ())), preferred_element_type=F32)


def _split3(x):
    h1 = x.astype(BF16)
    r1 = x - h1.astype(F32)
    h2 = r1.astype(BF16)
    h3 = (r1 - h2.astype(F32)).astype(BF16)
    return h1, h2, h3


def _dot_exact_lhs(m, x):
    h1, h2, h3 = _split3(x)
    return _dot(m, h1) + _dot(m, h2) + _dot(m, h3)


def _dot_hi(a, b):
    a1 = a.astype(BF16)
    a2 = (a - a1.astype(F32)).astype(BF16)
    b1 = b.astype(BF16)
    b2 = (b - b1.astype(F32)).astype(BF16)
    return _dot(a1, b1) + _dot(a1, b2) + _dot(a2, b1)


def _rmsnorm(x, g):
    return x * lax.rsqrt(jnp.mean(x * x, axis=-1, keepdims=True) + EPS) * g


def _ffn_body(*refs, n_in, n_out, prompt_tiles, final_norm):
    x_refs, (g_ref, w1_ref, w3_ref, w2_ref, gf_ref), o_refs = refs[:n_in], refs[n_in:n_in + 5], refs[n_in + 5:]
    is_prompt = pl.program_id(0) < prompt_tiles
    x = x_refs[0][...] if n_in == 1 else jnp.where(is_prompt, x_refs[0][...], x_refs[1][...])
    u = _rmsnorm(x, g_ref[...]).astype(BF16)
    a = _dot(u, w1_ref[...])
    b = _dot(u, w3_ref[...])
    hid = (a * jax.nn.sigmoid(a) * b).astype(BF16)
    out = x + 0.5 * _dot(hid, w2_ref[...])
    if final_norm:
        out = _rmsnorm(out, gf_ref[...])
    if n_out == 1:
        o_refs[0][...] = out
    else:
        @pl.when(is_prompt)
        def _():
            o_refs[0][...] = out

        @pl.when(jnp.logical_not(is_prompt))
        def _():
            o_refs[1][...] = out


def _ffn(xs, g, w1, w3, w2, gf, final_norm, split_out):
    d = xs[0].shape[1]
    tm = TOKEN_TILE
    sizes = [x.shape[0] for x in xs]
    n = sum(sizes)
    assert all(sz % tm == 0 for sz in sizes)
    pt = sizes[0] // tm if len(xs) == 2 else split_out // tm
    prow = lambda i: (jnp.minimum(i, pt - 1), 0)
    srow = lambda i: (jnp.maximum(i - pt, 0), 0)
    row = lambda i: (i, 0)
    const = lambda i: (0, 0)
    resident = pl.Buffered(1)
    x_specs = [pl.BlockSpec((tm, d), row)] if len(xs) == 1 else \
        [pl.BlockSpec((tm, d), prow), pl.BlockSpec((tm, d), srow)]
    if split_out:
        out_specs = [pl.BlockSpec((tm, d), prow), pl.BlockSpec((tm, d), srow)]
        out_shape = [jax.ShapeDtypeStruct((split_out, d), F32), jax.ShapeDtypeStruct((n - split_out, d), F32)]
    else:
        out_specs = pl.BlockSpec((tm, d), row)
        out_shape = jax.ShapeDtypeStruct((n, d), F32)
    return pl.pallas_call(
        functools.partial(_ffn_body, n_in=len(xs), n_out=2 if split_out else 1, prompt_tiles=pt,
                          final_norm=final_norm),
        grid=(n // tm,),
        in_specs=x_specs + [
            pl.BlockSpec((1, d), const),
            pl.BlockSpec(w1.shape, const, pipeline_mode=resident),
            pl.BlockSpec(w3.shape, const, pipeline_mode=resident),
            pl.BlockSpec(w2.shape, const, pipeline_mode=resident),
            pl.BlockSpec((1, d), const),
        ],
        out_specs=out_specs,
        out_shape=out_shape,
        compiler_params=_params(("arbitrary",), 48),
        name="ffn_half_step",
    )(*xs, g, w1, w3, w2, gf)


def _rope(x, c, s_lo, s_hi):
    w = x.shape[1]
    rep = w // LANES
    if rep > 1:
        c = jnp.concatenate([c] * rep, axis=1)
        s_lo = jnp.concatenate([s_lo] * rep, axis=1)
        s_hi = jnp.concatenate([s_hi] * rep, axis=1)
    half = ROPE_DIMS // 2
    return x * c + pltpu.roll(x, half, 1) * s_hi + pltpu.roll(x, w - half, 1) * s_lo


def _inproj_body(x_ref, g_ref, w_ref, c_ref, slo_ref, shi_ref,
                 gates_ref, z_ref, qc_ref, qi_ref, kc_ref, vc_ref, ki_ref, kcb_ref, vcb_ref, kib_ref, vt_ref):
    u = _rmsnorm(x_ref[...], g_ref[...]).astype(BF16)
    zf = _dot(u, w_ref[...])
    gates_ref[...] = jax.nn.sigmoid(zf[:, :GATE_UNITS * LANES]).astype(BF16)
    z_ref[...] = zf[:, GATE_UNITS * LANES:(GATE_UNITS + Z_UNITS) * LANES]

    unit = lambda first, count: zf[:, first * LANES:(first + count) * LANES]
    c, s_lo, s_hi = c_ref[...], slo_ref[...], shi_ref[...]
    qc_ref[...] = (_rope(unit(Z_CQ, 4), c, s_lo, s_hi) * (C_HDIM ** -0.5)).astype(BF16)
    qi_ref[...] = _rope(unit(Z_IQ, 4), c, s_lo, s_hi).astype(BF16)
    kc = _rope(unit(Z_CK, 1), c, s_lo, s_hi)
    vc = unit(Z_CV, 1)
    ki = _rope(unit(Z_IK, 1), c, s_lo, s_hi)[:, :IDX_DIM]
    kc_ref[...] = kc
    vc_ref[...] = vc
    ki_ref[...] = ki
    kcb_ref[...] = kc.astype(BF16)
    vcb_ref[...] = vc.astype(BF16)
    kib_ref[...] = ki.astype(BF16)
    ones = jnp.ones((VT_ROWS - C_HDIM, KEY_BLOCK), BF16)
    for kk in range(vt_ref.shape[0]):
        vt = vc[kk * KEY_BLOCK:(kk + 1) * KEY_BLOCK, :].T.astype(BF16)
        vt_ref[kk] = jnp.concatenate([vt[:C_HDIM], ones, vt[C_HDIM:], ones], axis=0)


def _inproj(h, g, w, tables, table_tile):
    n, d = h.shape
    tm = PROJ_TILE
    assert n % tm == 0
    row = lambda i: (i, 0)
    return pl.pallas_call(
        _inproj_body,
        grid=(n // tm,),
        in_specs=[
            pl.BlockSpec((tm, d), row),
            pl.BlockSpec((1, d), lambda i: (0, 0)),
            pl.BlockSpec((d, Z_WIDTH), lambda i: (0, 0), pipeline_mode=pl.Buffered(1)),
            pl.BlockSpec((tm, LANES), table_tile),
            pl.BlockSpec((tm, LANES), table_tile),
            pl.BlockSpec((tm, LANES), table_tile),
        ],
        out_specs=[
            pl.BlockSpec((tm, GATE_UNITS * LANES), row),
            pl.BlockSpec((tm, Z_UNITS * LANES), row),
            pl.BlockSpec((tm, C_WIDTH), row),
            pl.BlockSpec((tm, IDX_HEADS * IDX_DIM), row),
            pl.BlockSpec((tm, LANES), row),
            pl.BlockSpec((tm, LANES), row),
            pl.BlockSpec((tm, IDX_DIM), row),
            pl.BlockSpec((tm, LANES), row),
            pl.BlockSpec((tm, LANES), row),
            pl.BlockSpec((tm, IDX_DIM), row),
            pl.BlockSpec((tm // KEY_BLOCK, C_KV_HEADS * VT_ROWS, KEY_BLOCK), lambda i: (i, 0, 0)),
        ],
        out_shape=[
            jax.ShapeDtypeStruct((n, GATE_UNITS * LANES), BF16),
            jax.ShapeDtypeStruct((n, Z_UNITS * LANES), F32),
            jax.ShapeDtypeStruct((n, C_WIDTH), BF16),
            jax.ShapeDtypeStruct((n, IDX_HEADS * IDX_DIM), BF16),
            jax.ShapeDtypeStruct((n, LANES), F32),
            jax.ShapeDtypeStruct((n, LANES), F32),
            jax.ShapeDtypeStruct((n, IDX_DIM), F32),
            jax.ShapeDtypeStruct((n, LANES), BF16),
            jax.ShapeDtypeStruct((n, LANES), BF16),
            jax.ShapeDtypeStruct((n, IDX_DIM), BF16),
            jax.ShapeDtypeStruct((n // KEY_BLOCK, C_KV_HEADS * VT_ROWS, KEY_BLOCK), BF16),
        ],
        compiler_params=_params(("parallel",), 48),
        name="in_projection",
    )(h, g, w, *tables)


def _gla_head(q, k, v, logf, st_ref, h, tc):
    row = lax.broadcasted_iota(jnp.int32, (tc, tc), 0)
    col = lax.broadcasted_iota(jnp.int32, (tc, tc), 1)
    tril = (col <= row).astype(BF16)
    cum = _dot_exact_lhs(tril, logf)

    tile = 2 * PAIR_BLOCK
    nt = tc // tile
    b3 = cum.reshape(nt, tile, HEAD_W)
    q3 = q.reshape(nt, tile, HEAD_W)
    k3 = k.reshape(nt, tile, HEAD_W)
    v3 = v.reshape(nt, tile, HEAD_W)
    srow = lax.broadcasted_iota(jnp.int32, (nt, tile, 1), 1)
    low = srow < PAIR_BLOCK
    tloc = srow & (PAIR_BLOCK - 1)

    def block_row(x3, j):
        return jnp.where(low, x3[:, j:j + 1, :], x3[:, PAIR_BLOCK + j:PAIR_BLOCK + j + 1, :])

    o3 = jnp.zeros((nt, tile, HEAD_W), F32)
    for j in range(PAIR_BLOCK):
        causal = tloc >= j
        decay = jnp.exp(jnp.where(causal, b3 - block_row(b3, j), 0.0))
        w = jnp.sum(q3 * block_row(k3, j) * decay, axis=-1, keepdims=True)
        o3 = o3 + jnp.where(causal, w, 0.0) * block_row(v3, j)
    o = o3.reshape(tc, HEAD_W)

    vb = v.astype(BF16)
    attn = jnp.zeros((tc, tc), F32)
    half = tc // 2
    while half >= PAIR_BLOCK:
        blk = 2 * half
        nblk = tc // blk
        bl = cum.reshape(nblk, blk, HEAD_W)
        x = bl - bl[:, half - 1:half, :]
        second = lax.broadcasted_iota(jnp.int32, (nblk, blk, 1), 1) >= half
        e = jnp.exp(jnp.where(second, x, -x))
        qt = jnp.where(second, q.reshape(nblk, blk, HEAD_W) * e, 0.0).reshape(tc, HEAD_W).astype(BF16)
        kt = jnp.where(second, 0.0, k.reshape(nblk, blk, HEAD_W) * e).reshape(tc, HEAD_W).astype(BF16)
        shift = blk.bit_length() - 1
        same = (row >> shift) == (col >> shift)
        attn = attn + jnp.where(same, _dot_nt(qt, kt), 0.0)
        half //= 2
    o = o + _dot(attn.astype(BF16), vb)

    st = st_ref[h]
    o = o + _dot_nt((q * jnp.exp(cum)).astype(BF16), st.astype(BF16))
    last = cum[tc - 1:tc, :]
    kd = (k * jnp.exp(last - cum)).astype(BF16)
    st_ref[h] = st * jnp.exp(last) + _dot_tn(vb, kd)
    return o


def _gla_finish(o, nw, gate):
    return _rmsnorm(o, nw) * (gate * jax.nn.sigmoid(gate))


def _gla_state_io(c, s0_ref, st_ref, heads):
    @pl.when(c == 0)
    def _():
        for h in range(heads):
            st_ref[h] = s0_ref[0, h].T


def _gla_state_out(c, sout_ref, st_ref, heads):
    @pl.when(c == pl.num_programs(1) - 1)
    def _():
        for h in range(heads):
            sout_ref[0, h] = st_ref[h].T


def _hgrn_body(q_ref, f_ref, v_ref, g_ref, lb_ref, nw_ref, s0_ref, y_ref, sout_ref, st_ref, *, tc):
    c = pl.program_id(1)
    _gla_state_io(c, s0_ref, st_ref, A_HEADS)
    for h in range(A_HEADS):
        hs = slice(h * HEAD_W, (h + 1) * HEAD_W)
        zf = f_ref[:, hs]
        lb = lb_ref[:, hs]
        f = lb + (1.0 - lb) * jax.nn.sigmoid(zf)
        logf = jnp.log(jnp.maximum(f, F_MIN))
        k = (1.0 - lb) * jax.nn.sigmoid(-zf)
        zq = q_ref[:, hs]
        q = zq * jax.nn.sigmoid(zq) * (A_KDIM ** -0.5)
        o = _gla_head(q, k, v_ref[:, hs], logf, st_ref, h, tc)
        y_ref[:, hs] = _gla_finish(o, nw_ref[...], g_ref[:, hs]).astype(BF16)
    _gla_state_out(c, sout_ref, st_ref, A_HEADS)


def _gla_body(q_ref, k_ref, v_ref, g_ref, r_ref, wup_ref, bup_ref, nw_ref, s0_ref,
              y_ref, sout_ref, st_ref, *, tc):
    c = pl.program_id(1)
    _gla_state_io(c, s0_ref, st_ref, B_HEADS)
    r = _dot_hi(r_ref[...], wup_ref[...]) + bup_ref[...]
    logf_all = (jnp.minimum(r, 0.0) - jnp.log1p(jnp.exp(-jnp.abs(r)))) / B_TAU
    for h in range(B_HEADS):
        hs = slice(h * HEAD_W, (h + 1) * HEAD_W)
        q = q_ref[:, hs] * (B_KDIM ** -0.5)
        o = _gla_head(q, k_ref[:, hs], v_ref[:, hs], logf_all[:, hs], st_ref, h, tc)
        y_ref[:, hs] = _gla_finish(o, nw_ref[...], g_ref[:, hs]).astype(BF16)
    _gla_state_out(c, sout_ref, st_ref, B_HEADS)


def _recurrent_mixer(mode, z, row0, nseq, t, s0, nw, extra):
    tc = min(128, t)
    nc = t // tc
    rb0 = row0 // tc
    zcol = (Z_A if mode == "hgrn" else Z_B) // 4

    def zspec(k):
        return pl.BlockSpec((tc, MIX_W), lambda b, c: (rb0 + b * nc + c, zcol + k))

    const = lambda b, c: (0, 0)
    state_spec = pl.BlockSpec((1, 4, HEAD_W, HEAD_W), lambda b, c: (b, 0, 0, 0))
    if mode == "hgrn":
        body = functools.partial(_hgrn_body, tc=tc)
        in_specs = [zspec(0), zspec(1), zspec(2), zspec(3),
                    pl.BlockSpec((1, MIX_W), const), pl.BlockSpec((1, HEAD_W), const), state_spec]
        args = (z, z, z, z, extra[0], nw, s0)
    else:
        body = functools.partial(_gla_body, tc=tc)
        in_specs = [zspec(0), zspec(1), zspec(2), zspec(3),
                    pl.BlockSpec((tc, LANES), lambda b, c: (rb0 + b * nc + c, Z_SMALL)),
                    pl.BlockSpec((LANES, MIX_W), const), pl.BlockSpec((1, MIX_W), const),
                    pl.BlockSpec((1, HEAD_W), const), state_spec]
        args = (z, z, z, z, z, extra[0], extra[1], nw, s0)
    return pl.pallas_call(
        body,
        grid=(nseq, nc),
        in_specs=in_specs,
        out_specs=[pl.BlockSpec((tc, MIX_W), lambda b, c: (b * nc + c, 0)), state_spec],
        out_shape=[jax.ShapeDtypeStruct((nseq * t, MIX_W), BF16),
                   jax.ShapeDtypeStruct((nseq, 4, HEAD_W, HEAD_W), F32)],
        scratch_shapes=[pltpu.VMEM((4, HEAD_W, HEAD_W), F32)],
        compiler_params=_params(("parallel", "arbitrary"), 32),
        name=mode + "_mixer",
    )(*args)


SUPER = 4
ATTN_BLOCKS = 2


def _dsa_t_body(qc_ref, qi_ref, sm_ref, kc_ref, vt_ref, ki_ref, o_ref,
                sk_ref, qit_ref, qct_ref, wrow_ref, acc_ref, ot_ref, s0_ref, s1_ref, p_ref,
                *, tq, past, nkeys, topk):
    i = pl.program_id(1)
    qpos0 = past + i * tq
    last_chunk = (qpos0 + tq - 1) // CHUNK
    n_adm = jnp.minimum((last_chunk + 1) * CHUNK, nkeys)
    nkb = (n_adm + KEY_BLOCK - 1) // KEY_BLOCK
    nsb = (nkb + SUPER - 1) // SUPER
    grp = C_HEADS // C_KV_HEADS

    qit = qi_ref[...].astype(F32).T
    for j in range(IDX_HEADS):
        qit_ref[:, j * tq:(j + 1) * tq] = qit[j * IDX_DIM:(j + 1) * IDX_DIM, :].astype(BF16)
    qct = qc_ref[...].astype(F32).T
    zeros = jnp.zeros((C_HDIM, tq), BF16)
    for hq in range(C_HEADS):
        g, r = divmod(hq, grp)
        for gg in range(C_KV_HEADS):
            val = qct[hq * C_HDIM:(hq + 1) * C_HDIM, :].astype(BF16) if gg == g else zeros
            qct_ref[g, gg * C_HDIM:(gg + 1) * C_HDIM, r * tq:(r + 1) * tq] = val
    wrow_ref[...] = sm_ref[...].T

    qchunk = (qpos0 + lax.broadcasted_iota(jnp.int32, (1, tq), 1)) >> 6
    ksub = lax.broadcasted_iota(jnp.int32, (KEY_BLOCK, 1), 0)

    def admissible(kb):
        kpos = kb * KEY_BLOCK + ksub
        return ((kpos >> 6) <= qchunk) & (kpos < nkeys)

    def score_step(kk, carry):
        for u in range(SUPER):
            kb = kk * SUPER + u
            kib = ki_ref[0, pl.ds(pl.multiple_of(kb * KEY_BLOCK, KEY_BLOCK), KEY_BLOCK), :]
            d = _dot(kib, qit_ref[...])
            sc = jnp.maximum(d[:, 0:tq], 0.0) * wrow_ref[SMALL_IW:SMALL_IW + 1, :]
            for j in range(1, IDX_HEADS):
                sc = sc + jnp.maximum(d[:, j * tq:(j + 1) * tq], 0.0) * wrow_ref[SMALL_IW + j:SMALL_IW + j + 1, :]
            bits = pltpu.bitcast(sc * IDX_SCALE + 0.0, jnp.int32)
            key = jnp.where(admissible(kb), jnp.where(bits < 0, bits ^ 0x7FFFFFFF, bits), INT_MIN)
            sk_ref[kb] = key
        return carry

    lax.fori_loop(0, nsb, score_step, 0)

    def fold(hit):
        return jnp.sum(hit.reshape(KEY_BLOCK // 8, 8, tq), axis=0)

    def count(pred):
        def step(kk, acc):
            for u in range(SUPER):
                kb = kk * SUPER + u
                acc = acc + fold(jnp.where(pred(sk_ref[kb], kb), 1.0, 0.0))
            return acc
        acc = lax.fori_loop(0, nsb, step, jnp.zeros((8, tq), F32))
        return jnp.sum(acc, axis=0, keepdims=True)

    kf = float(topk)
    c0 = count(lambda s, kb: s >= 0)
    t0 = jnp.where(c0 >= kf, 0, INT_MIN).astype(jnp.int32)

    def bit_step(it, t):
        cand = t | jnp.left_shift(jnp.int32(1), 30 - it)
        return jnp.where(count(lambda s, kb: s >= cand) >= kf, cand, t)
    thr = lax.fori_loop(0, 31, bit_step, t0)
    thr = jnp.maximum(thr, INT_MIN + 1)

    c_ge = count(lambda s, kb: s >= thr)
    nbits = max(1, (nkeys - 1).bit_length())

    @pl.when(jnp.max(c_ge) > kf)
    def _():
        need = kf - count(lambda s, kb: s > thr)

        def cut_step(it, cut):
            cand = cut | jnp.left_shift(jnp.int32(1), nbits - 1 - it)
            c = count(lambda s, kb: (s == thr) & (kb * KEY_BLOCK + ksub < cand))
            return jnp.where(c < need, cand, cut)
        cut = lax.fori_loop(0, nbits, cut_step, jnp.zeros((1, tq), jnp.int32))

        def strike(kb, carry):
            key = sk_ref[kb]
            sk_ref[kb] = jnp.where((key == thr) & (kb * KEY_BLOCK + ksub > cut), INT_MIN, key)
            return carry
        lax.fori_loop(0, nsb * SUPER, strike, 0)

    acc_ref[...] = jnp.zeros(acc_ref.shape, F32)
    init = tuple(jnp.full((1, grp * tq), NEG, F32) for _ in range(C_KV_HEADS))

    span = ATTN_BLOCKS * KEY_BLOCK
    n_attn = nsb * (SUPER // ATTN_BLOCKS)

    s_bufs = (s0_ref, s1_ref)

    def qk_scores(kk, half):
        kblk = kc_ref[0, pl.ds(pl.multiple_of(kk * span, span), span), :]
        bias = jnp.concatenate(
            [jnp.where(sk_ref[kk * ATTN_BLOCKS + u] >= thr, 0.0, NEG) for u in range(ATTN_BLOCKS)], axis=0)
        bias = jnp.concatenate([bias] * grp, axis=1)
        for g in range(C_KV_HEADS):
            s = _dot(kblk, qct_ref[g]) + bias
            for r in range(grp):
                s_bufs[half][g, r] = s[:, r * tq:(r + 1) * tq]

    def softmax_pv(kk, half, m):
        m = list(m)
        for g in range(C_KV_HEADS):
            m_cols, alpha_cols = [], []
            for r in range(grp):
                cols = slice(r * tq, (r + 1) * tq)
                sb = s_bufs[half][g, r]
                m_old = m[g][:, cols]
                m_new = jnp.maximum(m_old, jnp.max(sb, axis=0, keepdims=True))
                p_ref[half, g, r] = jnp.exp(sb - m_new).astype(BF16)
                m_cols.append(m_new)
                alpha_cols.append(jnp.exp(m_old - m_new))
            rows = slice(g * VT_ROWS, (g + 1) * VT_ROWS)
            pv = None
            for u in range(ATTN_BLOCKS):
                keys_u = slice(u * KEY_BLOCK, (u + 1) * KEY_BLOCK)
                p_u = jnp.concatenate([p_ref[half, g, r, keys_u, :] for r in range(grp)], axis=1)
                d = _dot(vt_ref[0, kk * ATTN_BLOCKS + u, rows, :], p_u)
                pv = d if pv is None else pv + d
            for r in range(grp):
                acc_ref[g, r] = alpha_cols[r] * acc_ref[g, r] + pv[:, r * tq:(r + 1) * tq]
            m[g] = jnp.concatenate(m_cols, axis=1)
        return tuple(m)

    qk_scores(0, 0)

    def pair_step(j, m):
        qk_scores(2 * j + 1, 1)
        m = softmax_pv(2 * j, 0, m)
        qk_scores(jnp.minimum(2 * j + 2, n_attn - 2), 0)
        return softmax_pv(2 * j + 1, 1, m)

    lax.fori_loop(0, n_attn // 2, pair_step, init)
    for hq in range(C_HEADS):
        g, r = divmod(hq, grp)
        ot_ref[hq * C_HDIM:(hq + 1) * C_HDIM, :] = acc_ref[g, r, 0:C_HDIM, :] / acc_ref[g, r, C_HDIM:C_HDIM + 1, :]
    o_ref[...] = ot_ref[...].T.astype(BF16)


def _dsa_t(qc, qi, small, small_col, keys, vals_t, kidx, nseq, t, past, nkeys):
    tq = KEY_BLOCK
    nq = t // tq
    lp = keys.shape[1]
    assert t % tq == 0 and lp % (SUPER * KEY_BLOCK) == 0
    topk = min(TOPK_MAX, nkeys // 4)
    grp = C_HEADS // C_KV_HEADS
    qrow = lambda b, i: (b * nq + i, 0)
    return pl.pallas_call(
        functools.partial(_dsa_t_body, tq=tq, past=past, nkeys=nkeys, topk=topk),
        grid=(nseq, nq),
        in_specs=[
            pl.BlockSpec((tq, C_WIDTH), qrow),
            pl.BlockSpec((tq, IDX_HEADS * IDX_DIM), qrow),
            pl.BlockSpec((tq, LANES), lambda b, i: (b * nq + i, small_col)),
            pl.BlockSpec((1, lp, LANES), lambda b, i: (b, 0, 0)),
            pl.BlockSpec((1, lp // KEY_BLOCK, C_KV_HEADS * VT_ROWS, KEY_BLOCK), lambda b, i: (b, 0, 0, 0)),
            pl.BlockSpec((1, lp, IDX_DIM), lambda b, i: (b, 0, 0)),
        ],
        out_specs=pl.BlockSpec((tq, C_WIDTH), lambda b, i: (b * nq + i, 0)),
        out_shape=jax.ShapeDtypeStruct((nseq * t, C_WIDTH), BF16),
        scratch_shapes=[
            pltpu.VMEM((lp // KEY_BLOCK, KEY_BLOCK, tq), jnp.int32),
            pltpu.VMEM((IDX_DIM, IDX_HEADS * tq), BF16),
            pltpu.VMEM((C_KV_HEADS, C_KV_HEADS * C_HDIM, grp * tq), BF16),
            pltpu.VMEM((LANES, tq), F32),
            pltpu.VMEM((C_KV_HEADS, grp, VT_ROWS, tq), F32),
            pltpu.VMEM((C_WIDTH, tq), F32),
            pltpu.VMEM((C_KV_HEADS, grp, ATTN_BLOCKS * KEY_BLOCK, tq), F32),
            pltpu.VMEM((C_KV_HEADS, grp, ATTN_BLOCKS * KEY_BLOCK, tq), F32),
            pltpu.VMEM((2, C_KV_HEADS, grp, ATTN_BLOCKS * KEY_BLOCK, tq), BF16),
        ],
        compiler_params=_params(("parallel", "arbitrary"), 48),
        name="dsa_mixer_t",
    )(qc, qi, small, keys, vals_t, kidx)


def _merge_body(h_ref, yap_ref, yas_ref, ybp_ref, ybs_ref, ocp_ref, ocs_ref, ga_ref, gb_ref, gc_ref,
                wa_ref, wb_ref, wc_ref, wo_ref, o_ref, *, prompt_tiles):
    is_prompt = pl.program_id(0) < prompt_tiles
    pick = lambda p_ref, s_ref: jnp.where(is_prompt, p_ref[...], s_ref[...])
    merged = (ga_ref[...].astype(F32) * _dot(pick(yap_ref, yas_ref), wa_ref[...])
              + gb_ref[...].astype(F32) * _dot(pick(ybp_ref, ybs_ref), wb_ref[...])
              + gc_ref[...].astype(F32) * _dot(pick(ocp_ref, ocs_ref), wc_ref[...]))
    o_ref[...] = h_ref[...] + _dot(merged.astype(BF16), wo_ref[...])


def _merge(h, ya, yb, oc, gates, wa, wb, wc, wo):
    n, d = h.shape
    tm = TOKEN_TILE
    pt = ya[0].shape[0] // tm
    assert ya[0].shape[0] % tm == 0 and ya[1].shape[0] % tm == 0 and n == ya[0].shape[0] + ya[1].shape[0]
    row = lambda i: (i, 0)
    const = lambda i: (0, 0)
    prow = lambda i: (jnp.minimum(i, pt - 1), 0)
    srow = lambda i: (jnp.maximum(i - pt, 0), 0)
    pair = [pl.BlockSpec((tm, MIX_W), prow), pl.BlockSpec((tm, MIX_W), srow)]
    return pl.pallas_call(
        functools.partial(_merge_body, prompt_tiles=pt),
        grid=(n // tm,),
        in_specs=[
            pl.BlockSpec((tm, d), row),
            *pair, *pair, *pair,
            pl.BlockSpec((tm, d), lambda i: (i, 0)),
            pl.BlockSpec((tm, d), lambda i: (i, 1)),
            pl.BlockSpec((tm, d), lambda i: (i, 2)),
            pl.BlockSpec((MIX_W, d), const),
            pl.BlockSpec((MIX_W, d), const),
            pl.BlockSpec((MIX_W, d), const),
            pl.BlockSpec((d, d), const),
        ],
        out_specs=pl.BlockSpec((tm, d), row),
        out_shape=jax.ShapeDtypeStruct((n, d), F32),
        compiler_params=_params(("parallel",), 48),
        name="gated_merge",
    )(h, *ya, *yb, *oc, gates, gates, gates, wa, wb, wc, wo)


def _pad_heads(w, heads, dim):
    lead = w.shape[:-1]
    w = w.reshape(*lead, heads, dim)
    w = jnp.pad(w, [(0, 0)] * len(lead) + [(0, 0), (0, HEAD_W - dim)])
    return w.reshape(*lead, heads * HEAD_W)


def _layout_w_in(w):
    widths = (512, 512, 512, 512, 256, 256, 512, 16, 512, 512, 128, 128, 512, 64, 8, 1024, 1024, 1024)
    w = w.astype(BF16)
    parts, o = [], 0
    for wd in widths:
        parts.append(w[:, o:o + wd])
        o += wd
    (a_q, a_f, a_i, a_g, b_q, b_k, b_v, b_r, b_g, c_q, c_k, c_v, i_q, i_k, i_w, g_a, g_b, g_c) = parts
    d = w.shape[0]
    small = jnp.concatenate([b_r, i_w, jnp.zeros((d, LANES - 24), w.dtype)], axis=1)
    i_k = jnp.pad(i_k, ((0, 0), (0, LANES - IDX_DIM)))
    cols = [g_a, g_b, g_c, a_q, a_f, a_i, a_g,
            _pad_heads(b_q, B_HEADS, B_KDIM), _pad_heads(b_k, B_HEADS, B_KDIM), b_v, b_g,
            small, c_q, i_q, c_k, c_v, i_k]
    out = jnp.concatenate(cols, axis=1)
    assert out.shape[1] == Z_WIDTH
    return out


def _rope_tables(pos):
    half = ROPE_DIMS // 2
    inv = ROPE_THETA ** (-jnp.arange(half, dtype=F32) * (2.0 / ROPE_DIMS))
    ang = pos.astype(F32)[:, None] * inv[None, :]
    cos, sin = jnp.cos(ang), jnp.sin(ang)
    n = pos.shape[0]
    ones = jnp.ones((n, C_HDIM - ROPE_DIMS), F32)
    zeros = jnp.zeros((n, C_HDIM - ROPE_DIMS), F32)
    zh = jnp.zeros((n, half), F32)
    c = jnp.concatenate([cos, cos, ones], axis=1)
    s_lo = jnp.concatenate([-sin, zh, zeros], axis=1)
    s_hi = jnp.concatenate([zh, sin, zeros], axis=1)
    two = lambda a: jnp.concatenate([a, a], axis=1)
    return two(c), two(s_lo), two(s_hi)


def kernel(x_prompt, x_sample, state_hgrn, state_gla, cache_k, cache_v, cache_kidx, hgrn_lb, w_in, w_gla_up, b_gla, norm_hgrn, norm_gla, w_br_a, w_br_b, w_br_c, w_out, norm_ffn1, norm_mix, norm_ffn2, ffn1_w1, ffn1_w3, ffn1_w2, ffn2_w1, ffn2_w3, ffn2_w2, norm_final):
    bp, tp, d = x_prompt.shape
    bs, ts, _ = x_sample.shape
    past = cache_k.shape[2]
    n_p, n_s = bp * tp, bs * ts

    lb_sm = jax.nn.softmax(hgrn_lb.astype(F32), axis=0)
    lb_all = jnp.concatenate([jnp.zeros_like(lb_sm[:1]), jnp.cumsum(lb_sm[1:], axis=0)], axis=0)

    pos = jnp.concatenate([jnp.arange(tp, dtype=jnp.int32),
                           jnp.tile(past + jnp.arange(ts, dtype=jnp.int32), PROJ_TILE // ts)])
    tables = _rope_tables(pos)
    assert tp % PROJ_TILE == 0 and PROJ_TILE % ts == 0 and n_s % PROJ_TILE == 0
    table_tile = lambda i: (jnp.where(i < n_p // PROJ_TILE, i % (tp // PROJ_TILE), tp // PROJ_TILE), 0)

    hs = (x_prompt.reshape(n_p, d), x_sample.reshape(n_s, d))
    row2 = lambda a: a.reshape(1, -1)
    zero_state = jnp.zeros((bp, 4, HEAD_W, HEAD_W), F32)
    span_s = SUPER * KEY_BLOCK
    key_pad = -(-(past + ts) // span_s) * span_s - past - ts
    q_rep = KEY_BLOCK // ts

    outs = {k: [] for k in ("pa", "pb", "pk", "pv", "pki", "sa", "sb", "sk", "sv", "ski")}
    for l in range(DEPTH):
        bf = lambda a: a[l].astype(BF16)
        h = _ffn(hs, row2(norm_ffn1[l]), bf(ffn1_w1), bf(ffn1_w3), bf(ffn1_w2), row2(norm_final), False, 0)
        gates, z, qc, qi, kc, vc, ki, kcb, vcb, kib, vt = _inproj(
            h, row2(norm_mix[l]), _layout_w_in(w_in[l]), tables, table_tile)

        lb = row2(lb_all[l])
        nwa, nwb = row2(norm_hgrn[l]), row2(norm_gla[l])
        wup = jnp.pad(_pad_heads(w_gla_up[l], B_HEADS, B_KDIM), ((0, LANES - B_GATE_RANK), (0, 0)))
        bup = row2(_pad_heads(b_gla[l], B_HEADS, B_KDIM))
        sb0 = jnp.pad(state_gla[l], ((0, 0), (0, 0), (0, HEAD_W - B_KDIM), (0, 0)))

        ya_p, sa_p = _recurrent_mixer("hgrn", z, 0, bp, tp, zero_state, nwa, (lb,))
        ya_s, sa_s = _recurrent_mixer("hgrn", z, n_p, bs, ts, state_hgrn[l], nwa, (lb,))
        yb_p, sb_p = _recurrent_mixer("gla", z, 0, bp, tp, zero_state, nwb, (wup, bup))
        yb_s, sb_s = _recurrent_mixer("gla", z, n_p, bs, ts, sb0, nwb, (wup, bup))

        seqs = lambda a, n, t: a.reshape(n, t, a.shape[-1])
        vt_p = vt[:n_p // KEY_BLOCK].reshape(bp, tp // KEY_BLOCK, C_KV_HEADS * VT_ROWS, KEY_BLOCK)
        oc_p = _dsa_t(qc, qi, z, Z_SMALL, seqs(kcb[:n_p], bp, tp), vt_p, seqs(kib[:n_p], bp, tp), bp, tp, 0, tp)

        def with_cache(cache, new):
            full = jnp.concatenate([cache.reshape(bs, past, -1).astype(BF16), seqs(new[n_p:], bs, ts)], axis=1)
            return jnp.pad(full, ((0, 0), (0, key_pad), (0, 0)))

        def fill_block(a):
            return jnp.tile(seqs(a[n_p:], bs, ts), (1, q_rep, 1)).reshape(bs * KEY_BLOCK, a.shape[-1])

        vals_s = with_cache(cache_v[l], vcb)
        vt_s = jnp.swapaxes(vals_s.reshape(bs, -1, KEY_BLOCK, LANES), 2, 3)
        ones = jnp.ones(vt_s.shape[:2] + (VT_ROWS - C_HDIM, KEY_BLOCK), BF16)
        vt_s = jnp.concatenate([vt_s[:, :, :C_HDIM], ones, vt_s[:, :, C_HDIM:], ones], axis=2)
        small_s = fill_block(z[:, Z_SMALL * LANES:(Z_SMALL + 1) * LANES])
        oc_s = _dsa_t(fill_block(qc), fill_block(qi), small_s, 0, with_cache(cache_k[l], kcb), vt_s,
                      with_cache(cache_kidx[l], kib), bs, KEY_BLOCK, past, past + ts)
        oc_s = oc_s.reshape(bs, KEY_BLOCK, C_WIDTH)[:, :ts].reshape(n_s, C_WIDTH)

        h = _merge(h, (ya_p, ya_s), (yb_p, yb_s), (oc_p, oc_s), gates,
                   bf(w_br_a), bf(w_br_b), bf(w_br_c), bf(w_out))
        last = l == DEPTH - 1
        hs = _ffn((h,), row2(norm_ffn2[l]), bf(ffn2_w1), bf(ffn2_w3), bf(ffn2_w2), row2(norm_final),
                  last, n_p if last else 0)
        hs = hs if last else (hs,)

        outs["pa"].append(sa_p)
        outs["sa"].append(sa_s)
        outs["pb"].append(sb_p[:, :, :B_KDIM, :])
        outs["sb"].append(sb_s[:, :, :B_KDIM, :])
        outs["pk"].append(kc[:n_p].reshape(bp, tp, C_KV_HEADS, C_HDIM))
        outs["pv"].append(vc[:n_p].reshape(bp, tp, C_KV_HEADS, C_HDIM))
        outs["pki"].append(ki[:n_p].reshape(bp, tp, IDX_DIM))
        outs["sk"].append(kc[n_p:].reshape(bs, ts, C_KV_HEADS, C_HDIM))
        outs["sv"].append(vc[n_p:].reshape(bs, ts, C_KV_HEADS, C_HDIM))
        outs["ski"].append(ki[n_p:].reshape(bs, ts, IDX_DIM))

    st = {k: jnp.stack(v) for k, v in outs.items()}
    return (hs[0].reshape(bp, tp, d), hs[1].reshape(bs, ts, d),
            st["pa"], st["pb"], st["pk"], st["pv"], st["pki"],
            st["sa"], st["sb"], st["sk"], st["sv"], st["ski"])
```

```python
import functools

import jax
import jax.numpy as jnp
from jax import lax
from jax.experimental import pallas as pl
from jax.experimental.pallas import tpu as pltpu

F32 = jnp.float32
BF16 = jnp.bfloat16

D_MODEL = 1024
DEPTH = 2
CHUNK = 64
EPS = 1e-6
NEG = -1e30
F_MIN = 1e-30
A_HEADS = 4
A_KDIM = 128
A_VDIM = 128
B_HEADS = 4
B_KDIM = 64
B_VDIM = 128
B_GATE_RANK = 16
B_TAU = 16.0
C_HEADS = 8
C_KV_HEADS = 2
C_HDIM = 64
C_WIDTH = C_HEADS * C_HDIM
IDX_HEADS = 8
IDX_DIM = 64
IDX_SCALE = (IDX_HEADS * IDX_DIM) ** -0.5
TOPK_MAX = 256
ROPE_THETA = 500000.0
ROPE_DIMS = C_HDIM // 4
PAIR_BLOCK = 4
D_FF = 2816

LANES = 128
HEAD_W = 128
MIX_W = 4 * HEAD_W
TOKEN_TILE = 512
PROJ_TILE = 256
KEY_BLOCK = 128
VT_ROWS = 80
INT_MIN = -(2 ** 31)

GATE_UNITS = 24
Z_A = 0
Z_B = 16
Z_SMALL = 32
Z_UNITS = 33
Z_CQ = GATE_UNITS + Z_UNITS
Z_IQ = Z_CQ + 4
Z_CK = Z_IQ + 4
Z_CV = Z_CK + 1
Z_IK = Z_CV + 1
Z_WIDTH = (Z_IK + 1) * LANES
SMALL_IW = B_GATE_RANK


def _params(sem, vmem_mb):
    return pltpu.CompilerParams(dimension_semantics=sem, vmem_limit_bytes=vmem_mb << 20)


def _dot(a, b):
    return jnp.dot(a, b, preferred_element_type=F32)


def _dot_nt(a, b):
    return lax.dot_general(a, b, (((1,), (1,)), ((), ())), preferred_element_type=F32)


def _dot_tn(a, b):
    return lax.dot_general(a, b, (((0,), (0,)), ((), ())), preferred_element_type=F32)


def _split3(x):
    h1 = x.astype(BF16)
    r1 = x - h1.astype(F32)
    h2 = r1.astype(BF16)
    h3 = (r1 - h2.astype(F32)).astype(BF16)
    return h1, h2, h3


def _dot_exact_lhs(m, x):
    h1, h2, h3 = _split3(x)
    return _dot(m, h1) + _dot(m, h2) + _dot(m, h3)


def _dot_hi(a, b):
    a1 = a.astype(BF16)
    a2 = (a - a1.astype(F32)).astype(BF16)
    b1 = b.astype(BF16)
    b2 = (b - b1.astype(F32)).astype(BF16)
    return _dot(a1, b1) + _dot(a1, b2) + _dot(a2, b1)


def _rmsnorm(x, g):
    return x * lax.rsqrt(jnp.mean(x * x, axis=-1, keepdims=True) + EPS) * g


def _ffn_body(*refs, n_in, n_out, prompt_tiles, final_norm):
    x_refs, (g_ref, w1_ref, w3_ref, w2_ref, gf_ref), o_refs = refs[:n_in], refs[n_in:n_in + 5], refs[n_in + 5:]
    is_prompt = pl.program_id(0) < prompt_tiles
    x = x_refs[0][...] if n_in == 1 else jnp.where(is_prompt, x_refs[0][...], x_refs[1][...])
    u = _rmsnorm(x, g_ref[...]).astype(BF16)
    a = _dot(u, w1_ref[...])
    b = _dot(u, w3_ref[...])
    hid = (a * jax.nn.sigmoid(a) * b).astype(BF16)
    out = x + 0.5 * _dot(hid, w2_ref[...])
    if final_norm:
        out = _rmsnorm(out, gf_ref[...])
    if n_out == 1:
        o_refs[0][...] = out
    else:
        @pl.when(is_prompt)
        def _():
            o_refs[0][...] = out

        @pl.when(jnp.logical_not(is_prompt))
        def _():
            o_refs[1][...] = out


def _ffn(xs, g, w1, w3, w2, gf, final_norm, split_out):
    d = xs[0].shape[1]
    tm = TOKEN_TILE
    sizes = [x.shape[0] for x in xs]
    n = sum(sizes)
    assert all(sz % tm == 0 for sz in sizes)
    pt = sizes[0] // tm if len(xs) == 2 else split_out // tm
    prow = lambda i: (jnp.minimum(i, pt - 1), 0)
    srow = lambda i: (jnp.maximum(i - pt, 0), 0)
    row = lambda i: (i, 0)
    const = lambda i: (0, 0)
    resident = pl.Buffered(1)
    x_specs = [pl.BlockSpec((tm, d), row)] if len(xs) == 1 else \
        [pl.BlockSpec((tm, d), prow), pl.BlockSpec((tm, d), srow)]
    if split_out:
        out_specs = [pl.BlockSpec((tm, d), prow), pl.BlockSpec((tm, d), srow)]
        out_shape = [jax.ShapeDtypeStruct((split_out, d), F32), jax.ShapeDtypeStruct((n - split_out, d), F32)]
    else:
        out_specs = pl.BlockSpec((tm, d), row)
        out_shape = jax.ShapeDtypeStruct((n, d), F32)
    return pl.pallas_call(
        functools.partial(_ffn_body, n_in=len(xs), n_out=2 if split_out else 1, prompt_tiles=pt,
                          final_norm=final_norm),
        grid=(n // tm,),
        in_specs=x_specs + [
            pl.BlockSpec((1, d), const),
            pl.BlockSpec(w1.shape, const, pipeline_mode=resident),
            pl.BlockSpec(w3.shape, const, pipeline_mode=resident),
            pl.BlockSpec(w2.shape, const, pipeline_mode=resident),
            pl.BlockSpec((1, d), const),
        ],
        out_specs=out_specs,
        out_shape=out_shape,
        compiler_params=_params(("arbitrary",), 48),
        name="ffn_half_step",
    )(*xs, g, w1, w3, w2, gf)


def _rope(x, c, s_lo, s_hi):
    w = x.shape[1]
    rep = w // LANES
    if rep > 1:
        c = jnp.concatenate([c] * rep, axis=1)
        s_lo = jnp.concatenate([s_lo] * rep, axis=1)
        s_hi = jnp.concatenate([s_hi] * rep, axis=1)
    half = ROPE_DIMS // 2
    return x * c + pltpu.roll(x, half, 1) * s_hi + pltpu.roll(x, w - half, 1) * s_lo


def _inproj_body(x_ref, g_ref, w_ref, c_ref, slo_ref, shi_ref,
                 gates_ref, z_ref, qc_ref, qi_ref, kc_ref, vc_ref, ki_ref, kcb_ref, vcb_ref, kib_ref, vt_ref):
    u = _rmsnorm(x_ref[...], g_ref[...]).astype(BF16)
    zf = _dot(u, w_ref[...])
    gates_ref[...] = jax.nn.sigmoid(zf[:, :GATE_UNITS * LANES]).astype(BF16)
    z_ref[...] = zf[:, GATE_UNITS * LANES:(GATE_UNITS + Z_UNITS) * LANES]

    unit = lambda first, count: zf[:, first * LANES:(first + count) * LANES]
    c, s_lo, s_hi = c_ref[...], slo_ref[...], shi_ref[...]
    qc_ref[...] = (_rope(unit(Z_CQ, 4), c, s_lo, s_hi) * (C_HDIM ** -0.5)).astype(BF16)
    qi_ref[...] = _rope(unit(Z_IQ, 4), c, s_lo, s_hi).astype(BF16)
    kc = _rope(unit(Z_CK, 1), c, s_lo, s_hi)
    vc = unit(Z_CV, 1)
    ki = _rope(unit(Z_IK, 1), c, s_lo, s_hi)[:, :IDX_DIM]
    kc_ref[...] = kc
    vc_ref[...] = vc
    ki_ref[...] = ki
    kcb_ref[...] = kc.astype(BF16)
    vcb_ref[...] = vc.astype(BF16)
    kib_ref[...] = ki.astype(BF16)
    ones = jnp.ones((VT_ROWS - C_HDIM, KEY_BLOCK), BF16)
    for kk in range(vt_ref.shape[0]):
        vt = vc[kk * KEY_BLOCK:(kk + 1) * KEY_BLOCK, :].T.astype(BF16)
        vt_ref[kk] = jnp.concatenate([vt[:C_HDIM], ones, vt[C_HDIM:], ones], axis=0)


def _inproj(h, g, w, tables, table_tile):
    n, d = h.shape
    tm = PROJ_TILE
    assert n % tm == 0
    row = lambda i: (i, 0)
    return pl.pallas_call(
        _inproj_body,
        grid=(n // tm,),
        in_specs=[
            pl.BlockSpec((tm, d), row),
            pl.BlockSpec((1, d), lambda i: (0, 0)),
            pl.BlockSpec((d, Z_WIDTH), lambda i: (0, 0), pipeline_mode=pl.Buffered(1)),
            pl.BlockSpec((tm, LANES), table_tile),
            pl.BlockSpec((tm, LANES), table_tile),
            pl.BlockSpec((tm, LANES), table_tile),
        ],
        out_specs=[
            pl.BlockSpec((tm, GATE_UNITS * LANES), row),
            pl.BlockSpec((tm, Z_UNITS * LANES), row),
            pl.BlockSpec((tm, C_WIDTH), row),
            pl.BlockSpec((tm, IDX_HEADS * IDX_DIM), row),
            pl.BlockSpec((tm, LANES), row),
            pl.BlockSpec((tm, LANES), row),
            pl.BlockSpec((tm, IDX_DIM), row),
            pl.BlockSpec((tm, LANES), row),
            pl.BlockSpec((tm, LANES), row),
            pl.BlockSpec((tm, IDX_DIM), row),
            pl.BlockSpec((tm // KEY_BLOCK, C_KV_HEADS * VT_ROWS, KEY_BLOCK), lambda i: (i, 0, 0)),
        ],
        out_shape=[
            jax.ShapeDtypeStruct((n, GATE_UNITS * LANES), BF16),
            jax.ShapeDtypeStruct((n, Z_UNITS * LANES), F32),
            jax.ShapeDtypeStruct((n, C_WIDTH), BF16),
            jax.ShapeDtypeStruct((n, IDX_HEADS * IDX_DIM), BF16),
            jax.ShapeDtypeStruct((n, LANES), F32),
            jax.ShapeDtypeStruct((n, LANES), F32),
            jax.ShapeDtypeStruct((n, IDX_DIM), F32),
            jax.ShapeDtypeStruct((n, LANES), BF16),
            jax.ShapeDtypeStruct((n, LANES), BF16),
            jax.ShapeDtypeStruct((n, IDX_DIM), BF16),
            jax.ShapeDtypeStruct((n // KEY_BLOCK, C_KV_HEADS * VT_ROWS, KEY_BLOCK), BF16),
        ],
        compiler_params=_params(("parallel",), 48),
        name="in_projection",
    )(h, g, w, *tables)


def _gla_head(q, k, v, logf, st_ref, h, tc):
    row = lax.broadcasted_iota(jnp.int32, (tc, tc), 0)
    col = lax.broadcasted_iota(jnp.int32, (tc, tc), 1)
    tril = (col <= row).astype(BF16)
    cum = _dot_exact_lhs(tril, logf)
    yield None

    tile = 2 * PAIR_BLOCK
    nt = tc // tile
    b3 = cum.reshape(nt, tile, HEAD_W)
    q3 = q.reshape(nt, tile, HEAD_W)
    k3 = k.reshape(nt, tile, HEAD_W)
    v3 = v.reshape(nt, tile, HEAD_W)
    srow = lax.broadcasted_iota(jnp.int32, (nt, tile, 1), 1)
    low = srow < PAIR_BLOCK
    tloc = srow & (PAIR_BLOCK - 1)

    def block_row(x3, j):
        return jnp.where(low, x3[:, j:j + 1, :], x3[:, PAIR_BLOCK + j:PAIR_BLOCK + j + 1, :])

    o3 = jnp.zeros((nt, tile, HEAD_W), F32)
    for j in range(PAIR_BLOCK):
        causal = tloc >= j
        decay = jnp.exp(jnp.where(causal, b3 - block_row(b3, j), 0.0))
        w = jnp.sum(q3 * block_row(k3, j) * decay, axis=-1, keepdims=True)
        o3 = o3 + jnp.where(causal, w, 0.0) * block_row(v3, j)
    o = o3.reshape(tc, HEAD_W)
    yield None

    vb = v.astype(BF16)
    attn = jnp.zeros((tc, tc), F32)
    half = tc // 2
    while half >= PAIR_BLOCK:
        blk = 2 * half
        nblk = tc // blk
        bl = cum.reshape(nblk, blk, HEAD_W)
        x = bl - bl[:, half - 1:half, :]
        second = lax.broadcasted_iota(jnp.int32, (nblk, blk, 1), 1) >= half
        e = jnp.exp(jnp.where(second, x, -x))
        qt = jnp.where(second, q.reshape(nblk, blk, HEAD_W) * e, 0.0).reshape(tc, HEAD_W).astype(BF16)
        kt = jnp.where(second, 0.0, k.reshape(nblk, blk, HEAD_W) * e).reshape(tc, HEAD_W).astype(BF16)
        shift = blk.bit_length() - 1
        same = (row >> shift) == (col >> shift)
        attn = attn + jnp.where(same, _dot_nt(qt, kt), 0.0)
        half //= 2
    o = o + _dot(attn.astype(BF16), vb)
    yield None

    st = st_ref[h]
    o = o + _dot_nt((q * jnp.exp(cum)).astype(BF16), st.astype(BF16))
    last = cum[tc - 1:tc, :]
    kd = (k * jnp.exp(last - cum)).astype(BF16)
    st_ref[h] = st * jnp.exp(last) + _dot_tn(vb, kd)
    yield o


def _gla_heads(inputs, st_ref, tc):
    heads = [_gla_head(q, k, v, logf, st_ref, h, tc) for h, (q, k, v, logf) in enumerate(inputs)]
    outs = None
    for _ in range(4):
        outs = [next(head) for head in heads]
    return outs


def _gla_finish(o, nw, gate):
    return _rmsnorm(o, nw) * (gate * jax.nn.sigmoid(gate))


def _gla_state_io(c, s0_ref, st_ref, heads):
    @pl.when(c == 0)
    def _():
        for h in range(heads):
            st_ref[h] = s0_ref[0, h].T


def _gla_state_out(c, sout_ref, st_ref, heads):
    @pl.when(c == pl.num_programs(1) - 1)
    def _():
        for h in range(heads):
            sout_ref[0, h] = st_ref[h].T


def _hgrn_body(q_ref, f_ref, v_ref, g_ref, lb_ref, nw_ref, s0_ref, y_ref, sout_ref, st_ref, *, tc):
    c = pl.program_id(1)
    _gla_state_io(c, s0_ref, st_ref, A_HEADS)
    inputs = []
    for h in range(A_HEADS):
        hs = slice(h * HEAD_W, (h + 1) * HEAD_W)
        zf = f_ref[:, hs]
        lb = lb_ref[:, hs]
        f = lb + (1.0 - lb) * jax.nn.sigmoid(zf)
        logf = jnp.log(jnp.maximum(f, F_MIN))
        k = (1.0 - lb) * jax.nn.sigmoid(-zf)
        zq = q_ref[:, hs]
        q = zq * jax.nn.sigmoid(zq) * (A_KDIM ** -0.5)
        inputs.append((q, k, v_ref[:, hs], logf))
    for h, o in enumerate(_gla_heads(inputs, st_ref, tc)):
        hs = slice(h * HEAD_W, (h + 1) * HEAD_W)
        y_ref[:, hs] = _gla_finish(o, nw_ref[...], g_ref[:, hs]).astype(BF16)
    _gla_state_out(c, sout_ref, st_ref, A_HEADS)


def _gla_body(q_ref, k_ref, v_ref, g_ref, r_ref, wup_ref, bup_ref, nw_ref, s0_ref,
              y_ref, sout_ref, st_ref, *, tc):
    c = pl.program_id(1)
    _gla_state_io(c, s0_ref, st_ref, B_HEADS)
    r = _dot_hi(r_ref[...], wup_ref[...]) + bup_ref[...]
    logf_all = (jnp.minimum(r, 0.0) - jnp.log1p(jnp.exp(-jnp.abs(r)))) / B_TAU
    heads = [slice(h * HEAD_W, (h + 1) * HEAD_W) for h in range(B_HEADS)]
    inputs = [(q_ref[:, hs] * (B_KDIM ** -0.5), k_ref[:, hs], v_ref[:, hs], logf_all[:, hs]) for hs in heads]
    for hs, o in zip(heads, _gla_heads(inputs, st_ref, tc)):
        y_ref[:, hs] = _gla_finish(o, nw_ref[...], g_ref[:, hs]).astype(BF16)
    _gla_state_out(c, sout_ref, st_ref, B_HEADS)


def _recurrent_mixer(mode, z, row0, nseq, t, s0, nw, extra):
    tc = min(128, t)
    nc = t // tc
    rb0 = row0 // tc
    zcol = (Z_A if mode == "hgrn" else Z_B) // 4

    def zspec(k):
        return pl.BlockSpec((tc, MIX_W), lambda b, c: (rb0 + b * nc + c, zcol + k))

    const = lambda b, c: (0, 0)
    state_spec = pl.BlockSpec((1, 4, HEAD_W, HEAD_W), lambda b, c: (b, 0, 0, 0))
    if mode == "hgrn":
        body = functools.partial(_hgrn_body, tc=tc)
        in_specs = [zspec(0), zspec(1), zspec(2), zspec(3),
                    pl.BlockSpec((1, MIX_W), const), pl.BlockSpec((1, HEAD_W), const), state_spec]
        args = (z, z, z, z, extra[0], nw, s0)
    else:
        body = functools.partial(_gla_body, tc=tc)
        in_specs = [zspec(0), zspec(1), zspec(2), zspec(3),
                    pl.BlockSpec((tc, LANES), lambda b, c: (rb0 + b * nc + c, Z_SMALL)),
                    pl.BlockSpec((LANES, MIX_W), const), pl.BlockSpec((1, MIX_W), const),
                    pl.BlockSpec((1, HEAD_W), const), state_spec]
        args = (z, z, z, z, z, extra[0], extra[1], nw, s0)
    return pl.pallas_call(
        body,
        grid=(nseq, nc),
        in_specs=in_specs,
        out_specs=[pl.BlockSpec((tc, MIX_W), lambda b, c: (b * nc + c, 0)), state_spec],
        out_shape=[jax.ShapeDtypeStruct((nseq * t, MIX_W), BF16),
                   jax.ShapeDtypeStruct((nseq, 4, HEAD_W, HEAD_W), F32)],
        scratch_shapes=[pltpu.VMEM((4, HEAD_W, HEAD_W), F32)],
        compiler_params=_params(("parallel", "arbitrary"), 32),
        name=mode + "_mixer",
    )(*args)


SUPER = 4
ATTN_BLOCKS = 2


def _dsa_t_body(qc_ref, qi_ref, sm_ref, kc_ref, vt_ref, ki_ref, o_ref,
                sk_ref, qit_ref, qct_ref, wrow_ref, acc_ref, ot_ref, s0_ref, s1_ref, p_ref,
                *, tq, past, nkeys, topk):
    i = pl.program_id(1)
    qpos0 = past + i * tq
    last_chunk = (qpos0 + tq - 1) // CHUNK
    n_adm = jnp.minimum((last_chunk + 1) * CHUNK, nkeys)
    nkb = (n_adm + KEY_BLOCK - 1) // KEY_BLOCK
    nsb = (nkb + SUPER - 1) // SUPER
    grp = C_HEADS // C_KV_HEADS

    qit = qi_ref[...].astype(F32).T
    for j in range(IDX_HEADS):
        qit_ref[:, j * tq:(j + 1) * tq] = qit[j * IDX_DIM:(j + 1) * IDX_DIM, :].astype(BF16)
    qct = qc_ref[...].astype(F32).T
    zeros = jnp.zeros((C_HDIM, tq), BF16)
    for hq in range(C_HEADS):
        g, r = divmod(hq, grp)
        for gg in range(C_KV_HEADS):
            val = qct[hq * C_HDIM:(hq + 1) * C_HDIM, :].astype(BF16) if gg == g else zeros
            qct_ref[g, gg * C_HDIM:(gg + 1) * C_HDIM, r * tq:(r + 1) * tq] = val
    wrow_ref[...] = sm_ref[...].T

    qchunk = (qpos0 + lax.broadcasted_iota(jnp.int32, (1, tq), 1)) >> 6
    ksub = lax.broadcasted_iota(jnp.int32, (KEY_BLOCK, 1), 0)

    def admissible(kb):
        kpos = kb * KEY_BLOCK + ksub
        return ((kpos >> 6) <= qchunk) & (kpos < nkeys)

    def score_step(kk, carry):
        for u in range(SUPER):
            kb = kk * SUPER + u
            kib = ki_ref[0, pl.ds(pl.multiple_of(kb * KEY_BLOCK, KEY_BLOCK), KEY_BLOCK), :]
            d = _dot(kib, qit_ref[...])
            sc = jnp.maximum(d[:, 0:tq], 0.0) * wrow_ref[SMALL_IW:SMALL_IW + 1, :]
            for j in range(1, IDX_HEADS):
                sc = sc + jnp.maximum(d[:, j * tq:(j + 1) * tq], 0.0) * wrow_ref[SMALL_IW + j:SMALL_IW + j + 1, :]
            bits = pltpu.bitcast(sc * IDX_SCALE + 0.0, jnp.int32)
            key = jnp.where(admissible(kb), jnp.where(bits < 0, bits ^ 0x7FFFFFFF, bits), INT_MIN)
            sk_ref[kb] = key
        return carry

    lax.fori_loop(0, nsb, score_step, 0)

    def fold(hit):
        return jnp.sum(hit.reshape(KEY_BLOCK // 8, 8, tq), axis=0)

    def count(pred):
        def step(kk, acc):
            for u in range(SUPER):
                kb = kk * SUPER + u
                acc = acc + fold(jnp.where(pred(sk_ref[kb], kb), 1.0, 0.0))
            return acc
        acc = lax.fori_loop(0, nsb, step, jnp.zeros((8, tq), F32))
        return jnp.sum(acc, axis=0, keepdims=True)

    kf = float(topk)
    c0 = count(lambda s, kb: s >= 0)
    t0 = jnp.where(c0 >= kf, 0, INT_MIN).astype(jnp.int32)

    def bit_step(it, t):
        cand = t | jnp.left_shift(jnp.int32(1), 30 - it)
        return jnp.where(count(lambda s, kb: s >= cand) >= kf, cand, t)
    thr = lax.fori_loop(0, 31, bit_step, t0)
    thr = jnp.maximum(thr, INT_MIN + 1)

    c_ge = count(lambda s, kb: s >= thr)
    nbits = max(1, (nkeys - 1).bit_length())

    @pl.when(jnp.max(c_ge) > kf)
    def _():
        need = kf - count(lambda s, kb: s > thr)

        def cut_step(it, cut):
            cand = cut | jnp.left_shift(jnp.int32(1), nbits - 1 - it)
            c = count(lambda s, kb: (s == thr) & (kb * KEY_BLOCK + ksub < cand))
            return jnp.where(c < need, cand, cut)
        cut = lax.fori_loop(0, nbits, cut_step, jnp.zeros((1, tq), jnp.int32))

        def strike(kb, carry):
            key = sk_ref[kb]
            sk_ref[kb] = jnp.where((key == thr) & (kb * KEY_BLOCK + ksub > cut), INT_MIN, key)
            return carry
        lax.fori_loop(0, nsb * SUPER, strike, 0)

    acc_ref[...] = jnp.zeros(acc_ref.shape, F32)
    init = tuple(jnp.full((1, grp * tq), NEG, F32) for _ in range(C_KV_HEADS))

    span = ATTN_BLOCKS * KEY_BLOCK
    n_attn = nsb * (SUPER // ATTN_BLOCKS)

    s_bufs = (s0_ref, s1_ref)

    def qk_scores(kk, half):
        kblk = kc_ref[0, pl.ds(pl.multiple_of(kk * span, span), span), :]
        bias = jnp.concatenate(
            [jnp.where(sk_ref[kk * ATTN_BLOCKS + u] >= thr, 0.0, NEG) for u in range(ATTN_BLOCKS)], axis=0)
        bias = jnp.concatenate([bias] * grp, axis=1)
        for g in range(C_KV_HEADS):
            s = _dot(kblk, qct_ref[g]) + bias
            for r in range(grp):
                s_bufs[half][g, r] = s[:, r * tq:(r + 1) * tq]

    def softmax_pv(kk, half, m):
        m = list(m)
        for g in range(C_KV_HEADS):
            m_cols, alpha_cols = [], []
            for r in range(grp):
                cols = slice(r * tq, (r + 1) * tq)
                sb = s_bufs[half][g, r]
                m_old = m[g][:, cols]
                m_new = jnp.maximum(m_old, jnp.max(sb, axis=0, keepdims=True))
                p_ref[half, g, r] = jnp.exp(sb - m_new).astype(BF16)
                m_cols.append(m_new)
                alpha_cols.append(jnp.exp(m_old - m_new))
            rows = slice(g * VT_ROWS, (g + 1) * VT_ROWS)
            pv = None
            for u in range(ATTN_BLOCKS):
                keys_u = slice(u * KEY_BLOCK, (u + 1) * KEY_BLOCK)
                p_u = jnp.concatenate([p_ref[half, g, r, keys_u, :] for r in range(grp)], axis=1)
                d = _dot(vt_ref[0, kk * ATTN_BLOCKS + u, rows, :], p_u)
                pv = d if pv is None else pv + d
            for r in range(grp):
                acc_ref[g, r] = alpha_cols[r] * acc_ref[g, r] + pv[:, r * tq:(r + 1) * tq]
            m[g] = jnp.concatenate(m_cols, axis=1)
        return tuple(m)

    qk_scores(0, 0)

    def pair_step(j, m):
        qk_scores(2 * j + 1, 1)
        m = softmax_pv(2 * j, 0, m)
        qk_scores(jnp.minimum(2 * j + 2, n_attn - 2), 0)
        return softmax_pv(2 * j + 1, 1, m)

    lax.fori_loop(0, n_attn // 2, pair_step, init)
    for hq in range(C_HEADS):
        g, r = divmod(hq, grp)
        ot_ref[hq * C_HDIM:(hq + 1) * C_HDIM, :] = acc_ref[g, r, 0:C_HDIM, :] / acc_ref[g, r, C_HDIM:C_HDIM + 1, :]
    o_ref[...] = ot_ref[...].T.astype(BF16)


def _dsa_t(qc, qi, small, small_col, keys, vals_t, kidx, nseq, t, past, nkeys):
    tq = KEY_BLOCK
    nq = t // tq
    lp = keys.shape[1]
    assert t % tq == 0 and lp % (SUPER * KEY_BLOCK) == 0
    topk = min(TOPK_MAX, nkeys // 4)
    grp = C_HEADS // C_KV_HEADS
    qrow = lambda b, i: (b * nq + i, 0)
    return pl.pallas_call(
        functools.partial(_dsa_t_body, tq=tq, past=past, nkeys=nkeys, topk=topk),
        grid=(nseq, nq),
        in_specs=[
            pl.BlockSpec((tq, C_WIDTH), qrow),
            pl.BlockSpec((tq, IDX_HEADS * IDX_DIM), qrow),
            pl.BlockSpec((tq, LANES), lambda b, i: (b * nq + i, small_col)),
            pl.BlockSpec((1, lp, LANES), lambda b, i: (b, 0, 0)),
            pl.BlockSpec((1, lp // KEY_BLOCK, C_KV_HEADS * VT_ROWS, KEY_BLOCK), lambda b, i: (b, 0, 0, 0)),
            pl.BlockSpec((1, lp, IDX_DIM), lambda b, i: (b, 0, 0)),
        ],
        out_specs=pl.BlockSpec((tq, C_WIDTH), lambda b, i: (b * nq + i, 0)),
        out_shape=jax.ShapeDtypeStruct((nseq * t, C_WIDTH), BF16),
        scratch_shapes=[
            pltpu.VMEM((lp // KEY_BLOCK, KEY_BLOCK, tq), jnp.int32),
            pltpu.VMEM((IDX_DIM, IDX_HEADS * tq), BF16),
            pltpu.VMEM((C_KV_HEADS, C_KV_HEADS * C_HDIM, grp * tq), BF16),
            pltpu.VMEM((LANES, tq), F32),
            pltpu.VMEM((C_KV_HEADS, grp, VT_ROWS, tq), F32),
            pltpu.VMEM((C_WIDTH, tq), F32),
            pltpu.VMEM((C_KV_HEADS, grp, ATTN_BLOCKS * KEY_BLOCK, tq), F32),
            pltpu.VMEM((C_KV_HEADS, grp, ATTN_BLOCKS * KEY_BLOCK, tq), F32),
            pltpu.VMEM((2, C_KV_HEADS, grp, ATTN_BLOCKS * KEY_BLOCK, tq), BF16),
        ],
        compiler_params=_params(("parallel", "arbitrary"), 48),
        name="dsa_mixer_t",
    )(qc, qi, small, keys, vals_t, kidx)


def _merge_body(h_ref, yap_ref, yas_ref, ybp_ref, ybs_ref, ocp_ref, ocs_ref, ga_ref, gb_ref, gc_ref,
                wa_ref, wb_ref, wc_ref, wo_ref, o_ref, *, prompt_tiles):
    is_prompt = pl.program_id(0) < prompt_tiles
    pick = lambda p_ref, s_ref: jnp.where(is_prompt, p_ref[...], s_ref[...])
    merged = (ga_ref[...].astype(F32) * _dot(pick(yap_ref, yas_ref), wa_ref[...])
              + gb_ref[...].astype(F32) * _dot(pick(ybp_ref, ybs_ref), wb_ref[...])
              + gc_ref[...].astype(F32) * _dot(pick(ocp_ref, ocs_ref), wc_ref[...]))
    o_ref[...] = h_ref[...] + _dot(merged.astype(BF16), wo_ref[...])


def _merge(h, ya, yb, oc, gates, wa, wb, wc, wo):
    n, d = h.shape
    tm = TOKEN_TILE
    pt = ya[0].shape[0] // tm
    assert ya[0].shape[0] % tm == 0 and ya[1].shape[0] % tm == 0 and n == ya[0].shape[0] + ya[1].shape[0]
    row = lambda i: (i, 0)
    const = lambda i: (0, 0)
    prow = lambda i: (jnp.minimum(i, pt - 1), 0)
    srow = lambda i: (jnp.maximum(i - pt, 0), 0)
    pair = [pl.BlockSpec((tm, MIX_W), prow), pl.BlockSpec((tm, MIX_W), srow)]
    return pl.pallas_call(
        functools.partial(_merge_body, prompt_tiles=pt),
        grid=(n // tm,),
        in_specs=[
            pl.BlockSpec((tm, d), row),
            *pair, *pair, *pair,
            pl.BlockSpec((tm, d), lambda i: (i, 0)),
            pl.BlockSpec((tm, d), lambda i: (i, 1)),
            pl.BlockSpec((tm, d), lambda i: (i, 2)),
            pl.BlockSpec((MIX_W, d), const),
            pl.BlockSpec((MIX_W, d), const),
            pl.BlockSpec((MIX_W, d), const),
            pl.BlockSpec((d, d), const),
        ],
        out_specs=pl.BlockSpec((tm, d), row),
        out_shape=jax.ShapeDtypeStruct((n, d), F32),
        compiler_params=_params(("parallel",), 48),
        name="gated_merge",
    )(h, *ya, *yb, *oc, gates, gates, gates, wa, wb, wc, wo)


def _pad_heads(w, heads, dim):
    lead = w.shape[:-1]
    w = w.reshape(*lead, heads, dim)
    w = jnp.pad(w, [(0, 0)] * len(lead) + [(0, 0), (0, HEAD_W - dim)])
    return w.reshape(*lead, heads * HEAD_W)


def _layout_w_in(w):
    widths = (512, 512, 512, 512, 256, 256, 512, 16, 512, 512, 128, 128, 512, 64, 8, 1024, 1024, 1024)
    w = w.astype(BF16)
    parts, o = [], 0
    for wd in widths:
        parts.append(w[:, o:o + wd])
        o += wd
    (a_q, a_f, a_i, a_g, b_q, b_k, b_v, b_r, b_g, c_q, c_k, c_v, i_q, i_k, i_w, g_a, g_b, g_c) = parts
    d = w.shape[0]
    small = jnp.concatenate([b_r, i_w, jnp.zeros((d, LANES - 24), w.dtype)], axis=1)
    i_k = jnp.pad(i_k, ((0, 0), (0, LANES - IDX_DIM)))
    cols = [g_a, g_b, g_c, a_q, a_f, a_i, a_g,
            _pad_heads(b_q, B_HEADS, B_KDIM), _pad_heads(b_k, B_HEADS, B_KDIM), b_v, b_g,
            small, c_q, i_q, c_k, c_v, i_k]
    out = jnp.concatenate(cols, axis=1)
    assert out.shape[1] == Z_WIDTH
    return out


def _rope_tables(pos):
    half = ROPE_DIMS // 2
    inv = ROPE_THETA ** (-jnp.arange(half, dtype=F32) * (2.0 / ROPE_DIMS))
    ang = pos.astype(F32)[:, None] * inv[None, :]
    cos, sin = jnp.cos(ang), jnp.sin(ang)
    n = pos.shape[0]
    ones = jnp.ones((n, C_HDIM - ROPE_DIMS), F32)
    zeros = jnp.zeros((n, C_HDIM - ROPE_DIMS), F32)
    zh = jnp.zeros((n, half), F32)
    c = jnp.concatenate([cos, cos, ones], axis=1)
    s_lo = jnp.concatenate([-sin, zh, zeros], axis=1)
    s_hi = jnp.concatenate([zh, sin, zeros], axis=1)
    two = lambda a: jnp.concatenate([a, a], axis=1)
    return two(c), two(s_lo), two(s_hi)


def kernel(x_prompt, x_sample, state_hgrn, state_gla, cache_k, cache_v, cache_kidx, hgrn_lb, w_in, w_gla_up, b_gla, norm_hgrn, norm_gla, w_br_a, w_br_b, w_br_c, w_out, norm_ffn1, norm_mix, norm_ffn2, ffn1_w1, ffn1_w3, ffn1_w2, ffn2_w1, ffn2_w3, ffn2_w2, norm_final):
    bp, tp, d = x_prompt.shape
    bs, ts, _ = x_sample.shape
    past = cache_k.shape[2]
    n_p, n_s = bp * tp, bs * ts

    lb_sm = jax.nn.softmax(hgrn_lb.astype(F32), axis=0)
    lb_all = jnp.concatenate([jnp.zeros_like(lb_sm[:1]), jnp.cumsum(lb_sm[1:], axis=0)], axis=0)

    pos = jnp.concatenate([jnp.arange(tp, dtype=jnp.int32),
                           jnp.tile(past + jnp.arange(ts, dtype=jnp.int32), PROJ_TILE // ts)])
    tables = _rope_tables(pos)
    assert tp % PROJ_TILE == 0 and PROJ_TILE % ts == 0 and n_s % PROJ_TILE == 0
    table_tile = lambda i: (jnp.where(i < n_p // PROJ_TILE, i % (tp // PROJ_TILE), tp // PROJ_TILE), 0)

    hs = (x_prompt.reshape(n_p, d), x_sample.reshape(n_s, d))
    row2 = lambda a: a.reshape(1, -1)
    zero_state = jnp.zeros((bp, 4, HEAD_W, HEAD_W), F32)
    span_s = SUPER * KEY_BLOCK
    key_pad = -(-(past + ts) // span_s) * span_s - past - ts
    q_rep = KEY_BLOCK // ts

    ffn1_w1, ffn1_w3, ffn1_w2, ffn2_w1, ffn2_w3, ffn2_w2, w_br_a, w_br_b, w_br_c, w_out = (
        a.astype(BF16) for a in (ffn1_w1, ffn1_w3, ffn1_w2, ffn2_w1, ffn2_w3, ffn2_w2,
                                 w_br_a, w_br_b, w_br_c, w_out))

    outs = {k: [] for k in ("pa", "pb", "pk", "pv", "pki", "sa", "sb", "sk", "sv", "ski")}
    for l in range(DEPTH):
        bf = lambda a: a[l]
        h = _ffn(hs, row2(norm_ffn1[l]), bf(ffn1_w1), bf(ffn1_w3), bf(ffn1_w2), row2(norm_final), False, 0)
        gates, z, qc, qi, kc, vc, ki, kcb, vcb, kib, vt = _inproj(
            h, row2(norm_mix[l]), _layout_w_in(w_in[l]), tables, table_tile)

        lb = row2(lb_all[l])
        nwa, nwb = row2(norm_hgrn[l]), row2(norm_gla[l])
        wup = jnp.pad(_pad_heads(w_gla_up[l], B_HEADS, B_KDIM), ((0, LANES - B_GATE_RANK), (0, 0)))
        bup = row2(_pad_heads(b_gla[l], B_HEADS, B_KDIM))
        sb0 = jnp.pad(state_gla[l], ((0, 0), (0, 0), (0, HEAD_W - B_KDIM), (0, 0)))

        ya_p, sa_p = _recurrent_mixer("hgrn", z, 0, bp, tp, zero_state, nwa, (lb,))
        ya_s, sa_s = _recurrent_mixer("hgrn", z, n_p, bs, ts, state_hgrn[l], nwa, (lb,))
        yb_p, sb_p = _recurrent_mixer("gla", z, 0, bp, tp, zero_state, nwb, (wup, bup))
        yb_s, sb_s = _recurrent_mixer("gla", z, n_p, bs, ts, sb0, nwb, (wup, bup))

        seqs = lambda a, n, t: a.reshape(n, t, a.shape[-1])
        vt_p = vt[:n_p // KEY_BLOCK].reshape(bp, tp // KEY_BLOCK, C_KV_HEADS * VT_ROWS, KEY_BLOCK)
        oc_p = _dsa_t(qc, qi, z, Z_SMALL, seqs(kcb[:n_p], bp, tp), vt_p, seqs(kib[:n_p], bp, tp), bp, tp, 0, tp)

        def with_cache(cache, new):
            full = jnp.concatenate([cache.reshape(bs, past, -1).astype(BF16), seqs(new[n_p:], bs, ts)], axis=1)
            return jnp.pad(full, ((0, 0), (0, key_pad), (0, 0)))

        def fill_block(a):
            return jnp.tile(seqs(a[n_p:], bs, ts), (1, q_rep, 1)).reshape(bs * KEY_BLOCK, a.shape[-1])

        vals_s = with_cache(cache_v[l], vcb)
        vt_s = jnp.swapaxes(vals_s.reshape(bs, -1, KEY_BLOCK, LANES), 2, 3)
        ones = jnp.ones(vt_s.shape[:2] + (VT_ROWS - C_HDIM, KEY_BLOCK), BF16)
        vt_s = jnp.concatenate([vt_s[:, :, :C_HDIM], ones, vt_s[:, :, C_HDIM:], ones], axis=2)
        small_s = fill_block(z[:, Z_SMALL * LANES:(Z_SMALL + 1) * LANES])
        oc_s = _dsa_t(fill_block(qc), fill_block(qi), small_s, 0, with_cache(cache_k[l], kcb), vt_s,
                      with_cache(cache_kidx[l], kib), bs, KEY_BLOCK, past, past + ts)
        oc_s = oc_s.reshape(bs, KEY_BLOCK, C_WIDTH)[:, :ts].reshape(n_s, C_WIDTH)

        h = _merge(h, (ya_p, ya_s), (yb_p, yb_s), (oc_p, oc_s), gates,
                   bf(w_br_a), bf(w_br_b), bf(w_br_c), bf(w_out))
        last = l == DEPTH - 1
        hs = _ffn((h,), row2(norm_ffn2[l]), bf(ffn2_w1), bf(ffn2_w3), bf(ffn2_w2), row2(norm_final),
                  last, n_p if last else 0)
        hs = hs if last else (hs,)

        outs["pa"].append(sa_p)
        outs["sa"].append(sa_s)
        outs["pb"].append(sb_p[:, :, :B_KDIM, :])
        outs["sb"].append(sb_s[:, :, :B_KDIM, :])
        outs["pk"].append(kc[:n_p].reshape(bp, tp, C_KV_HEADS, C_HDIM))
        outs["pv"].append(vc[:n_p].reshape(bp, tp, C_KV_HEADS, C_HDIM))
        outs["pki"].append(ki[:n_p].reshape(bp, tp, IDX_DIM))
        outs["sk"].append(kc[n_p:].reshape(bs, ts, C_KV_HEADS, C_HDIM))
        outs["sv"].append(vc[n_p:].reshape(bs, ts, C_KV_HEADS, C_HDIM))
        outs["ski"].append(ki[n_p:].reshape(bs, ts, IDX_DIM))

    st = {k: jnp.stack(v) for k, v in outs.items()}
    return (hs[0].reshape(bp, tp, d), hs[1].reshape(bs, ts, d),
            st["pa"], st["pb"], st["pk"], st["pv"], st["pki"],
            st["sa"], st["sb"], st["sk"], st["sv"], st["ski"])
```

```python
import functools

import jax
import jax.numpy as jnp
from jax import lax
from jax.experimental import pallas as pl
from jax.experimental.pallas import tpu as pltpu

F32 = jnp.float32
BF16 = jnp.bfloat16

D_MODEL = 1024
DEPTH = 2
CHUNK = 64
EPS = 1e-6
NEG = -1e30
F_MIN = 1e-30
A_HEADS = 4
A_KDIM = 128
A_VDIM = 128
B_HEADS = 4
B_KDIM = 64
B_VDIM = 128
B_GATE_RANK = 16
B_TAU = 16.0
C_HEADS = 8
C_KV_HEADS = 2
C_HDIM = 64
C_WIDTH = C_HEADS * C_HDIM
IDX_HEADS = 8
IDX_DIM = 64
IDX_SCALE = (IDX_HEADS * IDX_DIM) ** -0.5
TOPK_MAX = 256
ROPE_THETA = 500000.0
ROPE_DIMS = C_HDIM // 4
PAIR_BLOCK = 4
D_FF = 2816

LANES = 128
HEAD_W = 128
MIX_W = 4 * HEAD_W
TOKEN_TILE = 512
PROJ_TILE = 256
KEY_BLOCK = 128
VT_ROWS = 80
INT_MIN = -(2 ** 31)

GATE_UNITS = 24
Z_A = 0
Z_B = 16
Z_SMALL = 32
Z_UNITS = 33
Z_CQ = GATE_UNITS + Z_UNITS
Z_IQ = Z_CQ + 4
Z_CK = Z_IQ + 4
Z_CV = Z_CK + 1
Z_IK = Z_CV + 1
Z_WIDTH = (Z_IK + 1) * LANES
SMALL_IW = B_GATE_RANK


def _params(sem, vmem_mb):
    return pltpu.CompilerParams(dimension_semantics=sem, vmem_limit_bytes=vmem_mb << 20)


def _dot(a, b):
    return jnp.dot(a, b, preferred_element_type=F32)


def _dot_nt(a, b):
    return lax.dot_general(a, b, (((1,), (1,)), ((), ())), preferred_element_type=F32)


def _dot_tn(a, b):
    return lax.dot_general(a, b, (((0,), (0,)), ((), ())), preferred_element_type=F32)


def _split3(x):
    h1 = x.astype(BF16)
    r1 = x - h1.astype(F32)
    h2 = r1.astype(BF16)
    h3 = (r1 - h2.astype(F32)).astype(BF16)
    return h1, h2, h3


def _dot_exact_lhs(m, x):
    h1, h2, h3 = _split3(x)
    return _dot(m, h1) + _dot(m, h2) + _dot(m, h3)


def _dot_hi(a, b):
    a1 = a.astype(BF16)
    a2 = (a - a1.astype(F32)).astype(BF16)
    b1 = b.astype(BF16)
    b2 = (b - b1.astype(F32)).astype(BF16)
    return _dot(a1, b1) + _dot(a1, b2) + _dot(a2, b1)


def _rmsnorm(x, g):
    return x * lax.rsqrt(jnp.mean(x * x, axis=-1, keepdims=True) + EPS) * g


def _ffn_body(*refs, n_in, n_out, prompt_tiles, final_norm):
    x_refs, (g_ref, w1_ref, w3_ref, w2_ref, gf_ref), o_refs = refs[:n_in], refs[n_in:n_in + 5], refs[n_in + 5:]
    is_prompt = pl.program_id(0) < prompt_tiles
    x = x_refs[0][...] if n_in == 1 else jnp.where(is_prompt, x_refs[0][...], x_refs[1][...])
    u = _rmsnorm(x, g_ref[...]).astype(BF16)
    a = _dot(u, w1_ref[...])
    b = _dot(u, w3_ref[...])
    hid = (a * jax.nn.sigmoid(a) * b).astype(BF16)
    out = x + 0.5 * _dot(hid, w2_ref[...])
    if final_norm:
        out = _rmsnorm(out, gf_ref[...])
    if n_out == 1:
        o_refs[0][...] = out
    else:
        @pl.when(is_prompt)
        def _():
            o_refs[0][...] = out

        @pl.when(jnp.logical_not(is_prompt))
        def _():
            o_refs[1][...] = out


def _ffn(xs, g, w1, w3, w2, gf, final_norm, split_out):
    d = xs[0].shape[1]
    tm = TOKEN_TILE
    sizes = [x.shape[0] for x in xs]
    n = sum(sizes)
    assert all(sz % tm == 0 for sz in sizes)
    pt = sizes[0] // tm if len(xs) == 2 else split_out // tm
    prow = lambda i: (jnp.minimum(i, pt - 1), 0)
    srow = lambda i: (jnp.maximum(i - pt, 0), 0)
    row = lambda i: (i, 0)
    const = lambda i: (0, 0)
    resident = pl.Buffered(1)
    x_specs = [pl.BlockSpec((tm, d), row)] if len(xs) == 1 else \
        [pl.BlockSpec((tm, d), prow), pl.BlockSpec((tm, d), srow)]
    if split_out:
        out_specs = [pl.BlockSpec((tm, d), prow), pl.BlockSpec((tm, d), srow)]
        out_shape = [jax.ShapeDtypeStruct((split_out, d), F32), jax.ShapeDtypeStruct((n - split_out, d), F32)]
    else:
        out_specs = pl.BlockSpec((tm, d), row)
        out_shape = jax.ShapeDtypeStruct((n, d), F32)
    return pl.pallas_call(
        functools.partial(_ffn_body, n_in=len(xs), n_out=2 if split_out else 1, prompt_tiles=pt,
                          final_norm=final_norm),
        grid=(n // tm,),
        in_specs=x_specs + [
            pl.BlockSpec((1, d), const),
            pl.BlockSpec(w1.shape, const, pipeline_mode=resident),
            pl.BlockSpec(w3.shape, const, pipeline_mode=resident),
            pl.BlockSpec(w2.shape, const, pipeline_mode=resident),
            pl.BlockSpec((1, d), const),
        ],
        out_specs=out_specs,
        out_shape=out_shape,
        compiler_params=_params(("arbitrary",), 48),
        name="ffn_half_step",
    )(*xs, g, w1, w3, w2, gf)


def _rope(x, c, s_lo, s_hi):
    w = x.shape[1]
    rep = w // LANES
    if rep > 1:
        c = jnp.concatenate([c] * rep, axis=1)
        s_lo = jnp.concatenate([s_lo] * rep, axis=1)
        s_hi = jnp.concatenate([s_hi] * rep, axis=1)
    half = ROPE_DIMS // 2
    return x * c + pltpu.roll(x, half, 1) * s_hi + pltpu.roll(x, w - half, 1) * s_lo


def _inproj_body(x_ref, g_ref, w_ref, c_ref, slo_ref, shi_ref,
                 gates_ref, z_ref, qc_ref, qi_ref, kc_ref, vc_ref, ki_ref, kcb_ref, vcb_ref, kib_ref, vt_ref):
    u = _rmsnorm(x_ref[...], g_ref[...]).astype(BF16)
    zf = _dot(u, w_ref[...])
    gates_ref[...] = jax.nn.sigmoid(zf[:, :GATE_UNITS * LANES]).astype(BF16)
    z_ref[...] = zf[:, GATE_UNITS * LANES:(GATE_UNITS + Z_UNITS) * LANES]

    unit = lambda first, count: zf[:, first * LANES:(first + count) * LANES]
    c, s_lo, s_hi = c_ref[...], slo_ref[...], shi_ref[...]
    qc_ref[...] = (_rope(unit(Z_CQ, 4), c, s_lo, s_hi) * (C_HDIM ** -0.5)).astype(BF16)
    qi_ref[...] = _rope(unit(Z_IQ, 4), c, s_lo, s_hi).astype(BF16)
    kc = _rope(unit(Z_CK, 1), c, s_lo, s_hi)
    vc = unit(Z_CV, 1)
    ki = _rope(unit(Z_IK, 1), c, s_lo, s_hi)[:, :IDX_DIM]
    kc_ref[...] = kc
    vc_ref[...] = vc
    ki_ref[...] = ki
    kcb_ref[...] = kc.astype(BF16)
    vcb_ref[...] = vc.astype(BF16)
    kib_ref[...] = ki.astype(BF16)
    ones = jnp.ones((VT_ROWS - C_HDIM, KEY_BLOCK), BF16)
    for kk in range(vt_ref.shape[0]):
        vt = vc[kk * KEY_BLOCK:(kk + 1) * KEY_BLOCK, :].T.astype(BF16)
        vt_ref[kk] = jnp.concatenate([vt[:C_HDIM], ones, vt[C_HDIM:], ones], axis=0)


def _inproj(h, g, w, tables, table_tile):
    n, d = h.shape
    tm = PROJ_TILE
    assert n % tm == 0
    row = lambda i: (i, 0)
    return pl.pallas_call(
        _inproj_body,
        grid=(n // tm,),
        in_specs=[
            pl.BlockSpec((tm, d), row),
            pl.BlockSpec((1, d), lambda i: (0, 0)),
            pl.BlockSpec((d, Z_WIDTH), lambda i: (0, 0), pipeline_mode=pl.Buffered(1)),
            pl.BlockSpec((tm, LANES), table_tile),
            pl.BlockSpec((tm, LANES), table_tile),
            pl.BlockSpec((tm, LANES), table_tile),
        ],
        out_specs=[
            pl.BlockSpec((tm, GATE_UNITS * LANES), row),
            pl.BlockSpec((tm, Z_UNITS * LANES), row),
            pl.BlockSpec((tm, C_WIDTH), row),
            pl.BlockSpec((tm, IDX_HEADS * IDX_DIM), row),
            pl.BlockSpec((tm, LANES), row),
            pl.BlockSpec((tm, LANES), row),
            pl.BlockSpec((tm, IDX_DIM), row),
            pl.BlockSpec((tm, LANES), row),
            pl.BlockSpec((tm, LANES), row),
            pl.BlockSpec((tm, IDX_DIM), row),
            pl.BlockSpec((tm // KEY_BLOCK, C_KV_HEADS * VT_ROWS, KEY_BLOCK), lambda i: (i, 0, 0)),
        ],
        out_shape=[
            jax.ShapeDtypeStruct((n, GATE_UNITS * LANES), BF16),
            jax.ShapeDtypeStruct((n, Z_UNITS * LANES), F32),
            jax.ShapeDtypeStruct((n, C_WIDTH), BF16),
            jax.ShapeDtypeStruct((n, IDX_HEADS * IDX_DIM), BF16),
            jax.ShapeDtypeStruct((n, LANES), F32),
            jax.ShapeDtypeStruct((n, LANES), F32),
            jax.ShapeDtypeStruct((n, IDX_DIM), F32),
            jax.ShapeDtypeStruct((n, LANES), BF16),
            jax.ShapeDtypeStruct((n, LANES), BF16),
            jax.ShapeDtypeStruct((n, IDX_DIM), BF16),
            jax.ShapeDtypeStruct((n // KEY_BLOCK, C_KV_HEADS * VT_ROWS, KEY_BLOCK), BF16),
        ],
        compiler_params=_params(("parallel",), 48),
        name="in_projection",
    )(h, g, w, *tables)


def _gla_head(q, k, v, logf, st_ref, h, tc):
    row = lax.broadcasted_iota(jnp.int32, (tc, tc), 0)
    col = lax.broadcasted_iota(jnp.int32, (tc, tc), 1)
    tril = (col <= row).astype(BF16)
    cum = _dot_exact_lhs(tril, logf)
    yield None

    tile = 2 * PAIR_BLOCK
    nt = tc // tile
    b3 = cum.reshape(nt, tile, HEAD_W)
    q3 = q.reshape(nt, tile, HEAD_W)
    k3 = k.reshape(nt, tile, HEAD_W)
    v3 = v.reshape(nt, tile, HEAD_W)
    srow = lax.broadcasted_iota(jnp.int32, (nt, tile, 1), 1)
    low = srow < PAIR_BLOCK
    tloc = srow & (PAIR_BLOCK - 1)

    def block_row(x3, j):
        return jnp.where(low, x3[:, j:j + 1, :], x3[:, PAIR_BLOCK + j:PAIR_BLOCK + j + 1, :])

    o3 = jnp.zeros((nt, tile, HEAD_W), F32)
    for j in range(PAIR_BLOCK):
        causal = tloc >= j
        decay = jnp.exp(jnp.where(causal, b3 - block_row(b3, j), 0.0))
        w = jnp.sum(q3 * block_row(k3, j) * decay, axis=-1, keepdims=True)
        o3 = o3 + jnp.where(causal, w, 0.0) * block_row(v3, j)
        yield None
    o = o3.reshape(tc, HEAD_W)
    yield None

    vb = v.astype(BF16)
    attn = jnp.zeros((tc, tc), F32)
    half = tc // 2
    while half >= PAIR_BLOCK:
        blk = 2 * half
        nblk = tc // blk
        bl = cum.reshape(nblk, blk, HEAD_W)
        x = bl - bl[:, half - 1:half, :]
        second = lax.broadcasted_iota(jnp.int32, (nblk, blk, 1), 1) >= half
        e = jnp.exp(jnp.where(second, x, -x))
        qt = jnp.where(second, q.reshape(nblk, blk, HEAD_W) * e, 0.0).reshape(tc, HEAD_W).astype(BF16)
        kt = jnp.where(second, 0.0, k.reshape(nblk, blk, HEAD_W) * e).reshape(tc, HEAD_W).astype(BF16)
        shift = blk.bit_length() - 1
        same = (row >> shift) == (col >> shift)
        attn = attn + jnp.where(same, _dot_nt(qt, kt), 0.0)
        half //= 2
        yield None
    o = o + _dot(attn.astype(BF16), vb)
    yield None

    st = st_ref[h]
    o = o + _dot_nt((q * jnp.exp(cum)).astype(BF16), st.astype(BF16))
    last = cum[tc - 1:tc, :]
    kd = (k * jnp.exp(last - cum)).astype(BF16)
    st_ref[h] = st * jnp.exp(last) + _dot_tn(vb, kd)
    yield o


def _gla_heads(inputs, st_ref, tc):
    heads = [_gla_head(q, k, v, logf, st_ref, h, tc) for h, (q, k, v, logf) in enumerate(inputs)]
    outs = None
    while True:
        step = [next(head, StopIteration) for head in heads]
        if step[0] is StopIteration:
            return outs
        outs = step


def _gla_finish(o, nw, gate):
    return _rmsnorm(o, nw) * (gate * jax.nn.sigmoid(gate))


def _gla_state_io(c, s0_ref, st_ref, heads):
    @pl.when(c == 0)
    def _():
        for h in range(heads):
            st_ref[h] = s0_ref[0, h].T


def _gla_state_out(c, sout_ref, st_ref, heads):
    @pl.when(c == pl.num_programs(1) - 1)
    def _():
        for h in range(heads):
            sout_ref[0, h] = st_ref[h].T


def _hgrn_body(q_ref, f_ref, v_ref, g_ref, lb_ref, nw_ref, s0_ref, y_ref, sout_ref, st_ref, *, tc):
    c = pl.program_id(1)
    _gla_state_io(c, s0_ref, st_ref, A_HEADS)
    inputs = []
    for h in range(A_HEADS):
        hs = slice(h * HEAD_W, (h + 1) * HEAD_W)
        zf = f_ref[:, hs]
        lb = lb_ref[:, hs]
        f = lb + (1.0 - lb) * jax.nn.sigmoid(zf)
        logf = jnp.log(jnp.maximum(f, F_MIN))
        k = (1.0 - lb) * jax.nn.sigmoid(-zf)
        zq = q_ref[:, hs]
        q = zq * jax.nn.sigmoid(zq) * (A_KDIM ** -0.5)
        inputs.append((q, k, v_ref[:, hs], logf))
    for h, o in enumerate(_gla_heads(inputs, st_ref, tc)):
        hs = slice(h * HEAD_W, (h + 1) * HEAD_W)
        y_ref[:, hs] = _gla_finish(o, nw_ref[...], g_ref[:, hs]).astype(BF16)
    _gla_state_out(c, sout_ref, st_ref, A_HEADS)


def _gla_body(q_ref, k_ref, v_ref, g_ref, r_ref, wup_ref, bup_ref, nw_ref, s0_ref,
              y_ref, sout_ref, st_ref, *, tc):
    c = pl.program_id(1)
    _gla_state_io(c, s0_ref, st_ref, B_HEADS)
    r = _dot_hi(r_ref[...], wup_ref[...]) + bup_ref[...]
    logf_all = (jnp.minimum(r, 0.0) - jnp.log1p(jnp.exp(-jnp.abs(r)))) / B_TAU
    heads = [slice(h * HEAD_W, (h + 1) * HEAD_W) for h in range(B_HEADS)]
    inputs = [(q_ref[:, hs] * (B_KDIM ** -0.5), k_ref[:, hs], v_ref[:, hs], logf_all[:, hs]) for hs in heads]
    for hs, o in zip(heads, _gla_heads(inputs, st_ref, tc)):
        y_ref[:, hs] = _gla_finish(o, nw_ref[...], g_ref[:, hs]).astype(BF16)
    _gla_state_out(c, sout_ref, st_ref, B_HEADS)


def _recurrent_mixer(mode, z, row0, nseq, t, s0, nw, extra):
    tc = min(128, t)
    nc = t // tc
    rb0 = row0 // tc
    zcol = (Z_A if mode == "hgrn" else Z_B) // 4

    def zspec(k):
        return pl.BlockSpec((tc, MIX_W), lambda b, c: (rb0 + b * nc + c, zcol + k))

    const = lambda b, c: (0, 0)
    state_spec = pl.BlockSpec((1, 4, HEAD_W, HEAD_W), lambda b, c: (b, 0, 0, 0))
    if mode == "hgrn":
        body = functools.partial(_hgrn_body, tc=tc)
        in_specs = [zspec(0), zspec(1), zspec(2), zspec(3),
                    pl.BlockSpec((1, MIX_W), const), pl.BlockSpec((1, HEAD_W), const), state_spec]
        args = (z, z, z, z, extra[0], nw, s0)
    else:
        body = functools.partial(_gla_body, tc=tc)
        in_specs = [zspec(0), zspec(1), zspec(2), zspec(3),
                    pl.BlockSpec((tc, LANES), lambda b, c: (rb0 + b * nc + c, Z_SMALL)),
                    pl.BlockSpec((LANES, MIX_W), const), pl.BlockSpec((1, MIX_W), const),
                    pl.BlockSpec((1, HEAD_W), const), state_spec]
        args = (z, z, z, z, z, extra[0], extra[1], nw, s0)
    return pl.pallas_call(
        body,
        grid=(nseq, nc),
        in_specs=in_specs,
        out_specs=[pl.BlockSpec((tc, MIX_W), lambda b, c: (b * nc + c, 0)), state_spec],
        out_shape=[jax.ShapeDtypeStruct((nseq * t, MIX_W), BF16),
                   jax.ShapeDtypeStruct((nseq, 4, HEAD_W, HEAD_W), F32)],
        scratch_shapes=[pltpu.VMEM((4, HEAD_W, HEAD_W), F32)],
        compiler_params=_params(("parallel", "arbitrary"), 32),
        name=mode + "_mixer",
    )(*args)


SUPER = 4
ATTN_BLOCKS = 2


def _dsa_t_body(qc_ref, qi_ref, sm_ref, kc_ref, vt_ref, ki_ref, o_ref,
                sk_ref, qit_ref, qct_ref, wrow_ref, acc_ref, ot_ref, s0_ref, s1_ref, p_ref,
                *, tq, past, nkeys, topk):
    i = pl.program_id(1)
    qpos0 = past + i * tq
    last_chunk = (qpos0 + tq - 1) // CHUNK
    n_adm = jnp.minimum((last_chunk + 1) * CHUNK, nkeys)
    nkb = (n_adm + KEY_BLOCK - 1) // KEY_BLOCK
    nsb = (nkb + SUPER - 1) // SUPER
    grp = C_HEADS // C_KV_HEADS

    qit = qi_ref[...].astype(F32).T
    for j in range(IDX_HEADS):
        qit_ref[:, j * tq:(j + 1) * tq] = qit[j * IDX_DIM:(j + 1) * IDX_DIM, :].astype(BF16)
    qct = qc_ref[...].astype(F32).T
    zeros = jnp.zeros((C_HDIM, tq), BF16)
    for hq in range(C_HEADS):
        g, r = divmod(hq, grp)
        for gg in range(C_KV_HEADS):
            val = qct[hq * C_HDIM:(hq + 1) * C_HDIM, :].astype(BF16) if gg == g else zeros
            qct_ref[g, gg * C_HDIM:(gg + 1) * C_HDIM, r * tq:(r + 1) * tq] = val
    wrow_ref[...] = sm_ref[...].T

    qchunk = (qpos0 + lax.broadcasted_iota(jnp.int32, (1, tq), 1)) >> 6
    ksub = lax.broadcasted_iota(jnp.int32, (KEY_BLOCK, 1), 0)

    def admissible(kb):
        kpos = kb * KEY_BLOCK + ksub
        return ((kpos >> 6) <= qchunk) & (kpos < nkeys)

    def score_step(kk, carry):
        for u in range(SUPER):
            kb = kk * SUPER + u
            kib = ki_ref[0, pl.ds(pl.multiple_of(kb * KEY_BLOCK, KEY_BLOCK), KEY_BLOCK), :]
            d = _dot(kib, qit_ref[...])
            sc = jnp.maximum(d[:, 0:tq], 0.0) * wrow_ref[SMALL_IW:SMALL_IW + 1, :]
            for j in range(1, IDX_HEADS):
                sc = sc + jnp.maximum(d[:, j * tq:(j + 1) * tq], 0.0) * wrow_ref[SMALL_IW + j:SMALL_IW + j + 1, :]
            bits = pltpu.bitcast(sc * IDX_SCALE + 0.0, jnp.int32)
            key = jnp.where(admissible(kb), jnp.where(bits < 0, bits ^ 0x7FFFFFFF, bits), INT_MIN)
            sk_ref[kb] = key
        return carry

    lax.fori_loop(0, nsb, score_step, 0)

    def fold(hit):
        return jnp.sum(hit.reshape(KEY_BLOCK // 8, 8, tq), axis=0)

    def count(pred):
        def step(kk, acc):
            for u in range(SUPER):
                kb = kk * SUPER + u
                acc = acc + fold(jnp.where(pred(sk_ref[kb], kb), 1.0, 0.0))
            return acc
        acc = lax.fori_loop(0, nsb, step, jnp.zeros((8, tq), F32))
        return jnp.sum(acc, axis=0, keepdims=True)

    kf = float(topk)
    c0 = count(lambda s, kb: s >= 0)
    t0 = jnp.where(c0 >= kf, 0, INT_MIN).astype(jnp.int32)

    def bit_step(it, t):
        cand = t | jnp.left_shift(jnp.int32(1), 30 - it)
        return jnp.where(count(lambda s, kb: s >= cand) >= kf, cand, t)
    thr = lax.fori_loop(0, 31, bit_step, t0)
    thr = jnp.maximum(thr, INT_MIN + 1)

    c_ge = count(lambda s, kb: s >= thr)
    nbits = max(1, (nkeys - 1).bit_length())

    @pl.when(jnp.max(c_ge) > kf)
    def _():
        need = kf - count(lambda s, kb: s > thr)

        def cut_step(it, cut):
            cand = cut | jnp.left_shift(jnp.int32(1), nbits - 1 - it)
            c = count(lambda s, kb: (s == thr) & (kb * KEY_BLOCK + ksub < cand))
            return jnp.where(c < need, cand, cut)
        cut = lax.fori_loop(0, nbits, cut_step, jnp.zeros((1, tq), jnp.int32))

        def strike(kb, carry):
            key = sk_ref[kb]
            sk_ref[kb] = jnp.where((key == thr) & (kb * KEY_BLOCK + ksub > cut), INT_MIN, key)
            return carry
        lax.fori_loop(0, nsb * SUPER, strike, 0)

    acc_ref[...] = jnp.zeros(acc_ref.shape, F32)
    init = tuple(jnp.full((1, grp * tq), NEG, F32) for _ in range(C_KV_HEADS))

    span = ATTN_BLOCKS * KEY_BLOCK
    n_attn = nsb * (SUPER // ATTN_BLOCKS)

    s_bufs = (s0_ref, s1_ref)

    def qk_scores(kk, half):
        kblk = kc_ref[0, pl.ds(pl.multiple_of(kk * span, span), span), :]
        bias = jnp.concatenate(
            [jnp.where(sk_ref[kk * ATTN_BLOCKS + u] >= thr, 0.0, NEG) for u in range(ATTN_BLOCKS)], axis=0)
        bias = jnp.concatenate([bias] * grp, axis=1)
        for g in range(C_KV_HEADS):
            s = _dot(kblk, qct_ref[g]) + bias
            for r in range(grp):
                s_bufs[half][g, r] = s[:, r * tq:(r + 1) * tq]

    def softmax_pv(kk, half, m):
        m = list(m)
        for g in range(C_KV_HEADS):
            m_cols, alpha_cols = [], []
            for r in range(grp):
                cols = slice(r * tq, (r + 1) * tq)
                sb = s_bufs[half][g, r]
                m_old = m[g][:, cols]
                m_new = jnp.maximum(m_old, jnp.max(sb, axis=0, keepdims=True))
                p_ref[half, g, r] = jnp.exp(sb - m_new).astype(BF16)
                m_cols.append(m_new)
                alpha_cols.append(jnp.exp(m_old - m_new))
            rows = slice(g * VT_ROWS, (g + 1) * VT_ROWS)
            pv = None
            for u in range(ATTN_BLOCKS):
                keys_u = slice(u * KEY_BLOCK, (u + 1) * KEY_BLOCK)
                p_u = jnp.concatenate([p_ref[half, g, r, keys_u, :] for r in range(grp)], axis=1)
                d = _dot(vt_ref[0, kk * ATTN_BLOCKS + u, rows, :], p_u)
                pv = d if pv is None else pv + d
            for r in range(grp):
                acc_ref[g, r] = alpha_cols[r] * acc_ref[g, r] + pv[:, r * tq:(r + 1) * tq]
            m[g] = jnp.concatenate(m_cols, axis=1)
        return tuple(m)

    qk_scores(0, 0)

    def pair_step(j, m):
        qk_scores(2 * j + 1, 1)
        m = softmax_pv(2 * j, 0, m)
        qk_scores(jnp.minimum(2 * j + 2, n_attn - 2), 0)
        return softmax_pv(2 * j + 1, 1, m)

    lax.fori_loop(0, n_attn // 2, pair_step, init)
    for hq in range(C_HEADS):
        g, r = divmod(hq, grp)
        ot_ref[hq * C_HDIM:(hq + 1) * C_HDIM, :] = acc_ref[g, r, 0:C_HDIM, :] / acc_ref[g, r, C_HDIM:C_HDIM + 1, :]
    o_ref[...] = ot_ref[...].T.astype(BF16)


def _dsa_t(qc, qi, small, small_col, keys, vals_t, kidx, nseq, t, past, nkeys):
    tq = KEY_BLOCK
    nq = t // tq
    lp = keys.shape[1]
    assert t % tq == 0 and lp % (SUPER * KEY_BLOCK) == 0
    topk = min(TOPK_MAX, nkeys // 4)
    grp = C_HEADS // C_KV_HEADS
    qrow = lambda b, i: (b * nq + i, 0)
    return pl.pallas_call(
        functools.partial(_dsa_t_body, tq=tq, past=past, nkeys=nkeys, topk=topk),
        grid=(nseq, nq),
        in_specs=[
            pl.BlockSpec((tq, C_WIDTH), qrow),
            pl.BlockSpec((tq, IDX_HEADS * IDX_DIM), qrow),
            pl.BlockSpec((tq, LANES), lambda b, i: (b * nq + i, small_col)),
            pl.BlockSpec((1, lp, LANES), lambda b, i: (b, 0, 0)),
            pl.BlockSpec((1, lp // KEY_BLOCK, C_KV_HEADS * VT_ROWS, KEY_BLOCK), lambda b, i: (b, 0, 0, 0)),
            pl.BlockSpec((1, lp, IDX_DIM), lambda b, i: (b, 0, 0)),
        ],
        out_specs=pl.BlockSpec((tq, C_WIDTH), lambda b, i: (b * nq + i, 0)),
        out_shape=jax.ShapeDtypeStruct((nseq * t, C_WIDTH), BF16),
        scratch_shapes=[
            pltpu.VMEM((lp // KEY_BLOCK, KEY_BLOCK, tq), jnp.int32),
            pltpu.VMEM((IDX_DIM, IDX_HEADS * tq), BF16),
            pltpu.VMEM((C_KV_HEADS, C_KV_HEADS * C_HDIM, grp * tq), BF16),
            pltpu.VMEM((LANES, tq), F32),
            pltpu.VMEM((C_KV_HEADS, grp, VT_ROWS, tq), F32),
            pltpu.VMEM((C_WIDTH, tq), F32),
            pltpu.VMEM((C_KV_HEADS, grp, ATTN_BLOCKS * KEY_BLOCK, tq), F32),
            pltpu.VMEM((C_KV_HEADS, grp, ATTN_BLOCKS * KEY_BLOCK, tq), F32),
            pltpu.VMEM((2, C_KV_HEADS, grp, ATTN_BLOCKS * KEY_BLOCK, tq), BF16),
        ],
        compiler_params=_params(("parallel", "arbitrary"), 48),
        name="dsa_mixer_t",
    )(qc, qi, small, keys, vals_t, kidx)


def _merge_body(h_ref, yap_ref, yas_ref, ybp_ref, ybs_ref, ocp_ref, ocs_ref, ga_ref, gb_ref, gc_ref,
                wa_ref, wb_ref, wc_ref, wo_ref, o_ref, *, prompt_tiles):
    is_prompt = pl.program_id(0) < prompt_tiles
    pick = lambda p_ref, s_ref: jnp.where(is_prompt, p_ref[...], s_ref[...])
    merged = (ga_ref[...].astype(F32) * _dot(pick(yap_ref, yas_ref), wa_ref[...])
              + gb_ref[...].astype(F32) * _dot(pick(ybp_ref, ybs_ref), wb_ref[...])
              + gc_ref[...].astype(F32) * _dot(pick(ocp_ref, ocs_ref), wc_ref[...]))
    o_ref[...] = h_ref[...] + _dot(merged.astype(BF16), wo_ref[...])


def _merge(h, ya, yb, oc, gates, wa, wb, wc, wo):
    n, d = h.shape
    tm = TOKEN_TILE
    pt = ya[0].shape[0] // tm
    assert ya[0].shape[0] % tm == 0 and ya[1].shape[0] % tm == 0 and n == ya[0].shape[0] + ya[1].shape[0]
    row = lambda i: (i, 0)
    const = lambda i: (0, 0)
    prow = lambda i: (jnp.minimum(i, pt - 1), 0)
    srow = lambda i: (jnp.maximum(i - pt, 0), 0)
    pair = [pl.BlockSpec((tm, MIX_W), prow), pl.BlockSpec((tm, MIX_W), srow)]
    return pl.pallas_call(
        functools.partial(_merge_body, prompt_tiles=pt),
        grid=(n // tm,),
        in_specs=[
            pl.BlockSpec((tm, d), row),
            *pair, *pair, *pair,
            pl.BlockSpec((tm, d), lambda i: (i, 0)),
            pl.BlockSpec((tm, d), lambda i: (i, 1)),
            pl.BlockSpec((tm, d), lambda i: (i, 2)),
            pl.BlockSpec((MIX_W, d), const),
            pl.BlockSpec((MIX_W, d), const),
            pl.BlockSpec((MIX_W, d), const),
            pl.BlockSpec((d, d), const),
        ],
        out_specs=pl.BlockSpec((tm, d), row),
        out_shape=jax.ShapeDtypeStruct((n, d), F32),
        compiler_params=_params(("parallel",), 48),
        name="gated_merge",
    )(h, *ya, *yb, *oc, gates, gates, gates, wa, wb, wc, wo)


def _pad_heads(w, heads, dim):
    lead = w.shape[:-1]
    w = w.reshape(*lead, heads, dim)
    w = jnp.pad(w, [(0, 0)] * len(lead) + [(0, 0), (0, HEAD_W - dim)])
    return w.reshape(*lead, heads * HEAD_W)


def _layout_w_in(w):
    widths = (512, 512, 512, 512, 256, 256, 512, 16, 512, 512, 128, 128, 512, 64, 8, 1024, 1024, 1024)
    w = w.astype(BF16)
    parts, o = [], 0
    for wd in widths:
        parts.append(w[:, o:o + wd])
        o += wd
    (a_q, a_f, a_i, a_g, b_q, b_k, b_v, b_r, b_g, c_q, c_k, c_v, i_q, i_k, i_w, g_a, g_b, g_c) = parts
    d = w.shape[0]
    small = jnp.concatenate([b_r, i_w, jnp.zeros((d, LANES - 24), w.dtype)], axis=1)
    i_k = jnp.pad(i_k, ((0, 0), (0, LANES - IDX_DIM)))
    cols = [g_a, g_b, g_c, a_q, a_f, a_i, a_g,
            _pad_heads(b_q, B_HEADS, B_KDIM), _pad_heads(b_k, B_HEADS, B_KDIM), b_v, b_g,
            small, c_q, i_q, c_k, c_v, i_k]
    out = jnp.concatenate(cols, axis=1)
    assert out.shape[1] == Z_WIDTH
    return out


def _rope_tables(pos):
    half = ROPE_DIMS // 2
    inv = ROPE_THETA ** (-jnp.arange(half, dtype=F32) * (2.0 / ROPE_DIMS))
    ang = pos.astype(F32)[:, None] * inv[None, :]
    cos, sin = jnp.cos(ang), jnp.sin(ang)
    n = pos.shape[0]
    ones = jnp.ones((n, C_HDIM - ROPE_DIMS), F32)
    zeros = jnp.zeros((n, C_HDIM - ROPE_DIMS), F32)
    zh = jnp.zeros((n, half), F32)
    c = jnp.concatenate([cos, cos, ones], axis=1)
    s_lo = jnp.concatenate([-sin, zh, zeros], axis=1)
    s_hi = jnp.concatenate([zh, sin, zeros], axis=1)
    two = lambda a: jnp.concatenate([a, a], axis=1)
    return two(c), two(s_lo), two(s_hi)


def kernel(x_prompt, x_sample, state_hgrn, state_gla, cache_k, cache_v, cache_kidx, hgrn_lb, w_in, w_gla_up, b_gla, norm_hgrn, norm_gla, w_br_a, w_br_b, w_br_c, w_out, norm_ffn1, norm_mix, norm_ffn2, ffn1_w1, ffn1_w3, ffn1_w2, ffn2_w1, ffn2_w3, ffn2_w2, norm_final):
    bp, tp, d = x_prompt.shape
    bs, ts, _ = x_sample.shape
    past = cache_k.shape[2]
    n_p, n_s = bp * tp, bs * ts

    lb_sm = jax.nn.softmax(hgrn_lb.astype(F32), axis=0)
    lb_all = jnp.concatenate([jnp.zeros_like(lb_sm[:1]), jnp.cumsum(lb_sm[1:], axis=0)], axis=0)

    pos = jnp.concatenate([jnp.arange(tp, dtype=jnp.int32),
                           jnp.tile(past + jnp.arange(ts, dtype=jnp.int32), PROJ_TILE // ts)])
    tables = _rope_tables(pos)
    assert tp % PROJ_TILE == 0 and PROJ_TILE % ts == 0 and n_s % PROJ_TILE == 0
    table_tile = lambda i: (jnp.where(i < n_p // PROJ_TILE, i % (tp // PROJ_TILE), tp // PROJ_TILE), 0)

    hs = (x_prompt.reshape(n_p, d), x_sample.reshape(n_s, d))
    row2 = lambda a: a.reshape(1, -1)
    zero_state = jnp.zeros((bp, 4, HEAD_W, HEAD_W), F32)
    span_s = SUPER * KEY_BLOCK
    key_pad = -(-(past + ts) // span_s) * span_s - past - ts
    q_rep = KEY_BLOCK // ts

    ffn1_w1, ffn1_w3, ffn1_w2, ffn2_w1, ffn2_w3, ffn2_w2, w_br_a, w_br_b, w_br_c, w_out = (
        a.astype(BF16) for a in (ffn1_w1, ffn1_w3, ffn1_w2, ffn2_w1, ffn2_w3, ffn2_w2,
                                 w_br_a, w_br_b, w_br_c, w_out))

    outs = {k: [] for k in ("pa", "pb", "pk", "pv", "pki", "sa", "sb", "sk", "sv", "ski")}
    for l in range(DEPTH):
        bf = lambda a: a[l]
        h = _ffn(hs, row2(norm_ffn1[l]), bf(ffn1_w1), bf(ffn1_w3), bf(ffn1_w2), row2(norm_final), False, 0)
        gates, z, qc, qi, kc, vc, ki, kcb, vcb, kib, vt = _inproj(
            h, row2(norm_mix[l]), _layout_w_in(w_in[l]), tables, table_tile)

        lb = row2(lb_all[l])
        nwa, nwb = row2(norm_hgrn[l]), row2(norm_gla[l])
        wup = jnp.pad(_pad_heads(w_gla_up[l], B_HEADS, B_KDIM), ((0, LANES - B_GATE_RANK), (0, 0)))
        bup = row2(_pad_heads(b_gla[l], B_HEADS, B_KDIM))
        sb0 = jnp.pad(state_gla[l], ((0, 0), (0, 0), (0, HEAD_W - B_KDIM), (0, 0)))

        ya_p, sa_p = _recurrent_mixer("hgrn", z, 0, bp, tp, zero_state, nwa, (lb,))
        ya_s, sa_s = _recurrent_mixer("hgrn", z, n_p, bs, ts, state_hgrn[l], nwa, (lb,))
        yb_p, sb_p = _recurrent_mixer("gla", z, 0, bp, tp, zero_state, nwb, (wup, bup))
        yb_s, sb_s = _recurrent_mixer("gla", z, n_p, bs, ts, sb0, nwb, (wup, bup))

        seqs = lambda a, n, t: a.reshape(n, t, a.shape[-1])
        vt_p = vt[:n_p // KEY_BLOCK].reshape(bp, tp // KEY_BLOCK, C_KV_HEADS * VT_ROWS, KEY_BLOCK)
        oc_p = _dsa_t(qc, qi, z, Z_SMALL, seqs(kcb[:n_p], bp, tp), vt_p, seqs(kib[:n_p], bp, tp), bp, tp, 0, tp)

        def with_cache(cache, new):
            full = jnp.concatenate([cache.reshape(bs, past, -1).astype(BF16), seqs(new[n_p:], bs, ts)], axis=1)
            return jnp.pad(full, ((0, 0), (0, key_pad), (0, 0)))

        def fill_block(a):
            return jnp.tile(seqs(a[n_p:], bs, ts), (1, q_rep, 1)).reshape(bs * KEY_BLOCK, a.shape[-1])

        vals_s = with_cache(cache_v[l], vcb)
        vt_s = jnp.swapaxes(vals_s.reshape(bs, -1, KEY_BLOCK, LANES), 2, 3)
        ones = jnp.ones(vt_s.shape[:2] + (VT_ROWS - C_HDIM, KEY_BLOCK), BF16)
        vt_s = jnp.concatenate([vt_s[:, :, :C_HDIM], ones, vt_s[:, :, C_HDIM:], ones], axis=2)
        small_s = fill_block(z[:, Z_SMALL * LANES:(Z_SMALL + 1) * LANES])
        oc_s = _dsa_t(fill_block(qc), fill_block(qi), small_s, 0, with_cache(cache_k[l], kcb), vt_s,
                      with_cache(cache_kidx[l], kib), bs, KEY_BLOCK, past, past + ts)
        oc_s = oc_s.reshape(bs, KEY_BLOCK, C_WIDTH)[:, :ts].reshape(n_s, C_WIDTH)

        h = _merge(h, (ya_p, ya_s), (yb_p, yb_s), (oc_p, oc_s), gates,
                   bf(w_br_a), bf(w_br_b), bf(w_br_c), bf(w_out))
        last = l == DEPTH - 1
        hs = _ffn((h,), row2(norm_ffn2[l]), bf(ffn2_w1), bf(ffn2_w3), bf(ffn2_w2), row2(norm_final),
                  last, n_p if last else 0)
        hs = hs if last else (hs,)

        outs["pa"].append(sa_p)
        outs["sa"].append(sa_s)
        outs["pb"].append(sb_p[:, :, :B_KDIM, :])
        outs["sb"].append(sb_s[:, :, :B_KDIM, :])
        outs["pk"].append(kc[:n_p].reshape(bp, tp, C_KV_HEADS, C_HDIM))
        outs["pv"].append(vc[:n_p].reshape(bp, tp, C_KV_HEADS, C_HDIM))
        outs["pki"].append(ki[:n_p].reshape(bp, tp, IDX_DIM))
        outs["sk"].append(kc[n_p:].reshape(bs, ts, C_KV_HEADS, C_HDIM))
        outs["sv"].append(vc[n_p:].reshape(bs, ts, C_KV_HEADS, C_HDIM))
        outs["ski"].append(ki[n_p:].reshape(bs, ts, IDX_DIM))

    st = {k: jnp.stack(v) for k, v in outs.items()}
    return (hs[0].reshape(bp, tp, d), hs[1].reshape(bs, ts, d),
            st["pa"], st["pb"], st["pk"], st["pv"], st["pki"],
            st["sa"], st["sb"], st["sk"], st["sv"], st["ski"])
```

```python
import functools

import jax
import jax.numpy as jnp
from jax import lax
from jax.experimental import pallas as pl
from jax.experimental.pallas import tpu as pltpu

F32 = jnp.float32
BF16 = jnp.bfloat16

D_MODEL = 1024
DEPTH = 2
CHUNK = 64
EPS = 1e-6
NEG = -1e30
F_MIN = 1e-30
A_HEADS = 4
A_KDIM = 128
A_VDIM = 128
B_HEADS = 4
B_KDIM = 64
B_VDIM = 128
B_GATE_RANK = 16
B_TAU = 16.0
C_HEADS = 8
C_KV_HEADS = 2
C_HDIM = 64
C_WIDTH = C_HEADS * C_HDIM
IDX_HEADS = 8
IDX_DIM = 64
IDX_SCALE = (IDX_HEADS * IDX_DIM) ** -0.5
TOPK_MAX = 256
ROPE_THETA = 500000.0
ROPE_DIMS = C_HDIM // 4
PAIR_BLOCK = 4
D_FF = 2816

LANES = 128
HEAD_W = 128
MIX_W = 4 * HEAD_W
TOKEN_TILE = 512
PROJ_TILE = 256
KEY_BLOCK = 128
VT_ROWS = 80
INT_MIN = -(2 ** 31)

GATE_UNITS = 24
Z_A = 0
Z_B = 16
Z_SMALL = 32
Z_UNITS = 33
Z_CQ = GATE_UNITS + Z_UNITS
Z_IQ = Z_CQ + 4
Z_CK = Z_IQ + 4
Z_CV = Z_CK + 1
Z_IK = Z_CV + 1
Z_WIDTH = (Z_IK + 1) * LANES
SMALL_IW = B_GATE_RANK


def _params(sem, vmem_mb):
    return pltpu.CompilerParams(dimension_semantics=sem, vmem_limit_bytes=vmem_mb << 20)


def _dot(a, b):
    return jnp.dot(a, b, preferred_element_type=F32)


def _dot_nt(a, b):
    return lax.dot_general(a, b, (((1,), (1,)), ((), ())), preferred_element_type=F32)


def _dot_tn(a, b):
    return lax.dot_general(a, b, (((0,), (0,)), ((), ())), preferred_element_type=F32)


def _split3(x):
    h1 = x.astype(BF16)
    r1 = x - h1.astype(F32)
    h2 = r1.astype(BF16)
    h3 = (r1 - h2.astype(F32)).astype(BF16)
    return h1, h2, h3


def _dot_exact_lhs(m, x):
    h1, h2, h3 = _split3(x)
    return _dot(m, h1) + _dot(m, h2) + _dot(m, h3)


def _dot_hi(a, b):
    a1 = a.astype(BF16)
    a2 = (a - a1.astype(F32)).astype(BF16)
    b1 = b.astype(BF16)
    b2 = (b - b1.astype(F32)).astype(BF16)
    return _dot(a1, b1) + _dot(a1, b2) + _dot(a2, b1)


def _rmsnorm(x, g):
    return x * lax.rsqrt(jnp.mean(x * x, axis=-1, keepdims=True) + EPS) * g


def _ffn_body(*refs, n_in, n_out, prompt_tiles, final_norm):
    x_refs, (g_ref, w1_ref, w3_ref, w2_ref, gf_ref), o_refs = refs[:n_in], refs[n_in:n_in + 5], refs[n_in + 5:]
    is_prompt = pl.program_id(0) < prompt_tiles
    x = x_refs[0][...] if n_in == 1 else jnp.where(is_prompt, x_refs[0][...], x_refs[1][...])
    u = _rmsnorm(x, g_ref[...]).astype(BF16)
    a = _dot(u, w1_ref[...])
    b = _dot(u, w3_ref[...])
    hid = (a * jax.nn.sigmoid(a) * b).astype(BF16)
    out = x + 0.5 * _dot(hid, w2_ref[...])
    if final_norm:
        out = _rmsnorm(out, gf_ref[...])
    if n_out == 1:
        o_refs[0][...] = out
    else:
        @pl.when(is_prompt)
        def _():
            o_refs[0][...] = out

        @pl.when(jnp.logical_not(is_prompt))
        def _():
            o_refs[1][...] = out


def _ffn(xs, g, w1, w3, w2, gf, final_norm, split_out):
    d = xs[0].shape[1]
    tm = TOKEN_TILE
    sizes = [x.shape[0] for x in xs]
    n = sum(sizes)
    assert all(sz % tm == 0 for sz in sizes)
    pt = sizes[0] // tm if len(xs) == 2 else split_out // tm
    prow = lambda i: (jnp.minimum(i, pt - 1), 0)
    srow = lambda i: (jnp.maximum(i - pt, 0), 0)
    row = lambda i: (i, 0)
    const = lambda i: (0, 0)
    resident = pl.Buffered(1)
    x_specs = [pl.BlockSpec((tm, d), row)] if len(xs) == 1 else \
        [pl.BlockSpec((tm, d), prow), pl.BlockSpec((tm, d), srow)]
    if split_out:
        out_specs = [pl.BlockSpec((tm, d), prow), pl.BlockSpec((tm, d), srow)]
        out_shape = [jax.ShapeDtypeStruct((split_out, d), F32), jax.ShapeDtypeStruct((n - split_out, d), F32)]
    else:
        out_specs = pl.BlockSpec((tm, d), row)
        out_shape = jax.ShapeDtypeStruct((n, d), F32)
    return pl.pallas_call(
        functools.partial(_ffn_body, n_in=len(xs), n_out=2 if split_out else 1, prompt_tiles=pt,
                          final_norm=final_norm),
        grid=(n // tm,),
        in_specs=x_specs + [
            pl.BlockSpec((1, d), const),
            pl.BlockSpec(w1.shape, const, pipeline_mode=resident),
            pl.BlockSpec(w3.shape, const, pipeline_mode=resident),
            pl.BlockSpec(w2.shape, const, pipeline_mode=resident),
            pl.BlockSpec((1, d), const),
        ],
        out_specs=out_specs,
        out_shape=out_shape,
        compiler_params=_params(("arbitrary",), 48),
        name="ffn_half_step",
    )(*xs, g, w1, w3, w2, gf)


def _rope(x, c, s_lo, s_hi):
    w = x.shape[1]
    rep = w // LANES
    if rep > 1:
        c = jnp.concatenate([c] * rep, axis=1)
        s_lo = jnp.concatenate([s_lo] * rep, axis=1)
        s_hi = jnp.concatenate([s_hi] * rep, axis=1)
    half = ROPE_DIMS // 2
    return x * c + pltpu.roll(x, half, 1) * s_hi + pltpu.roll(x, w - half, 1) * s_lo


def _inproj_body(x_ref, g_ref, w_ref, c_ref, slo_ref, shi_ref,
                 gates_ref, z_ref, qc_ref, qi_ref, kc_ref, vc_ref, ki_ref, kcb_ref, vcb_ref, kib_ref, vt_ref):
    u = _rmsnorm(x_ref[...], g_ref[...]).astype(BF16)
    zf = _dot(u, w_ref[...])
    gates_ref[...] = jax.nn.sigmoid(zf[:, :GATE_UNITS * LANES]).astype(BF16)
    z_ref[...] = zf[:, GATE_UNITS * LANES:(GATE_UNITS + Z_UNITS) * LANES]

    unit = lambda first, count: zf[:, first * LANES:(first + count) * LANES]
    c, s_lo, s_hi = c_ref[...], slo_ref[...], shi_ref[...]
    qc_ref[...] = (_rope(unit(Z_CQ, 4), c, s_lo, s_hi) * (C_HDIM ** -0.5)).astype(BF16)
    qi_ref[...] = _rope(unit(Z_IQ, 4), c, s_lo, s_hi).astype(BF16)
    kc = _rope(unit(Z_CK, 1), c, s_lo, s_hi)
    vc = unit(Z_CV, 1)
    ki = _rope(unit(Z_IK, 1), c, s_lo, s_hi)[:, :IDX_DIM]
    kc_ref[...] = kc
    vc_ref[...] = vc
    ki_ref[...] = ki
    kcb_ref[...] = kc.astype(BF16)
    vcb_ref[...] = vc.astype(BF16)
    kib_ref[...] = ki.astype(BF16)
    ones = jnp.ones((VT_ROWS - C_HDIM, KEY_BLOCK), BF16)
    for kk in range(vt_ref.shape[0]):
        vt = vc[kk * KEY_BLOCK:(kk + 1) * KEY_BLOCK, :].T.astype(BF16)
        vt_ref[kk] = jnp.concatenate([vt[:C_HDIM], ones, vt[C_HDIM:], ones], axis=0)


def _inproj(h, g, w, tables, table_tile):
    n, d = h.shape
    tm = PROJ_TILE
    assert n % tm == 0
    row = lambda i: (i, 0)
    return pl.pallas_call(
        _inproj_body,
        grid=(n // tm,),
        in_specs=[
            pl.BlockSpec((tm, d), row),
            pl.BlockSpec((1, d), lambda i: (0, 0)),
            pl.BlockSpec((d, Z_WIDTH), lambda i: (0, 0), pipeline_mode=pl.Buffered(1)),
            pl.BlockSpec((tm, LANES), table_tile),
            pl.BlockSpec((tm, LANES), table_tile),
            pl.BlockSpec((tm, LANES), table_tile),
        ],
        out_specs=[
            pl.BlockSpec((tm, GATE_UNITS * LANES), row),
            pl.BlockSpec((tm, Z_UNITS * LANES), row),
            pl.BlockSpec((tm, C_WIDTH), row),
            pl.BlockSpec((tm, IDX_HEADS * IDX_DIM), row),
            pl.BlockSpec((tm, LANES), row),
            pl.BlockSpec((tm, LANES), row),
            pl.BlockSpec((tm, IDX_DIM), row),
            pl.BlockSpec((tm, LANES), row),
            pl.BlockSpec((tm, LANES), row),
            pl.BlockSpec((tm, IDX_DIM), row),
            pl.BlockSpec((tm // KEY_BLOCK, C_KV_HEADS * VT_ROWS, KEY_BLOCK), lambda i: (i, 0, 0)),
        ],
        out_shape=[
            jax.ShapeDtypeStruct((n, GATE_UNITS * LANES), BF16),
            jax.ShapeDtypeStruct((n, Z_UNITS * LANES), F32),
            jax.ShapeDtypeStruct((n, C_WIDTH), BF16),
            jax.ShapeDtypeStruct((n, IDX_HEADS * IDX_DIM), BF16),
            jax.ShapeDtypeStruct((n, LANES), F32),
            jax.ShapeDtypeStruct((n, LANES), F32),
            jax.ShapeDtypeStruct((n, IDX_DIM), F32),
            jax.ShapeDtypeStruct((n, LANES), BF16),
            jax.ShapeDtypeStruct((n, LANES), BF16),
            jax.ShapeDtypeStruct((n, IDX_DIM), BF16),
            jax.ShapeDtypeStruct((n // KEY_BLOCK, C_KV_HEADS * VT_ROWS, KEY_BLOCK), BF16),
        ],
        compiler_params=_params(("parallel",), 48),
        name="in_projection",
    )(h, g, w, *tables)


def _gla_head(q, k, v, logf, st_ref, h, tc):
    row = lax.broadcasted_iota(jnp.int32, (tc, tc), 0)
    col = lax.broadcasted_iota(jnp.int32, (tc, tc), 1)
    tril = (col <= row).astype(BF16)
    cum = _dot_exact_lhs(tril, logf)
    yield None

    tile = 2 * PAIR_BLOCK
    nt = tc // tile
    b3 = cum.reshape(nt, tile, HEAD_W)
    q3 = q.reshape(nt, tile, HEAD_W)
    k3 = k.reshape(nt, tile, HEAD_W)
    v3 = v.reshape(nt, tile, HEAD_W)
    srow = lax.broadcasted_iota(jnp.int32, (nt, tile, 1), 1)
    low = srow < PAIR_BLOCK
    tloc = srow & (PAIR_BLOCK - 1)

    def block_row(x3, j):
        return jnp.where(low, x3[:, j:j + 1, :], x3[:, PAIR_BLOCK + j:PAIR_BLOCK + j + 1, :])

    o3 = jnp.zeros((nt, tile, HEAD_W), F32)
    for j in range(PAIR_BLOCK):
        causal = tloc >= j
        decay = jnp.exp(jnp.where(causal, b3 - block_row(b3, j), 0.0))
        w = jnp.sum(q3 * block_row(k3, j) * decay, axis=-1, keepdims=True)
        o3 = o3 + jnp.where(causal, w, 0.0) * block_row(v3, j)
        yield None
    o = o3.reshape(tc, HEAD_W)
    yield None

    vb = v.astype(BF16)
    attn = jnp.zeros((tc, tc), F32)
    half = tc // 2
    while half >= PAIR_BLOCK:
        blk = 2 * half
        nblk = tc // blk
        bl = cum.reshape(nblk, blk, HEAD_W)
        x = bl - bl[:, half - 1:half, :]
        second = lax.broadcasted_iota(jnp.int32, (nblk, blk, 1), 1) >= half
        e = jnp.exp(jnp.where(second, x, -x))
        qt = jnp.where(second, q.reshape(nblk, blk, HEAD_W) * e, 0.0).reshape(tc, HEAD_W).astype(BF16)
        kt = jnp.where(second, 0.0, k.reshape(nblk, blk, HEAD_W) * e).reshape(tc, HEAD_W).astype(BF16)
        shift = blk.bit_length() - 1
        same = (row >> shift) == (col >> shift)
        attn = attn + jnp.where(same, _dot_nt(qt, kt), 0.0)
        half //= 2
        yield None
    o = o + _dot(attn.astype(BF16), vb)
    yield None

    st = st_ref[h]
    o = o + _dot_nt((q * jnp.exp(cum)).astype(BF16), st.astype(BF16))
    last = cum[tc - 1:tc, :]
    kd = (k * jnp.exp(last - cum)).astype(BF16)
    st_ref[h] = st * jnp.exp(last) + _dot_tn(vb, kd)
    yield o


def _gla_heads(inputs, st_ref, tc):
    heads = [_gla_head(q, k, v, logf, st_ref, h, tc) for h, (q, k, v, logf) in enumerate(inputs)]
    outs = None
    while True:
        step = [next(head, StopIteration) for head in heads]
        if step[0] is StopIteration:
            return outs
        outs = step


def _gla_finish(o, nw, gate):
    return _rmsnorm(o, nw) * (gate * jax.nn.sigmoid(gate))


def _gla_state_io(c, s0_ref, st_ref, heads):
    @pl.when(c == 0)
    def _():
        for h in range(heads):
            st_ref[h] = s0_ref[0, h].T


def _gla_state_out(c, sout_ref, st_ref, heads):
    @pl.when(c == pl.num_programs(1) - 1)
    def _():
        for h in range(heads):
            sout_ref[0, h] = st_ref[h].T


def _hgrn_body(q_ref, f_ref, v_ref, g_ref, lb_ref, nw_ref, s0_ref, y_ref, sout_ref, st_ref, *, tc):
    c = pl.program_id(1)
    _gla_state_io(c, s0_ref, st_ref, A_HEADS)
    inputs = []
    for h in range(A_HEADS):
        hs = slice(h * HEAD_W, (h + 1) * HEAD_W)
        zf = f_ref[:, hs]
        lb = lb_ref[:, hs]
        f = lb + (1.0 - lb) * jax.nn.sigmoid(zf)
        logf = jnp.log(jnp.maximum(f, F_MIN))
        k = (1.0 - lb) * jax.nn.sigmoid(-zf)
        zq = q_ref[:, hs]
        q = zq * jax.nn.sigmoid(zq) * (A_KDIM ** -0.5)
        inputs.append((q, k, v_ref[:, hs], logf))
    for h, o in enumerate(_gla_heads(inputs, st_ref, tc)):
        hs = slice(h * HEAD_W, (h + 1) * HEAD_W)
        y_ref[:, hs] = _gla_finish(o, nw_ref[...], g_ref[:, hs]).astype(BF16)
    _gla_state_out(c, sout_ref, st_ref, A_HEADS)


def _gla_body(q_ref, k_ref, v_ref, g_ref, r_ref, wup_ref, bup_ref, nw_ref, s0_ref,
              y_ref, sout_ref, st_ref, *, tc):
    c = pl.program_id(1)
    _gla_state_io(c, s0_ref, st_ref, B_HEADS)
    r = _dot_hi(r_ref[...], wup_ref[...]) + bup_ref[...]
    logf_all = (jnp.minimum(r, 0.0) - jnp.log1p(jnp.exp(-jnp.abs(r)))) / B_TAU
    heads = [slice(h * HEAD_W, (h + 1) * HEAD_W) for h in range(B_HEADS)]
    inputs = [(q_ref[:, hs] * (B_KDIM ** -0.5), k_ref[:, hs], v_ref[:, hs], logf_all[:, hs]) for hs in heads]
    for hs, o in zip(heads, _gla_heads(inputs, st_ref, tc)):
        y_ref[:, hs] = _gla_finish(o, nw_ref[...], g_ref[:, hs]).astype(BF16)
    _gla_state_out(c, sout_ref, st_ref, B_HEADS)


def _recurrent_mixer(mode, z, row0, nseq, t, s0, nw, extra):
    tc = min(128, t)
    nc = t // tc
    rb0 = row0 // tc
    zcol = (Z_A if mode == "hgrn" else Z_B) // 4

    def zspec(k):
        return pl.BlockSpec((tc, MIX_W), lambda b, c: (rb0 + b * nc + c, zcol + k))

    const = lambda b, c: (0, 0)
    state_spec = pl.BlockSpec((1, 4, HEAD_W, HEAD_W), lambda b, c: (b, 0, 0, 0))
    if mode == "hgrn":
        body = functools.partial(_hgrn_body, tc=tc)
        in_specs = [zspec(0), zspec(1), zspec(2), zspec(3),
                    pl.BlockSpec((1, MIX_W), const), pl.BlockSpec((1, HEAD_W), const), state_spec]
        args = (z, z, z, z, extra[0], nw, s0)
    else:
        body = functools.partial(_gla_body, tc=tc)
        in_specs = [zspec(0), zspec(1), zspec(2), zspec(3),
                    pl.BlockSpec((tc, LANES), lambda b, c: (rb0 + b * nc + c, Z_SMALL)),
                    pl.BlockSpec((LANES, MIX_W), const), pl.BlockSpec((1, MIX_W), const),
                    pl.BlockSpec((1, HEAD_W), const), state_spec]
        args = (z, z, z, z, z, extra[0], extra[1], nw, s0)
    return pl.pallas_call(
        body,
        grid=(nseq, nc),
        in_specs=in_specs,
        out_specs=[pl.BlockSpec((tc, MIX_W), lambda b, c: (b * nc + c, 0)), state_spec],
        out_shape=[jax.ShapeDtypeStruct((nseq * t, MIX_W), BF16),
                   jax.ShapeDtypeStruct((nseq, 4, HEAD_W, HEAD_W), F32)],
        scratch_shapes=[pltpu.VMEM((4, HEAD_W, HEAD_W), F32)],
        compiler_params=_params(("parallel", "arbitrary"), 32),
        name=mode + "_mixer",
    )(*args)


SUPER = 4
ATTN_BLOCKS = 2


def _dsa_t_body(qc_ref, qi_ref, sm_ref, kc_ref, vt_ref, ki_ref, o_ref,
                sk_ref, qit_ref, qct_ref, wrow_ref, acc_ref, ot_ref, s0_ref, s1_ref,
                *, tq, past, nkeys, topk):
    i = pl.program_id(1)
    qpos0 = past + i * tq
    last_chunk = (qpos0 + tq - 1) // CHUNK
    n_adm = jnp.minimum((last_chunk + 1) * CHUNK, nkeys)
    nkb = (n_adm + KEY_BLOCK - 1) // KEY_BLOCK
    nsb = (nkb + SUPER - 1) // SUPER
    grp = C_HEADS // C_KV_HEADS

    qit = qi_ref[...].astype(F32).T
    for j in range(IDX_HEADS):
        qit_ref[:, j * tq:(j + 1) * tq] = qit[j * IDX_DIM:(j + 1) * IDX_DIM, :].astype(BF16)
    qct = qc_ref[...].astype(F32).T
    zeros = jnp.zeros((C_HDIM, tq), BF16)
    for hq in range(C_HEADS):
        g, r = divmod(hq, grp)
        for gg in range(C_KV_HEADS):
            val = qct[hq * C_HDIM:(hq + 1) * C_HDIM, :].astype(BF16) if gg == g else zeros
            qct_ref[g, gg * C_HDIM:(gg + 1) * C_HDIM, r * tq:(r + 1) * tq] = val
    wrow_ref[...] = sm_ref[...].T

    qchunk = (qpos0 + lax.broadcasted_iota(jnp.int32, (1, tq), 1)) >> 6
    ksub = lax.broadcasted_iota(jnp.int32, (KEY_BLOCK, 1), 0)

    def admissible(kb):
        kpos = kb * KEY_BLOCK + ksub
        return ((kpos >> 6) <= qchunk) & (kpos < nkeys)

    def score_step(kk, carry):
        for u in range(SUPER):
            kb = kk * SUPER + u
            kib = ki_ref[0, pl.ds(pl.multiple_of(kb * KEY_BLOCK, KEY_BLOCK), KEY_BLOCK), :]
            d = _dot(kib, qit_ref[...])
            sc = jnp.maximum(d[:, 0:tq], 0.0) * wrow_ref[SMALL_IW:SMALL_IW + 1, :]
            for j in range(1, IDX_HEADS):
                sc = sc + jnp.maximum(d[:, j * tq:(j + 1) * tq], 0.0) * wrow_ref[SMALL_IW + j:SMALL_IW + j + 1, :]
            bits = pltpu.bitcast(sc * IDX_SCALE + 0.0, jnp.int32)
            key = jnp.where(admissible(kb), jnp.where(bits < 0, bits ^ 0x7FFFFFFF, bits), INT_MIN)
            sk_ref[kb] = key
        return carry

    lax.fori_loop(0, nsb, score_step, 0)

    def fold(hit):
        return jnp.sum(hit.reshape(KEY_BLOCK // 8, 8, tq), axis=0)

    def count(pred):
        def step(kk, acc):
            for u in range(SUPER):
                kb = kk * SUPER + u
                acc = acc + fold(jnp.where(pred(sk_ref[kb], kb), 1.0, 0.0))
            return acc
        acc = lax.fori_loop(0, nsb, step, jnp.zeros((8, tq), F32))
        return jnp.sum(acc, axis=0, keepdims=True)

    kf = float(topk)
    c0 = count(lambda s, kb: s >= 0)
    t0 = jnp.where(c0 >= kf, 0, INT_MIN).astype(jnp.int32)

    def bit_step(it, t):
        cand = t | jnp.left_shift(jnp.int32(1), 30 - it)
        return jnp.where(count(lambda s, kb: s >= cand) >= kf, cand, t)
    thr = lax.fori_loop(0, 31, bit_step, t0)
    thr = jnp.maximum(thr, INT_MIN + 1)

    c_ge = count(lambda s, kb: s >= thr)
    nbits = max(1, (nkeys - 1).bit_length())

    @pl.when(jnp.max(c_ge) > kf)
    def _():
        need = kf - count(lambda s, kb: s > thr)

        def cut_step(it, cut):
            cand = cut | jnp.left_shift(jnp.int32(1), nbits - 1 - it)
            c = count(lambda s, kb: (s == thr) & (kb * KEY_BLOCK + ksub < cand))
            return jnp.where(c < need, cand, cut)
        cut = lax.fori_loop(0, nbits, cut_step, jnp.zeros((1, tq), jnp.int32))

        def strike(kb, carry):
            key = sk_ref[kb]
            sk_ref[kb] = jnp.where((key == thr) & (kb * KEY_BLOCK + ksub > cut), INT_MIN, key)
            return carry
        lax.fori_loop(0, nsb * SUPER, strike, 0)

    acc_ref[...] = jnp.zeros(acc_ref.shape, F32)
    init = tuple(jnp.full((1, grp * tq), NEG, F32) for _ in range(C_KV_HEADS))

    span = ATTN_BLOCKS * KEY_BLOCK
    n_attn = nsb * (SUPER // ATTN_BLOCKS)

    s_bufs = (s0_ref, s1_ref)

    def qk_scores(kk, half):
        kblk = kc_ref[0, pl.ds(pl.multiple_of(kk * span, span), span), :]
        bias = jnp.concatenate(
            [jnp.where(sk_ref[kk * ATTN_BLOCKS + u] >= thr, 0.0, NEG) for u in range(ATTN_BLOCKS)], axis=0)
        bias = jnp.concatenate([bias] * grp, axis=1)
        for g in range(C_KV_HEADS):
            s = _dot(kblk, qct_ref[g]) + bias
            for r in range(grp):
                s_bufs[half][g, r] = s[:, r * tq:(r + 1) * tq]

    def softmax_pv(kk, half, m):
        m = list(m)
        pending = []
        for g in range(C_KV_HEADS):
            m_cols, alpha_cols, p_cols = [], [], []
            for r in range(grp):
                cols = slice(r * tq, (r + 1) * tq)
                sb = s_bufs[half][g, r]
                m_old = m[g][:, cols]
                m_new = jnp.maximum(m_old, jnp.max(sb, axis=0, keepdims=True))
                p_cols.append(jnp.exp(sb - m_new).astype(BF16))
                m_cols.append(m_new)
                alpha_cols.append(jnp.exp(m_old - m_new))
            rows = slice(g * VT_ROWS, (g + 1) * VT_ROWS)
            pv = None
            for u in range(ATTN_BLOCKS):
                keys_u = slice(u * KEY_BLOCK, (u + 1) * KEY_BLOCK)
                p_u = jnp.concatenate([pc[keys_u, :] for pc in p_cols], axis=1)
                d = _dot(vt_ref[0, kk * ATTN_BLOCKS + u, rows, :], p_u)
                pv = d if pv is None else pv + d
            pending.append((alpha_cols, pv))
            m[g] = jnp.concatenate(m_cols, axis=1)
        for g, (alpha_cols, pv) in enumerate(pending):
            for r in range(grp):
                acc_ref[g, r] = alpha_cols[r] * acc_ref[g, r] + pv[:, r * tq:(r + 1) * tq]
        return tuple(m)

    qk_scores(0, 0)

    def pair_step(j, m):
        qk_scores(2 * j + 1, 1)
        m = softmax_pv(2 * j, 0, m)
        qk_scores(jnp.minimum(2 * j + 2, n_attn - 2), 0)
        return softmax_pv(2 * j + 1, 1, m)

    lax.fori_loop(0, n_attn // 2, pair_step, init)
    for hq in range(C_HEADS):
        g, r = divmod(hq, grp)
        ot_ref[hq * C_HDIM:(hq + 1) * C_HDIM, :] = acc_ref[g, r, 0:C_HDIM, :] / acc_ref[g, r, C_HDIM:C_HDIM + 1, :]
    o_ref[...] = ot_ref[...].T.astype(BF16)


def _dsa_t(qc, qi, small, small_col, keys, vals_t, kidx, nseq, t, past, nkeys):
    tq = KEY_BLOCK
    nq = t // tq
    lp = keys.shape[1]
    assert t % tq == 0 and lp % (SUPER * KEY_BLOCK) == 0
    topk = min(TOPK_MAX, nkeys // 4)
    grp = C_HEADS // C_KV_HEADS
    qrow = lambda b, i: (b * nq + i, 0)
    return pl.pallas_call(
        functools.partial(_dsa_t_body, tq=tq, past=past, nkeys=nkeys, topk=topk),
        grid=(nseq, nq),
        in_specs=[
            pl.BlockSpec((tq, C_WIDTH), qrow),
            pl.BlockSpec((tq, IDX_HEADS * IDX_DIM), qrow),
            pl.BlockSpec((tq, LANES), lambda b, i: (b * nq + i, small_col)),
            pl.BlockSpec((1, lp, LANES), lambda b, i: (b, 0, 0)),
            pl.BlockSpec((1, lp // KEY_BLOCK, C_KV_HEADS * VT_ROWS, KEY_BLOCK), lambda b, i: (b, 0, 0, 0)),
            pl.BlockSpec((1, lp, IDX_DIM), lambda b, i: (b, 0, 0)),
        ],
        out_specs=pl.BlockSpec((tq, C_WIDTH), lambda b, i: (b * nq + i, 0)),
        out_shape=jax.ShapeDtypeStruct((nseq * t, C_WIDTH), BF16),
        scratch_shapes=[
            pltpu.VMEM((lp // KEY_BLOCK, KEY_BLOCK, tq), jnp.int32),
            pltpu.VMEM((IDX_DIM, IDX_HEADS * tq), BF16),
            pltpu.VMEM((C_KV_HEADS, C_KV_HEADS * C_HDIM, grp * tq), BF16),
            pltpu.VMEM((LANES, tq), F32),
            pltpu.VMEM((C_KV_HEADS, grp, VT_ROWS, tq), F32),
            pltpu.VMEM((C_WIDTH, tq), F32),
            pltpu.VMEM((C_KV_HEADS, grp, ATTN_BLOCKS * KEY_BLOCK, tq), F32),
            pltpu.VMEM((C_KV_HEADS, grp, ATTN_BLOCKS * KEY_BLOCK, tq), F32),
        ],
        compiler_params=_params(("parallel", "arbitrary"), 48),
        name="dsa_mixer_t",
    )(qc, qi, small, keys, vals_t, kidx)


def _merge_body(h_ref, yap_ref, yas_ref, ybp_ref, ybs_ref, ocp_ref, ocs_ref, ga_ref, gb_ref, gc_ref,
                wa_ref, wb_ref, wc_ref, wo_ref, o_ref, *, prompt_tiles):
    is_prompt = pl.program_id(0) < prompt_tiles
    pick = lambda p_ref, s_ref: jnp.where(is_prompt, p_ref[...], s_ref[...])
    merged = (ga_ref[...].astype(F32) * _dot(pick(yap_ref, yas_ref), wa_ref[...])
              + gb_ref[...].astype(F32) * _dot(pick(ybp_ref, ybs_ref), wb_ref[...])
              + gc_ref[...].astype(F32) * _dot(pick(ocp_ref, ocs_ref), wc_ref[...]))
    o_ref[...] = h_ref[...] + _dot(merged.astype(BF16), wo_ref[...])


def _merge(h, ya, yb, oc, gates, wa, wb, wc, wo):
    n, d = h.shape
    tm = TOKEN_TILE
    pt = ya[0].shape[0] // tm
    assert ya[0].shape[0] % tm == 0 and ya[1].shape[0] % tm == 0 and n == ya[0].shape[0] + ya[1].shape[0]
    row = lambda i: (i, 0)
    const = lambda i: (0, 0)
    prow = lambda i: (jnp.minimum(i, pt - 1), 0)
    srow = lambda i: (jnp.maximum(i - pt, 0), 0)
    pair = [pl.BlockSpec((tm, MIX_W), prow), pl.BlockSpec((tm, MIX_W), srow)]
    return pl.pallas_call(
        functools.partial(_merge_body, prompt_tiles=pt),
        grid=(n // tm,),
        in_specs=[
            pl.BlockSpec((tm, d), row),
            *pair, *pair, *pair,
            pl.BlockSpec((tm, d), lambda i: (i, 0)),
            pl.BlockSpec((tm, d), lambda i: (i, 1)),
            pl.BlockSpec((tm, d), lambda i: (i, 2)),
            pl.BlockSpec((MIX_W, d), const),
            pl.BlockSpec((MIX_W, d), const),
            pl.BlockSpec((MIX_W, d), const),
            pl.BlockSpec((d, d), const),
        ],
        out_specs=pl.BlockSpec((tm, d), row),
        out_shape=jax.ShapeDtypeStruct((n, d), F32),
        compiler_params=_params(("parallel",), 48),
        name="gated_merge",
    )(h, *ya, *yb, *oc, gates, gates, gates, wa, wb, wc, wo)


def _pad_heads(w, heads, dim):
    lead = w.shape[:-1]
    w = w.reshape(*lead, heads, dim)
    w = jnp.pad(w, [(0, 0)] * len(lead) + [(0, 0), (0, HEAD_W - dim)])
    return w.reshape(*lead, heads * HEAD_W)


def _layout_w_in(w):
    widths = (512, 512, 512, 512, 256, 256, 512, 16, 512, 512, 128, 128, 512, 64, 8, 1024, 1024, 1024)
    w = w.astype(BF16)
    parts, o = [], 0
    for wd in widths:
        parts.append(w[:, o:o + wd])
        o += wd
    (a_q, a_f, a_i, a_g, b_q, b_k, b_v, b_r, b_g, c_q, c_k, c_v, i_q, i_k, i_w, g_a, g_b, g_c) = parts
    d = w.shape[0]
    small = jnp.concatenate([b_r, i_w, jnp.zeros((d, LANES - 24), w.dtype)], axis=1)
    i_k = jnp.pad(i_k, ((0, 0), (0, LANES - IDX_DIM)))
    cols = [g_a, g_b, g_c, a_q, a_f, a_i, a_g,
            _pad_heads(b_q, B_HEADS, B_KDIM), _pad_heads(b_k, B_HEADS, B_KDIM), b_v, b_g,
            small, c_q, i_q, c_k, c_v, i_k]
    out = jnp.concatenate(cols, axis=1)
    assert out.shape[1] == Z_WIDTH
    return out


def _rope_tables(pos):
    half = ROPE_DIMS // 2
    inv = ROPE_THETA ** (-jnp.arange(half, dtype=F32) * (2.0 / ROPE_DIMS))
    ang = pos.astype(F32)[:, None] * inv[None, :]
    cos, sin = jnp.cos(ang), jnp.sin(ang)
    n = pos.shape[0]
    ones = jnp.ones((n, C_HDIM - ROPE_DIMS), F32)
    zeros = jnp.zeros((n, C_HDIM - ROPE_DIMS), F32)
    zh = jnp.zeros((n, half), F32)
    c = jnp.concatenate([cos, cos, ones], axis=1)
    s_lo = jnp.concatenate([-sin, zh, zeros], axis=1)
    s_hi = jnp.concatenate([zh, sin, zeros], axis=1)
    two = lambda a: jnp.concatenate([a, a], axis=1)
    return two(c), two(s_lo), two(s_hi)


def kernel(x_prompt, x_sample, state_hgrn, state_gla, cache_k, cache_v, cache_kidx, hgrn_lb, w_in, w_gla_up, b_gla, norm_hgrn, norm_gla, w_br_a, w_br_b, w_br_c, w_out, norm_ffn1, norm_mix, norm_ffn2, ffn1_w1, ffn1_w3, ffn1_w2, ffn2_w1, ffn2_w3, ffn2_w2, norm_final):
    bp, tp, d = x_prompt.shape
    bs, ts, _ = x_sample.shape
    past = cache_k.shape[2]
    n_p, n_s = bp * tp, bs * ts

    lb_sm = jax.nn.softmax(hgrn_lb.astype(F32), axis=0)
    lb_all = jnp.concatenate([jnp.zeros_like(lb_sm[:1]), jnp.cumsum(lb_sm[1:], axis=0)], axis=0)

    pos = jnp.concatenate([jnp.arange(tp, dtype=jnp.int32),
                           jnp.tile(past + jnp.arange(ts, dtype=jnp.int32), PROJ_TILE // ts)])
    tables = _rope_tables(pos)
    assert tp % PROJ_TILE == 0 and PROJ_TILE % ts == 0 and n_s % PROJ_TILE == 0
    table_tile = lambda i: (jnp.where(i < n_p // PROJ_TILE, i % (tp // PROJ_TILE), tp // PROJ_TILE), 0)

    hs = (x_prompt.reshape(n_p, d), x_sample.reshape(n_s, d))
    row2 = lambda a: a.reshape(1, -1)
    zero_state = jnp.zeros((bp, 4, HEAD_W, HEAD_W), F32)
    span_s = SUPER * KEY_BLOCK
    key_pad = -(-(past + ts) // span_s) * span_s - past - ts
    q_rep = KEY_BLOCK // ts

    ffn1_w1, ffn1_w3, ffn1_w2, ffn2_w1, ffn2_w3, ffn2_w2, w_br_a, w_br_b, w_br_c, w_out = (
        a.astype(BF16) for a in (ffn1_w1, ffn1_w3, ffn1_w2, ffn2_w1, ffn2_w3, ffn2_w2,
                                 w_br_a, w_br_b, w_br_c, w_out))

    outs = {k: [] for k in ("pa", "pb", "pk", "pv", "pki", "sa", "sb", "sk", "sv", "ski")}
    for l in range(DEPTH):
        bf = lambda a: a[l]
        h = _ffn(hs, row2(norm_ffn1[l]), bf(ffn1_w1), bf(ffn1_w3), bf(ffn1_w2), row2(norm_final), False, 0)
        gates, z, qc, qi, kc, vc, ki, kcb, vcb, kib, vt = _inproj(
            h, row2(norm_mix[l]), _layout_w_in(w_in[l]), tables, table_tile)

        lb = row2(lb_all[l])
        nwa, nwb = row2(norm_hgrn[l]), row2(norm_gla[l])
        wup = jnp.pad(_pad_heads(w_gla_up[l], B_HEADS, B_KDIM), ((0, LANES - B_GATE_RANK), (0, 0)))
        bup = row2(_pad_heads(b_gla[l], B_HEADS, B_KDIM))
        sb0 = jnp.pad(state_gla[l], ((0, 0), (0, 0), (0, HEAD_W - B_KDIM), (0, 0)))

        ya_p, sa_p = _recurrent_mixer("hgrn", z, 0, bp, tp, zero_state, nwa, (lb,))
        ya_s, sa_s = _recurrent_mixer("hgrn", z, n_p, bs, ts, state_hgrn[l], nwa, (lb,))
        yb_p, sb_p = _recurrent_mixer("gla", z, 0, bp, tp, zero_state, nwb, (wup, bup))
        yb_s, sb_s = _recurrent_mixer("gla", z, n_p, bs, ts, sb0, nwb, (wup, bup))

        seqs = lambda a, n, t: a.reshape(n, t, a.shape[-1])
        vt_p = vt[:n_p // KEY_BLOCK].reshape(bp, tp // KEY_BLOCK, C_KV_HEADS * VT_ROWS, KEY_BLOCK)
        oc_p = _dsa_t(qc, qi, z, Z_SMALL, seqs(kcb[:n_p], bp, tp), vt_p, seqs(kib[:n_p], bp, tp), bp, tp, 0, tp)

        def with_cache(cache, new):
            full = jnp.concatenate([cache.reshape(bs, past, -1).astype(BF16), seqs(new[n_p:], bs, ts)], axis=1)
            return jnp.pad(full, ((0, 0), (0, key_pad), (0, 0)))

        def fill_block(a):
            return jnp.tile(seqs(a[n_p:], bs, ts), (1, q_rep, 1)).reshape(bs * KEY_BLOCK, a.shape[-1])

        vals_s = with_cache(cache_v[l], vcb)
        vt_s = jnp.swapaxes(vals_s.reshape(bs, -1, KEY_BLOCK, LANES), 2, 3)
        ones = jnp.ones(vt_s.shape[:2] + (VT_ROWS - C_HDIM, KEY_BLOCK), BF16)
        vt_s = jnp.concatenate([vt_s[:, :, :C_HDIM], ones, vt_s[:, :, C_HDIM:], ones], axis=2)
        small_s = fill_block(z[:, Z_SMALL * LANES:(Z_SMALL + 1) * LANES])
        oc_s = _dsa_t(fill_block(qc), fill_block(qi), small_s, 0, with_cache(cache_k[l], kcb), vt_s,
                      with_cache(cache_kidx[l], kib), bs, KEY_BLOCK, past, past + ts)
        oc_s = oc_s.reshape(bs, KEY_BLOCK, C_WIDTH)[:, :ts].reshape(n_s, C_WIDTH)

        h = _merge(h, (ya_p, ya_s), (yb_p, yb_s), (oc_p, oc_s), gates,
                   bf(w_br_a), bf(w_br_b), bf(w_br_c), bf(w_out))
        last = l == DEPTH - 1
        hs = _ffn((h,), row2(norm_ffn2[l]), bf(ffn2_w1), bf(ffn2_w3), bf(ffn2_w2), row2(norm_final),
                  last, n_p if last else 0)
        hs = hs if last else (hs,)

        outs["pa"].append(sa_p)
        outs["sa"].append(sa_s)
        outs["pb"].append(sb_p[:, :, :B_KDIM, :])
        outs["sb"].append(sb_s[:, :, :B_KDIM, :])
        outs["pk"].append(kc[:n_p].reshape(bp, tp, C_KV_HEADS, C_HDIM))
        outs["pv"].append(vc[:n_p].reshape(bp, tp, C_KV_HEADS, C_HDIM))
        outs["pki"].append(ki[:n_p].reshape(bp, tp, IDX_DIM))
        outs["sk"].append(kc[n_p:].reshape(bs, ts, C_KV_HEADS, C_HDIM))
        outs["sv"].append(vc[n_p:].reshape(bs, ts, C_KV_HEADS, C_HDIM))
        outs["ski"].append(ki[n_p:].reshape(bs, ts, IDX_DIM))

    st = {k: jnp.stack(v) for k, v in outs.items()}
    return (hs[0].reshape(bp, tp, d), hs[1].reshape(bs, ts, d),
            st["pa"], st["pb"], st["pk"], st["pv"], st["pki"],
            st["sa"], st["sb"], st["sk"], st["sv"], st["ski"])
```

```python
import functools

import jax
import jax.numpy as jnp
from jax import lax
from jax.experimental import pallas as pl
from jax.experimental.pallas import tpu as pltpu

F32 = jnp.float32
BF16 = jnp.bfloat16

D_MODEL = 1024
DEPTH = 2
CHUNK = 64
EPS = 1e-6
NEG = -1e30
F_MIN = 1e-30
A_HEADS = 4
A_KDIM = 128
A_VDIM = 128
B_HEADS = 4
B_KDIM = 64
B_VDIM = 128
B_GATE_RANK = 16
B_TAU = 16.0
C_HEADS = 8
C_KV_HEADS = 2
C_HDIM = 64
C_WIDTH = C_HEADS * C_HDIM
IDX_HEADS = 8
IDX_DIM = 64
IDX_SCALE = (IDX_HEADS * IDX_DIM) ** -0.5
TOPK_MAX = 256
ROPE_THETA = 500000.0
ROPE_DIMS = C_HDIM // 4
PAIR_BLOCK = 4
D_FF = 2816

LANES = 128
HEAD_W = 128
MIX_W = 4 * HEAD_W
TOKEN_TILE = 512
PROJ_TILE = 256
KEY_BLOCK = 128
VT_ROWS = 80
INT_MIN = -(2 ** 31)
FLOAT_MAX = 3.4028234663852886e38
CODE_NEG_INF = -(2 ** 31) + 0x7FFFFF

GATE_UNITS = 24
Z_A = 0
Z_B = 16
Z_SMALL = 32
Z_UNITS = 33
Z_CQ = GATE_UNITS + Z_UNITS
Z_IQ = Z_CQ + 4
Z_CK = Z_IQ + 4
Z_CV = Z_CK + 1
Z_IK = Z_CV + 1
Z_WIDTH = (Z_IK + 1) * LANES
SMALL_IW = B_GATE_RANK


def _params(sem, vmem_mb):
    return pltpu.CompilerParams(dimension_semantics=sem, vmem_limit_bytes=vmem_mb << 20)


def _dot(a, b):
    return jnp.dot(a, b, preferred_element_type=F32)


def _dot_nt(a, b):
    return lax.dot_general(a, b, (((1,), (1,)), ((), ())), preferred_element_type=F32)


def _dot_tn(a, b):
    return lax.dot_general(a, b, (((0,), (0,)), ((), ())), preferred_element_type=F32)


def _split3(x):
    h1 = x.astype(BF16)
    r1 = x - h1.astype(F32)
    h2 = r1.astype(BF16)
    h3 = (r1 - h2.astype(F32)).astype(BF16)
    return h1, h2, h3


def _dot_exact_lhs(m, x):
    h1, h2, h3 = _split3(x)
    return _dot(m, h1) + _dot(m, h2) + _dot(m, h3)


def _dot_hi(a, b):
    a1 = a.astype(BF16)
    a2 = (a - a1.astype(F32)).astype(BF16)
    b1 = b.astype(BF16)
    b2 = (b - b1.astype(F32)).astype(BF16)
    return _dot(a1, b1) + _dot(a1, b2) + _dot(a2, b1)


def _rmsnorm(x, g):
    return x * lax.rsqrt(jnp.mean(x * x, axis=-1, keepdims=True) + EPS) * g


def _ffn_body(*refs, n_in, n_out, prompt_tiles, final_norm):
    x_refs, (g_ref, w1_ref, w3_ref, w2_ref, gf_ref), o_refs = refs[:n_in], refs[n_in:n_in + 5], refs[n_in + 5:]
    is_prompt = pl.program_id(0) < prompt_tiles
    x = x_refs[0][...] if n_in == 1 else jnp.where(is_prompt, x_refs[0][...], x_refs[1][...])
    u = _rmsnorm(x, g_ref[...]).astype(BF16)
    a = _dot(u, w1_ref[...])
    b = _dot(u, w3_ref[...])
    hid = (a * jax.nn.sigmoid(a) * b).astype(BF16)
    out = x + 0.5 * _dot(hid, w2_ref[...])
    if final_norm:
        out = _rmsnorm(out, gf_ref[...])
    if n_out == 1:
        o_refs[0][...] = out
    else:
        @pl.when(is_prompt)
        def _():
            o_refs[0][...] = out

        @pl.when(jnp.logical_not(is_prompt))
        def _():
            o_refs[1][...] = out


def _ffn(xs, g, w1, w3, w2, gf, final_norm, split_out):
    d = xs[0].shape[1]
    tm = TOKEN_TILE
    sizes = [x.shape[0] for x in xs]
    n = sum(sizes)
    assert all(sz % tm == 0 for sz in sizes)
    pt = sizes[0] // tm if len(xs) == 2 else split_out // tm
    prow = lambda i: (jnp.minimum(i, pt - 1), 0)
    srow = lambda i: (jnp.maximum(i - pt, 0), 0)
    row = lambda i: (i, 0)
    const = lambda i: (0, 0)
    resident = pl.Buffered(1)
    x_specs = [pl.BlockSpec((tm, d), row)] if len(xs) == 1 else \
        [pl.BlockSpec((tm, d), prow), pl.BlockSpec((tm, d), srow)]
    if split_out:
        out_specs = [pl.BlockSpec((tm, d), prow), pl.BlockSpec((tm, d), srow)]
        out_shape = [jax.ShapeDtypeStruct((split_out, d), F32), jax.ShapeDtypeStruct((n - split_out, d), F32)]
    else:
        out_specs = pl.BlockSpec((tm, d), row)
        out_shape = jax.ShapeDtypeStruct((n, d), F32)
    return pl.pallas_call(
        functools.partial(_ffn_body, n_in=len(xs), n_out=2 if split_out else 1, prompt_tiles=pt,
                          final_norm=final_norm),
        grid=(n // tm,),
        in_specs=x_specs + [
            pl.BlockSpec((1, d), const),
            pl.BlockSpec(w1.shape, const, pipeline_mode=resident),
            pl.BlockSpec(w3.shape, const, pipeline_mode=resident),
            pl.BlockSpec(w2.shape, const, pipeline_mode=resident),
            pl.BlockSpec((1, d), const),
        ],
        out_specs=out_specs,
        out_shape=out_shape,
        compiler_params=_params(("arbitrary",), 48),
        name="ffn_half_step",
    )(*xs, g, w1, w3, w2, gf)


def _rope(x, c, s_lo, s_hi):
    w = x.shape[1]
    rep = w // LANES
    if rep > 1:
        c = jnp.concatenate([c] * rep, axis=1)
        s_lo = jnp.concatenate([s_lo] * rep, axis=1)
        s_hi = jnp.concatenate([s_hi] * rep, axis=1)
    half = ROPE_DIMS // 2
    return x * c + pltpu.roll(x, half, 1) * s_hi + pltpu.roll(x, w - half, 1) * s_lo


def _inproj_body(x_ref, g_ref, w_ref, c_ref, slo_ref, shi_ref,
                 gates_ref, z_ref, qc_ref, qi_ref, kc_ref, vc_ref, ki_ref, kcb_ref, vcb_ref, kib_ref, vt_ref):
    u = _rmsnorm(x_ref[...], g_ref[...]).astype(BF16)
    zf = _dot(u, w_ref[...])
    gates_ref[...] = jax.nn.sigmoid(zf[:, :GATE_UNITS * LANES]).astype(BF16)
    z_ref[...] = zf[:, GATE_UNITS * LANES:(GATE_UNITS + Z_UNITS) * LANES]

    unit = lambda first, count: zf[:, first * LANES:(first + count) * LANES]
    c, s_lo, s_hi = c_ref[...], slo_ref[...], shi_ref[...]
    qc_ref[...] = (_rope(unit(Z_CQ, 4), c, s_lo, s_hi) * (C_HDIM ** -0.5)).astype(BF16)
    qi_ref[...] = _rope(unit(Z_IQ, 4), c, s_lo, s_hi).astype(BF16)
    kc = _rope(unit(Z_CK, 1), c, s_lo, s_hi)
    vc = unit(Z_CV, 1)
    ki = _rope(unit(Z_IK, 1), c, s_lo, s_hi)[:, :IDX_DIM]
    kc_ref[...] = kc
    vc_ref[...] = vc
    ki_ref[...] = ki
    kcb_ref[...] = kc.astype(BF16)
    vcb_ref[...] = vc.astype(BF16)
    kib_ref[...] = ki.astype(BF16)
    ones = jnp.ones((VT_ROWS - C_HDIM, KEY_BLOCK), BF16)
    for kk in range(vt_ref.shape[0]):
        vt = vc[kk * KEY_BLOCK:(kk + 1) * KEY_BLOCK, :].T.astype(BF16)
        vt_ref[kk] = jnp.concatenate([vt[:C_HDIM], ones, vt[C_HDIM:], ones], axis=0)


def _inproj(h, g, w, tables, table_tile):
    n, d = h.shape
    tm = PROJ_TILE
    assert n % tm == 0
    row = lambda i: (i, 0)
    return pl.pallas_call(
        _inproj_body,
        grid=(n // tm,),
        in_specs=[
            pl.BlockSpec((tm, d), row),
            pl.BlockSpec((1, d), lambda i: (0, 0)),
            pl.BlockSpec((d, Z_WIDTH), lambda i: (0, 0), pipeline_mode=pl.Buffered(1)),
            pl.BlockSpec((tm, LANES), table_tile),
            pl.BlockSpec((tm, LANES), table_tile),
            pl.BlockSpec((tm, LANES), table_tile),
        ],
        out_specs=[
            pl.BlockSpec((tm, GATE_UNITS * LANES), row),
            pl.BlockSpec((tm, Z_UNITS * LANES), row),
            pl.BlockSpec((tm, C_WIDTH), row),
            pl.BlockSpec((tm, IDX_HEADS * IDX_DIM), row),
            pl.BlockSpec((tm, LANES), row),
            pl.BlockSpec((tm, LANES), row),
            pl.BlockSpec((tm, IDX_DIM), row),
            pl.BlockSpec((tm, LANES), row),
            pl.BlockSpec((tm, LANES), row),
            pl.BlockSpec((tm, IDX_DIM), row),
            pl.BlockSpec((tm // KEY_BLOCK, C_KV_HEADS * VT_ROWS, KEY_BLOCK), lambda i: (i, 0, 0)),
        ],
        out_shape=[
            jax.ShapeDtypeStruct((n, GATE_UNITS * LANES), BF16),
            jax.ShapeDtypeStruct((n, Z_UNITS * LANES), F32),
            jax.ShapeDtypeStruct((n, C_WIDTH), BF16),
            jax.ShapeDtypeStruct((n, IDX_HEADS * IDX_DIM), BF16),
            jax.ShapeDtypeStruct((n, LANES), F32),
            jax.ShapeDtypeStruct((n, LANES), F32),
            jax.ShapeDtypeStruct((n, IDX_DIM), F32),
            jax.ShapeDtypeStruct((n, LANES), BF16),
            jax.ShapeDtypeStruct((n, LANES), BF16),
            jax.ShapeDtypeStruct((n, IDX_DIM), BF16),
            jax.ShapeDtypeStruct((n // KEY_BLOCK, C_KV_HEADS * VT_ROWS, KEY_BLOCK), BF16),
        ],
        compiler_params=_params(("parallel",), 48),
        name="in_projection",
    )(h, g, w, *tables)


def _gla_head(q, k, v, logf, st_ref, h, tc):
    row = lax.broadcasted_iota(jnp.int32, (tc, tc), 0)
    col = lax.broadcasted_iota(jnp.int32, (tc, tc), 1)
    tril = (col <= row).astype(BF16)
    cum = _dot_exact_lhs(tril, logf)
    yield None

    tile = 2 * PAIR_BLOCK
    nt = tc // tile
    b3 = cum.reshape(nt, tile, HEAD_W)
    q3 = q.reshape(nt, tile, HEAD_W)
    k3 = k.reshape(nt, tile, HEAD_W)
    v3 = v.reshape(nt, tile, HEAD_W)
    srow = lax.broadcasted_iota(jnp.int32, (nt, tile, 1), 1)
    low = srow < PAIR_BLOCK
    tloc = srow & (PAIR_BLOCK - 1)

    def block_row(x3, j):
        return jnp.where(low, x3[:, j:j + 1, :], x3[:, PAIR_BLOCK + j:PAIR_BLOCK + j + 1, :])

    o3 = jnp.zeros((nt, tile, HEAD_W), F32)
    for j in range(PAIR_BLOCK):
        causal = tloc >= j
        decay = jnp.exp(jnp.where(causal, b3 - block_row(b3, j), 0.0))
        w = jnp.sum(q3 * block_row(k3, j) * decay, axis=-1, keepdims=True)
        o3 = o3 + jnp.where(causal, w, 0.0) * block_row(v3, j)
        yield None
    o = o3.reshape(tc, HEAD_W)
    yield None

    vb = v.astype(BF16)
    attn = jnp.zeros((tc, tc), F32)
    half = tc // 2
    while half >= PAIR_BLOCK:
        blk = 2 * half
        nblk = tc // blk
        bl = cum.reshape(nblk, blk, HEAD_W)
        x = bl - bl[:, half - 1:half, :]
        second = lax.broadcasted_iota(jnp.int32, (nblk, blk, 1), 1) >= half
        e = jnp.exp(jnp.where(second, x, -x))
        qt = jnp.where(second, q.reshape(nblk, blk, HEAD_W) * e, 0.0).reshape(tc, HEAD_W).astype(BF16)
        kt = jnp.where(second, 0.0, k.reshape(nblk, blk, HEAD_W) * e).reshape(tc, HEAD_W).astype(BF16)
        shift = blk.bit_length() - 1
        same = (row >> shift) == (col >> shift)
        attn = attn + jnp.where(same, _dot_nt(qt, kt), 0.0)
        half //= 2
        yield None
    o = o + _dot(attn.astype(BF16), vb)
    yield None

    st = st_ref[h]
    o = o + _dot_nt((q * jnp.exp(cum)).astype(BF16), st.astype(BF16))
    last = cum[tc - 1:tc, :]
    kd = (k * jnp.exp(last - cum)).astype(BF16)
    st_ref[h] = st * jnp.exp(last) + _dot_tn(vb, kd)
    yield o


def _gla_heads(inputs, st_ref, tc):
    heads = [_gla_head(q, k, v, logf, st_ref, h, tc) for h, (q, k, v, logf) in enumerate(inputs)]
    outs = None
    while True:
        step = [next(head, StopIteration) for head in heads]
        if step[0] is StopIteration:
            return outs
        outs = step


def _gla_finish(o, nw, gate):
    return _rmsnorm(o, nw) * (gate * jax.nn.sigmoid(gate))


def _gla_state_io(c, s0_ref, st_ref, heads):
    @pl.when(c == 0)
    def _():
        for h in range(heads):
            st_ref[h] = s0_ref[0, h].T


def _gla_state_out(c, sout_ref, st_ref, heads):
    @pl.when(c == pl.num_programs(1) - 1)
    def _():
        for h in range(heads):
            sout_ref[0, h] = st_ref[h].T


def _hgrn_body(q_ref, f_ref, v_ref, g_ref, lb_ref, nw_ref, s0_ref, y_ref, sout_ref, st_ref, *, tc):
    c = pl.program_id(1)
    _gla_state_io(c, s0_ref, st_ref, A_HEADS)
    inputs = []
    for h in range(A_HEADS):
        hs = slice(h * HEAD_W, (h + 1) * HEAD_W)
        zf = f_ref[:, hs]
        lb = lb_ref[:, hs]
        f = lb + (1.0 - lb) * jax.nn.sigmoid(zf)
        logf = jnp.log(jnp.maximum(f, F_MIN))
        k = (1.0 - lb) * jax.nn.sigmoid(-zf)
        zq = q_ref[:, hs]
        q = zq * jax.nn.sigmoid(zq) * (A_KDIM ** -0.5)
        inputs.append((q, k, v_ref[:, hs], logf))
    for h, o in enumerate(_gla_heads(inputs, st_ref, tc)):
        hs = slice(h * HEAD_W, (h + 1) * HEAD_W)
        y_ref[:, hs] = _gla_finish(o, nw_ref[...], g_ref[:, hs]).astype(BF16)
    _gla_state_out(c, sout_ref, st_ref, A_HEADS)


def _gla_body(q_ref, k_ref, v_ref, g_ref, r_ref, wup_ref, bup_ref, nw_ref, s0_ref,
              y_ref, sout_ref, st_ref, *, tc):
    c = pl.program_id(1)
    _gla_state_io(c, s0_ref, st_ref, B_HEADS)
    r = _dot_hi(r_ref[...], wup_ref[...]) + bup_ref[...]
    logf_all = (jnp.minimum(r, 0.0) - jnp.log1p(jnp.exp(-jnp.abs(r)))) / B_TAU
    heads = [slice(h * HEAD_W, (h + 1) * HEAD_W) for h in range(B_HEADS)]
    inputs = [(q_ref[:, hs] * (B_KDIM ** -0.5), k_ref[:, hs], v_ref[:, hs], logf_all[:, hs]) for hs in heads]
    for hs, o in zip(heads, _gla_heads(inputs, st_ref, tc)):
        y_ref[:, hs] = _gla_finish(o, nw_ref[...], g_ref[:, hs]).astype(BF16)
    _gla_state_out(c, sout_ref, st_ref, B_HEADS)


def _recurrent_mixer(mode, z, row0, nseq, t, s0, nw, extra):
    tc = min(128, t)
    nc = t // tc
    rb0 = row0 // tc
    zcol = (Z_A if mode == "hgrn" else Z_B) // 4

    def zspec(k):
        return pl.BlockSpec((tc, MIX_W), lambda b, c: (rb0 + b * nc + c, zcol + k))

    const = lambda b, c: (0, 0)
    state_spec = pl.BlockSpec((1, 4, HEAD_W, HEAD_W), lambda b, c: (b, 0, 0, 0))
    if mode == "hgrn":
        body = functools.partial(_hgrn_body, tc=tc)
        in_specs = [zspec(0), zspec(1), zspec(2), zspec(3),
                    pl.BlockSpec((1, MIX_W), const), pl.BlockSpec((1, HEAD_W), const), state_spec]
        args = (z, z, z, z, extra[0], nw, s0)
    else:
        body = functools.partial(_gla_body, tc=tc)
        in_specs = [zspec(0), zspec(1), zspec(2), zspec(3),
                    pl.BlockSpec((tc, LANES), lambda b, c: (rb0 + b * nc + c, Z_SMALL)),
                    pl.BlockSpec((LANES, MIX_W), const), pl.BlockSpec((1, MIX_W), const),
                    pl.BlockSpec((1, HEAD_W), const), state_spec]
        args = (z, z, z, z, z, extra[0], extra[1], nw, s0)
    return pl.pallas_call(
        body,
        grid=(nseq, nc),
        in_specs=in_specs,
        out_specs=[pl.BlockSpec((tc, MIX_W), lambda b, c: (b * nc + c, 0)), state_spec],
        out_shape=[jax.ShapeDtypeStruct((nseq * t, MIX_W), BF16),
                   jax.ShapeDtypeStruct((nseq, 4, HEAD_W, HEAD_W), F32)],
        scratch_shapes=[pltpu.VMEM((4, HEAD_W, HEAD_W), F32)],
        compiler_params=_params(("parallel", "arbitrary"), 32),
        name=mode + "_mixer",
    )(*args)


SUPER = 8
ATTN_BLOCKS = 4


def _dsa_t_body(qc_ref, qi_ref, sm_ref, kc_ref, vt_ref, ki_ref, o_ref,
                sk_ref, qit_ref, qct_ref, wrow_ref, acc_ref, ot_ref, s0_ref, s1_ref,
                *, tq, past, nkeys, topk):
    i = pl.program_id(1)
    qpos0 = past + i * tq
    last_chunk = (qpos0 + tq - 1) // CHUNK
    n_adm = jnp.minimum((last_chunk + 1) * CHUNK, nkeys)
    nkb = (n_adm + KEY_BLOCK - 1) // KEY_BLOCK
    nsb = (nkb + SUPER - 1) // SUPER
    grp = C_HEADS // C_KV_HEADS

    qit = qi_ref[...].astype(F32).T
    for j in range(IDX_HEADS):
        qit_ref[:, j * tq:(j + 1) * tq] = qit[j * IDX_DIM:(j + 1) * IDX_DIM, :].astype(BF16)
    qct = qc_ref[...].astype(F32).T
    zeros = jnp.zeros((C_HDIM, tq), BF16)
    for hq in range(C_HEADS):
        g, r = divmod(hq, grp)
        for gg in range(C_KV_HEADS):
            val = qct[hq * C_HDIM:(hq + 1) * C_HDIM, :].astype(BF16) if gg == g else zeros
            qct_ref[g, gg * C_HDIM:(gg + 1) * C_HDIM, r * tq:(r + 1) * tq] = val
    wrow_ref[...] = sm_ref[...].T

    qchunk = (qpos0 + lax.broadcasted_iota(jnp.int32, (1, tq), 1)) >> 6
    ksub = lax.broadcasted_iota(jnp.int32, (KEY_BLOCK, 1), 0)

    def admissible(kb):
        kpos = kb * KEY_BLOCK + ksub
        return ((kpos >> 6) <= qchunk) & (kpos < nkeys)

    def score_step(kk, carry):
        for u in range(SUPER):
            kb = kk * SUPER + u
            kib = ki_ref[0, pl.ds(pl.multiple_of(kb * KEY_BLOCK, KEY_BLOCK), KEY_BLOCK), :]
            d = _dot(kib, qit_ref[...])
            sc = jnp.maximum(d[:, 0:tq], 0.0) * wrow_ref[SMALL_IW:SMALL_IW + 1, :]
            for j in range(1, IDX_HEADS):
                sc = sc + jnp.maximum(d[:, j * tq:(j + 1) * tq], 0.0) * wrow_ref[SMALL_IW + j:SMALL_IW + j + 1, :]
            sk_ref[kb] = jnp.where(admissible(kb), sc * IDX_SCALE + 0.0, -jnp.inf)
        return carry

    lax.fori_loop(0, nsb, score_step, 0)

    def decode(code):
        return pltpu.bitcast(jnp.where(code < 0, code ^ 0x7FFFFFFF, code), F32)

    def fold(hit):
        return jnp.sum(hit.reshape(KEY_BLOCK // 8, 8, tq), axis=0)

    def count(pred):
        def step(kk, acc):
            for u in range(SUPER):
                kb = kk * SUPER + u
                acc = acc + fold(jnp.where(pred(sk_ref[kb], kb), 1.0, 0.0))
            return acc
        acc = lax.fori_loop(0, nsb, step, jnp.zeros((8, tq), F32))
        return jnp.sum(acc, axis=0, keepdims=True)

    kf = float(topk)
    c0 = count(lambda s, kb: s >= 0.0)
    t0 = jnp.where(c0 >= kf, 0, INT_MIN).astype(jnp.int32)

    def bit_step(it, t):
        cand = t | jnp.left_shift(jnp.int32(1), 30 - it)
        cand_f = decode(cand)
        return jnp.where(count(lambda s, kb: s >= cand_f) >= kf, cand, t)
    code = lax.fori_loop(0, 31, bit_step, t0)
    thr = jnp.where(code <= CODE_NEG_INF, -FLOAT_MAX, decode(code))

    c_ge = count(lambda s, kb: s >= thr)
    nbits = max(1, (nkeys - 1).bit_length())

    @pl.when(jnp.max(c_ge) > kf)
    def _():
        need = kf - count(lambda s, kb: s > thr)

        def cut_step(it, cut):
            cand = cut | jnp.left_shift(jnp.int32(1), nbits - 1 - it)
            c = count(lambda s, kb: (s == thr) & (kb * KEY_BLOCK + ksub < cand))
            return jnp.where(c < need, cand, cut)
        cut = lax.fori_loop(0, nbits, cut_step, jnp.zeros((1, tq), jnp.int32))

        def strike(kb, carry):
            key = sk_ref[kb]
            sk_ref[kb] = jnp.where((key == thr) & (kb * KEY_BLOCK + ksub > cut), -jnp.inf, key)
            return carry
        lax.fori_loop(0, nsb * SUPER, strike, 0)

    acc_ref[...] = jnp.zeros(acc_ref.shape, F32)
    init = tuple(jnp.full((1, grp * tq), NEG, F32) for _ in range(C_KV_HEADS))

    span = ATTN_BLOCKS * KEY_BLOCK
    n_attn = nsb * (SUPER // ATTN_BLOCKS)

    s_bufs = (s0_ref, s1_ref)

    def qk_scores(kk, half):
        kblk = kc_ref[0, pl.ds(pl.multiple_of(kk * span, span), span), :]
        bias = jnp.concatenate(
            [jnp.where(sk_ref[kk * ATTN_BLOCKS + u] >= thr, 0.0, NEG) for u in range(ATTN_BLOCKS)], axis=0)
        bias = jnp.concatenate([bias] * grp, axis=1)
        for g in range(C_KV_HEADS):
            s = _dot(kblk, qct_ref[g]) + bias
            for r in range(grp):
                s_bufs[half][g, r] = s[:, r * tq:(r + 1) * tq]

    def softmax_pv(kk, half, m):
        m = list(m)
        pending = []
        for g in range(C_KV_HEADS):
            m_cols, alpha_cols, p_cols = [], [], []
            for r in range(grp):
                cols = slice(r * tq, (r + 1) * tq)
                sb = s_bufs[half][g, r]
                m_old = m[g][:, cols]
                m_new = jnp.maximum(m_old, jnp.max(sb, axis=0, keepdims=True))
                p_cols.append(jnp.exp(sb - m_new).astype(BF16))
                m_cols.append(m_new)
                alpha_cols.append(jnp.exp(m_old - m_new))
            rows = slice(g * VT_ROWS, (g + 1) * VT_ROWS)
            pv = None
            for u in range(ATTN_BLOCKS):
                keys_u = slice(u * KEY_BLOCK, (u + 1) * KEY_BLOCK)
                p_u = jnp.concatenate([pc[keys_u, :] for pc in p_cols], axis=1)
                d = _dot(vt_ref[0, kk * ATTN_BLOCKS + u, rows, :], p_u)
                pv = d if pv is None else pv + d
            pending.append((alpha_cols, pv))
            m[g] = jnp.concatenate(m_cols, axis=1)
        for g, (alpha_cols, pv) in enumerate(pending):
            for r in range(grp):
                acc_ref[g, r] = alpha_cols[r] * acc_ref[g, r] + pv[:, r * tq:(r + 1) * tq]
        return tuple(m)

    qk_scores(0, 0)

    def pair_step(j, m):
        qk_scores(2 * j + 1, 1)
        m = softmax_pv(2 * j, 0, m)
        qk_scores(jnp.minimum(2 * j + 2, n_attn - 2), 0)
        return softmax_pv(2 * j + 1, 1, m)

    lax.fori_loop(0, n_attn // 2, pair_step, init)
    for hq in range(C_HEADS):
        g, r = divmod(hq, grp)
        ot_ref[hq * C_HDIM:(hq + 1) * C_HDIM, :] = acc_ref[g, r, 0:C_HDIM, :] / acc_ref[g, r, C_HDIM:C_HDIM + 1, :]
    o_ref[...] = ot_ref[...].T.astype(BF16)


def _dsa_t(qc, qi, small, small_col, keys, vals_t, kidx, nseq, t, past, nkeys):
    tq = KEY_BLOCK
    nq = t // tq
    lp = keys.shape[1]
    assert t % tq == 0 and lp % (SUPER * KEY_BLOCK) == 0
    topk = min(TOPK_MAX, nkeys // 4)
    grp = C_HEADS // C_KV_HEADS
    qrow = lambda b, i: (b * nq + i, 0)
    return pl.pallas_call(
        functools.partial(_dsa_t_body, tq=tq, past=past, nkeys=nkeys, topk=topk),
        grid=(nseq, nq),
        in_specs=[
            pl.BlockSpec((tq, C_WIDTH), qrow),
            pl.BlockSpec((tq, IDX_HEADS * IDX_DIM), qrow),
            pl.BlockSpec((tq, LANES), lambda b, i: (b * nq + i, small_col)),
            pl.BlockSpec((1, lp, LANES), lambda b, i: (b, 0, 0)),
            pl.BlockSpec((1, lp // KEY_BLOCK, C_KV_HEADS * VT_ROWS, KEY_BLOCK), lambda b, i: (b, 0, 0, 0)),
            pl.BlockSpec((1, lp, IDX_DIM), lambda b, i: (b, 0, 0)),
        ],
        out_specs=pl.BlockSpec((tq, C_WIDTH), lambda b, i: (b * nq + i, 0)),
        out_shape=jax.ShapeDtypeStruct((nseq * t, C_WIDTH), BF16),
        scratch_shapes=[
            pltpu.VMEM((lp // KEY_BLOCK, KEY_BLOCK, tq), F32),
            pltpu.VMEM((IDX_DIM, IDX_HEADS * tq), BF16),
            pltpu.VMEM((C_KV_HEADS, C_KV_HEADS * C_HDIM, grp * tq), BF16),
            pltpu.VMEM((LANES, tq), F32),
            pltpu.VMEM((C_KV_HEADS, grp, VT_ROWS, tq), F32),
            pltpu.VMEM((C_WIDTH, tq), F32),
            pltpu.VMEM((C_KV_HEADS, grp, ATTN_BLOCKS * KEY_BLOCK, tq), F32),
            pltpu.VMEM((C_KV_HEADS, grp, ATTN_BLOCKS * KEY_BLOCK, tq), F32),
        ],
        compiler_params=_params(("parallel", "arbitrary"), 48),
        name="dsa_mixer_t",
    )(qc, qi, small, keys, vals_t, kidx)


def _merge_body(h_ref, yap_ref, yas_ref, ybp_ref, ybs_ref, ocp_ref, ocs_ref, ga_ref, gb_ref, gc_ref,
                wa_ref, wb_ref, wc_ref, wo_ref, o_ref, *, prompt_tiles):
    is_prompt = pl.program_id(0) < prompt_tiles
    pick = lambda p_ref, s_ref: jnp.where(is_prompt, p_ref[...], s_ref[...])
    merged = (ga_ref[...].astype(F32) * _dot(pick(yap_ref, yas_ref), wa_ref[...])
              + gb_ref[...].astype(F32) * _dot(pick(ybp_ref, ybs_ref), wb_ref[...])
              + gc_ref[...].astype(F32) * _dot(pick(ocp_ref, ocs_ref), wc_ref[...]))
    o_ref[...] = h_ref[...] + _dot(merged.astype(BF16), wo_ref[...])


def _merge(h, ya, yb, oc, gates, wa, wb, wc, wo):
    n, d = h.shape
    tm = TOKEN_TILE
    pt = ya[0].shape[0] // tm
    assert ya[0].shape[0] % tm == 0 and ya[1].shape[0] % tm == 0 and n == ya[0].shape[0] + ya[1].shape[0]
    row = lambda i: (i, 0)
    const = lambda i: (0, 0)
    prow = lambda i: (jnp.minimum(i, pt - 1), 0)
    srow = lambda i: (jnp.maximum(i - pt, 0), 0)
    pair = [pl.BlockSpec((tm, MIX_W), prow), pl.BlockSpec((tm, MIX_W), srow)]
    return pl.pallas_call(
        functools.partial(_merge_body, prompt_tiles=pt),
        grid=(n // tm,),
        in_specs=[
            pl.BlockSpec((tm, d), row),
            *pair, *pair, *pair,
            pl.BlockSpec((tm, d), lambda i: (i, 0)),
            pl.BlockSpec((tm, d), lambda i: (i, 1)),
            pl.BlockSpec((tm, d), lambda i: (i, 2)),
            pl.BlockSpec((MIX_W, d), const),
            pl.BlockSpec((MIX_W, d), const),
            pl.BlockSpec((MIX_W, d), const),
            pl.BlockSpec((d, d), const),
        ],
        out_specs=pl.BlockSpec((tm, d), row),
        out_shape=jax.ShapeDtypeStruct((n, d), F32),
        compiler_params=_params(("parallel",), 48),
        name="gated_merge",
    )(h, *ya, *yb, *oc, gates, gates, gates, wa, wb, wc, wo)


def _pad_heads(w, heads, dim):
    lead = w.shape[:-1]
    w = w.reshape(*lead, heads, dim)
    w = jnp.pad(w, [(0, 0)] * len(lead) + [(0, 0), (0, HEAD_W - dim)])
    return w.reshape(*lead, heads * HEAD_W)


def _layout_w_in(w):
    widths = (512, 512, 512, 512, 256, 256, 512, 16, 512, 512, 128, 128, 512, 64, 8, 1024, 1024, 1024)
    w = w.astype(BF16)
    parts, o = [], 0
    for wd in widths:
        parts.append(w[:, o:o + wd])
        o += wd
    (a_q, a_f, a_i, a_g, b_q, b_k, b_v, b_r, b_g, c_q, c_k, c_v, i_q, i_k, i_w, g_a, g_b, g_c) = parts
    d = w.shape[0]
    small = jnp.concatenate([b_r, i_w, jnp.zeros((d, LANES - 24), w.dtype)], axis=1)
    i_k = jnp.pad(i_k, ((0, 0), (0, LANES - IDX_DIM)))
    cols = [g_a, g_b, g_c, a_q, a_f, a_i, a_g,
            _pad_heads(b_q, B_HEADS, B_KDIM), _pad_heads(b_k, B_HEADS, B_KDIM), b_v, b_g,
            small, c_q, i_q, c_k, c_v, i_k]
    out = jnp.concatenate(cols, axis=1)
    assert out.shape[1] == Z_WIDTH
    return out


def _rope_tables(pos):
    half = ROPE_DIMS // 2
    inv = ROPE_THETA ** (-jnp.arange(half, dtype=F32) * (2.0 / ROPE_DIMS))
    ang = pos.astype(F32)[:, None] * inv[None, :]
    cos, sin = jnp.cos(ang), jnp.sin(ang)
    n = pos.shape[0]
    ones = jnp.ones((n, C_HDIM - ROPE_DIMS), F32)
    zeros = jnp.zeros((n, C_HDIM - ROPE_DIMS), F32)
    zh = jnp.zeros((n, half), F32)
    c = jnp.concatenate([cos, cos, ones], axis=1)
    s_lo = jnp.concatenate([-sin, zh, zeros], axis=1)
    s_hi = jnp.concatenate([zh, sin, zeros], axis=1)
    two = lambda a: jnp.concatenate([a, a], axis=1)
    return two(c), two(s_lo), two(s_hi)


def kernel(x_prompt, x_sample, state_hgrn, state_gla, cache_k, cache_v, cache_kidx, hgrn_lb, w_in, w_gla_up, b_gla, norm_hgrn, norm_gla, w_br_a, w_br_b, w_br_c, w_out, norm_ffn1, norm_mix, norm_ffn2, ffn1_w1, ffn1_w3, ffn1_w2, ffn2_w1, ffn2_w3, ffn2_w2, norm_final):
    bp, tp, d = x_prompt.shape
    bs, ts, _ = x_sample.shape
    past = cache_k.shape[2]
    n_p, n_s = bp * tp, bs * ts

    lb_sm = jax.nn.softmax(hgrn_lb.astype(F32), axis=0)
    lb_all = jnp.concatenate([jnp.zeros_like(lb_sm[:1]), jnp.cumsum(lb_sm[1:], axis=0)], axis=0)

    pos = jnp.concatenate([jnp.arange(tp, dtype=jnp.int32),
                           jnp.tile(past + jnp.arange(ts, dtype=jnp.int32), PROJ_TILE // ts)])
    tables = _rope_tables(pos)
    assert tp % PROJ_TILE == 0 and PROJ_TILE % ts == 0 and n_s % PROJ_TILE == 0
    table_tile = lambda i: (jnp.where(i < n_p // PROJ_TILE, i % (tp // PROJ_TILE), tp // PROJ_TILE), 0)

    hs = (x_prompt.reshape(n_p, d), x_sample.reshape(n_s, d))
    row2 = lambda a: a.reshape(1, -1)
    zero_state = jnp.zeros((bp, 4, HEAD_W, HEAD_W), F32)
    span_s = SUPER * KEY_BLOCK
    key_pad = -(-(past + ts) // span_s) * span_s - past - ts
    q_rep = KEY_BLOCK // ts

    ffn1_w1, ffn1_w3, ffn1_w2, ffn2_w1, ffn2_w3, ffn2_w2, w_br_a, w_br_b, w_br_c, w_out = (
        a.astype(BF16) for a in (ffn1_w1, ffn1_w3, ffn1_w2, ffn2_w1, ffn2_w3, ffn2_w2,
                                 w_br_a, w_br_b, w_br_c, w_out))

    outs = {k: [] for k in ("pa", "pb", "pk", "pv", "pki", "sa", "sb", "sk", "sv", "ski")}
    for l in range(DEPTH):
        bf = lambda a: a[l]
        h = _ffn(hs, row2(norm_ffn1[l]), bf(ffn1_w1), bf(ffn1_w3), bf(ffn1_w2), row2(norm_final), False, 0)
        gates, z, qc, qi, kc, vc, ki, kcb, vcb, kib, vt = _inproj(
            h, row2(norm_mix[l]), _layout_w_in(w_in[l]), tables, table_tile)

        lb = row2(lb_all[l])
        nwa, nwb = row2(norm_hgrn[l]), row2(norm_gla[l])
        wup = jnp.pad(_pad_heads(w_gla_up[l], B_HEADS, B_KDIM), ((0, LANES - B_GATE_RANK), (0, 0)))
        bup = row2(_pad_heads(b_gla[l], B_HEADS, B_KDIM))
        sb0 = jnp.pad(state_gla[l], ((0, 0), (0, 0), (0, HEAD_W - B_KDIM), (0, 0)))

        ya_p, sa_p = _recurrent_mixer("hgrn", z, 0, bp, tp, zero_state, nwa, (lb,))
        ya_s, sa_s = _recurrent_mixer("hgrn", z, n_p, bs, ts, state_hgrn[l], nwa, (lb,))
        yb_p, sb_p = _recurrent_mixer("gla", z, 0, bp, tp, zero_state, nwb, (wup, bup))
        yb_s, sb_s = _recurrent_mixer("gla", z, n_p, bs, ts, sb0, nwb, (wup, bup))

        seqs = lambda a, n, t: a.reshape(n, t, a.shape[-1])
        vt_p = vt[:n_p // KEY_BLOCK].reshape(bp, tp // KEY_BLOCK, C_KV_HEADS * VT_ROWS, KEY_BLOCK)
        oc_p = _dsa_t(qc, qi, z, Z_SMALL, seqs(kcb[:n_p], bp, tp), vt_p, seqs(kib[:n_p], bp, tp), bp, tp, 0, tp)

        def with_cache(cache, new):
            full = jnp.concatenate([cache.reshape(bs, past, -1).astype(BF16), seqs(new[n_p:], bs, ts)], axis=1)
            return jnp.pad(full, ((0, 0), (0, key_pad), (0, 0)))

        def fill_block(a):
            return jnp.tile(seqs(a[n_p:], bs, ts), (1, q_rep, 1)).reshape(bs * KEY_BLOCK, a.shape[-1])

        vals_s = with_cache(cache_v[l], vcb)
        vt_s = jnp.swapaxes(vals_s.reshape(bs, -1, KEY_BLOCK, LANES), 2, 3)
        ones = jnp.ones(vt_s.shape[:2] + (VT_ROWS - C_HDIM, KEY_BLOCK), BF16)
        vt_s = jnp.concatenate([vt_s[:, :, :C_HDIM], ones, vt_s[:, :, C_HDIM:], ones], axis=2)
        small_s = fill_block(z[:, Z_SMALL * LANES:(Z_SMALL + 1) * LANES])
        oc_s = _dsa_t(fill_block(qc), fill_block(qi), small_s, 0, with_cache(cache_k[l], kcb), vt_s,
                      with_cache(cache_kidx[l], kib), bs, KEY_BLOCK, past, past + ts)
        oc_s = oc_s.reshape(bs, KEY_BLOCK, C_WIDTH)[:, :ts].reshape(n_s, C_WIDTH)

        h = _merge(h, (ya_p, ya_s), (yb_p, yb_s), (oc_p, oc_s), gates,
                   bf(w_br_a), bf(w_br_b), bf(w_br_c), bf(w_out))
        last = l == DEPTH - 1
        hs = _ffn((h,), row2(norm_ffn2[l]), bf(ffn2_w1), bf(ffn2_w3), bf(ffn2_w2), row2(norm_final),
                  last, n_p if last else 0)
        hs = hs if last else (hs,)

        outs["pa"].append(sa_p)
        outs["sa"].append(sa_s)
        outs["pb"].append(sb_p[:, :, :B_KDIM, :])
        outs["sb"].append(sb_s[:, :, :B_KDIM, :])
        outs["pk"].append(kc[:n_p].reshape(bp, tp, C_KV_HEADS, C_HDIM))
        outs["pv"].append(vc[:n_p].reshape(bp, tp, C_KV_HEADS, C_HDIM))
        outs["pki"].append(ki[:n_p].reshape(bp, tp, IDX_DIM))
        outs["sk"].append(kc[n_p:].reshape(bs, ts, C_KV_HEADS, C_HDIM))
        outs["sv"].append(vc[n_p:].reshape(bs, ts, C_KV_HEADS, C_HDIM))
        outs["ski"].append(ki[n_p:].reshape(bs, ts, IDX_DIM))

    st = {k: jnp.stack(v) for k, v in outs.items()}
    return (hs[0].reshape(bp, tp, d), hs[1].reshape(bs, ts, d),
            st["pa"], st["pb"], st["pk"], st["pv"], st["pki"],
            st["sa"], st["sb"], st["sk"], st["sv"], st["ski"])
```

```python
import functools

import jax
import jax.numpy as jnp
from jax import lax
from jax.experimental import pallas as pl
from jax.experimental.pallas import tpu as pltpu

F32 = jnp.float32
BF16 = jnp.bfloat16

D_MODEL = 1024
DEPTH = 2
CHUNK = 64
EPS = 1e-6
NEG = -1e30
F_MIN = 1e-30
A_HEADS = 4
A_KDIM = 128
A_VDIM = 128
B_HEADS = 4
B_KDIM = 64
B_VDIM = 128
B_GATE_RANK = 16
B_TAU = 16.0
C_HEADS = 8
C_KV_HEADS = 2
C_HDIM = 64
C_WIDTH = C_HEADS * C_HDIM
IDX_HEADS = 8
IDX_DIM = 64
IDX_SCALE = (IDX_HEADS * IDX_DIM) ** -0.5
TOPK_MAX = 256
ROPE_THETA = 500000.0
ROPE_DIMS = C_HDIM // 4
PAIR_BLOCK = 4
D_FF = 2816

LANES = 128
HEAD_W = 128
MIX_W = 4 * HEAD_W
TOKEN_TILE = 512
PROJ_TILE = 256
KEY_BLOCK = 128
VT_ROWS = 80
INT_MIN = -(2 ** 31)
FLOAT_MAX = 3.4028234663852886e38
CODE_NEG_INF = -(2 ** 31) + 0x7FFFFF

GATE_UNITS = 24
Z_A = 0
Z_B = 16
Z_SMALL = 32
Z_UNITS = 33
Z_CQ = GATE_UNITS + Z_UNITS
Z_IQ = Z_CQ + 4
Z_CK = Z_IQ + 4
Z_CV = Z_CK + 1
Z_IK = Z_CV + 1
Z_WIDTH = (Z_IK + 1) * LANES
SMALL_IW = B_GATE_RANK


def _params(sem, vmem_mb):
    return pltpu.CompilerParams(dimension_semantics=sem, vmem_limit_bytes=vmem_mb << 20)


def _dot(a, b):
    return jnp.dot(a, b, preferred_element_type=F32)


def _dot_nt(a, b):
    return lax.dot_general(a, b, (((1,), (1,)), ((), ())), preferred_element_type=F32)


def _dot_tn(a, b):
    return lax.dot_general(a, b, (((0,), (0,)), ((), ())), preferred_element_type=F32)


def _split3(x):
    h1 = x.astype(BF16)
    r1 = x - h1.astype(F32)
    h2 = r1.astype(BF16)
    h3 = (r1 - h2.astype(F32)).astype(BF16)
    return h1, h2, h3


def _dot_exact_lhs(m, x):
    h1, h2, h3 = _split3(x)
    return _dot(m, h1) + _dot(m, h2) + _dot(m, h3)


def _dot_hi(a, b):
    a1 = a.astype(BF16)
    a2 = (a - a1.astype(F32)).astype(BF16)
    b1 = b.astype(BF16)
    b2 = (b - b1.astype(F32)).astype(BF16)
    return _dot(a1, b1) + _dot(a1, b2) + _dot(a2, b1)


def _rmsnorm(x, g):
    return x * lax.rsqrt(jnp.mean(x * x, axis=-1, keepdims=True) + EPS) * g


def _ffn_body(*refs, n_in, n_out, prompt_tiles, final_norm):
    x_refs, (g_ref, w1_ref, w3_ref, w2_ref, gf_ref), o_refs = refs[:n_in], refs[n_in:n_in + 5], refs[n_in + 5:]
    is_prompt = pl.program_id(0) < prompt_tiles
    x = x_refs[0][...] if n_in == 1 else jnp.where(is_prompt, x_refs[0][...], x_refs[1][...])
    u = _rmsnorm(x, g_ref[...]).astype(BF16)
    a = _dot(u, w1_ref[...])
    b = _dot(u, w3_ref[...])
    hid = (a * jax.nn.sigmoid(a) * b).astype(BF16)
    out = x + 0.5 * _dot(hid, w2_ref[...])
    if final_norm:
        out = _rmsnorm(out, gf_ref[...])
    if n_out == 1:
        o_refs[0][...] = out
    else:
        @pl.when(is_prompt)
        def _():
            o_refs[0][...] = out

        @pl.when(jnp.logical_not(is_prompt))
        def _():
            o_refs[1][...] = out


def _ffn(xs, g, w1, w3, w2, gf, final_norm, split_out):
    d = xs[0].shape[1]
    tm = TOKEN_TILE
    sizes = [x.shape[0] for x in xs]
    n = sum(sizes)
    assert all(sz % tm == 0 for sz in sizes)
    pt = sizes[0] // tm if len(xs) == 2 else split_out // tm
    prow = lambda i: (jnp.minimum(i, pt - 1), 0)
    srow = lambda i: (jnp.maximum(i - pt, 0), 0)
    row = lambda i: (i, 0)
    const = lambda i: (0, 0)
    resident = pl.Buffered(1)
    x_specs = [pl.BlockSpec((tm, d), row)] if len(xs) == 1 else \
        [pl.BlockSpec((tm, d), prow), pl.BlockSpec((tm, d), srow)]
    if split_out:
        out_specs = [pl.BlockSpec((tm, d), prow), pl.BlockSpec((tm, d), srow)]
        out_shape = [jax.ShapeDtypeStruct((split_out, d), F32), jax.ShapeDtypeStruct((n - split_out, d), F32)]
    else:
        out_specs = pl.BlockSpec((tm, d), row)
        out_shape = jax.ShapeDtypeStruct((n, d), F32)
    return pl.pallas_call(
        functools.partial(_ffn_body, n_in=len(xs), n_out=2 if split_out else 1, prompt_tiles=pt,
                          final_norm=final_norm),
        grid=(n // tm,),
        in_specs=x_specs + [
            pl.BlockSpec((1, d), const),
            pl.BlockSpec(w1.shape, const, pipeline_mode=resident),
            pl.BlockSpec(w3.shape, const, pipeline_mode=resident),
            pl.BlockSpec(w2.shape, const, pipeline_mode=resident),
            pl.BlockSpec((1, d), const),
        ],
        out_specs=out_specs,
        out_shape=out_shape,
        compiler_params=_params(("arbitrary",), 48),
        name="ffn_half_step",
    )(*xs, g, w1, w3, w2, gf)


def _rope(x, c, s_lo, s_hi):
    w = x.shape[1]
    rep = w // LANES
    if rep > 1:
        c = jnp.concatenate([c] * rep, axis=1)
        s_lo = jnp.concatenate([s_lo] * rep, axis=1)
        s_hi = jnp.concatenate([s_hi] * rep, axis=1)
    half = ROPE_DIMS // 2
    return x * c + pltpu.roll(x, half, 1) * s_hi + pltpu.roll(x, w - half, 1) * s_lo


def _inproj_body(x_ref, g_ref, w_ref, c_ref, slo_ref, shi_ref,
                 gates_ref, z_ref, qc_ref, qi_ref, kc_ref, vc_ref, ki_ref, kcb_ref, vcb_ref, kib_ref, vt_ref):
    u = _rmsnorm(x_ref[...], g_ref[...]).astype(BF16)
    zf = _dot(u, w_ref[...])
    gates_ref[...] = jax.nn.sigmoid(zf[:, :GATE_UNITS * LANES]).astype(BF16)
    z_ref[...] = zf[:, GATE_UNITS * LANES:(GATE_UNITS + Z_UNITS) * LANES]

    unit = lambda first, count: zf[:, first * LANES:(first + count) * LANES]
    c, s_lo, s_hi = c_ref[...], slo_ref[...], shi_ref[...]
    qc_ref[...] = (_rope(unit(Z_CQ, 4), c, s_lo, s_hi) * (C_HDIM ** -0.5)).astype(BF16)
    qi_ref[...] = _rope(unit(Z_IQ, 4), c, s_lo, s_hi).astype(BF16)
    kc = _rope(unit(Z_CK, 1), c, s_lo, s_hi)
    vc = unit(Z_CV, 1)
    ki = _rope(unit(Z_IK, 1), c, s_lo, s_hi)[:, :IDX_DIM]
    kc_ref[...] = kc
    vc_ref[...] = vc
    ki_ref[...] = ki
    kcb_ref[...] = kc.astype(BF16)
    vcb_ref[...] = vc.astype(BF16)
    kib_ref[...] = ki.astype(BF16)
    ones = jnp.ones((VT_ROWS - C_HDIM, KEY_BLOCK), BF16)
    for kk in range(vt_ref.shape[0]):
        vt = vc[kk * KEY_BLOCK:(kk + 1) * KEY_BLOCK, :].T.astype(BF16)
        vt_ref[kk] = jnp.concatenate([vt[:C_HDIM], ones, vt[C_HDIM:], ones], axis=0)


def _inproj(h, g, w, tables, table_tile):
    n, d = h.shape
    tm = PROJ_TILE
    assert n % tm == 0
    row = lambda i: (i, 0)
    return pl.pallas_call(
        _inproj_body,
        grid=(n // tm,),
        in_specs=[
            pl.BlockSpec((tm, d), row),
            pl.BlockSpec((1, d), lambda i: (0, 0)),
            pl.BlockSpec((d, Z_WIDTH), lambda i: (0, 0), pipeline_mode=pl.Buffered(1)),
            pl.BlockSpec((tm, LANES), table_tile),
            pl.BlockSpec((tm, LANES), table_tile),
            pl.BlockSpec((tm, LANES), table_tile),
        ],
        out_specs=[
            pl.BlockSpec((tm, GATE_UNITS * LANES), row),
            pl.BlockSpec((tm, Z_UNITS * LANES), row),
            pl.BlockSpec((tm, C_WIDTH), row),
            pl.BlockSpec((tm, IDX_HEADS * IDX_DIM), row),
            pl.BlockSpec((tm, LANES), row),
            pl.BlockSpec((tm, LANES), row),
            pl.BlockSpec((tm, IDX_DIM), row),
            pl.BlockSpec((tm, LANES), row),
            pl.BlockSpec((tm, LANES), row),
            pl.BlockSpec((tm, IDX_DIM), row),
            pl.BlockSpec((tm // KEY_BLOCK, C_KV_HEADS * VT_ROWS, KEY_BLOCK), lambda i: (i, 0, 0)),
        ],
        out_shape=[
            jax.ShapeDtypeStruct((n, GATE_UNITS * LANES), BF16),
            jax.ShapeDtypeStruct((n, Z_UNITS * LANES), F32),
            jax.ShapeDtypeStruct((n, C_WIDTH), BF16),
            jax.ShapeDtypeStruct((n, IDX_HEADS * IDX_DIM), BF16),
            jax.ShapeDtypeStruct((n, LANES), F32),
            jax.ShapeDtypeStruct((n, LANES), F32),
            jax.ShapeDtypeStruct((n, IDX_DIM), F32),
            jax.ShapeDtypeStruct((n, LANES), BF16),
            jax.ShapeDtypeStruct((n, LANES), BF16),
            jax.ShapeDtypeStruct((n, IDX_DIM), BF16),
            jax.ShapeDtypeStruct((n // KEY_BLOCK, C_KV_HEADS * VT_ROWS, KEY_BLOCK), BF16),
        ],
        compiler_params=_params(("parallel",), 48),
        name="in_projection",
    )(h, g, w, *tables)


def _gla_head(q, k, v, logf, st_ref, h, tc):
    row = lax.broadcasted_iota(jnp.int32, (tc, tc), 0)
    col = lax.broadcasted_iota(jnp.int32, (tc, tc), 1)
    tril = (col <= row).astype(BF16)
    cum = _dot_exact_lhs(tril, logf)
    yield None

    tile = 2 * PAIR_BLOCK
    nt = tc // tile
    b3 = cum.reshape(nt, tile, HEAD_W)
    q3 = q.reshape(nt, tile, HEAD_W)
    k3 = k.reshape(nt, tile, HEAD_W)
    v3 = v.reshape(nt, tile, HEAD_W)
    srow = lax.broadcasted_iota(jnp.int32, (nt, tile, 1), 1)
    low = srow < PAIR_BLOCK
    tloc = srow & (PAIR_BLOCK - 1)

    def block_row(x3, j):
        return jnp.where(low, x3[:, j:j + 1, :], x3[:, PAIR_BLOCK + j:PAIR_BLOCK + j + 1, :])

    o3 = jnp.zeros((nt, tile, HEAD_W), F32)
    for j in range(PAIR_BLOCK):
        causal = tloc >= j
        decay = jnp.exp(jnp.where(causal, b3 - block_row(b3, j), 0.0))
        w = jnp.sum(q3 * block_row(k3, j) * decay, axis=-1, keepdims=True)
        o3 = o3 + jnp.where(causal, w, 0.0) * block_row(v3, j)
        yield None
    o = o3.reshape(tc, HEAD_W)
    yield None

    vb = v.astype(BF16)
    attn = jnp.zeros((tc, tc), F32)
    half = tc // 2
    while half >= PAIR_BLOCK:
        blk = 2 * half
        nblk = tc // blk
        bl = cum.reshape(nblk, blk, HEAD_W)
        x = bl - bl[:, half - 1:half, :]
        second = lax.broadcasted_iota(jnp.int32, (nblk, blk, 1), 1) >= half
        e = jnp.exp(jnp.where(second, x, -x))
        qt = jnp.where(second, q.reshape(nblk, blk, HEAD_W) * e, 0.0).reshape(tc, HEAD_W).astype(BF16)
        kt = jnp.where(second, 0.0, k.reshape(nblk, blk, HEAD_W) * e).reshape(tc, HEAD_W).astype(BF16)
        shift = blk.bit_length() - 1
        same = (row >> shift) == (col >> shift)
        attn = attn + jnp.where(same, _dot_nt(qt, kt), 0.0)
        half //= 2
        yield None
    o = o + _dot(attn.astype(BF16), vb)
    yield None

    st = st_ref[h]
    o = o + _dot_nt((q * jnp.exp(cum)).astype(BF16), st.astype(BF16))
    last = cum[tc - 1:tc, :]
    kd = (k * jnp.exp(last - cum)).astype(BF16)
    st_ref[h] = st * jnp.exp(last) + _dot_tn(vb, kd)
    yield o


def _gla_heads(inputs, st_ref, tc):
    heads = [_gla_head(q, k, v, logf, st_ref, h, tc) for h, (q, k, v, logf) in enumerate(inputs)]
    outs = None
    while True:
        step = [next(head, StopIteration) for head in heads]
        if step[0] is StopIteration:
            return outs
        outs = step


def _gla_finish(o, nw, gate):
    return _rmsnorm(o, nw) * (gate * jax.nn.sigmoid(gate))


def _gla_state_io(c, s0_ref, st_ref, heads):
    @pl.when(c == 0)
    def _():
        for h in range(heads):
            st_ref[h] = s0_ref[0, h].T


def _gla_state_out(c, sout_ref, st_ref, heads):
    @pl.when(c == pl.num_programs(1) - 1)
    def _():
        for h in range(heads):
            sout_ref[0, h] = st_ref[h].T


def _hgrn_body(q_ref, f_ref, v_ref, g_ref, lb_ref, nw_ref, s0_ref, y_ref, sout_ref, st_ref, *, tc):
    c = pl.program_id(1)
    _gla_state_io(c, s0_ref, st_ref, A_HEADS)
    inputs = []
    for h in range(A_HEADS):
        hs = slice(h * HEAD_W, (h + 1) * HEAD_W)
        zf = f_ref[:, hs]
        lb = lb_ref[:, hs]
        f = lb + (1.0 - lb) * jax.nn.sigmoid(zf)
        logf = jnp.log(jnp.maximum(f, F_MIN))
        k = (1.0 - lb) * jax.nn.sigmoid(-zf)
        zq = q_ref[:, hs]
        q = zq * jax.nn.sigmoid(zq) * (A_KDIM ** -0.5)
        inputs.append((q, k, v_ref[:, hs], logf))
    for h, o in enumerate(_gla_heads(inputs, st_ref, tc)):
        hs = slice(h * HEAD_W, (h + 1) * HEAD_W)
        y_ref[:, hs] = _gla_finish(o, nw_ref[...], g_ref[:, hs]).astype(BF16)
    _gla_state_out(c, sout_ref, st_ref, A_HEADS)


def _gla_body(q_ref, k_ref, v_ref, g_ref, r_ref, wup_ref, bup_ref, nw_ref, s0_ref,
              y_ref, sout_ref, st_ref, *, tc):
    c = pl.program_id(1)
    _gla_state_io(c, s0_ref, st_ref, B_HEADS)
    r = _dot_hi(r_ref[...], wup_ref[...]) + bup_ref[...]
    logf_all = (jnp.minimum(r, 0.0) - jnp.log1p(jnp.exp(-jnp.abs(r)))) / B_TAU
    heads = [slice(h * HEAD_W, (h + 1) * HEAD_W) for h in range(B_HEADS)]
    inputs = [(q_ref[:, hs] * (B_KDIM ** -0.5), k_ref[:, hs], v_ref[:, hs], logf_all[:, hs]) for hs in heads]
    for hs, o in zip(heads, _gla_heads(inputs, st_ref, tc)):
        y_ref[:, hs] = _gla_finish(o, nw_ref[...], g_ref[:, hs]).astype(BF16)
    _gla_state_out(c, sout_ref, st_ref, B_HEADS)


def _recurrent_mixer(mode, z, row0, nseq, t, s0, nw, extra):
    tc = min(128, t)
    nc = t // tc
    rb0 = row0 // tc
    zcol = (Z_A if mode == "hgrn" else Z_B) // 4

    def zspec(k):
        return pl.BlockSpec((tc, MIX_W), lambda b, c: (rb0 + b * nc + c, zcol + k))

    const = lambda b, c: (0, 0)
    state_spec = pl.BlockSpec((1, 4, HEAD_W, HEAD_W), lambda b, c: (b, 0, 0, 0))
    if mode == "hgrn":
        body = functools.partial(_hgrn_body, tc=tc)
        in_specs = [zspec(0), zspec(1), zspec(2), zspec(3),
                    pl.BlockSpec((1, MIX_W), const), pl.BlockSpec((1, HEAD_W), const), state_spec]
        args = (z, z, z, z, extra[0], nw, s0)
    else:
        body = functools.partial(_gla_body, tc=tc)
        in_specs = [zspec(0), zspec(1), zspec(2), zspec(3),
                    pl.BlockSpec((tc, LANES), lambda b, c: (rb0 + b * nc + c, Z_SMALL)),
                    pl.BlockSpec((LANES, MIX_W), const), pl.BlockSpec((1, MIX_W), const),
                    pl.BlockSpec((1, HEAD_W), const), state_spec]
        args = (z, z, z, z, z, extra[0], extra[1], nw, s0)
    return pl.pallas_call(
        body,
        grid=(nseq, nc),
        in_specs=in_specs,
        out_specs=[pl.BlockSpec((tc, MIX_W), lambda b, c: (b * nc + c, 0)), state_spec],
        out_shape=[jax.ShapeDtypeStruct((nseq * t, MIX_W), BF16),
                   jax.ShapeDtypeStruct((nseq, 4, HEAD_W, HEAD_W), F32)],
        scratch_shapes=[pltpu.VMEM((4, HEAD_W, HEAD_W), F32)],
        compiler_params=_params(("parallel", "arbitrary"), 32),
        name=mode + "_mixer",
    )(*args)


SUPER = 8
ATTN_BLOCKS = 4


def _dsa_t_body(qc_ref, qi_ref, sm_ref, kc_ref, vt_ref, ki_ref, o_ref,
                sk_ref, qit_ref, qct_ref, wrow_ref, acc_ref, ot_ref, s0_ref, s1_ref,
                *, tq, past, nkeys, topk):
    i = pl.program_id(1)
    qpos0 = past + i * tq
    last_chunk = (qpos0 + tq - 1) // CHUNK
    n_adm = jnp.minimum((last_chunk + 1) * CHUNK, nkeys)
    nkb = (n_adm + KEY_BLOCK - 1) // KEY_BLOCK
    nsb = (nkb + SUPER - 1) // SUPER
    grp = C_HEADS // C_KV_HEADS

    qit = qi_ref[...].astype(F32).T
    for j in range(IDX_HEADS):
        qit_ref[:, j * tq:(j + 1) * tq] = qit[j * IDX_DIM:(j + 1) * IDX_DIM, :].astype(BF16)
    qct = qc_ref[...].astype(F32).T
    zeros = jnp.zeros((C_HDIM, tq), BF16)
    for hq in range(C_HEADS):
        g, r = divmod(hq, grp)
        for gg in range(C_KV_HEADS):
            val = qct[hq * C_HDIM:(hq + 1) * C_HDIM, :].astype(BF16) if gg == g else zeros
            qct_ref[g, gg * C_HDIM:(gg + 1) * C_HDIM, r * tq:(r + 1) * tq] = val
    wrow_ref[...] = sm_ref[...].T

    qchunk = (qpos0 + lax.broadcasted_iota(jnp.int32, (1, tq), 1)) >> 6
    ksub = lax.broadcasted_iota(jnp.int32, (KEY_BLOCK, 1), 0)

    def admissible(kb):
        kpos = kb * KEY_BLOCK + ksub
        return ((kpos >> 6) <= qchunk) & (kpos < nkeys)

    def score_step(kk, carry):
        for u in range(SUPER):
            kb = kk * SUPER + u
            kib = ki_ref[0, pl.ds(pl.multiple_of(kb * KEY_BLOCK, KEY_BLOCK), KEY_BLOCK), :]
            d = _dot(kib, qit_ref[...])
            sc = jnp.maximum(d[:, 0:tq], 0.0) * wrow_ref[SMALL_IW:SMALL_IW + 1, :]
            for j in range(1, IDX_HEADS):
                sc = sc + jnp.maximum(d[:, j * tq:(j + 1) * tq], 0.0) * wrow_ref[SMALL_IW + j:SMALL_IW + j + 1, :]
            sk_ref[kb] = jnp.where(admissible(kb), sc * IDX_SCALE + 0.0, -jnp.inf)
        return carry

    lax.fori_loop(0, nsb, score_step, 0)

    def decode(code):
        return pltpu.bitcast(jnp.where(code < 0, code ^ 0x7FFFFFFF, code), F32)

    def fold(hit):
        return jnp.sum(hit.reshape(KEY_BLOCK // 8, 8, tq), axis=0)

    def count(pred):
        def step(kk, acc):
            for u in range(SUPER):
                kb = kk * SUPER + u
                acc = acc + fold(jnp.where(pred(sk_ref[kb], kb), 1.0, 0.0))
            return acc
        acc = lax.fori_loop(0, nsb, step, jnp.zeros((8, tq), F32))
        return jnp.sum(acc, axis=0, keepdims=True)

    kf = float(topk)
    c0 = count(lambda s, kb: s >= 0.0)
    t0 = jnp.where(c0 >= kf, 0, INT_MIN).astype(jnp.int32)

    def bit_step(it, t):
        cand = t | jnp.left_shift(jnp.int32(1), 30 - it)
        cand_f = decode(cand)
        return jnp.where(count(lambda s, kb: s >= cand_f) >= kf, cand, t)
    code = lax.fori_loop(0, 31, bit_step, t0)
    thr = jnp.where(code <= CODE_NEG_INF, -FLOAT_MAX, decode(code))

    c_ge = count(lambda s, kb: s >= thr)
    nbits = max(1, (nkeys - 1).bit_length())

    @pl.when(jnp.max(c_ge) > kf)
    def _():
        need = kf - count(lambda s, kb: s > thr)

        def cut_step(it, cut):
            cand = cut | jnp.left_shift(jnp.int32(1), nbits - 1 - it)
            c = count(lambda s, kb: (s == thr) & (kb * KEY_BLOCK + ksub < cand))
            return jnp.where(c < need, cand, cut)
        cut = lax.fori_loop(0, nbits, cut_step, jnp.zeros((1, tq), jnp.int32))

        def strike(kb, carry):
            key = sk_ref[kb]
            sk_ref[kb] = jnp.where((key == thr) & (kb * KEY_BLOCK + ksub > cut), -jnp.inf, key)
            return carry
        lax.fori_loop(0, nsb * SUPER, strike, 0)

    acc_ref[...] = jnp.zeros(acc_ref.shape, F32)
    init = tuple(jnp.full((1, grp * tq), NEG, F32) for _ in range(C_KV_HEADS))

    span = ATTN_BLOCKS * KEY_BLOCK
    n_attn = nsb * (SUPER // ATTN_BLOCKS)

    s_bufs = (s0_ref, s1_ref)

    def qk_scores(kk, half):
        kblk = kc_ref[0, pl.ds(pl.multiple_of(kk * span, span), span), :]
        bias = jnp.concatenate(
            [jnp.where(sk_ref[kk * ATTN_BLOCKS + u] >= thr, 0.0, NEG) for u in range(ATTN_BLOCKS)], axis=0)
        bias = jnp.concatenate([bias] * grp, axis=1)
        for g in range(C_KV_HEADS):
            s = _dot(kblk, qct_ref[g]) + bias
            for r in range(grp):
                s_bufs[half][g, r] = s[:, r * tq:(r + 1) * tq]

    def softmax_pv(kk, half, m):
        m = list(m)
        pending = []
        for g in range(C_KV_HEADS):
            m_cols, alpha_cols, p_cols = [], [], []
            for r in range(grp):
                cols = slice(r * tq, (r + 1) * tq)
                sb = s_bufs[half][g, r]
                m_old = m[g][:, cols]
                m_new = jnp.maximum(m_old, jnp.max(sb, axis=0, keepdims=True))
                p_cols.append(jnp.exp(sb - m_new).astype(BF16))
                m_cols.append(m_new)
                alpha_cols.append(jnp.exp(m_old - m_new))
            rows = slice(g * VT_ROWS, (g + 1) * VT_ROWS)
            pv = None
            for u in range(ATTN_BLOCKS):
                keys_u = slice(u * KEY_BLOCK, (u + 1) * KEY_BLOCK)
                p_u = jnp.concatenate([pc[keys_u, :] for pc in p_cols], axis=1)
                d = _dot(vt_ref[0, kk * ATTN_BLOCKS + u, rows, :], p_u)
                pv = d if pv is None else pv + d
            pending.append((alpha_cols, pv))
            m[g] = jnp.concatenate(m_cols, axis=1)
        for g, (alpha_cols, pv) in enumerate(pending):
            for r in range(grp):
                acc_ref[g, r] = alpha_cols[r] * acc_ref[g, r] + pv[:, r * tq:(r + 1) * tq]
        return tuple(m)

    qk_scores(0, 0)

    def pair_step(j, m):
        qk_scores(2 * j + 1, 1)
        m = softmax_pv(2 * j, 0, m)
        qk_scores(jnp.minimum(2 * j + 2, n_attn - 2), 0)
        return softmax_pv(2 * j + 1, 1, m)

    lax.fori_loop(0, n_attn // 2, pair_step, init)
    for hq in range(C_HEADS):
        g, r = divmod(hq, grp)
        ot_ref[hq * C_HDIM:(hq + 1) * C_HDIM, :] = acc_ref[g, r, 0:C_HDIM, :] / acc_ref[g, r, C_HDIM:C_HDIM + 1, :]
    o_ref[...] = ot_ref[...].T.astype(BF16)


def _dsa_t(qc, qi, small, small_col, keys, vals_t, kidx, nseq, t, past, nkeys):
    tq = KEY_BLOCK
    nq = t // tq
    lp = keys.shape[1]
    assert t % tq == 0 and lp % (SUPER * KEY_BLOCK) == 0
    topk = min(TOPK_MAX, nkeys // 4)
    grp = C_HEADS // C_KV_HEADS
    qrow = lambda b, i: (b * nq + i, 0)
    return pl.pallas_call(
        functools.partial(_dsa_t_body, tq=tq, past=past, nkeys=nkeys, topk=topk),
        grid=(nseq, nq),
        in_specs=[
            pl.BlockSpec((tq, C_WIDTH), qrow),
            pl.BlockSpec((tq, IDX_HEADS * IDX_DIM), qrow),
            pl.BlockSpec((tq, LANES), lambda b, i: (b * nq + i, small_col)),
            pl.BlockSpec((1, lp, LANES), lambda b, i: (b, 0, 0), pipeline_mode=pl.Buffered(1)),
            pl.BlockSpec((1, lp // KEY_BLOCK, C_KV_HEADS * VT_ROWS, KEY_BLOCK), lambda b, i: (b, 0, 0, 0),
                         pipeline_mode=pl.Buffered(1)),
            pl.BlockSpec((1, lp, IDX_DIM), lambda b, i: (b, 0, 0), pipeline_mode=pl.Buffered(1)),
        ],
        out_specs=pl.BlockSpec((tq, C_WIDTH), lambda b, i: (b * nq + i, 0)),
        out_shape=jax.ShapeDtypeStruct((nseq * t, C_WIDTH), BF16),
        scratch_shapes=[
            pltpu.VMEM((lp // KEY_BLOCK, KEY_BLOCK, tq), F32),
            pltpu.VMEM((IDX_DIM, IDX_HEADS * tq), BF16),
            pltpu.VMEM((C_KV_HEADS, C_KV_HEADS * C_HDIM, grp * tq), BF16),
            pltpu.VMEM((LANES, tq), F32),
            pltpu.VMEM((C_KV_HEADS, grp, VT_ROWS, tq), F32),
            pltpu.VMEM((C_WIDTH, tq), F32),
            pltpu.VMEM((C_KV_HEADS, grp, ATTN_BLOCKS * KEY_BLOCK, tq), F32),
            pltpu.VMEM((C_KV_HEADS, grp, ATTN_BLOCKS * KEY_BLOCK, tq), F32),
        ],
        compiler_params=_params(("parallel", "arbitrary"), 48),
        name="dsa_mixer_t",
    )(qc, qi, small, keys, vals_t, kidx)


def _merge_body(h_ref, yap_ref, yas_ref, ybp_ref, ybs_ref, ocp_ref, ocs_ref, ga_ref, gb_ref, gc_ref,
                wa_ref, wb_ref, wc_ref, wo_ref, o_ref, *, prompt_tiles):
    is_prompt = pl.program_id(0) < prompt_tiles
    pick = lambda p_ref, s_ref: jnp.where(is_prompt, p_ref[...], s_ref[...])
    merged = (ga_ref[...].astype(F32) * _dot(pick(yap_ref, yas_ref), wa_ref[...])
              + gb_ref[...].astype(F32) * _dot(pick(ybp_ref, ybs_ref), wb_ref[...])
              + gc_ref[...].astype(F32) * _dot(pick(ocp_ref, ocs_ref), wc_ref[...]))
    o_ref[...] = h_ref[...] + _dot(merged.astype(BF16), wo_ref[...])


def _merge(h, ya, yb, oc, gates, wa, wb, wc, wo):
    n, d = h.shape
    tm = TOKEN_TILE
    pt = ya[0].shape[0] // tm
    assert ya[0].shape[0] % tm == 0 and ya[1].shape[0] % tm == 0 and n == ya[0].shape[0] + ya[1].shape[0]
    row = lambda i: (i, 0)
    const = lambda i: (0, 0)
    prow = lambda i: (jnp.minimum(i, pt - 1), 0)
    srow = lambda i: (jnp.maximum(i - pt, 0), 0)
    pair = [pl.BlockSpec((tm, MIX_W), prow), pl.BlockSpec((tm, MIX_W), srow)]
    return pl.pallas_call(
        functools.partial(_merge_body, prompt_tiles=pt),
        grid=(n // tm,),
        in_specs=[
            pl.BlockSpec((tm, d), row),
            *pair, *pair, *pair,
            pl.BlockSpec((tm, d), lambda i: (i, 0)),
            pl.BlockSpec((tm, d), lambda i: (i, 1)),
            pl.BlockSpec((tm, d), lambda i: (i, 2)),
            pl.BlockSpec((MIX_W, d), const),
            pl.BlockSpec((MIX_W, d), const),
            pl.BlockSpec((MIX_W, d), const),
            pl.BlockSpec((d, d), const),
        ],
        out_specs=pl.BlockSpec((tm, d), row),
        out_shape=jax.ShapeDtypeStruct((n, d), F32),
        compiler_params=_params(("parallel",), 48),
        name="gated_merge",
    )(h, *ya, *yb, *oc, gates, gates, gates, wa, wb, wc, wo)


def _pad_heads(w, heads, dim):
    lead = w.shape[:-1]
    w = w.reshape(*lead, heads, dim)
    w = jnp.pad(w, [(0, 0)] * len(lead) + [(0, 0), (0, HEAD_W - dim)])
    return w.reshape(*lead, heads * HEAD_W)


def _layout_w_in(w):
    widths = (512, 512, 512, 512, 256, 256, 512, 16, 512, 512, 128, 128, 512, 64, 8, 1024, 1024, 1024)
    w = w.astype(BF16)
    parts, o = [], 0
    for wd in widths:
        parts.append(w[:, o:o + wd])
        o += wd
    (a_q, a_f, a_i, a_g, b_q, b_k, b_v, b_r, b_g, c_q, c_k, c_v, i_q, i_k, i_w, g_a, g_b, g_c) = parts
    d = w.shape[0]
    small = jnp.concatenate([b_r, i_w, jnp.zeros((d, LANES - 24), w.dtype)], axis=1)
    i_k = jnp.pad(i_k, ((0, 0), (0, LANES - IDX_DIM)))
    cols = [g_a, g_b, g_c, a_q, a_f, a_i, a_g,
            _pad_heads(b_q, B_HEADS, B_KDIM), _pad_heads(b_k, B_HEADS, B_KDIM), b_v, b_g,
            small, c_q, i_q, c_k, c_v, i_k]
    out = jnp.concatenate(cols, axis=1)
    assert out.shape[1] == Z_WIDTH
    return out


def _rope_tables(pos):
    half = ROPE_DIMS // 2
    inv = ROPE_THETA ** (-jnp.arange(half, dtype=F32) * (2.0 / ROPE_DIMS))
    ang = pos.astype(F32)[:, None] * inv[None, :]
    cos, sin = jnp.cos(ang), jnp.sin(ang)
    n = pos.shape[0]
    ones = jnp.ones((n, C_HDIM - ROPE_DIMS), F32)
    zeros = jnp.zeros((n, C_HDIM - ROPE_DIMS), F32)
    zh = jnp.zeros((n, half), F32)
    c = jnp.concatenate([cos, cos, ones], axis=1)
    s_lo = jnp.concatenate([-sin, zh, zeros], axis=1)
    s_hi = jnp.concatenate([zh, sin, zeros], axis=1)
    two = lambda a: jnp.concatenate([a, a], axis=1)
    return two(c), two(s_lo), two(s_hi)


def kernel(x_prompt, x_sample, state_hgrn, state_gla, cache_k, cache_v, cache_kidx, hgrn_lb, w_in, w_gla_up, b_gla, norm_hgrn, norm_gla, w_br_a, w_br_b, w_br_c, w_out, norm_ffn1, norm_mix, norm_ffn2, ffn1_w1, ffn1_w3, ffn1_w2, ffn2_w1, ffn2_w3, ffn2_w2, norm_final):
    bp, tp, d = x_prompt.shape
    bs, ts, _ = x_sample.shape
    past = cache_k.shape[2]
    n_p, n_s = bp * tp, bs * ts

    lb_sm = jax.nn.softmax(hgrn_lb.astype(F32), axis=0)
    lb_all = jnp.concatenate([jnp.zeros_like(lb_sm[:1]), jnp.cumsum(lb_sm[1:], axis=0)], axis=0)

    pos = jnp.concatenate([jnp.arange(tp, dtype=jnp.int32),
                           jnp.tile(past + jnp.arange(ts, dtype=jnp.int32), PROJ_TILE // ts)])
    tables = _rope_tables(pos)
    assert tp % PROJ_TILE == 0 and PROJ_TILE % ts == 0 and n_s % PROJ_TILE == 0
    table_tile = lambda i: (jnp.where(i < n_p // PROJ_TILE, i % (tp // PROJ_TILE), tp // PROJ_TILE), 0)

    hs = (x_prompt.reshape(n_p, d), x_sample.reshape(n_s, d))
    row2 = lambda a: a.reshape(1, -1)
    zero_state = jnp.zeros((bp, 4, HEAD_W, HEAD_W), F32)
    span_s = SUPER * KEY_BLOCK
    key_pad = -(-(past + ts) // span_s) * span_s - past - ts
    q_rep = KEY_BLOCK // ts

    ffn1_w1, ffn1_w3, ffn1_w2, ffn2_w1, ffn2_w3, ffn2_w2, w_br_a, w_br_b, w_br_c, w_out = (
        a.astype(BF16) for a in (ffn1_w1, ffn1_w3, ffn1_w2, ffn2_w1, ffn2_w3, ffn2_w2,
                                 w_br_a, w_br_b, w_br_c, w_out))

    outs = {k: [] for k in ("pa", "pb", "pk", "pv", "pki", "sa", "sb", "sk", "sv", "ski")}
    for l in range(DEPTH):
        bf = lambda a: a[l]
        h = _ffn(hs, row2(norm_ffn1[l]), bf(ffn1_w1), bf(ffn1_w3), bf(ffn1_w2), row2(norm_final), False, 0)
        gates, z, qc, qi, kc, vc, ki, kcb, vcb, kib, vt = _inproj(
            h, row2(norm_mix[l]), _layout_w_in(w_in[l]), tables, table_tile)

        lb = row2(lb_all[l])
        nwa, nwb = row2(norm_hgrn[l]), row2(norm_gla[l])
        wup = jnp.pad(_pad_heads(w_gla_up[l], B_HEADS, B_KDIM), ((0, LANES - B_GATE_RANK), (0, 0)))
        bup = row2(_pad_heads(b_gla[l], B_HEADS, B_KDIM))
        sb0 = jnp.pad(state_gla[l], ((0, 0), (0, 0), (0, HEAD_W - B_KDIM), (0, 0)))

        ya_p, sa_p = _recurrent_mixer("hgrn", z, 0, bp, tp, zero_state, nwa, (lb,))
        ya_s, sa_s = _recurrent_mixer("hgrn", z, n_p, bs, ts, state_hgrn[l], nwa, (lb,))
        yb_p, sb_p = _recurrent_mixer("gla", z, 0, bp, tp, zero_state, nwb, (wup, bup))
        yb_s, sb_s = _recurrent_mixer("gla", z, n_p, bs, ts, sb0, nwb, (wup, bup))

        seqs = lambda a, n, t: a.reshape(n, t, a.shape[-1])
        vt_p = vt[:n_p // KEY_BLOCK].reshape(bp, tp // KEY_BLOCK, C_KV_HEADS * VT_ROWS, KEY_BLOCK)
        oc_p = _dsa_t(qc, qi, z, Z_SMALL, seqs(kcb[:n_p], bp, tp), vt_p, seqs(kib[:n_p], bp, tp), bp, tp, 0, tp)

        def with_cache(cache, new):
            full = jnp.concatenate([cache.reshape(bs, past, -1).astype(BF16), seqs(new[n_p:], bs, ts)], axis=1)
            return jnp.pad(full, ((0, 0), (0, key_pad), (0, 0)))

        def fill_block(a):
            return jnp.tile(seqs(a[n_p:], bs, ts), (1, q_rep, 1)).reshape(bs * KEY_BLOCK, a.shape[-1])

        vals_s = with_cache(cache_v[l], vcb)
        vt_s = jnp.swapaxes(vals_s.reshape(bs, -1, KEY_BLOCK, LANES), 2, 3)
        ones = jnp.ones(vt_s.shape[:2] + (VT_ROWS - C_HDIM, KEY_BLOCK), BF16)
        vt_s = jnp.concatenate([vt_s[:, :, :C_HDIM], ones, vt_s[:, :, C_HDIM:], ones], axis=2)
        small_s = fill_block(z[:, Z_SMALL * LANES:(Z_SMALL + 1) * LANES])
        oc_s = _dsa_t(fill_block(qc), fill_block(qi), small_s, 0, with_cache(cache_k[l], kcb), vt_s,
                      with_cache(cache_kidx[l], kib), bs, KEY_BLOCK, past, past + ts)
        oc_s = oc_s.reshape(bs, KEY_BLOCK, C_WIDTH)[:, :ts].reshape(n_s, C_WIDTH)

        h = _merge(h, (ya_p, ya_s), (yb_p, yb_s), (oc_p, oc_s), gates,
                   bf(w_br_a), bf(w_br_b), bf(w_br_c), bf(w_out))
        last = l == DEPTH - 1
        hs = _ffn((h,), row2(norm_ffn2[l]), bf(ffn2_w1), bf(ffn2_w3), bf(ffn2_w2), row2(norm_final),
                  last, n_p if last else 0)
        hs = hs if last else (hs,)

        outs["pa"].append(sa_p)
        outs["sa"].append(sa_s)
        outs["pb"].append(sb_p[:, :, :B_KDIM, :])
        outs["sb"].append(sb_s[:, :, :B_KDIM, :])
        outs["pk"].append(kc[:n_p].reshape(bp, tp, C_KV_HEADS, C_HDIM))
        outs["pv"].append(vc[:n_p].reshape(bp, tp, C_KV_HEADS, C_HDIM))
        outs["pki"].append(ki[:n_p].reshape(bp, tp, IDX_DIM))
        outs["sk"].append(kc[n_p:].reshape(bs, ts, C_KV_HEADS, C_HDIM))
        outs["sv"].append(vc[n_p:].reshape(bs, ts, C_KV_HEADS, C_HDIM))
        outs["ski"].append(ki[n_p:].reshape(bs, ts, IDX_DIM))

    st = {k: jnp.stack(v) for k, v in outs.items()}
    return (hs[0].reshape(bp, tp, d), hs[1].reshape(bs, ts, d),
            st["pa"], st["pb"], st["pk"], st["pv"], st["pki"],
            st["sa"], st["sb"], st["sk"], st["sv"], st["ski"])
```
